```python
import math
import jax, jax.numpy as jnp
from jax import lax
import numpy as np


D_MODEL = 1024
BATCH = 16
SEQ = 256
DEPTH = 2
DEC_BATCH = 4
DEC_SEQ = 4096
PAST_LEN = 256

GRID_W = 64
N_MIXERS = 2
N_RET_LAYERS = (DEPTH + 1) // 2
N_MLA_LAYERS = DEPTH // 2
RET_HEADS = 4
RET_DK = D_MODEL // RET_HEADS
RET_DV = 2 * RET_DK
RET_QK_WIDTH = RET_HEADS * RET_DK
RET_V_WIDTH = RET_HEADS * RET_DV
RET_CHUNK = 128
MLA_HEADS = 8
MLA_NOPE = 128
MLA_ROPE = 64
MLA_V = 128
MLA_Q_LORA = 384
MLA_KV_LORA = 256
ROPE_BASE = 10000.0
Q_BLOCK = 128
N_EXPERTS = 16
EC_CAPACITY_FACTOR = 2
EXPERT_FF = 1024
NORM_EPS = 1e-6

kernel_name = "hybrid_retention_mla_ec_diffusion_step"


def rms_norm(x, gain):
    xf = x.astype(jnp.float32)
    y = xf * lax.rsqrt(jnp.mean(xf * xf, axis=-1, keepdims=True) + NORM_EPS)
    return (y * gain.astype(jnp.float32)).astype(x.dtype)


def head_group_norm(y):
    yf = y.astype(jnp.float32)
    mu = jnp.mean(yf, axis=-1, keepdims=True)
    var = jnp.mean(jnp.square(yf - mu), axis=-1, keepdims=True)
    return ((yf - mu) * lax.rsqrt(var + NORM_EPS)).astype(y.dtype)


def adaln_mod(cond, w, b):
    m = jax.nn.silu(cond) @ w + b
    m = m.reshape(m.shape[0], 1, 6, D_MODEL)
    return tuple(m[:, :, j] for j in range(6))


def axial_rope_tables(n_tokens):
    rows = n_tokens // GRID_W
    row = jnp.repeat(jnp.arange(rows, dtype=jnp.float32), GRID_W)
    col = jnp.tile(jnp.arange(GRID_W, dtype=jnp.float32), rows)
    half = MLA_ROPE // 2
    inv = 1.0 / (ROPE_BASE ** (jnp.arange(0, half, 2, dtype=jnp.float32) / half))
    ar = row[:, None] * inv
    ac = col[:, None] * inv
    ang = jnp.concatenate([ar, ar, ac, ac], axis=-1)
    return jnp.cos(ang), jnp.sin(ang)


def apply_axial_rope(x, cos, sin):
    x1, x2, x3, x4 = jnp.split(x, 4, axis=-1)
    rot = jnp.concatenate([-x2, x1, -x4, x3], axis=-1)
    return (x * cos + rot * sin).astype(x.dtype)


def chunk_retention(q, k, v, log_gamma, s0, strict):
    b, n, h, dk = q.shape
    dv = v.shape[-1]
    nc = n // RET_CHUNK
    dt = q.dtype
    pos = jnp.arange(RET_CHUNK, dtype=jnp.float32)
    diff = pos[:, None] - pos[None, :]
    mask = (diff > 0) if strict else (diff >= 0)
    lg = log_gamma[:, None, None]
    inner_decay = jnp.exp(jnp.where(mask[None], diff[None] * lg, -jnp.inf)).astype(dt)
    cross_decay = jnp.exp((pos + 1.0)[:, None] * log_gamma[None, :]).astype(dt)
    state_decay = jnp.exp((RET_CHUNK - 1.0 - pos)[:, None] * log_gamma[None, :]).astype(dt)
    chunk_decay = jnp.exp(RET_CHUNK * log_gamma).astype(dt)

    def to_chunks(t):
        return jnp.swapaxes(t.reshape(b, nc, RET_CHUNK, h, t.shape[-1]), 0, 1)

    def step(s, qkv):
        qc, kc, vc = qkv
        scores = jnp.einsum('bihd,bjhd->bhij', qc, kc) * inner_decay
        y = (jnp.einsum('bhij,bjhe->bihe', scores, vc)
             + jnp.einsum('bihd,bhde->bihe', qc, s) * cross_decay[None, :, :, None])
        s = (s * chunk_decay[None, :, None, None]
             + jnp.einsum('bjhd,bjhe->bhde', kc * state_decay[None, :, :, None], vc))
        return s, y

    s_final, ys = lax.scan(step, s0, (to_chunks(q), to_chunks(k), to_chunks(v)))
    return jnp.swapaxes(ys, 0, 1).reshape(b, n, h, dv), s_final


def retention_mixer(h, s_fwd0, s_bwd0, w_in, decay_fwd, decay_bwd, w_out):
    b, n, _ = h.shape
    q, k, v, g = jnp.split(h @ w_in, [RET_QK_WIDTH, 2 * RET_QK_WIDTH, 2 * RET_QK_WIDTH + RET_V_WIDTH], axis=-1)
    q = q.reshape(b, n, RET_HEADS, RET_DK)
    k = k.reshape(b, n, RET_HEADS, RET_DK) * (RET_DK ** -0.5)
    v = v.reshape(b, n, RET_HEADS, RET_DV)
    lg_f = jax.nn.log_sigmoid(decay_fwd.astype(jnp.float32))
    lg_b = jax.nn.log_sigmoid(decay_bwd.astype(jnp.float32))
    y_f, s_f = chunk_retention(q, k, v, lg_f, s_fwd0, False)
    flip = lambda t: jnp.flip(t, axis=1)
    y_b, s_b = chunk_retention(flip(q), flip(k), flip(v), lg_b, s_bwd0, True)
    y = head_group_norm(y_f + flip(y_b)).reshape(b, n, RET_V_WIDTH)
    return (jax.nn.silu(g) * y) @ w_out, s_f, s_b


def mla_project(h, w_in, q_norm, kv_norm, w_q_b):
    b, n, _ = h.shape
    q_lat, kv_lat, k_rope = jnp.split(h @ w_in, [MLA_Q_LORA, MLA_Q_LORA + MLA_KV_LORA], axis=-1)
    q = (rms_norm(q_lat, q_norm) @ w_q_b).reshape(b, n, MLA_HEADS, MLA_NOPE + MLA_ROPE)
    return q[..., :MLA_NOPE], q[..., MLA_NOPE:], rms_norm(kv_lat, kv_norm), k_rope


def mla_expand(kv_lat, w_kv_b):
    b, n, _ = kv_lat.shape
    kv = (kv_lat @ w_kv_b).reshape(b, n, MLA_HEADS, MLA_NOPE + MLA_V)
    return kv[..., :MLA_NOPE], kv[..., MLA_NOPE:]


def blocked_attention(q_nope, q_rope, k_nope, k_rope, v):
    b, nq, h, _ = q_nope.shape
    nb = nq // Q_BLOCK
    scale = 1.0 / math.sqrt(MLA_NOPE + MLA_ROPE)

    def to_blocks(t):
        return jnp.swapaxes(t.reshape(b, nb, Q_BLOCK, *t.shape[2:]), 0, 1)

    def one_block(qs):
        qn, qr = qs
        s = jnp.einsum('bqhd,bkhd->bhqk', qn, k_nope) + jnp.einsum('bqhd,bkd->bhqk', qr, k_rope)
        p = jax.nn.softmax(s.astype(jnp.float32) * scale, axis=-1).astype(v.dtype)
        return jnp.einsum('bhqk,bkhe->bqhe', p, v)

    out = lax.map(one_block, (to_blocks(q_nope), to_blocks(q_rope)))
    return jnp.swapaxes(out, 0, 1).reshape(b, nq, h, MLA_V)


def mla_context(h, w_in, q_norm, kv_norm, w_q_b, w_kv_b, w_out):
    b, n, _ = h.shape
    q_nope, q_rope, kv_lat, k_rope = mla_project(h, w_in, q_norm, kv_norm, w_q_b)
    k_nope, v = mla_expand(kv_lat, w_kv_b)
    o = blocked_attention(q_nope, q_rope, k_nope, k_rope, v)
    return o.reshape(b, n, MLA_HEADS * MLA_V) @ w_out, kv_lat, k_rope


def mla_latent(h, ckv_ctx, krope_ctx, cos, sin, w_in, q_norm, kv_norm, w_q_b, w_kv_b, w_out):
    b, n, _ = h.shape
    q_nope, q_rope, kv_lat, k_rope = mla_project(h, w_in, q_norm, kv_norm, w_q_b)
    q_rope = apply_axial_rope(q_rope, cos[None, :, None, :], sin[None, :, None, :])
    k_rope = apply_axial_rope(k_rope, cos[None], sin[None])
    k_nope_l, v_l = mla_expand(kv_lat, w_kv_b)
    k_nope_c, v_c = mla_expand(ckv_ctx, w_kv_b)
    k_nope = jnp.concatenate([k_nope_c, k_nope_l], axis=1)
    k_rope_all = jnp.concatenate([krope_ctx, k_rope], axis=1)
    v = jnp.concatenate([v_c, v_l], axis=1)
    o = blocked_attention(q_nope, q_rope, k_nope, k_rope_all, v)
    return o.reshape(b, n, MLA_HEADS * MLA_V) @ w_out


def expert_choice_ffn(h, w_router, w_gate, w_up, w_down):
    b, n, d = h.shape
    cap = EC_CAPACITY_FACTOR * n // N_EXPERTS
    affinity = jax.nn.softmax((h @ w_router).astype(jnp.float32), axis=-1)
    gate, idx = lax.top_k(jnp.swapaxes(affinity, 1, 2), cap)
    xe = jax.vmap(lambda hb, ib: hb[ib])(h, idx)
    a = jnp.einsum('becd,edf->becf', xe, w_gate)
    u = jnp.einsum('becd,edf->becf', xe, w_up)
    ye = jnp.einsum('becf,efd->becd', jax.nn.silu(a) * u, w_down) * gate[..., None].astype(h.dtype)
    return jax.vmap(lambda yb, ib: jnp.zeros((n, d), yb.dtype).at[ib.reshape(-1)].add(yb.reshape(-1, d)))(ye, idx)


def setup_inputs(seed: int = 0) -> dict:
    key = jax.random.key(seed)
    ks = jax.random.split(key, 26)

    def nrm(k, shape, scale):
        return jax.random.normal(k, shape, jnp.float32) * scale

    D = D_MODEL
    gam = 1.0 - 2.0 ** (-5.0 - jnp.arange(RET_HEADS, dtype=jnp.float32))
    decay_base = jnp.log(gam / (1.0 - gam))
    ret_in_width = 2 * RET_QK_WIDTH + 2 * RET_V_WIDTH
    return {
        "x_prompt": nrm(ks[0], (BATCH, SEQ, D), 1.0),
        "x_sample": nrm(ks[1], (DEC_BATCH, DEC_SEQ, D), 1.0),
        "state_ret_fwd": nrm(ks[2], (DEC_BATCH, N_RET_LAYERS, RET_HEADS, RET_DK, RET_DV), 0.5),
        "state_ret_bwd": nrm(ks[3], (DEC_BATCH, N_RET_LAYERS, RET_HEADS, RET_DK, RET_DV), 0.5),
        "cache_mla_ckv": nrm(ks[4], (DEC_BATCH, N_MLA_LAYERS, PAST_LEN, MLA_KV_LORA), 1.0),
        "cache_mla_krope": nrm(ks[5], (DEC_BATCH, N_MLA_LAYERS, PAST_LEN, MLA_ROPE), 1.0),
        "c": nrm(ks[6], (DEC_BATCH, D), 1.0),
        "c_ctx": nrm(ks[7], (D,), 1.0),
        "ada_w": nrm(ks[8], (DEPTH, D, 6 * D), 0.5 * D ** -0.5),
        "ada_b": nrm(ks[9], (DEPTH, 6 * D), 0.02),
        "norm_pre": 1.0 + nrm(ks[10], (DEPTH, 2, D), 0.02),
        "norm_post": 1.0 + nrm(ks[11], (DEPTH, 2, D), 0.02),
        "ret_w_in": nrm(ks[12], (N_RET_LAYERS, D, ret_in_width), D ** -0.5),
        "ret_decay_fwd": decay_base[None, :] + nrm(ks[13], (N_RET_LAYERS, RET_HEADS), 0.1),
        "ret_decay_bwd": decay_base[None, :] + nrm(ks[14], (N_RET_LAYERS, RET_HEADS), 0.1),
        "ret_w_out": nrm(ks[15], (N_RET_LAYERS, RET_V_WIDTH, D), RET_V_WIDTH ** -0.5),
        "mla_w_in": nrm(ks[16], (N_MLA_LAYERS, D, MLA_Q_LORA + MLA_KV_LORA + MLA_ROPE), D ** -0.5),
        "mla_q_norm": 1.0 + nrm(ks[17], (N_MLA_LAYERS, MLA_Q_LORA), 0.02),
        "mla_kv_norm": 1.0 + nrm(ks[18], (N_MLA_LAYERS, MLA_KV_LORA), 0.02),
        "mla_w_q_b": nrm(ks[19], (N_MLA_LAYERS, MLA_Q_LORA, MLA_HEADS * (MLA_NOPE + MLA_ROPE)), MLA_Q_LORA ** -0.5),
        "mla_w_kv_b": nrm(ks[20], (N_MLA_LAYERS, MLA_KV_LORA, MLA_HEADS * (MLA_NOPE + MLA_V)), MLA_KV_LORA ** -0.5),
        "mla_w_out": nrm(ks[21], (N_MLA_LAYERS, MLA_HEADS * MLA_V, D), (MLA_HEADS * MLA_V) ** -0.5),
        "moe_w_router": nrm(ks[22], (DEPTH, D, N_EXPERTS), D ** -0.5),
        "moe_w_gate": nrm(ks[23], (DEPTH, N_EXPERTS, D, EXPERT_FF), D ** -0.5),
        "moe_w_up": nrm(ks[24], (DEPTH, N_EXPERTS, D, EXPERT_FF), D ** -0.5),
        "moe_w_down": nrm(ks[25], (DEPTH, N_EXPERTS, EXPERT_FF, D), EXPERT_FF ** -0.5),
    }


def reference(x_prompt, x_sample, state_ret_fwd, state_ret_bwd, cache_mla_ckv, cache_mla_krope, c, c_ctx,
              ada_w, ada_b, norm_pre, norm_post, ret_w_in, ret_decay_fwd, ret_decay_bwd, ret_w_out,
              mla_w_in, mla_q_norm, mla_kv_norm, mla_w_q_b, mla_w_kv_b, mla_w_out,
              moe_w_router, moe_w_gate, moe_w_up, moe_w_down):
    cos, sin = axial_rope_tables(x_sample.shape[1])
    xp, xs = x_prompt, x_sample
    bp = xp.shape[0]
    new_fwd, new_bwd, new_ckv, new_krope = [], [], [], []
    for i in range(DEPTH):
        sh1_p, sc1_p, g1_p, sh2_p, sc2_p, g2_p = adaln_mod(c_ctx[None, :], ada_w[i], ada_b[i])
        sh1_s, sc1_s, g1_s, sh2_s, sc2_s, g2_s = adaln_mod(c, ada_w[i], ada_b[i])
        hp = rms_norm(xp, norm_pre[i, 0]) * (1.0 + sc1_p) + sh1_p
        hs = rms_norm(xs, norm_pre[i, 0]) * (1.0 + sc1_s) + sh1_s
        if i % N_MIXERS == 0:
            r = i // N_MIXERS
            zero_state = jnp.zeros((bp, RET_HEADS, RET_DK, RET_DV), xp.dtype)
            yp, s_f, s_b = retention_mixer(hp, zero_state, zero_state, ret_w_in[r],
                                           ret_decay_fwd[r], ret_decay_bwd[r], ret_w_out[r])
            ys = retention_mixer(hs, state_ret_fwd[:, r], state_ret_bwd[:, r], ret_w_in[r],
                                 ret_decay_fwd[r], ret_decay_bwd[r], ret_w_out[r])[0]
            new_fwd.append(s_f)
            new_bwd.append(s_b)
        else:
            m = i // N_MIXERS
            yp, ckv, krope = mla_context(hp, mla_w_in[m], mla_q_norm[m], mla_kv_norm[m],
                                         mla_w_q_b[m], mla_w_kv_b[m], mla_w_out[m])
            ys = mla_latent(hs, cache_mla_ckv[:, m], cache_mla_krope[:, m], cos, sin, mla_w_in[m],
                            mla_q_norm[m], mla_kv_norm[m], mla_w_q_b[m], mla_w_kv_b[m], mla_w_out[m])
            new_ckv.append(ckv)
            new_krope.append(krope)
        xp = xp + rms_norm(yp, norm_post[i, 0]) * g1_p
        xs = xs + rms_norm(ys, norm_post[i, 0]) * g1_s
        hp = rms_norm(xp, norm_pre[i, 1]) * (1.0 + sc2_p) + sh2_p
        hs = rms_norm(xs, norm_pre[i, 1]) * (1.0 + sc2_s) + sh2_s
        fp = expert_choice_ffn(hp, moe_w_router[i], moe_w_gate[i], moe_w_up[i], moe_w_down[i])
        fs = expert_choice_ffn(hs, moe_w_router[i], moe_w_gate[i], moe_w_up[i], moe_w_down[i])
        xp = xp + rms_norm(fp, norm_post[i, 1]) * g2_p
        xs = xs + rms_norm(fs, norm_post[i, 1]) * g2_s
    return (xp, xs, jnp.stack(new_fwd, axis=1), jnp.stack(new_bwd, axis=1),
            jnp.stack(new_ckv, axis=1), jnp.stack(new_krope, axis=1))
```

```python
import functools
import math

import jax
import jax.numpy as jnp
from jax import lax
from jax.experimental import pallas as pl
from jax.experimental.pallas import tpu as pltpu

F32 = jnp.float32
BF16 = jnp.bfloat16

D_MODEL = 1024
BATCH = 16
SEQ = 256
DEPTH = 2
DEC_BATCH = 4
DEC_SEQ = 4096
PAST_LEN = 256
GRID_W = 64
RET_HEADS = 4
RET_DK = 256
RET_DV = 512
RET_QK_WIDTH = RET_HEADS * RET_DK
RET_V_WIDTH = RET_HEADS * RET_DV
RET_IN_WIDTH = 2 * RET_QK_WIDTH + 2 * RET_V_WIDTH
RET_CHUNK = 128
MLA_HEADS = 8
MLA_NOPE = 128
MLA_ROPE = 64
MLA_V = 128
MLA_Q_LORA = 384
MLA_KV_LORA = 256
ROPE_BASE = 10000.0
N_EXPERTS = 16
EXPERT_FF = 1024
NORM_EPS = 1e-6

GT = DEC_SEQ
NG = DEC_BATCH + 1
NT = NG * GT
N_SAMPLE = DEC_BATCH * DEC_SEQ
CAP = 2 * GT // N_EXPERTS

LANES = 128
SUBLANES = 8
ROW_TILES = D_MODEL // LANES
TM = 512
TILES_PER_GROUP = GT // TM
VMEM_LIMIT = 56 * 1024 * 1024


def _cparams(sem):
    return pltpu.CompilerParams(dimension_semantics=sem, vmem_limit_bytes=VMEM_LIMIT)


def _rms(x):
    return x * lax.rsqrt(jnp.mean(x * x, axis=-1, keepdims=True) + NORM_EPS)


def _sigmoid(x):
    return 1.0 / (1.0 + jnp.exp(-x))


def _adaln_kernel(c_ref, w_ref, b_ref, o_ref):
    c = c_ref[...]
    s = (c * _sigmoid(c)).astype(BF16)
    o_ref[...] = jnp.dot(s, w_ref[...].astype(BF16), preferred_element_type=F32) + b_ref[...]


def _adaln(cond8, w, b):
    tn = 1536
    return pl.pallas_call(
        _adaln_kernel,
        grid=(6 * D_MODEL // tn,),
        in_specs=[pl.BlockSpec((8, D_MODEL), lambda j: (0, 0)),
                  pl.BlockSpec((D_MODEL, tn), lambda j: (0, j)),
                  pl.BlockSpec((1, tn), lambda j: (0, j))],
        out_specs=pl.BlockSpec((8, tn), lambda j: (0, j)),
        out_shape=jax.ShapeDtypeStruct((8, 6 * D_MODEL), F32),
        compiler_params=_cparams(("arbitrary",)),
        name="adaln",
    )(cond8, w, b.reshape(1, -1))


def _nmm_kernel(x_ref, mod_ref, gain_ref, w_ref, o_ref, h_scr):
    @pl.when(pl.program_id(1) == 0)
    def _():
        m = mod_ref[0]
        h = _rms(x_ref[...]) * gain_ref[...] * (1.0 + m[1:2]) + m[0:1]
        h_scr[...] = h.astype(BF16)

    o_ref[...] = jnp.dot(h_scr[...], w_ref[...], preferred_element_type=F32).astype(o_ref.dtype)


def _norm_mod_matmul(x, mods, gain, w, tn):
    n_out = w.shape[1]
    return pl.pallas_call(
        _nmm_kernel,
        grid=(NT // TM, n_out // tn),
        in_specs=[pl.BlockSpec((TM, D_MODEL), lambda i, j: (i, 0)),
                  pl.BlockSpec((1, 6, D_MODEL), lambda i, j: (i // TILES_PER_GROUP, 0, 0)),
                  pl.BlockSpec((1, D_MODEL), lambda i, j: (0, 0)),
                  pl.BlockSpec((D_MODEL, tn), lambda i, j: (0, j))],
        out_specs=pl.BlockSpec((TM, tn), lambda i, j: (i, j)),
        out_shape=jax.ShapeDtypeStruct((NT, n_out), BF16),
        scratch_shapes=[pltpu.VMEM((TM, D_MODEL), BF16)],
        compiler_params=_cparams(("parallel", "arbitrary")),
        name="norm_mod_matmul",
    )(x, mods, gain, w)


def _ret_kernel(lg_ref, q_ref, k_ref, v_ref, g_ref, *rest, seq_len, has_init):
    if has_init:
        s0f_ref, s0b_ref, _, yg_ref, sf_ref, sb_ref, ycross, sf_scr, sb_scr = rest
    else:
        _, yg_ref, sf_ref, sb_ref, ycross, sf_scr, sb_scr = rest
    C = RET_CHUNK
    nc = seq_len // C
    head = pl.program_id(1)
    lgf = lg_ref[0, head]
    lgb = lg_ref[1, head]
    k_scale = RET_DK ** -0.5

    ii = lax.broadcasted_iota(jnp.int32, (C, C), 0).astype(F32)
    jj = lax.broadcasted_iota(jnp.int32, (C, C), 1).astype(F32)
    diff = ii - jj
    inner = jnp.where(diff >= 0, jnp.exp(diff * lgf), jnp.exp(-diff * lgb)) * k_scale
    pos = lax.broadcasted_iota(jnp.int32, (C, 1), 0).astype(F32)
    cross_f = jnp.exp((pos + 1.0) * lgf)
    cross_b = jnp.exp((C - pos) * lgb)
    sdec_f = jnp.exp((C - 1.0 - pos) * lgf) * k_scale
    sdec_b = jnp.exp(pos * lgb) * k_scale
    one = jnp.ones((1, 1), F32)
    cdec_f = jnp.exp(one * (C * lgf))
    cdec_b = jnp.exp(one * (C * lgb))

    if has_init:
        sf_scr[...] = s0f_ref[0, 0]
        sb_scr[...] = s0b_ref[0, 0]
    else:
        sf_scr[...] = jnp.zeros_like(sf_scr)
        sb_scr[...] = jnp.zeros_like(sb_scr)

    contract0 = (((0,), (0,)), ((), ()))
    contract1 = (((1,), (1,)), ((), ()))

    def bwd(t, carry):
        rows = pl.ds(pl.multiple_of((nc - 1 - t) * C, C), C)
        q = q_ref[rows, :]
        s = sb_scr[...]
        ycross[rows, :] = jnp.dot(q, s.astype(BF16), preferred_element_type=F32) * cross_b
        kd = (k_ref[rows, :].astype(F32) * sdec_b).astype(BF16)
        sb_scr[...] = s * cdec_b + lax.dot_general(kd, v_ref[rows, :], contract0,
                                                   preferred_element_type=F32)
        return carry

    lax.fori_loop(0, nc, bwd, 0)

    def fwd(c, carry):
        rows = pl.ds(pl.multiple_of(c * C, C), C)
        q = q_ref[rows, :]
        k = k_ref[rows, :]
        v = v_ref[rows, :]
        s = sf_scr[...]
        scores = lax.dot_general(q, k, contract1, preferred_element_type=F32) * inner
        y = (jnp.dot(scores.astype(BF16), v, preferred_element_type=F32)
             + jnp.dot(q, s.astype(BF16), preferred_element_type=F32) * cross_f
             + ycross[rows, :])
        mu = jnp.mean(y, axis=-1, keepdims=True)
        yc = y - mu
        yn = yc * lax.rsqrt(jnp.mean(yc * yc, axis=-1, keepdims=True) + NORM_EPS)
        g = g_ref[rows, :].astype(F32)
        yg_ref[rows, :] = (g * _sigmoid(g) * yn).astype(yg_ref.dtype)
        kd = (k.astype(F32) * sdec_f).astype(BF16)
        sf_scr[...] = s * cdec_f + lax.dot_general(kd, v, contract0, preferred_element_type=F32)
        return carry

    lax.fori_loop(0, nc, fwd, 0)
    sf_ref[0, 0] = sf_scr[...]
    sb_ref[0, 0] = sb_scr[...]


def _retention(qkvg, lg, s0f, s0b, yg_prev, *, n_req, seq_len, row0):
    rb0 = row0 // seq_len
    has_init = s0f is not None
    qk_spec = lambda off: pl.BlockSpec((seq_len, RET_DK), lambda b, h, lg_: (rb0 + b, off + h))
    vg_spec = lambda off: pl.BlockSpec((seq_len, RET_DV), lambda b, h, lg_: (rb0 + b, off + h))
    st_spec = pl.BlockSpec((1, 1, RET_DK, RET_DV), lambda b, h, lg_: (b, h, 0, 0))
    in_specs = [qk_spec(0), qk_spec(RET_HEADS), vg_spec(RET_HEADS), vg_spec(2 * RET_HEADS)]
    args = [qkvg, qkvg, qkvg, qkvg]
    if has_init:
        in_specs += [st_spec, st_spec]
        args += [s0f, s0b]
    in_specs.append(pl.BlockSpec(memory_space=pl.ANY))
    args.append(yg_prev)
    st_shape = jax.ShapeDtypeStruct((n_req, RET_HEADS, RET_DK, RET_DV), F32)
    return pl.pallas_call(
        functools.partial(_ret_kernel, seq_len=seq_len, has_init=has_init),
        grid_spec=pltpu.PrefetchScalarGridSpec(
            num_scalar_prefetch=1,
            grid=(n_req, RET_HEADS),
            in_specs=in_specs,
            out_specs=[pl.BlockSpec((seq_len, RET_DV), lambda b, h, lg_: (rb0 + b, h)), st_spec, st_spec],
            scratch_shapes=[pltpu.VMEM((seq_len, RET_DV), F32),
                            pltpu.VMEM((RET_DK, RET_DV), F32),
                            pltpu.VMEM((RET_DK, RET_DV), F32)]),
        out_shape=[jax.ShapeDtypeStruct((NT, RET_V_WIDTH), BF16), st_shape, st_shape],
        input_output_aliases={len(args): 0},
        compiler_params=_cparams(("parallel", "arbitrary")),
        name=f"retention_{seq_len}",
    )(lg, *args)


def _out_post_kernel(y_ref, w_ref, x_ref, mod_ref, gpost_ref, gpre_ref, wr_ref,
                     x1_ref, h3_ref, aff_ref):
    m = mod_ref[0]
    y = jnp.dot(y_ref[...], w_ref[...], preferred_element_type=F32)
    x1 = x_ref[...] + _rms(y) * gpost_ref[...] * m[2:3]
    x1_ref[...] = x1
    h = _rms(x1) * gpre_ref[...] * (1.0 + m[4:5]) + m[3:4]
    for c in range(ROW_TILES):
        h3_ref[pl.ds(c, TM, stride=ROW_TILES), :] = h[:, c * LANES:(c + 1) * LANES]
    logits = jnp.dot(h, wr_ref[...], preferred_element_type=F32, precision=lax.Precision.HIGHEST)
    e = jnp.exp(logits - jnp.max(logits, axis=-1, keepdims=True))
    aff_ref[...] = e / jnp.sum(e, axis=-1, keepdims=True)


def _out_post(y, w, x, mods, gpost, gpre, w_router):
    kdim = y.shape[1]
    return pl.pallas_call(
        _out_post_kernel,
        grid=(NT // TM,),
        in_specs=[pl.BlockSpec((TM, kdim), lambda i: (i, 0)),
                  pl.BlockSpec((kdim, D_MODEL), lambda i: (0, 0)),
                  pl.BlockSpec((TM, D_MODEL), lambda i: (i, 0)),
                  pl.BlockSpec((1, 6, D_MODEL), lambda i: (i // TILES_PER_GROUP, 0, 0)),
                  pl.BlockSpec((1, D_MODEL), lambda i: (0, 0)),
                  pl.BlockSpec((1, D_MODEL), lambda i: (0, 0)),
                  pl.BlockSpec((D_MODEL, N_EXPERTS), lambda i: (0, 0))],
        out_specs=[pl.BlockSpec((TM, D_MODEL), lambda i: (i, 0)),
                   pl.BlockSpec((TM * ROW_TILES, LANES), lambda i: (i, 0)),
                   pl.BlockSpec((TM, N_EXPERTS), lambda i: (i, 0))],
        out_shape=[jax.ShapeDtypeStruct((NT, D_MODEL), F32),
                   jax.ShapeDtypeStruct((NT * ROW_TILES, LANES), F32),
                   jax.ShapeDtypeStruct((NT, N_EXPERTS), F32)],
        compiler_params=_cparams(("parallel",)),
        name="out_post",
    )(y, w, x, mods, gpost, gpre, w_router)


FF_SPLIT = 2
FF_TILE = EXPERT_FF // FF_SPLIT
GATHER_UNROLL = 8


def _moe_kernel(idx_ref, gate_ref, h3_ref, wg_ref, wu_ref, wd_ref, f3_ref, xg, x_scr, yacc, y3):
    e = pl.program_id(1)
    f = pl.program_id(2)

    @pl.when(jnp.logical_and(e == 0, f == 0))
    def _():
        f3_ref[...] = jnp.zeros_like(f3_ref)

    @pl.when(f == 0)
    def _():
        def gather(i, carry):
            for j in range(GATHER_UNROLL):
                r = i * GATHER_UNROLL + j
                t = idx_ref[0, 0, r]
                xg[pl.ds(pl.multiple_of(r * SUBLANES, SUBLANES), SUBLANES), :] = (
                    h3_ref[pl.ds(pl.multiple_of(t * SUBLANES, SUBLANES), SUBLANES), :])
            return carry

        lax.fori_loop(0, CAP // GATHER_UNROLL, gather, 0)
        x_scr[...] = jnp.concatenate(
            [xg[pl.ds(c, CAP, stride=ROW_TILES), :] for c in range(ROW_TILES)], axis=1).astype(BF16)
        yacc[...] = jnp.zeros_like(yacc)

    x = x_scr[...]
    a = jnp.dot(x, wg_ref[0], preferred_element_type=F32)
    u = jnp.dot(x, wu_ref[0], preferred_element_type=F32)
    act = (a * _sigmoid(a) * u).astype(BF16)
    yacc[...] += jnp.dot(act, wd_ref[0], preferred_element_type=F32)

    @pl.when(f == FF_SPLIT - 1)
    def _():
        y = yacc[...] * gate_ref[0]
        for c in range(ROW_TILES):
            y3[pl.ds(c, CAP, stride=ROW_TILES), :] = y[:, c * LANES:(c + 1) * LANES]

        def scatter(i, carry):
            dst, vals = [], []
            for j in range(GATHER_UNROLL):
                r = i * GATHER_UNROLL + j
                t = idx_ref[0, 0, r]
                d = pl.ds(pl.multiple_of(t * SUBLANES, SUBLANES), SUBLANES)
                dst.append(d)
                vals.append(f3_ref[d, :] + y3[pl.ds(pl.multiple_of(r * SUBLANES, SUBLANES), SUBLANES), :])
            for d, v in zip(dst, vals):
                f3_ref[d, :] = v
            return carry

        lax.fori_loop(0, CAP // GATHER_UNROLL, scatter, 0)


def _moe(idx, gate, h3, wg, wu, wd):
    grp_rows = GT * ROW_TILES
    return pl.pallas_call(
        _moe_kernel,
        grid=(NG, N_EXPERTS, FF_SPLIT),
        in_specs=[pl.BlockSpec((1, 1, CAP), lambda b, e, f: (b * N_EXPERTS + e, 0, 0),
                               memory_space=pltpu.SMEM),
                  pl.BlockSpec((1, CAP, 1), lambda b, e, f: (b * N_EXPERTS + e, 0, 0)),
                  pl.BlockSpec((grp_rows, LANES), lambda b, e, f: (b, 0), pipeline_mode=pl.Buffered(1)),
                  pl.BlockSpec((1, D_MODEL, FF_TILE), lambda b, e, f: (e, 0, f)),
                  pl.BlockSpec((1, D_MODEL, FF_TILE), lambda b, e, f: (e, 0, f)),
                  pl.BlockSpec((1, FF_TILE, D_MODEL), lambda b, e, f: (e, f, 0))],
        out_specs=pl.BlockSpec((grp_rows, LANES), lambda b, e, f: (b, 0), pipeline_mode=pl.Buffered(1)),
        out_shape=jax.ShapeDtypeStruct((NT * ROW_TILES, LANES), F32),
        scratch_shapes=[pltpu.VMEM((CAP * ROW_TILES, LANES), F32),
                        pltpu.VMEM((CAP, D_MODEL), BF16),
                        pltpu.VMEM((CAP, D_MODEL), F32),
                        pltpu.VMEM((CAP * ROW_TILES, LANES), F32)],
        compiler_params=_cparams(("arbitrary", "arbitrary", "arbitrary")),
        name="moe_experts",
    )(idx, gate, h3, wg, wu, wd)


def _moe_post_kernel(f3_ref, x_ref, mod_ref, gpost_ref, o_ref):
    m = mod_ref[0]
    fx = jnp.concatenate([f3_ref[pl.ds(c, TM, stride=ROW_TILES), :] for c in range(ROW_TILES)], axis=1)
    o_ref[...] = x_ref[...] + _rms(fx) * gpost_ref[...] * m[5:6]


def _moe_post(f3, x, mods, gpost):
    return pl.pallas_call(
        _moe_post_kernel,
        grid=(NT // TM,),
        in_specs=[pl.BlockSpec((TM * ROW_TILES, LANES), lambda i: (i, 0)),
                  pl.BlockSpec((TM, D_MODEL), lambda i: (i, 0)),
                  pl.BlockSpec((1, 6, D_MODEL), lambda i: (i // TILES_PER_GROUP, 0, 0)),
                  pl.BlockSpec((1, D_MODEL), lambda i: (0, 0))],
        out_specs=pl.BlockSpec((TM, D_MODEL), lambda i: (i, 0)),
        out_shape=jax.ShapeDtypeStruct((NT, D_MODEL), F32),
        compiler_params=_cparams(("parallel",)),
        name="moe_post",
    )(f3, x, mods, gpost)


def _route(aff):
    aff_s = aff[:N_SAMPLE].reshape(DEC_BATCH, DEC_SEQ, N_EXPERTS)
    aff_p = aff[N_SAMPLE:].reshape(BATCH, SEQ, N_EXPERTS)
    gate_s, idx_s = lax.top_k(jnp.swapaxes(aff_s, 1, 2), 2 * DEC_SEQ // N_EXPERTS)
    gate_p, idx_p = lax.top_k(jnp.swapaxes(aff_p, 1, 2), 2 * SEQ // N_EXPERTS)
    idx_p = idx_p + (jnp.arange(BATCH, dtype=jnp.int32) * SEQ)[:, None, None]
    idx_p = jnp.swapaxes(idx_p, 0, 1).reshape(1, N_EXPERTS, CAP)
    gate_p = jnp.swapaxes(gate_p, 0, 1).reshape(1, N_EXPERTS, CAP)
    idx = jnp.concatenate([idx_s, idx_p], axis=0).astype(jnp.int32)
    gate = jnp.concatenate([gate_s, gate_p], axis=0)
    return idx.reshape(NG * N_EXPERTS, 1, CAP), gate.reshape(NG * N_EXPERTS, CAP, 1)


MLA_IN_EXT = MLA_Q_LORA + MLA_KV_LORA + 2 * LANES
HD = MLA_HEADS * LANES


def _mla_proj_kernel(x_ref, mod_ref, gain_ref, win_ref, qn_g_ref, kvn_g_ref, wq_ref, wkv_ref,
                     cos_ref, sin_ref, qn_ref, qr_ref, kn_ref, v_ref, krp_ref, ckv_ref, kraw_ref):
    m = mod_ref[0]
    h = (_rms(x_ref[...]) * gain_ref[...] * (1.0 + m[1:2]) + m[0:1]).astype(BF16)
    lat = jnp.dot(h, win_ref[...], preferred_element_type=F32)
    q_lat = lat[:, :MLA_Q_LORA]
    kv_lat = lat[:, MLA_Q_LORA:MLA_Q_LORA + MLA_KV_LORA]
    kr = lat[:, MLA_Q_LORA + MLA_KV_LORA:MLA_Q_LORA + MLA_KV_LORA + LANES]
    kr_rot = lat[:, MLA_Q_LORA + MLA_KV_LORA + LANES:]
    is_latent = pl.program_id(0) < N_SAMPLE // TM
    cos = jnp.where(is_latent, cos_ref[...], 1.0)
    sin = jnp.where(is_latent, sin_ref[...], 0.0)

    qln = (_rms(q_lat) * qn_g_ref[...]).astype(BF16)
    q = jnp.dot(qln, wq_ref[...], preferred_element_type=F32)
    qn_ref[...] = q[:, :HD].astype(BF16)
    for hd in range(MLA_HEADS):
        lo = HD + hd * LANES
        qr_ref[:, hd * LANES:(hd + 1) * LANES] = (
            q[:, lo:lo + LANES] * cos + q[:, HD + lo:HD + lo + LANES] * sin).astype(BF16)

    ckv = _rms(kv_lat) * kvn_g_ref[...]
    ckv_ref[...] = ckv
    kv = jnp.dot(ckv.astype(BF16), wkv_ref[...], preferred_element_type=F32)
    kn_ref[...] = kv[:, :HD].astype(BF16)
    v_ref[...] = kv[:, HD:].astype(BF16)
    kraw_ref[...] = kr
    krp_ref[...] = (kr * cos + kr_rot * sin).astype(BF16)


def _mla_proj(x, mods, gain, win_ext, qn_g, kvn_g, wq_ext, wkv_ext, cos128, sin128):
    row = lambda w: pl.BlockSpec((TM, w), lambda i: (i, 0))
    full = lambda a: pl.BlockSpec(a.shape, lambda i: (0,) * a.ndim)
    tab = pl.BlockSpec((TM, LANES), lambda i: (i % TILES_PER_GROUP, 0))
    return pl.pallas_call(
        _mla_proj_kernel,
        grid=(NT // TM,),
        in_specs=[row(D_MODEL),
                  pl.BlockSpec((1, 6, D_MODEL), lambda i: (i // TILES_PER_GROUP, 0, 0)),
                  full(gain), full(win_ext), full(qn_g), full(kvn_g), full(wq_ext), full(wkv_ext),
                  tab, tab],
        out_specs=[row(HD), row(HD), row(HD), row(HD), row(LANES), row(MLA_KV_LORA), row(LANES)],
        out_shape=[jax.ShapeDtypeStruct((NT, HD), BF16),
                   jax.ShapeDtypeStruct((NT, HD), BF16),
                   jax.ShapeDtypeStruct((NT, HD), BF16),
                   jax.ShapeDtypeStruct((NT, HD), BF16),
                   jax.ShapeDtypeStruct((NT, LANES), BF16),
                   jax.ShapeDtypeStruct((NT, MLA_KV_LORA), F32),
                   jax.ShapeDtypeStruct((NT, LANES), F32)],
        compiler_params=_cparams(("parallel",)),
        name="mla_proj",
    )(x, mods, gain, win_ext, qn_g, kvn_g, wq_ext, wkv_ext, cos128, sin128)


def _matmul_kernel(a_ref, w_ref, o_ref):
    o_ref[...] = jnp.dot(a_ref[...].astype(BF16), w_ref[...],
                         preferred_element_type=F32).astype(o_ref.dtype)


def _ctx_expand(ckv_ctx, wkv_ext):
    n = ckv_ctx.shape[0]
    return pl.pallas_call(
        _matmul_kernel,
        grid=(n // TM,),
        in_specs=[pl.BlockSpec((TM, MLA_KV_LORA), lambda i: (i, 0)),
                  pl.BlockSpec(wkv_ext.shape, lambda i: (0, 0))],
        out_specs=pl.BlockSpec((TM, 2 * HD), lambda i: (i, 0)),
        out_shape=jax.ShapeDtypeStruct((n, 2 * HD), BF16),
        compiler_params=_cparams(("parallel",)),
        name="ctx_expand",
    )(ckv_ctx, wkv_ext)


def _attn_kernel(qn_ref, qr_ref, kn_ref, krp_ref, v_ref, *rest, tk, n_chunks, has_ctx):
    if has_ctx:
        knc_ref, krpc_ref, vc_ref, _, o_ref = rest
    else:
        _, o_ref = rest
    scale = 1.0 / math.sqrt(MLA_NOPE + MLA_ROPE)
    q = jnp.concatenate([qn_ref[...], qr_ref[...]], axis=1)
    tq = q.shape[0]
    contract1 = (((1,), (1,)), ((), ()))

    def step(kn, krp, v, carry):
        m, l, acc = carry
        kc = jnp.concatenate([kn, krp], axis=1)
        s = lax.dot_general(q, kc, contract1, preferred_element_type=F32) * scale
        m_new = jnp.maximum(m, jnp.max(s, axis=-1, keepdims=True))
        alpha = jnp.exp(m - m_new)
        p = jnp.exp(s - m_new)
        l = alpha * l + jnp.sum(p, axis=-1, keepdims=True)
        acc = alpha * acc + jnp.dot(p.astype(BF16), v, preferred_element_type=F32)
        return m_new, l, acc

    carry = (jnp.full((tq, 1), -jnp.inf, F32), jnp.zeros((tq, 1), F32), jnp.zeros((tq, MLA_V), F32))
    if has_ctx:
        carry = step(knc_ref[0], krpc_ref[0], vc_ref[0], carry)

    def body(c, carry):
        rows = pl.ds(pl.multiple_of(c * tk, tk), tk)
        return step(kn_ref[rows, :], krp_ref[rows, :], v_ref[rows, :], carry)

    m, l, acc = lax.fori_loop(0, n_chunks, body, carry)
    o_ref[...] = (acc / l).astype(o_ref.dtype)


def _attention(qn, qr, kn, krp, v, ctx, o_prev, *, n_req, seq_len, row0, tq, tk):
    rb0 = row0 // seq_len
    qb0 = row0 // tq
    nq = seq_len // tq
    q_spec = pl.BlockSpec((tq, LANES), lambda b, h, i: (qb0 + b * nq + i, h))
    k_spec = pl.BlockSpec((seq_len, LANES), lambda b, h, i: (rb0 + b, h))
    in_specs = [q_spec, q_spec, k_spec,
                pl.BlockSpec((seq_len, LANES), lambda b, h, i: (rb0 + b, 0)), k_spec]
    args = [qn, qr, kn, krp, v]
    if ctx is not None:
        c_spec = pl.BlockSpec((1, PAST_LEN, LANES), lambda b, h, i: (b, 0, h))
        in_specs += [c_spec, pl.BlockSpec((1, PAST_LEN, LANES), lambda b, h, i: (b, 0, 0)), c_spec]
        args += list(ctx)
    in_specs.append(pl.BlockSpec(memory_space=pl.ANY))
    args.append(o_prev)
    return pl.pallas_call(
        functools.partial(_attn_kernel, tk=tk, n_chunks=seq_len // tk, has_ctx=ctx is not None),
        grid=(n_req, MLA_HEADS, nq),
        in_specs=in_specs,
        out_specs=q_spec,
        out_shape=jax.ShapeDtypeStruct((NT, HD), BF16),
        input_output_aliases={len(args) - 1: 0},
        compiler_params=_cparams(("parallel", "parallel", "arbitrary")),
        name=f"attention_{seq_len}",
    )(*args)


def _rope_rot_cols(w):
    w1, w2, w3, w4 = jnp.split(w, 4, axis=-1)
    return jnp.concatenate([-w2, w1, -w4, w3], axis=-1)


def _pad_cols(w, width):
    return jnp.pad(w, ((0, 0), (0, width - w.shape[1])))


def _mla_weights(w_in, w_q_b, w_kv_b):
    w_rope = w_in[:, MLA_Q_LORA + MLA_KV_LORA:]
    win_ext = jnp.concatenate([w_in[:, :MLA_Q_LORA + MLA_KV_LORA], _pad_cols(w_rope, LANES),
                               _pad_cols(_rope_rot_cols(w_rope), LANES)], axis=1).astype(BF16)
    wq = w_q_b.reshape(MLA_Q_LORA, MLA_HEADS, MLA_NOPE + MLA_ROPE)
    wq_nope = wq[:, :, :MLA_NOPE].reshape(MLA_Q_LORA, HD)
    wq_rope = wq[:, :, MLA_NOPE:]
    pad = ((0, 0), (0, 0), (0, LANES - MLA_ROPE))
    wq_rope_p = jnp.pad(wq_rope, pad).reshape(MLA_Q_LORA, HD)
    wq_rot_p = jnp.pad(_rope_rot_cols(wq_rope), pad).reshape(MLA_Q_LORA, HD)
    wq_ext = jnp.concatenate([wq_nope, wq_rope_p, wq_rot_p], axis=1).astype(BF16)
    wkv = w_kv_b.reshape(MLA_KV_LORA, MLA_HEADS, MLA_NOPE + MLA_V)
    wkv_ext = jnp.concatenate([wkv[:, :, :MLA_NOPE].reshape(MLA_KV_LORA, HD),
                               wkv[:, :, MLA_NOPE:].reshape(MLA_KV_LORA, HD)], axis=1).astype(BF16)
    return win_ext, wq_ext, wkv_ext


def _rope_tables():
    rows = DEC_SEQ // GRID_W
    row = jnp.repeat(jnp.arange(rows, dtype=F32), GRID_W)
    col = jnp.tile(jnp.arange(GRID_W, dtype=F32), rows)
    half = MLA_ROPE // 2
    inv = 1.0 / (ROPE_BASE ** (jnp.arange(0, half, 2, dtype=F32) / half))
    ar = row[:, None] * inv
    ac = col[:, None] * inv
    ang = jnp.concatenate([ar, ar, ac, ac] * 2, axis=-1)
    return jnp.cos(ang), jnp.sin(ang)


def kernel(x_prompt, x_sample, state_ret_fwd, state_ret_bwd, cache_mla_ckv, cache_mla_krope, c, c_ctx,
           ada_w, ada_b, norm_pre, norm_post, ret_w_in, ret_decay_fwd, ret_decay_bwd, ret_w_out,
           mla_w_in, mla_q_norm, mla_kv_norm, mla_w_q_b, mla_w_kv_b, mla_w_out,
           moe_w_router, moe_w_gate, moe_w_up, moe_w_down):
    x = jnp.concatenate([x_sample.reshape(N_SAMPLE, D_MODEL), x_prompt.reshape(BATCH * SEQ, D_MODEL)], axis=0)
    cond8 = jnp.concatenate([c, c_ctx[None, :], jnp.zeros((8 - NG, D_MODEL), F32)], axis=0)
    cos128, sin128 = _rope_tables()
    outs = {}

    for i in range(DEPTH):
        mods = _adaln(cond8, ada_w[i], ada_b[i])[:NG].reshape(NG, 6, D_MODEL)
        gpre1 = norm_pre[i, 0][None, :]
        gpre2 = norm_pre[i, 1][None, :]
        gpost1 = norm_post[i, 0][None, :]
        gpost2 = norm_post[i, 1][None, :]
        if i % 2 == 0:
            r = i // 2
            qkvg = _norm_mod_matmul(x, mods, gpre1, ret_w_in[r].astype(BF16), 1536)
            lg = jnp.stack([jax.nn.log_sigmoid(ret_decay_fwd[r].astype(F32)),
                            jax.nn.log_sigmoid(ret_decay_bwd[r].astype(F32))])
            yg0 = jnp.zeros((NT, RET_V_WIDTH), BF16)
            yg, _, _ = _retention(qkvg, lg, state_ret_fwd[:, r], state_ret_bwd[:, r], yg0,
                                  n_req=DEC_BATCH, seq_len=DEC_SEQ, row0=0)
            yg, s_f, s_b = _retention(qkvg, lg, None, None, yg,
                                      n_req=BATCH, seq_len=SEQ, row0=N_SAMPLE)
            outs["fwd"], outs["bwd"] = s_f[:, None], s_b[:, None]
            mix, w_out = yg, ret_w_out[r].astype(BF16)
        else:
            mi = i // 2
            win_ext, wq_ext, wkv_ext = _mla_weights(mla_w_in[mi], mla_w_q_b[mi], mla_w_kv_b[mi])
            qn, qr, kn, v, krp, ckv, kraw = _mla_proj(
                x, mods, gpre1, win_ext, mla_q_norm[mi][None, :], mla_kv_norm[mi][None, :],
                wq_ext, wkv_ext, cos128, sin128)
            outs["ckv"] = ckv[N_SAMPLE:].reshape(BATCH, 1, SEQ, MLA_KV_LORA)
            outs["krope"] = kraw[N_SAMPLE:, :MLA_ROPE].reshape(BATCH, 1, SEQ, MLA_ROPE)
            kvc = _ctx_expand(cache_mla_ckv[:, mi].reshape(DEC_BATCH * PAST_LEN, MLA_KV_LORA), wkv_ext)
            kvc = kvc.reshape(DEC_BATCH, PAST_LEN, 2 * HD)
            krpc = jnp.pad(cache_mla_krope[:, mi], ((0, 0), (0, 0), (0, LANES - MLA_ROPE))).astype(BF16)
            o0 = jnp.zeros((NT, HD), BF16)
            o = _attention(qn, qr, kn, krp, v, (kvc[:, :, :HD], krpc, kvc[:, :, HD:]), o0,
                           n_req=DEC_BATCH, seq_len=DEC_SEQ, row0=0, tq=512, tk=512)
            o = _attention(qn, qr, kn, krp, v, None, o,
                           n_req=BATCH, seq_len=SEQ, row0=N_SAMPLE, tq=SEQ, tk=SEQ)
            mix, w_out = o, mla_w_out[mi].astype(BF16)
        x1, h3, aff = _out_post(mix, w_out, x, mods, gpost1, gpre2, moe_w_router[i])
        idx, gate = _route(aff)
        f3 = _moe(idx, gate, h3, moe_w_gate[i].astype(BF16), moe_w_up[i].astype(BF16),
                  moe_w_down[i].astype(BF16))
        x = _moe_post(f3, x1, mods, gpost2)

    y_sample = x[:N_SAMPLE].reshape(DEC_BATCH, DEC_SEQ, D_MODEL)
    y_prompt = x[N_SAMPLE:].reshape(BATCH, SEQ, D_MODEL)
    return (y_prompt, y_sample, outs["fwd"], outs["bwd"], outs["ckv"], outs["krope"])
```

```python
import functools
import math

import jax
import jax.numpy as jnp
from jax import lax
from jax.experimental import pallas as pl
from jax.experimental.pallas import tpu as pltpu

F32 = jnp.float32
BF16 = jnp.bfloat16

D_MODEL = 1024
BATCH = 16
SEQ = 256
DEPTH = 2
DEC_BATCH = 4
DEC_SEQ = 4096
PAST_LEN = 256
GRID_W = 64
RET_HEADS = 4
RET_DK = 256
RET_DV = 512
RET_QK_WIDTH = RET_HEADS * RET_DK
RET_V_WIDTH = RET_HEADS * RET_DV
RET_CHUNK = 128
MLA_HEADS = 8
MLA_NOPE = 128
MLA_ROPE = 64
MLA_V = 128
MLA_Q_LORA = 384
MLA_KV_LORA = 256
ROPE_BASE = 10000.0
N_EXPERTS = 16
EXPERT_FF = 1024
NORM_EPS = 1e-6

GT = DEC_SEQ
NG = DEC_BATCH + 1
NT = NG * GT
N_SAMPLE = DEC_BATCH * DEC_SEQ
N_PROMPT = BATCH * SEQ
CAP = 2 * GT // N_EXPERTS

LANES = 128
SUBLANES = 8
ROW_TILES = D_MODEL // LANES
TM = 512
TILES_PER_GROUP = GT // TM
SAMPLE_TILES = N_SAMPLE // TM
VMEM_LIMIT = 56 * 1024 * 1024


def _cparams(sem):
    return pltpu.CompilerParams(dimension_semantics=sem, vmem_limit_bytes=VMEM_LIMIT)


def _rms(x):
    return x * lax.rsqrt(jnp.mean(x * x, axis=-1, keepdims=True) + NORM_EPS)


def _sigmoid(x):
    return 1.0 / (1.0 + jnp.exp(-x))


def _part_specs(width, joint):
    lo = pl.BlockSpec((TM, width), lambda *g: (jnp.minimum(g[-1], SAMPLE_TILES - 1), 0))
    if joint:
        hi = pl.BlockSpec((TM, width), lambda *g: (jnp.maximum(g[-1], SAMPLE_TILES), 0))
    else:
        hi = pl.BlockSpec((TM, width), lambda *g: (jnp.maximum(g[-1] - SAMPLE_TILES, 0), 0))
    return [lo, hi]


def _load_rows(lo_ref, hi_ref, i):
    return jnp.where(i < SAMPLE_TILES, lo_ref[...], hi_ref[...])


def _mod_spec():
    return pl.BlockSpec((1, 6, D_MODEL), lambda *g: (g[-1] // TILES_PER_GROUP, 0, 0))


def _const_spec(shape):
    return pl.BlockSpec(shape, lambda *g: (0,) * len(shape))


def _adaln_kernel(c_ref, w_ref, b_ref, o_ref):
    c = c_ref[...]
    s = (c * _sigmoid(c)).astype(BF16)
    o_ref[...] = jnp.dot(s, w_ref[0].astype(BF16), preferred_element_type=F32) + b_ref[0]


def _adaln(cond8, w, b, layer):
    tn = 1536
    return pl.pallas_call(
        _adaln_kernel,
        grid=(6 * D_MODEL // tn,),
        in_specs=[pl.BlockSpec((8, D_MODEL), lambda j: (0, 0)),
                  pl.BlockSpec((1, D_MODEL, tn), lambda j: (layer, 0, j)),
                  pl.BlockSpec((1, 1, tn), lambda j: (layer, 0, j))],
        out_specs=pl.BlockSpec((8, tn), lambda j: (0, j)),
        out_shape=jax.ShapeDtypeStruct((8, 6 * D_MODEL), F32),
        compiler_params=_cparams(("arbitrary",)),
        name="adaln",
    )(cond8, w, b)


def _nmm_kernel(xlo_ref, xhi_ref, mod_ref, gain_ref, w_ref, o_ref, w_scr):
    i = pl.program_id(1)

    @pl.when(i == 0)
    def _():
        w_scr[...] = w_ref[...].astype(BF16)

    m = mod_ref[0]
    h = _rms(_load_rows(xlo_ref, xhi_ref, i)) * gain_ref[...] * (1.0 + m[1:2]) + m[0:1]
    o_ref[...] = jnp.dot(h.astype(BF16), w_scr[...], preferred_element_type=F32).astype(o_ref.dtype)


def _norm_mod_matmul(x_parts, joint, mods, gain, w, tn):
    n_out = w.shape[1]
    return pl.pallas_call(
        _nmm_kernel,
        grid=(n_out // tn, NT // TM),
        in_specs=[*_part_specs(D_MODEL, joint), _mod_spec(), _const_spec((1, D_MODEL)),
                  pl.BlockSpec((D_MODEL, tn), lambda j, i: (0, j))],
        out_specs=pl.BlockSpec((TM, tn), lambda j, i: (i, j)),
        out_shape=jax.ShapeDtypeStruct((NT, n_out), BF16),
        scratch_shapes=[pltpu.VMEM((D_MODEL, tn), BF16)],
        compiler_params=_cparams(("arbitrary", "arbitrary")),
        name="norm_mod_matmul",
    )(*x_parts, mods, gain, w)


def _ret_kernel(lg_ref, q_ref, k_ref, v_ref, g_ref, *rest, seq_len, has_init):
    if has_init:
        s0f_ref, s0b_ref, yg_ref, sf_ref, sb_ref, ycross, sf_scr, sb_scr = rest
    else:
        _, yg_ref, sf_ref, sb_ref, ycross, sf_scr, sb_scr = rest
    C = RET_CHUNK
    nc = seq_len // C
    head = pl.program_id(1)
    lgf = lg_ref[0, head]
    lgb = lg_ref[1, head]
    k_scale = RET_DK ** -0.5

    ii = lax.broadcasted_iota(jnp.int32, (C, C), 0).astype(F32)
    jj = lax.broadcasted_iota(jnp.int32, (C, C), 1).astype(F32)
    diff = ii - jj
    inner = jnp.where(diff >= 0, jnp.exp(diff * lgf), jnp.exp(-diff * lgb)) * k_scale
    pos = lax.broadcasted_iota(jnp.int32, (C, 1), 0).astype(F32)
    cross_f = jnp.exp((pos + 1.0) * lgf)
    cross_b = jnp.exp((C - pos) * lgb)
    sdec_f = jnp.exp((C - 1.0 - pos) * lgf) * k_scale
    sdec_b = jnp.exp(pos * lgb) * k_scale
    one = jnp.ones((1, 1), F32)
    cdec_f = jnp.exp(one * (C * lgf))
    cdec_b = jnp.exp(one * (C * lgb))

    if has_init:
        sf_scr[...] = s0f_ref[0, 0]
        sb_scr[...] = s0b_ref[0, 0]
    else:
        sf_scr[...] = jnp.zeros_like(sf_scr)
        sb_scr[...] = jnp.zeros_like(sb_scr)

    contract0 = (((0,), (0,)), ((), ()))
    contract1 = (((1,), (1,)), ((), ()))

    def bwd(t, carry):
        rows = pl.ds(pl.multiple_of((nc - 1 - t) * C, C), C)
        q = q_ref[rows, :]
        s = sb_scr[...]
        ycross[rows, :] = jnp.dot(q, s.astype(BF16), preferred_element_type=F32) * cross_b
        kd = (k_ref[rows, :].astype(F32) * sdec_b).astype(BF16)
        sb_scr[...] = s * cdec_b + lax.dot_general(kd, v_ref[rows, :], contract0,
                                                   preferred_element_type=F32)
        return carry

    lax.fori_loop(0, nc, bwd, 0)

    def fwd(c, carry):
        rows = pl.ds(pl.multiple_of(c * C, C), C)
        q = q_ref[rows, :]
        k = k_ref[rows, :]
        v = v_ref[rows, :]
        s = sf_scr[...]
        scores = lax.dot_general(q, k, contract1, preferred_element_type=F32) * inner
        y = (jnp.dot(scores.astype(BF16), v, preferred_element_type=F32)
             + jnp.dot(q, s.astype(BF16), preferred_element_type=F32) * cross_f
             + ycross[rows, :])
        mu = jnp.mean(y, axis=-1, keepdims=True)
        yc = y - mu
        yn = yc * lax.rsqrt(jnp.mean(yc * yc, axis=-1, keepdims=True) + NORM_EPS)
        g = g_ref[rows, :].astype(F32)
        yg_ref[rows, :] = (g * _sigmoid(g) * yn).astype(yg_ref.dtype)
        kd = (k.astype(F32) * sdec_f).astype(BF16)
        sf_scr[...] = s * cdec_f + lax.dot_general(kd, v, contract0, preferred_element_type=F32)
        return carry

    lax.fori_loop(0, nc, fwd, 0)
    sf_ref[0, 0] = sf_scr[...]
    sb_ref[0, 0] = sb_scr[...]


def _retention(qkvg, lg, s0f, s0b, yg_prev, *, n_req, seq_len, row0):
    rb0 = row0 // seq_len
    has_init = s0f is not None
    qk_spec = lambda off: pl.BlockSpec((seq_len, RET_DK), lambda b, h, lg_: (rb0 + b, off + h))
    vg_spec = lambda off: pl.BlockSpec((seq_len, RET_DV), lambda b, h, lg_: (rb0 + b, off + h))
    st_spec = pl.BlockSpec((1, 1, RET_DK, RET_DV), lambda b, h, lg_: (b, h, 0, 0))
    in_specs = [qk_spec(0), qk_spec(RET_HEADS), vg_spec(RET_HEADS), vg_spec(2 * RET_HEADS)]
    args = [qkvg, qkvg, qkvg, qkvg]
    if has_init:
        in_specs += [st_spec, st_spec]
        args += [s0f, s0b]
        aliases = {}
    else:
        in_specs.append(pl.BlockSpec(memory_space=pl.ANY))
        args.append(yg_prev)
        aliases = {len(args): 0}
    st_shape = jax.ShapeDtypeStruct((n_req, RET_HEADS, RET_DK, RET_DV), F32)
    return pl.pallas_call(
        functools.partial(_ret_kernel, seq_len=seq_len, has_init=has_init),
        grid_spec=pltpu.PrefetchScalarGridSpec(
            num_scalar_prefetch=1,
            grid=(n_req, RET_HEADS),
            in_specs=in_specs,
            out_specs=[pl.BlockSpec((seq_len, RET_DV), lambda b, h, lg_: (rb0 + b, h)), st_spec, st_spec],
            scratch_shapes=[pltpu.VMEM((seq_len, RET_DV), F32),
                            pltpu.VMEM((RET_DK, RET_DV), F32),
                            pltpu.VMEM((RET_DK, RET_DV), F32)]),
        out_shape=[jax.ShapeDtypeStruct((NT, RET_V_WIDTH), BF16), st_shape, st_shape],
        input_output_aliases=aliases,
        compiler_params=_cparams(("parallel", "arbitrary")),
        name=f"retention_{seq_len}",
    )(lg, *args)


def _split_bf16(x):
    hi = x.astype(BF16)
    return hi, (x - hi.astype(F32)).astype(BF16)


def _out_post_kernel(y_ref, w_ref, xlo_ref, xhi_ref, mod_ref, gpost_ref, gpre_ref, wr_ref,
                     x1_ref, h3_ref, aff_ref, w_scr):
    i = pl.program_id(0)

    @pl.when(i == 0)
    def _():
        w_scr[...] = w_ref[...].astype(BF16)

    m = mod_ref[0]
    y = jnp.dot(y_ref[...], w_scr[...], preferred_element_type=F32)
    x1 = _load_rows(xlo_ref, xhi_ref, i) + _rms(y) * gpost_ref[...] * m[2:3]
    x1_ref[...] = x1
    h = _rms(x1) * gpre_ref[...] * (1.0 + m[4:5]) + m[3:4]
    for c in range(ROW_TILES):
        h3_ref[pl.ds(c, TM, stride=ROW_TILES), :] = h[:, c * LANES:(c + 1) * LANES]
    contract1 = (((1,), (1,)), ((), ()))
    w_hi, w_lo = _split_bf16(wr_ref[...])
    h_hi, h_lo = _split_bf16(h)
    part = lax.dot_general(jnp.concatenate([w_hi, w_lo], axis=0), h_hi, contract1,
                           preferred_element_type=F32)
    logits = (part[:N_EXPERTS] + part[N_EXPERTS:]
              + lax.dot_general(w_hi, h_lo, contract1, preferred_element_type=F32))
    e = jnp.exp(logits - jnp.max(logits, axis=0, keepdims=True))
    aff_ref[...] = e / jnp.sum(e, axis=0, keepdims=True)


def _out_post(y, w, x_parts, joint, mods, gpost, gpre, w_router_t):
    kdim = y.shape[1]
    return pl.pallas_call(
        _out_post_kernel,
        grid=(NT // TM,),
        in_specs=[pl.BlockSpec((TM, kdim), lambda i: (i, 0)),
                  _const_spec((kdim, D_MODEL)),
                  *_part_specs(D_MODEL, joint), _mod_spec(),
                  _const_spec((1, D_MODEL)), _const_spec((1, D_MODEL)),
                  _const_spec((N_EXPERTS, D_MODEL))],
        out_specs=[pl.BlockSpec((TM, D_MODEL), lambda i: (i, 0)),
                   pl.BlockSpec((TM * ROW_TILES, LANES), lambda i: (i, 0)),
                   pl.BlockSpec((N_EXPERTS, TM), lambda i: (0, i))],
        out_shape=[jax.ShapeDtypeStruct((NT, D_MODEL), F32),
                   jax.ShapeDtypeStruct((NT * ROW_TILES, LANES), F32),
                   jax.ShapeDtypeStruct((N_EXPERTS, NT), F32)],
        scratch_shapes=[pltpu.VMEM((kdim, D_MODEL), BF16)],
        compiler_params=_cparams(("arbitrary",)),
        name="out_post",
    )(y, w, *x_parts, mods, gpost, gpre, w_router_t)


FF_SPLIT = 2
FF_TILE = EXPERT_FF // FF_SPLIT
GATHER_UNROLL = 8


def _moe_kernel(idx_ref, aff_ref, h3_ref, wg_ref, wu_ref, wd_ref, f3_ref, xg, x_scr, yacc, y3):
    e = pl.program_id(1)
    f = pl.program_id(2)

    @pl.when(jnp.logical_and(e == 0, f == 0))
    def _():
        f3_ref[...] = jnp.zeros_like(f3_ref)

    def tile(r):
        return pl.ds(pl.multiple_of(r * SUBLANES, SUBLANES), SUBLANES)

    @pl.when(f == 0)
    def _():
        def gather(i, carry):
            for j in range(GATHER_UNROLL):
                r = i * GATHER_UNROLL + j
                xg[tile(r), :] = h3_ref[tile(idx_ref[0, 0, r]), :]
            return carry

        lax.fori_loop(0, CAP // GATHER_UNROLL, gather, 0)
        x_scr[...] = jnp.concatenate(
            [xg[pl.ds(c, CAP, stride=ROW_TILES), :] for c in range(ROW_TILES)], axis=1).astype(BF16)
        yacc[...] = jnp.zeros_like(yacc)

    x = x_scr[...]
    a = jnp.dot(x, wg_ref[0, 0].astype(BF16), preferred_element_type=F32)
    u = jnp.dot(x, wu_ref[0, 0].astype(BF16), preferred_element_type=F32)
    act = (a * _sigmoid(a) * u).astype(BF16)
    yacc[...] += jnp.dot(act, wd_ref[0, 0].astype(BF16), preferred_element_type=F32)

    @pl.when(f == FF_SPLIT - 1)
    def _():
        y = yacc[...]
        for c in range(ROW_TILES):
            y3[pl.ds(c, CAP, stride=ROW_TILES), :] = y[:, c * LANES:(c + 1) * LANES]

        def scatter(i, carry):
            dst, vals = [], []
            for j in range(GATHER_UNROLL):
                r = i * GATHER_UNROLL + j
                t = idx_ref[0, 0, r]
                dst.append(tile(t))
                vals.append(f3_ref[tile(t), :] + aff_ref[0, 0, t] * y3[tile(r), :])
            for d, v in zip(dst, vals):
                f3_ref[d, :] = v
            return carry

        lax.fori_loop(0, CAP // GATHER_UNROLL, scatter, 0)


def _moe(idx, aff_rows, h3, wg, wu, wd, layer):
    grp_rows = GT * ROW_TILES
    return pl.pallas_call(
        _moe_kernel,
        grid=(NG, N_EXPERTS, FF_SPLIT),
        in_specs=[pl.BlockSpec((1, 1, CAP), lambda b, e, f: (b * N_EXPERTS + e, 0, 0),
                               memory_space=pltpu.SMEM),
                  pl.BlockSpec((1, 1, GT), lambda b, e, f: (e * NG + b, 0, 0),
                               memory_space=pltpu.SMEM),
                  pl.BlockSpec((grp_rows, LANES), lambda b, e, f: (b, 0), pipeline_mode=pl.Buffered(1)),
                  pl.BlockSpec((1, 1, D_MODEL, FF_TILE), lambda b, e, f: (layer, e, 0, f)),
                  pl.BlockSpec((1, 1, D_MODEL, FF_TILE), lambda b, e, f: (layer, e, 0, f)),
                  pl.BlockSpec((1, 1, FF_TILE, D_MODEL), lambda b, e, f: (layer, e, f, 0))],
        out_specs=pl.BlockSpec((grp_rows, LANES), lambda b, e, f: (b, 0), pipeline_mode=pl.Buffered(1)),
        out_shape=jax.ShapeDtypeStruct((NT * ROW_TILES, LANES), F32),
        scratch_shapes=[pltpu.VMEM((CAP * ROW_TILES, LANES), F32),
                        pltpu.VMEM((CAP, D_MODEL), BF16),
                        pltpu.VMEM((CAP, D_MODEL), F32),
                        pltpu.VMEM((CAP * ROW_TILES, LANES), F32)],
        compiler_params=_cparams(("arbitrary", "arbitrary", "arbitrary")),
        name="moe_experts",
    )(idx, aff_rows, h3, wg, wu, wd)


def _moe_post_kernel(f3_ref, x_ref, mod_ref, gpost_ref, *o_refs):
    i = pl.program_id(0)
    m = mod_ref[0]
    fx = jnp.concatenate([f3_ref[pl.ds(c, TM, stride=ROW_TILES), :] for c in range(ROW_TILES)], axis=1)
    out = x_ref[...] + _rms(fx) * gpost_ref[...] * m[5:6]
    if len(o_refs) == 1:
        o_refs[0][...] = out
    else:
        @pl.when(i < SAMPLE_TILES)
        def _():
            o_refs[0][...] = out

        @pl.when(i >= SAMPLE_TILES)
        def _():
            o_refs[1][...] = out


def _moe_post(f3, x, mods, gpost, split_out):
    if split_out:
        out_specs = _part_specs(D_MODEL, joint=False)
        out_shape = [jax.ShapeDtypeStruct((N_SAMPLE, D_MODEL), F32),
                     jax.ShapeDtypeStruct((N_PROMPT, D_MODEL), F32)]
    else:
        out_specs = pl.BlockSpec((TM, D_MODEL), lambda i: (i, 0))
        out_shape = jax.ShapeDtypeStruct((NT, D_MODEL), F32)
    return pl.pallas_call(
        _moe_post_kernel,
        grid=(NT // TM,),
        in_specs=[pl.BlockSpec((TM * ROW_TILES, LANES), lambda i: (i, 0)),
                  pl.BlockSpec((TM, D_MODEL), lambda i: (i, 0)),
                  _mod_spec(), _const_spec((1, D_MODEL))],
        out_specs=out_specs,
        out_shape=out_shape,
        compiler_params=_cparams(("arbitrary",)),
        name="moe_post",
    )(f3, x, mods, gpost)


def _route(aff_t):
    aff_s = jnp.swapaxes(aff_t[:, :N_SAMPLE].reshape(N_EXPERTS, DEC_BATCH, DEC_SEQ), 0, 1)
    aff_p = aff_t[:, N_SAMPLE:].reshape(N_EXPERTS, BATCH, SEQ)
    _, idx_s = lax.top_k(aff_s, 2 * DEC_SEQ // N_EXPERTS)
    _, idx_p = lax.top_k(aff_p, 2 * SEQ // N_EXPERTS)
    idx_p = idx_p + (jnp.arange(BATCH, dtype=jnp.int32) * SEQ)[None, :, None]
    idx = jnp.concatenate([idx_s, idx_p.reshape(1, N_EXPERTS, CAP)], axis=0).astype(jnp.int32)
    return idx.reshape(NG * N_EXPERTS, 1, CAP)


HD = MLA_HEADS * LANES


def _mla_proj_kernel(xlo_ref, xhi_ref, mod_ref, gain_ref, win_ref, qn_g_ref, kvn_g_ref, wq_ref, wkv_ref,
                     cos_ref, sin_ref, qn_ref, qr_ref, kn_ref, v_ref, krp_ref, ckv_ref, kraw_ref):
    i = pl.program_id(0)
    m = mod_ref[0]
    h = (_rms(_load_rows(xlo_ref, xhi_ref, i)) * gain_ref[...] * (1.0 + m[1:2]) + m[0:1]).astype(BF16)
    lat = jnp.dot(h, win_ref[...], preferred_element_type=F32)
    q_lat = lat[:, :MLA_Q_LORA]
    kv_lat = lat[:, MLA_Q_LORA:MLA_Q_LORA + MLA_KV_LORA]
    kr = lat[:, MLA_Q_LORA + MLA_KV_LORA:MLA_Q_LORA + MLA_KV_LORA + LANES]
    kr_rot = lat[:, MLA_Q_LORA + MLA_KV_LORA + LANES:]
    is_latent = i < SAMPLE_TILES
    cos = jnp.where(is_latent, cos_ref[...], 1.0)
    sin = jnp.where(is_latent, sin_ref[...], 0.0)

    qln = (_rms(q_lat) * qn_g_ref[...]).astype(BF16)
    q = jnp.dot(qln, wq_ref[...], preferred_element_type=F32)
    qn_ref[...] = q[:, :HD].astype(BF16)
    for hd in range(MLA_HEADS):
        lo = HD + hd * LANES
        qr_ref[:, hd * LANES:(hd + 1) * LANES] = (
            q[:, lo:lo + LANES] * cos + q[:, HD + lo:HD + lo + LANES] * sin).astype(BF16)

    ckv = _rms(kv_lat) * kvn_g_ref[...]
    kv = jnp.dot(ckv.astype(BF16), wkv_ref[...], preferred_element_type=F32)
    kn_ref[...] = kv[:, :HD].astype(BF16)
    v_ref[...] = kv[:, HD:].astype(BF16)
    krp_ref[...] = (kr * cos + kr_rot * sin).astype(BF16)

    @pl.when(i >= SAMPLE_TILES)
    def _():
        ckv_ref[...] = ckv
        kraw_ref[...] = kr


def _mla_proj(x_parts, joint, mods, gain, win_ext, qn_g, kvn_g, wq_ext, wkv_ext, cos128, sin128):
    row = lambda w: pl.BlockSpec((TM, w), lambda i: (i, 0))
    ctx_row = lambda w: pl.BlockSpec((TM, w), lambda i: (jnp.maximum(i - SAMPLE_TILES, 0), 0))
    tab = pl.BlockSpec((TM, LANES), lambda i: (i % TILES_PER_GROUP, 0))
    return pl.pallas_call(
        _mla_proj_kernel,
        grid=(NT // TM,),
        in_specs=[*_part_specs(D_MODEL, joint), _mod_spec(),
                  _const_spec(gain.shape), _const_spec(win_ext.shape), _const_spec(qn_g.shape),
                  _const_spec(kvn_g.shape), _const_spec(wq_ext.shape), _const_spec(wkv_ext.shape),
                  tab, tab],
        out_specs=[row(HD), row(HD), row(HD), row(HD), row(LANES), ctx_row(MLA_KV_LORA), ctx_row(LANES)],
        out_shape=[jax.ShapeDtypeStruct((NT, HD), BF16),
                   jax.ShapeDtypeStruct((NT, HD), BF16),
                   jax.ShapeDtypeStruct((NT, HD), BF16),
                   jax.ShapeDtypeStruct((NT, HD), BF16),
                   jax.ShapeDtypeStruct((NT, LANES), BF16),
                   jax.ShapeDtypeStruct((N_PROMPT, MLA_KV_LORA), F32),
                   jax.ShapeDtypeStruct((N_PROMPT, LANES), F32)],
        compiler_params=_cparams(("arbitrary",)),
        name="mla_proj",
    )(*x_parts, mods, gain, win_ext, qn_g, kvn_g, wq_ext, wkv_ext, cos128, sin128)


def _matmul_kernel(a_ref, w_ref, o_ref):
    o_ref[...] = jnp.dot(a_ref[...].astype(BF16), w_ref[...],
                         preferred_element_type=F32).astype(o_ref.dtype)


def _ctx_expand(ckv_ctx, wkv_ext):
    n = ckv_ctx.shape[0]
    return pl.pallas_call(
        _matmul_kernel,
        grid=(n // TM,),
        in_specs=[pl.BlockSpec((TM, MLA_KV_LORA), lambda i: (i, 0)),
                  pl.BlockSpec(wkv_ext.shape, lambda i: (0, 0))],
        out_specs=pl.BlockSpec((TM, 2 * HD), lambda i: (i, 0)),
        out_shape=jax.ShapeDtypeStruct((n, 2 * HD), BF16),
        compiler_params=_cparams(("parallel",)),
        name="ctx_expand",
    )(ckv_ctx, wkv_ext)


def _attn_kernel(qn_ref, qr_ref, kn_ref, krp_ref, v_ref, *rest, tk, n_chunks, has_ctx):
    if has_ctx:
        knc_ref, krpc_ref, vc_ref, o_ref = rest
    else:
        _, o_ref = rest
    scale = 1.0 / math.sqrt(MLA_NOPE + MLA_ROPE)
    q = jnp.concatenate([qn_ref[...], qr_ref[...]], axis=1)
    tq = q.shape[0]
    contract1 = (((1,), (1,)), ((), ()))

    def step(kn, krp, v, carry):
        m, l, acc = carry
        kc = jnp.concatenate([kn, krp], axis=1)
        s = lax.dot_general(q, kc, contract1, preferred_element_type=F32) * scale
        m_new = jnp.maximum(m, jnp.max(s, axis=-1, keepdims=True))
        alpha = jnp.exp(m - m_new)
        p = jnp.exp(s - m_new)
        l = alpha * l + jnp.sum(p, axis=-1, keepdims=True)
        acc = alpha * acc + jnp.dot(p.astype(BF16), v, preferred_element_type=F32)
        return m_new, l, acc

    carry = (jnp.full((tq, 1), -jnp.inf, F32), jnp.zeros((tq, 1), F32), jnp.zeros((tq, MLA_V), F32))
    if has_ctx:
        carry = step(knc_ref[0], krpc_ref[0], vc_ref[0], carry)

    def body(c, carry):
        rows = pl.ds(pl.multiple_of(c * tk, tk), tk)
        return step(kn_ref[rows, :], krp_ref[rows, :], v_ref[rows, :], carry)

    m, l, acc = lax.fori_loop(0, n_chunks, body, carry)
    o_ref[...] = (acc / l).astype(o_ref.dtype)


def _attention(qn, qr, kn, krp, v, ctx, o_prev, *, n_req, seq_len, row0, tq, tk):
    rb0 = row0 // seq_len
    qb0 = row0 // tq
    nq = seq_len // tq
    q_spec = pl.BlockSpec((tq, LANES), lambda b, h, i: (qb0 + b * nq + i, h))
    k_spec = pl.BlockSpec((seq_len, LANES), lambda b, h, i: (rb0 + b, h))
    in_specs = [q_spec, q_spec, k_spec,
                pl.BlockSpec((seq_len, LANES), lambda b, h, i: (rb0 + b, 0)), k_spec]
    args = [qn, qr, kn, krp, v]
    if ctx is not None:
        c_spec = pl.BlockSpec((1, PAST_LEN, LANES), lambda b, h, i: (b, 0, h))
        in_specs += [c_spec, pl.BlockSpec((1, PAST_LEN, LANES), lambda b, h, i: (b, 0, 0)), c_spec]
        args += list(ctx)
        aliases = {}
    else:
        in_specs.append(pl.BlockSpec(memory_space=pl.ANY))
        args.append(o_prev)
        aliases = {len(args) - 1: 0}
    return pl.pallas_call(
        functools.partial(_attn_kernel, tk=tk, n_chunks=seq_len // tk, has_ctx=ctx is not None),
        grid=(n_req, MLA_HEADS, nq),
        in_specs=in_specs,
        out_specs=q_spec,
        out_shape=jax.ShapeDtypeStruct((NT, HD), BF16),
        input_output_aliases=aliases,
        compiler_params=_cparams(("parallel", "parallel", "arbitrary")),
        name=f"attention_{seq_len}",
    )(*args)


def _rope_rot_cols(w):
    w1, w2, w3, w4 = jnp.split(w, 4, axis=-1)
    return jnp.concatenate([-w2, w1, -w4, w3], axis=-1)


def _pad_cols(w, width):
    return jnp.pad(w, ((0, 0), (0, width - w.shape[1])))


def _mla_weights(w_in, w_q_b, w_kv_b):
    w_rope = w_in[:, MLA_Q_LORA + MLA_KV_LORA:]
    win_ext = jnp.concatenate([w_in[:, :MLA_Q_LORA + MLA_KV_LORA], _pad_cols(w_rope, LANES),
                               _pad_cols(_rope_rot_cols(w_rope), LANES)], axis=1).astype(BF16)
    wq = w_q_b.reshape(MLA_Q_LORA, MLA_HEADS, MLA_NOPE + MLA_ROPE)
    wq_nope = wq[:, :, :MLA_NOPE].reshape(MLA_Q_LORA, HD)
    wq_rope = wq[:, :, MLA_NOPE:]
    pad = ((0, 0), (0, 0), (0, LANES - MLA_ROPE))
    wq_rope_p = jnp.pad(wq_rope, pad).reshape(MLA_Q_LORA, HD)
    wq_rot_p = jnp.pad(_rope_rot_cols(wq_rope), pad).reshape(MLA_Q_LORA, HD)
    wq_ext = jnp.concatenate([wq_nope, wq_rope_p, wq_rot_p], axis=1).astype(BF16)
    wkv = w_kv_b.reshape(MLA_KV_LORA, MLA_HEADS, MLA_NOPE + MLA_V)
    wkv_ext = jnp.concatenate([wkv[:, :, :MLA_NOPE].reshape(MLA_KV_LORA, HD),
                               wkv[:, :, MLA_NOPE:].reshape(MLA_KV_LORA, HD)], axis=1).astype(BF16)
    return win_ext, wq_ext, wkv_ext


def _rope_tables():
    rows = DEC_SEQ // GRID_W
    row = jnp.repeat(jnp.arange(rows, dtype=F32), GRID_W)
    col = jnp.tile(jnp.arange(GRID_W, dtype=F32), rows)
    half = MLA_ROPE // 2
    inv = 1.0 / (ROPE_BASE ** (jnp.arange(0, half, 2, dtype=F32) / half))
    ar = row[:, None] * inv
    ac = col[:, None] * inv
    ang = jnp.concatenate([ar, ar, ac, ac] * 2, axis=-1)
    return jnp.cos(ang), jnp.sin(ang)


def kernel(x_prompt, x_sample, state_ret_fwd, state_ret_bwd, cache_mla_ckv, cache_mla_krope, c, c_ctx,
           ada_w, ada_b, norm_pre, norm_post, ret_w_in, ret_decay_fwd, ret_decay_bwd, ret_w_out,
           mla_w_in, mla_q_norm, mla_kv_norm, mla_w_q_b, mla_w_kv_b, mla_w_out,
           moe_w_router, moe_w_gate, moe_w_up, moe_w_down):
    x_parts = (x_sample.reshape(N_SAMPLE, D_MODEL), x_prompt.reshape(N_PROMPT, D_MODEL))
    joint = False
    cond8 = jnp.concatenate([c, c_ctx[None, :], jnp.zeros((8 - NG, D_MODEL), F32)], axis=0)
    cos128, sin128 = _rope_tables()
    outs = {}

    for i in range(DEPTH):
        mods = _adaln(cond8, ada_w, ada_b.reshape(DEPTH, 1, 6 * D_MODEL), i)[:NG].reshape(NG, 6, D_MODEL)
        gpre1 = norm_pre[i, 0][None, :]
        gpre2 = norm_pre[i, 1][None, :]
        gpost1 = norm_post[i, 0][None, :]
        gpost2 = norm_post[i, 1][None, :]
        if i % 2 == 0:
            r = i // 2
            qkvg = _norm_mod_matmul(x_parts, joint, mods, gpre1, ret_w_in[r], 1536)
            lg = jnp.stack([jax.nn.log_sigmoid(ret_decay_fwd[r].astype(F32)),
                            jax.nn.log_sigmoid(ret_decay_bwd[r].astype(F32))])
            yg, _, _ = _retention(qkvg, lg, state_ret_fwd[:, r], state_ret_bwd[:, r], None,
                                  n_req=DEC_BATCH, seq_len=DEC_SEQ, row0=0)
            yg, s_f, s_b = _retention(qkvg, lg, None, None, yg,
                                      n_req=BATCH, seq_len=SEQ, row0=N_SAMPLE)
            outs["fwd"], outs["bwd"] = s_f[:, None], s_b[:, None]
            mix, w_out = yg, ret_w_out[r]
        else:
            mi = i // 2
            win_ext, wq_ext, wkv_ext = _mla_weights(mla_w_in[mi], mla_w_q_b[mi], mla_w_kv_b[mi])
            qn, qr, kn, v, krp, ckv, kraw = _mla_proj(
                x_parts, joint, mods, gpre1, win_ext, mla_q_norm[mi][None, :], mla_kv_norm[mi][None, :],
                wq_ext, wkv_ext, cos128, sin128)
            outs["ckv"] = ckv.reshape(BATCH, 1, SEQ, MLA_KV_LORA)
            outs["krope"] = kraw[:, :MLA_ROPE].reshape(BATCH, 1, SEQ, MLA_ROPE)
            kvc = _ctx_expand(cache_mla_ckv[:, mi].reshape(DEC_BATCH * PAST_LEN, MLA_KV_LORA), wkv_ext)
            kvc = kvc.reshape(DEC_BATCH, PAST_LEN, 2 * HD)
            krpc = jnp.pad(cache_mla_krope[:, mi], ((0, 0), (0, 0), (0, LANES - MLA_ROPE))).astype(BF16)
            o = _attention(qn, qr, kn, krp, v, (kvc[:, :, :HD], krpc, kvc[:, :, HD:]), None,
                           n_req=DEC_BATCH, seq_len=DEC_SEQ, row0=0, tq=512, tk=512)
            o = _attention(qn, qr, kn, krp, v, None, o,
                           n_req=BATCH, seq_len=SEQ, row0=N_SAMPLE, tq=SEQ, tk=SEQ)
            mix, w_out = o, mla_w_out[mi]
        x1, h3, aff_t = _out_post(mix, w_out, x_parts, joint, mods, gpost1, gpre2, moe_w_router[i].T)
        idx = _route(aff_t)
        f3 = _moe(idx, aff_t.reshape(N_EXPERTS * NG, 1, GT), h3, moe_w_gate, moe_w_up, moe_w_down, i)
        last = i == DEPTH - 1
        x_new = _moe_post(f3, x1, mods, gpost2, split_out=last)
        if not last:
            x_parts, joint = (x_new, x_new), True

    y_sample, y_prompt = x_new
    return (y_prompt.reshape(BATCH, SEQ, D_MODEL), y_sample.reshape(DEC_BATCH, DEC_SEQ, D_MODEL),
            outs["fwd"], outs["bwd"], outs["ckv"], outs["krope"])
```

```python
import functools
import math

import jax
import jax.numpy as jnp
from jax import lax
from jax.experimental import pallas as pl
from jax.experimental.pallas import tpu as pltpu

F32 = jnp.float32
BF16 = jnp.bfloat16

D_MODEL = 1024
BATCH = 16
SEQ = 256
DEPTH = 2
DEC_BATCH = 4
DEC_SEQ = 4096
PAST_LEN = 256
GRID_W = 64
RET_HEADS = 4
RET_DK = 256
RET_DV = 512
RET_QK_WIDTH = RET_HEADS * RET_DK
RET_V_WIDTH = RET_HEADS * RET_DV
RET_CHUNK = 128
MLA_HEADS = 8
MLA_NOPE = 128
MLA_ROPE = 64
MLA_V = 128
MLA_Q_LORA = 384
MLA_KV_LORA = 256
ROPE_BASE = 10000.0
N_EXPERTS = 16
EXPERT_FF = 1024
NORM_EPS = 1e-6

GT = DEC_SEQ
NG = DEC_BATCH + 1
NT = NG * GT
N_SAMPLE = DEC_BATCH * DEC_SEQ
N_PROMPT = BATCH * SEQ
CAP = 2 * GT // N_EXPERTS

LANES = 128
SUBLANES = 8
ROW_TILES = D_MODEL // LANES
TM = 512
TILES_PER_GROUP = GT // TM
SAMPLE_TILES = N_SAMPLE // TM
VMEM_LIMIT = 56 * 1024 * 1024


def _cparams(sem):
    return pltpu.CompilerParams(dimension_semantics=sem, vmem_limit_bytes=VMEM_LIMIT)


def _rms(x):
    return x * lax.rsqrt(jnp.mean(x * x, axis=-1, keepdims=True) + NORM_EPS)


def _sigmoid(x):
    return 1.0 / (1.0 + jnp.exp(-x))


def _part_specs(width, joint):
    lo = pl.BlockSpec((TM, width), lambda *g: (jnp.minimum(g[-1], SAMPLE_TILES - 1), 0))
    if joint:
        hi = pl.BlockSpec((TM, width), lambda *g: (jnp.maximum(g[-1], SAMPLE_TILES), 0))
    else:
        hi = pl.BlockSpec((TM, width), lambda *g: (jnp.maximum(g[-1] - SAMPLE_TILES, 0), 0))
    return [lo, hi]


def _load_rows(lo_ref, hi_ref, i):
    return jnp.where(i < SAMPLE_TILES, lo_ref[...], hi_ref[...])


def _mod_spec():
    return pl.BlockSpec((1, 6, D_MODEL), lambda *g: (g[-1] // TILES_PER_GROUP, 0, 0))


def _const_spec(shape):
    return pl.BlockSpec(shape, lambda *g: (0,) * len(shape))


def _adaln_kernel(c_ref, w_ref, b_ref, o_ref):
    c = c_ref[...]
    s = (c * _sigmoid(c)).astype(BF16)
    o_ref[...] = jnp.dot(s, w_ref[0].astype(BF16), preferred_element_type=F32) + b_ref[0]


def _adaln(cond8, w, b, layer):
    tn = 1536
    return pl.pallas_call(
        _adaln_kernel,
        grid=(6 * D_MODEL // tn,),
        in_specs=[pl.BlockSpec((8, D_MODEL), lambda j: (0, 0)),
                  pl.BlockSpec((1, D_MODEL, tn), lambda j: (layer, 0, j)),
                  pl.BlockSpec((1, 1, tn), lambda j: (layer, 0, j))],
        out_specs=pl.BlockSpec((8, tn), lambda j: (0, j)),
        out_shape=jax.ShapeDtypeStruct((8, 6 * D_MODEL), F32),
        compiler_params=_cparams(("arbitrary",)),
        name="adaln",
    )(cond8, w, b)


def _nmm_kernel(xlo_ref, xhi_ref, mod_ref, gain_ref, w_ref, o_ref, w_scr):
    i = pl.program_id(1)

    @pl.when(i == 0)
    def _():
        w_scr[...] = w_ref[...].astype(BF16)

    m = mod_ref[0]
    h = _rms(_load_rows(xlo_ref, xhi_ref, i)) * gain_ref[...] * (1.0 + m[1:2]) + m[0:1]
    o_ref[...] = jnp.dot(h.astype(BF16), w_scr[...], preferred_element_type=F32).astype(o_ref.dtype)


def _norm_mod_matmul(x_parts, joint, mods, gain, w, tn):
    n_out = w.shape[1]
    return pl.pallas_call(
        _nmm_kernel,
        grid=(n_out // tn, NT // TM),
        in_specs=[*_part_specs(D_MODEL, joint), _mod_spec(), _const_spec((1, D_MODEL)),
                  pl.BlockSpec((D_MODEL, tn), lambda j, i: (0, j))],
        out_specs=pl.BlockSpec((TM, tn), lambda j, i: (i, j)),
        out_shape=jax.ShapeDtypeStruct((NT, n_out), BF16),
        scratch_shapes=[pltpu.VMEM((D_MODEL, tn), BF16)],
        compiler_params=_cparams(("arbitrary", "arbitrary")),
        name="norm_mod_matmul",
    )(*x_parts, mods, gain, w)


def _ret_kernel(lg_ref, q_ref, k_ref, v_ref, g_ref, *rest, seq_len, has_init):
    if has_init:
        s0f_ref, s0b_ref, yg_ref, sf_ref, sb_ref, ycross, sf_scr, sb_scr = rest
    else:
        _, yg_ref, sf_ref, sb_ref, ycross, sf_scr, sb_scr = rest
    C = RET_CHUNK
    nc = seq_len // C
    head = pl.program_id(1)
    lgf = lg_ref[0, head]
    lgb = lg_ref[1, head]
    k_scale = RET_DK ** -0.5

    ii = lax.broadcasted_iota(jnp.int32, (C, C), 0).astype(F32)
    jj = lax.broadcasted_iota(jnp.int32, (C, C), 1).astype(F32)
    diff = ii - jj
    inner = jnp.where(diff >= 0, jnp.exp(diff * lgf), jnp.exp(-diff * lgb)) * k_scale
    pos = lax.broadcasted_iota(jnp.int32, (C, 1), 0).astype(F32)
    cross_f = jnp.exp((pos + 1.0) * lgf)
    cross_b = jnp.exp((C - pos) * lgb)
    sdec_f = jnp.exp((C - 1.0 - pos) * lgf) * k_scale
    sdec_b = jnp.exp(pos * lgb) * k_scale
    one = jnp.ones((1, 1), F32)
    cdec_f = jnp.exp(one * (C * lgf))
    cdec_b = jnp.exp(one * (C * lgb))

    if has_init:
        sf_scr[...] = s0f_ref[0, 0]
        sb_scr[...] = s0b_ref[0, 0]
    else:
        sf_scr[...] = jnp.zeros_like(sf_scr)
        sb_scr[...] = jnp.zeros_like(sb_scr)

    contract0 = (((0,), (0,)), ((), ()))
    contract1 = (((1,), (1,)), ((), ()))

    def bwd(t, carry):
        rows = pl.ds(pl.multiple_of((nc - 1 - t) * C, C), C)
        q = q_ref[rows, :]
        s = sb_scr[...]
        ycross[rows, :] = jnp.dot(q, s.astype(BF16), preferred_element_type=F32) * cross_b
        kd = (k_ref[rows, :].astype(F32) * sdec_b).astype(BF16)
        sb_scr[...] = s * cdec_b + lax.dot_general(kd, v_ref[rows, :], contract0,
                                                   preferred_element_type=F32)
        return carry

    lax.fori_loop(0, nc, bwd, 0)

    def fwd(c, carry):
        rows = pl.ds(pl.multiple_of(c * C, C), C)
        q = q_ref[rows, :]
        k = k_ref[rows, :]
        v = v_ref[rows, :]
        s = sf_scr[...]
        scores = lax.dot_general(q, k, contract1, preferred_element_type=F32) * inner
        y = (jnp.dot(scores.astype(BF16), v, preferred_element_type=F32)
             + jnp.dot(q, s.astype(BF16), preferred_element_type=F32) * cross_f
             + ycross[rows, :])
        mu = jnp.mean(y, axis=-1, keepdims=True)
        yc = y - mu
        yn = yc * lax.rsqrt(jnp.mean(yc * yc, axis=-1, keepdims=True) + NORM_EPS)
        g = g_ref[rows, :].astype(F32)
        yg_ref[rows, :] = (g * _sigmoid(g) * yn).astype(yg_ref.dtype)
        kd = (k.astype(F32) * sdec_f).astype(BF16)
        sf_scr[...] = s * cdec_f + lax.dot_general(kd, v, contract0, preferred_element_type=F32)
        return carry

    lax.fori_loop(0, nc, fwd, 0)
    sf_ref[0, 0] = sf_scr[...]
    sb_ref[0, 0] = sb_scr[...]


def _retention(qkvg, lg, s0f, s0b, yg_prev, *, n_req, seq_len, row0):
    rb0 = row0 // seq_len
    has_init = s0f is not None
    qk_spec = lambda off: pl.BlockSpec((seq_len, RET_DK), lambda b, h, lg_: (rb0 + b, off + h))
    vg_spec = lambda off: pl.BlockSpec((seq_len, RET_DV), lambda b, h, lg_: (rb0 + b, off + h))
    st_spec = pl.BlockSpec((1, 1, RET_DK, RET_DV), lambda b, h, lg_: (b, h, 0, 0))
    in_specs = [qk_spec(0), qk_spec(RET_HEADS), vg_spec(RET_HEADS), vg_spec(2 * RET_HEADS)]
    args = [qkvg, qkvg, qkvg, qkvg]
    if has_init:
        in_specs += [st_spec, st_spec]
        args += [s0f, s0b]
        aliases = {}
    else:
        in_specs.append(pl.BlockSpec(memory_space=pl.ANY))
        args.append(yg_prev)
        aliases = {len(args): 0}
    st_shape = jax.ShapeDtypeStruct((n_req, RET_HEADS, RET_DK, RET_DV), F32)
    return pl.pallas_call(
        functools.partial(_ret_kernel, seq_len=seq_len, has_init=has_init),
        grid_spec=pltpu.PrefetchScalarGridSpec(
            num_scalar_prefetch=1,
            grid=(n_req, RET_HEADS),
            in_specs=in_specs,
            out_specs=[pl.BlockSpec((seq_len, RET_DV), lambda b, h, lg_: (rb0 + b, h)), st_spec, st_spec],
            scratch_shapes=[pltpu.VMEM((seq_len, RET_DV), F32),
                            pltpu.VMEM((RET_DK, RET_DV), F32),
                            pltpu.VMEM((RET_DK, RET_DV), F32)]),
        out_shape=[jax.ShapeDtypeStruct((NT, RET_V_WIDTH), BF16), st_shape, st_shape],
        input_output_aliases=aliases,
        compiler_params=_cparams(("parallel", "arbitrary")),
        name=f"retention_{seq_len}",
    )(lg, *args)


def _split_bf16(x):
    hi = x.astype(BF16)
    return hi, (x - hi.astype(F32)).astype(BF16)


def _out_post_kernel(y_ref, w_ref, xlo_ref, xhi_ref, mod_ref, gpost_ref, gpre_ref, wr_ref,
                     x1_ref, h3_ref, aff_ref, w_scr):
    i = pl.program_id(0)

    @pl.when(i == 0)
    def _():
        w_scr[...] = w_ref[...].astype(BF16)

    m = mod_ref[0]
    y = jnp.dot(y_ref[...], w_scr[...], preferred_element_type=F32)
    x1 = _load_rows(xlo_ref, xhi_ref, i) + _rms(y) * gpost_ref[...] * m[2:3]
    x1_ref[...] = x1
    h = _rms(x1) * gpre_ref[...] * (1.0 + m[4:5]) + m[3:4]
    for c in range(ROW_TILES):
        h3_ref[pl.ds(c, TM, stride=ROW_TILES), :] = h[:, c * LANES:(c + 1) * LANES]
    contract1 = (((1,), (1,)), ((), ()))
    w_hi, w_lo = _split_bf16(wr_ref[...])
    h_hi, h_lo = _split_bf16(h)
    part = lax.dot_general(jnp.concatenate([w_hi, w_lo], axis=0), h_hi, contract1,
                           preferred_element_type=F32)
    logits = (part[:N_EXPERTS] + part[N_EXPERTS:]
              + lax.dot_general(w_hi, h_lo, contract1, preferred_element_type=F32))
    e = jnp.exp(logits - jnp.max(logits, axis=0, keepdims=True))
    aff_ref[...] = e / jnp.sum(e, axis=0, keepdims=True)


def _out_post(y, w, x_parts, joint, mods, gpost, gpre, w_router_t):
    kdim = y.shape[1]
    return pl.pallas_call(
        _out_post_kernel,
        grid=(NT // TM,),
        in_specs=[pl.BlockSpec((TM, kdim), lambda i: (i, 0)),
                  _const_spec((kdim, D_MODEL)),
                  *_part_specs(D_MODEL, joint), _mod_spec(),
                  _const_spec((1, D_MODEL)), _const_spec((1, D_MODEL)),
                  _const_spec((N_EXPERTS, D_MODEL))],
        out_specs=[pl.BlockSpec((TM, D_MODEL), lambda i: (i, 0)),
                   pl.BlockSpec((TM * ROW_TILES, LANES), lambda i: (i, 0)),
                   pl.BlockSpec((N_EXPERTS, TM), lambda i: (0, i))],
        out_shape=[jax.ShapeDtypeStruct((NT, D_MODEL), F32),
                   jax.ShapeDtypeStruct((NT * ROW_TILES, LANES), F32),
                   jax.ShapeDtypeStruct((N_EXPERTS, NT), F32)],
        scratch_shapes=[pltpu.VMEM((kdim, D_MODEL), BF16)],
        compiler_params=_cparams(("arbitrary",)),
        name="out_post",
    )(y, w, *x_parts, mods, gpost, gpre, w_router_t)


FF_SPLIT = 2
FF_TILE = EXPERT_FF // FF_SPLIT
GATHER_UNROLL = 8


def _moe_kernel(idx_ref, aff_ref, h3_ref, wg_ref, wu_ref, wd_ref, f3_ref, xg, x_scr, yacc, y3):
    e = pl.program_id(1)
    f = pl.program_id(2)

    @pl.when(jnp.logical_and(e == 0, f == 0))
    def _():
        f3_ref[...] = jnp.zeros_like(f3_ref)

    def tile(r):
        return pl.ds(pl.multiple_of(r * SUBLANES, SUBLANES), SUBLANES)

    @pl.when(f == 0)
    def _():
        def gather(i, carry):
            for j in range(GATHER_UNROLL):
                r = i * GATHER_UNROLL + j
                xg[tile(r), :] = h3_ref[tile(idx_ref[0, 0, r]), :]
            return carry

        lax.fori_loop(0, CAP // GATHER_UNROLL, gather, 0)
        x_scr[...] = jnp.concatenate(
            [xg[pl.ds(c, CAP, stride=ROW_TILES), :] for c in range(ROW_TILES)], axis=1).astype(BF16)
        yacc[...] = jnp.zeros_like(yacc)

    x = x_scr[...]
    a = jnp.dot(x, wg_ref[0, 0].astype(BF16), preferred_element_type=F32)
    u = jnp.dot(x, wu_ref[0, 0].astype(BF16), preferred_element_type=F32)
    act = (a * _sigmoid(a) * u).astype(BF16)
    yacc[...] += jnp.dot(act, wd_ref[0, 0].astype(BF16), preferred_element_type=F32)

    @pl.when(f == FF_SPLIT - 1)
    def _():
        y = yacc[...]
        for c in range(ROW_TILES):
            y3[pl.ds(c, CAP, stride=ROW_TILES), :] = y[:, c * LANES:(c + 1) * LANES]

        def scatter(i, carry):
            dst, vals = [], []
            for j in range(GATHER_UNROLL):
                r = i * GATHER_UNROLL + j
                t = idx_ref[0, 0, r]
                dst.append(tile(t))
                vals.append(f3_ref[tile(t), :] + aff_ref[0, 0, t] * y3[tile(r), :])
            for d, v in zip(dst, vals):
                f3_ref[d, :] = v
            return carry

        lax.fori_loop(0, CAP // GATHER_UNROLL, scatter, 0)


def _moe(idx, aff_rows, h3, wg, wu, wd, layer):
    grp_rows = GT * ROW_TILES
    return pl.pallas_call(
        _moe_kernel,
        grid=(NG, N_EXPERTS, FF_SPLIT),
        in_specs=[pl.BlockSpec((1, 1, CAP), lambda b, e, f: (b * N_EXPERTS + e, 0, 0),
                               memory_space=pltpu.SMEM),
                  pl.BlockSpec((1, 1, GT), lambda b, e, f: (e * NG + b, 0, 0),
                               memory_space=pltpu.SMEM),
                  pl.BlockSpec((grp_rows, LANES), lambda b, e, f: (b, 0), pipeline_mode=pl.Buffered(1)),
                  pl.BlockSpec((1, 1, D_MODEL, FF_TILE), lambda b, e, f: (layer, e, 0, f)),
                  pl.BlockSpec((1, 1, D_MODEL, FF_TILE), lambda b, e, f: (layer, e, 0, f)),
                  pl.BlockSpec((1, 1, FF_TILE, D_MODEL), lambda b, e, f: (layer, e, f, 0))],
        out_specs=pl.BlockSpec((grp_rows, LANES), lambda b, e, f: (b, 0), pipeline_mode=pl.Buffered(1)),
        out_shape=jax.ShapeDtypeStruct((NT * ROW_TILES, LANES), F32),
        scratch_shapes=[pltpu.VMEM((CAP * ROW_TILES, LANES), F32),
                        pltpu.VMEM((CAP, D_MODEL), BF16),
                        pltpu.VMEM((CAP, D_MODEL), F32),
                        pltpu.VMEM((CAP * ROW_TILES, LANES), F32)],
        compiler_params=_cparams(("arbitrary", "arbitrary", "arbitrary")),
        name="moe_experts",
    )(idx, aff_rows, h3, wg, wu, wd)


def _moe_post_kernel(f3_ref, x_ref, mod_ref, gpost_ref, *o_refs):
    i = pl.program_id(0)
    m = mod_ref[0]
    fx = jnp.concatenate([f3_ref[pl.ds(c, TM, stride=ROW_TILES), :] for c in range(ROW_TILES)], axis=1)
    out = x_ref[...] + _rms(fx) * gpost_ref[...] * m[5:6]
    if len(o_refs) == 1:
        o_refs[0][...] = out
    else:
        @pl.when(i < SAMPLE_TILES)
        def _():
            o_refs[0][...] = out

        @pl.when(i >= SAMPLE_TILES)
        def _():
            o_refs[1][...] = out


def _moe_post(f3, x, mods, gpost, split_out):
    if split_out:
        out_specs = _part_specs(D_MODEL, joint=False)
        out_shape = [jax.ShapeDtypeStruct((N_SAMPLE, D_MODEL), F32),
                     jax.ShapeDtypeStruct((N_PROMPT, D_MODEL), F32)]
    else:
        out_specs = pl.BlockSpec((TM, D_MODEL), lambda i: (i, 0))
        out_shape = jax.ShapeDtypeStruct((NT, D_MODEL), F32)
    return pl.pallas_call(
        _moe_post_kernel,
        grid=(NT // TM,),
        in_specs=[pl.BlockSpec((TM * ROW_TILES, LANES), lambda i: (i, 0)),
                  pl.BlockSpec((TM, D_MODEL), lambda i: (i, 0)),
                  _mod_spec(), _const_spec((1, D_MODEL))],
        out_specs=out_specs,
        out_shape=out_shape,
        compiler_params=_cparams(("arbitrary",)),
        name="moe_post",
    )(f3, x, mods, gpost)


def _route(aff_t):
    aff_s = jnp.swapaxes(aff_t[:, :N_SAMPLE].reshape(N_EXPERTS, DEC_BATCH, DEC_SEQ), 0, 1)
    aff_p = aff_t[:, N_SAMPLE:].reshape(N_EXPERTS, BATCH, SEQ)
    _, idx_s = lax.top_k(aff_s, 2 * DEC_SEQ // N_EXPERTS)
    _, idx_p = lax.top_k(aff_p, 2 * SEQ // N_EXPERTS)
    idx_p = idx_p + (jnp.arange(BATCH, dtype=jnp.int32) * SEQ)[None, :, None]
    idx = jnp.concatenate([idx_s, idx_p.reshape(1, N_EXPERTS, CAP)], axis=0).astype(jnp.int32)
    return idx.reshape(NG * N_EXPERTS, 1, CAP)


HD = MLA_HEADS * LANES
Q_SCALE = math.log2(math.e) / math.sqrt(MLA_NOPE + MLA_ROPE)


def _mla_proj_kernel(xlo_ref, xhi_ref, mod_ref, gain_ref, win_ref, qn_g_ref, kvn_g_ref, wq_ref, wkv_ref,
                     cos_ref, sin_ref, q_ref, kn_ref, v_ref, krp_ref, ckv_ref, kraw_ref):
    i = pl.program_id(0)
    m = mod_ref[0]
    h = (_rms(_load_rows(xlo_ref, xhi_ref, i)) * gain_ref[...] * (1.0 + m[1:2]) + m[0:1]).astype(BF16)
    lat = jnp.dot(h, win_ref[...], preferred_element_type=F32)
    q_lat = lat[:, :MLA_Q_LORA]
    kv_lat = lat[:, MLA_Q_LORA:MLA_Q_LORA + MLA_KV_LORA]
    kr = lat[:, MLA_Q_LORA + MLA_KV_LORA:MLA_Q_LORA + MLA_KV_LORA + LANES]
    kr_rot = lat[:, MLA_Q_LORA + MLA_KV_LORA + LANES:]
    is_latent = i < SAMPLE_TILES
    cos = jnp.where(is_latent, cos_ref[...], 1.0)
    sin = jnp.where(is_latent, sin_ref[...], 0.0)

    qln = (_rms(q_lat) * qn_g_ref[...]).astype(BF16)
    q = jnp.dot(qln, wq_ref[...], preferred_element_type=F32) * Q_SCALE
    for hd in range(MLA_HEADS):
        lo = HD + hd * LANES
        q_ref[:, 2 * hd * LANES:(2 * hd + 1) * LANES] = q[:, hd * LANES:(hd + 1) * LANES].astype(BF16)
        q_ref[:, (2 * hd + 1) * LANES:(2 * hd + 2) * LANES] = (
            q[:, lo:lo + LANES] * cos + q[:, HD + lo:HD + lo + LANES] * sin).astype(BF16)

    ckv = _rms(kv_lat) * kvn_g_ref[...]
    kv = jnp.dot(ckv.astype(BF16), wkv_ref[...], preferred_element_type=F32)
    kn_ref[...] = kv[:, :HD].astype(BF16)
    v_ref[...] = kv[:, HD:].astype(BF16)
    krp_ref[...] = (kr * cos + kr_rot * sin).astype(BF16)

    @pl.when(i >= SAMPLE_TILES)
    def _():
        ckv_ref[...] = ckv
        kraw_ref[...] = kr


def _mla_proj(x_parts, joint, mods, gain, win_ext, qn_g, kvn_g, wq_ext, wkv_ext, cos128, sin128):
    row = lambda w: pl.BlockSpec((TM, w), lambda i: (i, 0))
    ctx_row = lambda w: pl.BlockSpec((TM, w), lambda i: (jnp.maximum(i - SAMPLE_TILES, 0), 0))
    tab = pl.BlockSpec((TM, LANES), lambda i: (i % TILES_PER_GROUP, 0))
    return pl.pallas_call(
        _mla_proj_kernel,
        grid=(NT // TM,),
        in_specs=[*_part_specs(D_MODEL, joint), _mod_spec(),
                  _const_spec(gain.shape), _const_spec(win_ext.shape), _const_spec(qn_g.shape),
                  _const_spec(kvn_g.shape), _const_spec(wq_ext.shape), _const_spec(wkv_ext.shape),
                  tab, tab],
        out_specs=[row(2 * HD), row(HD), row(HD), row(LANES), ctx_row(MLA_KV_LORA), ctx_row(LANES)],
        out_shape=[jax.ShapeDtypeStruct((NT, 2 * HD), BF16),
                   jax.ShapeDtypeStruct((NT, HD), BF16),
                   jax.ShapeDtypeStruct((NT, HD), BF16),
                   jax.ShapeDtypeStruct((NT, LANES), BF16),
                   jax.ShapeDtypeStruct((N_PROMPT, MLA_KV_LORA), F32),
                   jax.ShapeDtypeStruct((N_PROMPT, LANES), F32)],
        compiler_params=_cparams(("arbitrary",)),
        name="mla_proj",
    )(*x_parts, mods, gain, win_ext, qn_g, kvn_g, wq_ext, wkv_ext, cos128, sin128)


def _matmul_kernel(a_ref, w_ref, o_ref):
    o_ref[...] = jnp.dot(a_ref[...].astype(BF16), w_ref[...],
                         preferred_element_type=F32).astype(o_ref.dtype)


def _ctx_expand(ckv_ctx, wkv_ext):
    n = ckv_ctx.shape[0]
    return pl.pallas_call(
        _matmul_kernel,
        grid=(n // TM,),
        in_specs=[pl.BlockSpec((TM, MLA_KV_LORA), lambda i: (i, 0)),
                  pl.BlockSpec(wkv_ext.shape, lambda i: (0, 0))],
        out_specs=pl.BlockSpec((TM, 2 * HD), lambda i: (i, 0)),
        out_shape=jax.ShapeDtypeStruct((n, 2 * HD), BF16),
        compiler_params=_cparams(("parallel",)),
        name="ctx_expand",
    )(ckv_ctx, wkv_ext)


ATTN_HEADS_PER_STEP = 2
ATTN_CHUNK_UNROLL = 8


def _attn_kernel(q_ref, kn_ref, krp_ref, v_ref, *rest, tk, n_chunks, has_ctx):
    if has_ctx:
        knc_ref, krpc_ref, vc_ref, o_ref = rest
    else:
        _, o_ref = rest
    tq = q_ref.shape[0]
    contract1 = (((1,), (1,)), ((), ()))
    heads = range(ATTN_HEADS_PER_STEP)
    head_cols = lambda hd: slice(hd * LANES, (hd + 1) * LANES)
    qs = [q_ref[:, 2 * hd * LANES:2 * (hd + 1) * LANES] for hd in heads]

    def ones_column(rows):
        return (lax.broadcasted_iota(jnp.int32, (rows, LANES), 1) == 0).astype(BF16)

    def step(q, kn, krp, v, ones, carry):
        m, acc = carry
        kc = jnp.concatenate([kn, krp], axis=1)
        s = lax.dot_general(q, kc, contract1, preferred_element_type=F32)
        m_new = jnp.maximum(m, jnp.max(s, axis=-1, keepdims=True))
        p = jnp.exp2(s - m_new).astype(BF16)
        pv = jnp.dot(p, jnp.concatenate([v, ones], axis=1), preferred_element_type=F32)
        return m_new, jnp.exp2(m - m_new) * acc + pv

    carry = tuple((jnp.full((tq, 1), -jnp.inf, F32), jnp.zeros((tq, 2 * LANES), F32)) for _ in heads)
    if has_ctx:
        ones = ones_column(PAST_LEN)
        carry = tuple(step(qs[hd], knc_ref[0, :, head_cols(hd)], krpc_ref[0], vc_ref[0, :, head_cols(hd)],
                           ones, carry[hd]) for hd in heads)
    ones = ones_column(tk)

    def chunk(c, carry):
        rows = pl.ds(pl.multiple_of(c * tk, tk), tk)
        krp = krp_ref[rows, :]
        return tuple(step(qs[hd], kn_ref[rows, head_cols(hd)], krp, v_ref[rows, head_cols(hd)],
                          ones, carry[hd]) for hd in heads)

    unroll = math.gcd(n_chunks, ATTN_CHUNK_UNROLL)

    def body(c, carry):
        for u in range(unroll):
            carry = chunk(c * unroll + u, carry)
        return carry

    carry = lax.fori_loop(0, n_chunks // unroll, body, carry)
    for hd in heads:
        acc = carry[hd][1]
        o_ref[:, head_cols(hd)] = (acc[:, :MLA_V] / acc[:, MLA_V:MLA_V + 1]).astype(o_ref.dtype)


def _attention(q, kn, krp, v, ctx, o_prev, *, n_req, seq_len, row0, tq, tk):
    rb0 = row0 // seq_len
    qb0 = row0 // tq
    nq = seq_len // tq
    hw = ATTN_HEADS_PER_STEP * LANES
    o_spec = pl.BlockSpec((tq, hw), lambda b, h, i: (qb0 + b * nq + i, h))
    k_spec = pl.BlockSpec((seq_len, hw), lambda b, h, i: (rb0 + b, h))
    in_specs = [pl.BlockSpec((tq, 2 * hw), lambda b, h, i: (qb0 + b * nq + i, h)), k_spec,
                pl.BlockSpec((seq_len, LANES), lambda b, h, i: (rb0 + b, 0)), k_spec]
    args = [q, kn, krp, v]
    if ctx is not None:
        c_spec = pl.BlockSpec((1, PAST_LEN, hw), lambda b, h, i: (b, 0, h))
        in_specs += [c_spec, pl.BlockSpec((1, PAST_LEN, LANES), lambda b, h, i: (b, 0, 0)), c_spec]
        args += list(ctx)
        aliases = {}
    else:
        in_specs.append(pl.BlockSpec(memory_space=pl.ANY))
        args.append(o_prev)
        aliases = {len(args) - 1: 0}
    return pl.pallas_call(
        functools.partial(_attn_kernel, tk=tk, n_chunks=seq_len // tk, has_ctx=ctx is not None),
        grid=(n_req, MLA_HEADS // ATTN_HEADS_PER_STEP, nq),
        in_specs=in_specs,
        out_specs=o_spec,
        out_shape=jax.ShapeDtypeStruct((NT, HD), BF16),
        input_output_aliases=aliases,
        compiler_params=_cparams(("parallel", "parallel", "arbitrary")),
        name=f"attention_{seq_len}",
    )(*args)


def _rope_rot_cols(w):
    w1, w2, w3, w4 = jnp.split(w, 4, axis=-1)
    return jnp.concatenate([-w2, w1, -w4, w3], axis=-1)


def _pad_cols(w, width):
    return jnp.pad(w, ((0, 0), (0, width - w.shape[1])))


def _mla_weights(w_in, w_q_b, w_kv_b):
    w_rope = w_in[:, MLA_Q_LORA + MLA_KV_LORA:]
    win_ext = jnp.concatenate([w_in[:, :MLA_Q_LORA + MLA_KV_LORA], _pad_cols(w_rope, LANES),
                               _pad_cols(_rope_rot_cols(w_rope), LANES)], axis=1).astype(BF16)
    wq = w_q_b.reshape(MLA_Q_LORA, MLA_HEADS, MLA_NOPE + MLA_ROPE)
    wq_nope = wq[:, :, :MLA_NOPE].reshape(MLA_Q_LORA, HD)
    wq_rope = wq[:, :, MLA_NOPE:]
    pad = ((0, 0), (0, 0), (0, LANES - MLA_ROPE))
    wq_rope_p = jnp.pad(wq_rope, pad).reshape(MLA_Q_LORA, HD)
    wq_rot_p = jnp.pad(_rope_rot_cols(wq_rope), pad).reshape(MLA_Q_LORA, HD)
    wq_ext = jnp.concatenate([wq_nope, wq_rope_p, wq_rot_p], axis=1).astype(BF16)
    wkv = w_kv_b.reshape(MLA_KV_LORA, MLA_HEADS, MLA_NOPE + MLA_V)
    wkv_ext = jnp.concatenate([wkv[:, :, :MLA_NOPE].reshape(MLA_KV_LORA, HD),
                               wkv[:, :, MLA_NOPE:].reshape(MLA_KV_LORA, HD)], axis=1).astype(BF16)
    return win_ext, wq_ext, wkv_ext


def _rope_tables():
    rows = DEC_SEQ // GRID_W
    row = jnp.repeat(jnp.arange(rows, dtype=F32), GRID_W)
    col = jnp.tile(jnp.arange(GRID_W, dtype=F32), rows)
    half = MLA_ROPE // 2
    inv = 1.0 / (ROPE_BASE ** (jnp.arange(0, half, 2, dtype=F32) / half))
    ar = row[:, None] * inv
    ac = col[:, None] * inv
    ang = jnp.concatenate([ar, ar, ac, ac] * 2, axis=-1)
    return jnp.cos(ang), jnp.sin(ang)


def kernel(x_prompt, x_sample, state_ret_fwd, state_ret_bwd, cache_mla_ckv, cache_mla_krope, c, c_ctx,
           ada_w, ada_b, norm_pre, norm_post, ret_w_in, ret_decay_fwd, ret_decay_bwd, ret_w_out,
           mla_w_in, mla_q_norm, mla_kv_norm, mla_w_q_b, mla_w_kv_b, mla_w_out,
           moe_w_router, moe_w_gate, moe_w_up, moe_w_down):
    x_parts = (x_sample.reshape(N_SAMPLE, D_MODEL), x_prompt.reshape(N_PROMPT, D_MODEL))
    joint = False
    cond8 = jnp.concatenate([c, c_ctx[None, :], jnp.zeros((8 - NG, D_MODEL), F32)], axis=0)
    cos128, sin128 = _rope_tables()
    outs = {}

    for i in range(DEPTH):
        mods = _adaln(cond8, ada_w, ada_b.reshape(DEPTH, 1, 6 * D_MODEL), i)[:NG].reshape(NG, 6, D_MODEL)
        gpre1 = norm_pre[i, 0][None, :]
        gpre2 = norm_pre[i, 1][None, :]
        gpost1 = norm_post[i, 0][None, :]
        gpost2 = norm_post[i, 1][None, :]
        if i % 2 == 0:
            r = i // 2
            qkvg = _norm_mod_matmul(x_parts, joint, mods, gpre1, ret_w_in[r], 1536)
            lg = jnp.stack([jax.nn.log_sigmoid(ret_decay_fwd[r].astype(F32)),
                            jax.nn.log_sigmoid(ret_decay_bwd[r].astype(F32))])
            yg, _, _ = _retention(qkvg, lg, state_ret_fwd[:, r], state_ret_bwd[:, r], None,
                                  n_req=DEC_BATCH, seq_len=DEC_SEQ, row0=0)
            yg, s_f, s_b = _retention(qkvg, lg, None, None, yg,
                                      n_req=BATCH, seq_len=SEQ, row0=N_SAMPLE)
            outs["fwd"], outs["bwd"] = s_f[:, None], s_b[:, None]
            mix, w_out = yg, ret_w_out[r]
        else:
            mi = i // 2
            win_ext, wq_ext, wkv_ext = _mla_weights(mla_w_in[mi], mla_w_q_b[mi], mla_w_kv_b[mi])
            q, kn, v, krp, ckv, kraw = _mla_proj(
                x_parts, joint, mods, gpre1, win_ext, mla_q_norm[mi][None, :], mla_kv_norm[mi][None, :],
                wq_ext, wkv_ext, cos128, sin128)
            outs["ckv"] = ckv.reshape(BATCH, 1, SEQ, MLA_KV_LORA)
            outs["krope"] = kraw[:, :MLA_ROPE].reshape(BATCH, 1, SEQ, MLA_ROPE)
            kvc = _ctx_expand(cache_mla_ckv[:, mi].reshape(DEC_BATCH * PAST_LEN, MLA_KV_LORA), wkv_ext)
            kvc = kvc.reshape(DEC_BATCH, PAST_LEN, 2 * HD)
            krpc = jnp.pad(cache_mla_krope[:, mi], ((0, 0), (0, 0), (0, LANES - MLA_ROPE))).astype(BF16)
            o = _attention(q, kn, krp, v, (kvc[:, :, :HD], krpc, kvc[:, :, HD:]), None,
                           n_req=DEC_BATCH, seq_len=DEC_SEQ, row0=0, tq=512, tk=512)
            o = _attention(q, kn, krp, v, None, o,
                           n_req=BATCH, seq_len=SEQ, row0=N_SAMPLE, tq=SEQ, tk=SEQ)
            mix, w_out = o, mla_w_out[mi]
        x1, h3, aff_t = _out_post(mix, w_out, x_parts, joint, mods, gpost1, gpre2, moe_w_router[i].T)
        idx = _route(aff_t)
        f3 = _moe(idx, aff_t.reshape(N_EXPERTS * NG, 1, GT), h3, moe_w_gate, moe_w_up, moe_w_down, i)
        last = i == DEPTH - 1
        x_new = _moe_post(f3, x1, mods, gpost2, split_out=last)
        if not last:
            x_parts, joint = (x_new, x_new), True

    y_sample, y_prompt = x_new
    return (y_prompt.reshape(BATCH, SEQ, D_MODEL), y_sample.reshape(DEC_BATCH, DEC_SEQ, D_MODEL),
            outs["fwd"], outs["bwd"], outs["ckv"], outs["krope"])
```

```python
import functools
import math

import jax
import jax.numpy as jnp
from jax import lax
from jax.experimental import pallas as pl
from jax.experimental.pallas import tpu as pltpu

F32 = jnp.float32
BF16 = jnp.bfloat16

D_MODEL = 1024
BATCH = 16
SEQ = 256
DEPTH = 2
DEC_BATCH = 4
DEC_SEQ = 4096
PAST_LEN = 256
GRID_W = 64
RET_HEADS = 4
RET_DK = 256
RET_DV = 512
RET_QK_WIDTH = RET_HEADS * RET_DK
RET_V_WIDTH = RET_HEADS * RET_DV
RET_CHUNK = 128
MLA_HEADS = 8
MLA_NOPE = 128
MLA_ROPE = 64
MLA_V = 128
MLA_Q_LORA = 384
MLA_KV_LORA = 256
ROPE_BASE = 10000.0
N_EXPERTS = 16
EXPERT_FF = 1024
NORM_EPS = 1e-6

GT = DEC_SEQ
NG = DEC_BATCH + 1
NT = NG * GT
N_SAMPLE = DEC_BATCH * DEC_SEQ
N_PROMPT = BATCH * SEQ
CAP = 2 * GT // N_EXPERTS

LANES = 128
SUBLANES = 8
ROW_TILES = D_MODEL // LANES
TM = 512
TILES_PER_GROUP = GT // TM
SAMPLE_TILES = N_SAMPLE // TM
VMEM_LIMIT = 60 * 1024 * 1024


def _cparams(sem):
    return pltpu.CompilerParams(dimension_semantics=sem, vmem_limit_bytes=VMEM_LIMIT)


def _rms(x):
    return x * lax.rsqrt(jnp.mean(x * x, axis=-1, keepdims=True) + NORM_EPS)


def _sigmoid(x):
    return 1.0 / (1.0 + jnp.exp(-x))


def _part_specs(width, joint):
    lo = pl.BlockSpec((TM, width), lambda *g: (jnp.minimum(g[-1], SAMPLE_TILES - 1), 0))
    if joint:
        hi = pl.BlockSpec((TM, width), lambda *g: (jnp.maximum(g[-1], SAMPLE_TILES), 0))
    else:
        hi = pl.BlockSpec((TM, width), lambda *g: (jnp.maximum(g[-1] - SAMPLE_TILES, 0), 0))
    return [lo, hi]


def _load_rows(lo_ref, hi_ref, i):
    return jnp.where(i < SAMPLE_TILES, lo_ref[...], hi_ref[...])


def _mod_spec():
    return pl.BlockSpec((1, 6, D_MODEL), lambda *g: (g[-1] // TILES_PER_GROUP, 0, 0))


def _const_spec(shape):
    return pl.BlockSpec(shape, lambda *g: (0,) * len(shape))


def _adaln_kernel(c_ref, w_ref, b_ref, o_ref):
    c = c_ref[...]
    s = (c * _sigmoid(c)).astype(BF16)
    o_ref[...] = jnp.dot(s, w_ref[0].astype(BF16), preferred_element_type=F32) + b_ref[0]


def _adaln(cond8, w, b, layer):
    tn = 1536
    return pl.pallas_call(
        _adaln_kernel,
        grid=(6 * D_MODEL // tn,),
        in_specs=[pl.BlockSpec((8, D_MODEL), lambda j: (0, 0)),
                  pl.BlockSpec((1, D_MODEL, tn), lambda j: (layer, 0, j)),
                  pl.BlockSpec((1, 1, tn), lambda j: (layer, 0, j))],
        out_specs=pl.BlockSpec((8, tn), lambda j: (0, j)),
        out_shape=jax.ShapeDtypeStruct((8, 6 * D_MODEL), F32),
        compiler_params=_cparams(("arbitrary",)),
        name="adaln",
    )(cond8, w, b)


def _nmm_kernel(xlo_ref, xhi_ref, mod_ref, gain_ref, w_ref, o_ref, w_scr):
    i = pl.program_id(1)

    @pl.when(i == 0)
    def _():
        w_scr[...] = w_ref[...].astype(BF16)

    m = mod_ref[0]
    h = _rms(_load_rows(xlo_ref, xhi_ref, i)) * gain_ref[...] * (1.0 + m[1:2]) + m[0:1]
    o_ref[...] = jnp.dot(h.astype(BF16), w_scr[...], preferred_element_type=F32).astype(o_ref.dtype)


def _norm_mod_matmul(x_parts, joint, mods, gain, w, tn):
    n_out = w.shape[1]
    return pl.pallas_call(
        _nmm_kernel,
        grid=(n_out // tn, NT // TM),
        in_specs=[*_part_specs(D_MODEL, joint), _mod_spec(), _const_spec((1, D_MODEL)),
                  pl.BlockSpec((D_MODEL, tn), lambda j, i: (0, j))],
        out_specs=pl.BlockSpec((TM, tn), lambda j, i: (i, j)),
        out_shape=jax.ShapeDtypeStruct((NT, n_out), BF16),
        scratch_shapes=[pltpu.VMEM((D_MODEL, tn), BF16)],
        compiler_params=_cparams(("arbitrary", "arbitrary")),
        name="norm_mod_matmul",
    )(*x_parts, mods, gain, w)


def _ret_kernel(lg_ref, q_ref, k_ref, v_ref, g_ref, *rest, seq_len, has_init):
    if has_init:
        s0f_ref, s0b_ref, yg_ref, sf_ref, sb_ref, ycross, sf_scr, sb_scr = rest
    else:
        _, yg_ref, sf_ref, sb_ref, ycross, sf_scr, sb_scr = rest
    C = RET_CHUNK
    nc = seq_len // C
    head = pl.program_id(1)
    lgf = lg_ref[0, head]
    lgb = lg_ref[1, head]
    k_scale = RET_DK ** -0.5

    ii = lax.broadcasted_iota(jnp.int32, (C, C), 0).astype(F32)
    jj = lax.broadcasted_iota(jnp.int32, (C, C), 1).astype(F32)
    diff = ii - jj
    inner = jnp.where(diff >= 0, jnp.exp(diff * lgf), jnp.exp(-diff * lgb)) * k_scale
    pos = lax.broadcasted_iota(jnp.int32, (C, 1), 0).astype(F32)
    cross_f = jnp.exp((pos + 1.0) * lgf)
    cross_b = jnp.exp((C - pos) * lgb)
    sdec_f = jnp.exp((C - 1.0 - pos) * lgf) * k_scale
    sdec_b = jnp.exp(pos * lgb) * k_scale
    one = jnp.ones((1, 1), F32)
    cdec_f = jnp.exp(one * (C * lgf))
    cdec_b = jnp.exp(one * (C * lgb))

    if has_init:
        sf_scr[...] = s0f_ref[0, 0]
        sb_scr[...] = s0b_ref[0, 0]
    else:
        sf_scr[...] = jnp.zeros_like(sf_scr)
        sb_scr[...] = jnp.zeros_like(sb_scr)

    contract0 = (((0,), (0,)), ((), ()))
    contract1 = (((1,), (1,)), ((), ()))

    def bwd(t, carry):
        rows = pl.ds(pl.multiple_of((nc - 1 - t) * C, C), C)
        q = q_ref[rows, :]
        s = sb_scr[...]
        ycross[rows, :] = jnp.dot(q, s.astype(BF16), preferred_element_type=F32) * cross_b
        kd = (k_ref[rows, :].astype(F32) * sdec_b).astype(BF16)
        sb_scr[...] = s * cdec_b + lax.dot_general(kd, v_ref[rows, :], contract0,
                                                   preferred_element_type=F32)
        return carry

    lax.fori_loop(0, nc, bwd, 0)

    def fwd(c, carry):
        rows = pl.ds(pl.multiple_of(c * C, C), C)
        q = q_ref[rows, :]
        k = k_ref[rows, :]
        v = v_ref[rows, :]
        s = sf_scr[...]
        scores = lax.dot_general(q, k, contract1, preferred_element_type=F32) * inner
        y = (jnp.dot(scores.astype(BF16), v, preferred_element_type=F32)
             + jnp.dot(q, s.astype(BF16), preferred_element_type=F32) * cross_f
             + ycross[rows, :])
        mu = jnp.mean(y, axis=-1, keepdims=True)
        yc = y - mu
        yn = yc * lax.rsqrt(jnp.mean(yc * yc, axis=-1, keepdims=True) + NORM_EPS)
        g = g_ref[rows, :].astype(F32)
        yg_ref[rows, :] = (g * _sigmoid(g) * yn).astype(yg_ref.dtype)
        kd = (k.astype(F32) * sdec_f).astype(BF16)
        sf_scr[...] = s * cdec_f + lax.dot_general(kd, v, contract0, preferred_element_type=F32)
        return carry

    lax.fori_loop(0, nc, fwd, 0)
    sf_ref[0, 0] = sf_scr[...]
    sb_ref[0, 0] = sb_scr[...]


def _retention(qkvg, lg, s0f, s0b, yg_prev, *, n_req, seq_len, row0):
    rb0 = row0 // seq_len
    has_init = s0f is not None
    qk_spec = lambda off: pl.BlockSpec((seq_len, RET_DK), lambda b, h, lg_: (rb0 + b, off + h))
    vg_spec = lambda off: pl.BlockSpec((seq_len, RET_DV), lambda b, h, lg_: (rb0 + b, off + h))
    st_spec = pl.BlockSpec((1, 1, RET_DK, RET_DV), lambda b, h, lg_: (b, h, 0, 0))
    in_specs = [qk_spec(0), qk_spec(RET_HEADS), vg_spec(RET_HEADS), vg_spec(2 * RET_HEADS)]
    args = [qkvg, qkvg, qkvg, qkvg]
    if has_init:
        in_specs += [st_spec, st_spec]
        args += [s0f, s0b]
        aliases = {}
    else:
        in_specs.append(pl.BlockSpec(memory_space=pl.ANY))
        args.append(yg_prev)
        aliases = {len(args): 0}
    st_shape = jax.ShapeDtypeStruct((n_req, RET_HEADS, RET_DK, RET_DV), F32)
    return pl.pallas_call(
        functools.partial(_ret_kernel, seq_len=seq_len, has_init=has_init),
        grid_spec=pltpu.PrefetchScalarGridSpec(
            num_scalar_prefetch=1,
            grid=(n_req, RET_HEADS),
            in_specs=in_specs,
            out_specs=[pl.BlockSpec((seq_len, RET_DV), lambda b, h, lg_: (rb0 + b, h)), st_spec, st_spec],
            scratch_shapes=[pltpu.VMEM((seq_len, RET_DV), F32),
                            pltpu.VMEM((RET_DK, RET_DV), F32),
                            pltpu.VMEM((RET_DK, RET_DV), F32)]),
        out_shape=[jax.ShapeDtypeStruct((NT, RET_V_WIDTH), BF16), st_shape, st_shape],
        input_output_aliases=aliases,
        compiler_params=_cparams(("parallel", "arbitrary")),
        name=f"retention_{seq_len}",
    )(lg, *args)


def _split_bf16(x):
    hi = x.astype(BF16)
    return hi, (x - hi.astype(F32)).astype(BF16)


def _out_post_kernel(y_ref, w_ref, xlo_ref, xhi_ref, mod_ref, gpost_ref, gpre_ref, wr_ref,
                     x1_ref, h3_ref, aff_ref, w_scr):
    i = pl.program_id(0)

    @pl.when(i == 0)
    def _():
        w_scr[...] = w_ref[...].astype(BF16)

    m = mod_ref[0]
    y = jnp.dot(y_ref[...], w_scr[...], preferred_element_type=F32)
    x1 = _load_rows(xlo_ref, xhi_ref, i) + _rms(y) * gpost_ref[...] * m[2:3]
    x1_ref[...] = x1
    h = _rms(x1) * gpre_ref[...] * (1.0 + m[4:5]) + m[3:4]
    for c in range(ROW_TILES):
        h3_ref[pl.ds(c, TM, stride=ROW_TILES), :] = h[:, c * LANES:(c + 1) * LANES]
    contract1 = (((1,), (1,)), ((), ()))
    w_hi, w_lo = _split_bf16(wr_ref[...])
    h_hi, h_lo = _split_bf16(h)
    part = lax.dot_general(jnp.concatenate([w_hi, w_lo], axis=0), h_hi, contract1,
                           preferred_element_type=F32)
    logits = (part[:N_EXPERTS] + part[N_EXPERTS:]
              + lax.dot_general(w_hi, h_lo, contract1, preferred_element_type=F32))
    e = jnp.exp(logits - jnp.max(logits, axis=0, keepdims=True))
    aff_ref[...] = e / jnp.sum(e, axis=0, keepdims=True)


def _out_post(y, w, x_parts, joint, mods, gpost, gpre, w_router_t):
    kdim = y.shape[1]
    return pl.pallas_call(
        _out_post_kernel,
        grid=(NT // TM,),
        in_specs=[pl.BlockSpec((TM, kdim), lambda i: (i, 0)),
                  _const_spec((kdim, D_MODEL)),
                  *_part_specs(D_MODEL, joint), _mod_spec(),
                  _const_spec((1, D_MODEL)), _const_spec((1, D_MODEL)),
                  _const_spec((N_EXPERTS, D_MODEL))],
        out_specs=[pl.BlockSpec((TM, D_MODEL), lambda i: (i, 0)),
                   pl.BlockSpec((TM * ROW_TILES, LANES), lambda i: (i, 0)),
                   pl.BlockSpec((N_EXPERTS, TM), lambda i: (0, i))],
        out_shape=[jax.ShapeDtypeStruct((NT, D_MODEL), F32),
                   jax.ShapeDtypeStruct((NT * ROW_TILES, LANES), F32),
                   jax.ShapeDtypeStruct((N_EXPERTS, NT), F32)],
        scratch_shapes=[pltpu.VMEM((kdim, D_MODEL), BF16)],
        compiler_params=_cparams(("arbitrary",)),
        name="out_post",
    )(y, w, *x_parts, mods, gpost, gpre, w_router_t)


FF_SPLIT = 2
FF_TILE = EXPERT_FF // FF_SPLIT
SCATTER_BATCH = 8


def _moe_kernel(idx_prev_ref, idx_ref, idx_next_ref, aff_prev_ref, aff_ref, h3_ref, wg_ref, wu_ref, wd_ref,
                f3_ref, xg, x_a, x_b, yacc, y_a, y_b):
    e = pl.program_id(1)
    f = pl.program_id(2)

    def tile(r):
        return pl.ds(pl.multiple_of(r * SUBLANES, SUBLANES), SUBLANES)

    def gather(ids_ref, x_dst):
        for r in range(CAP):
            xg[r * SUBLANES:(r + 1) * SUBLANES, :] = h3_ref[tile(ids_ref[0, 0, r]), :]
        x_dst[...] = jnp.concatenate(
            [xg[pl.ds(c, CAP, stride=ROW_TILES), :] for c in range(ROW_TILES)], axis=1).astype(BF16)

    def scatter(ids_ref, gate_ref, y_src):
        for r0 in range(0, CAP, SCATTER_BATCH):
            dst, vals = [], []
            for r in range(r0, r0 + SCATTER_BATCH):
                t = ids_ref[0, 0, r]
                dst.append(tile(t))
                vals.append(f3_ref[tile(t), :] + gate_ref[0, 0, t] * y_src[r * SUBLANES:(r + 1) * SUBLANES, :])
            for d, v in zip(dst, vals):
                f3_ref[d, :] = v

    def half_ffn(x_src):
        x = x_src[...]
        a = jnp.dot(x, wg_ref[0, 0].astype(BF16), preferred_element_type=F32)
        u = jnp.dot(x, wu_ref[0, 0].astype(BF16), preferred_element_type=F32)
        act = (a * _sigmoid(a) * u).astype(BF16)
        return jnp.dot(act, wd_ref[0, 0].astype(BF16), preferred_element_type=F32)

    @pl.when(jnp.logical_and(e == 0, f == 0))
    def _():
        f3_ref[...] = jnp.zeros_like(f3_ref)
        y_b[...] = jnp.zeros_like(y_b)
        gather(idx_ref, x_a)

    for parity, (x_cur, x_nxt, y_cur, y_prv) in enumerate(((x_a, x_b, y_a, y_b), (x_b, x_a, y_b, y_a))):
        @pl.when(jnp.logical_and(e % 2 == parity, f == 0))
        def _():
            gather(idx_next_ref, x_nxt)
            yacc[...] = half_ffn(x_cur)

        @pl.when(jnp.logical_and(e % 2 == parity, f == 1))
        def _():
            y = yacc[...] + half_ffn(x_cur)
            for c in range(ROW_TILES):
                y_cur[pl.ds(c, CAP, stride=ROW_TILES), :] = y[:, c * LANES:(c + 1) * LANES]
            scatter(idx_prev_ref, aff_prev_ref, y_prv)

    @pl.when(jnp.logical_and(e == N_EXPERTS - 1, f == 1))
    def _():
        scatter(idx_ref, aff_ref, y_b)


def _moe(idx, aff_rows, h3, wg, wu, wd, layer):
    grp_rows = GT * ROW_TILES
    clamp = lambda e: jnp.clip(e, 0, N_EXPERTS - 1)
    idx_spec = lambda d: pl.BlockSpec((1, 1, CAP), lambda b, e, f: (b * N_EXPERTS + clamp(e + d), 0, 0),
                                      memory_space=pltpu.SMEM)
    aff_spec = lambda d: pl.BlockSpec((1, 1, GT), lambda b, e, f: (clamp(e + d) * NG + b, 0, 0),
                                      memory_space=pltpu.SMEM)
    return pl.pallas_call(
        _moe_kernel,
        grid=(NG, N_EXPERTS, FF_SPLIT),
        in_specs=[idx_spec(-1), idx_spec(0), idx_spec(1), aff_spec(-1), aff_spec(0),
                  pl.BlockSpec((grp_rows, LANES), lambda b, e, f: (b, 0), pipeline_mode=pl.Buffered(1)),
                  pl.BlockSpec((1, 1, D_MODEL, FF_TILE), lambda b, e, f: (layer, e, 0, f)),
                  pl.BlockSpec((1, 1, D_MODEL, FF_TILE), lambda b, e, f: (layer, e, 0, f)),
                  pl.BlockSpec((1, 1, FF_TILE, D_MODEL), lambda b, e, f: (layer, e, f, 0))],
        out_specs=pl.BlockSpec((grp_rows, LANES), lambda b, e, f: (b, 0), pipeline_mode=pl.Buffered(1)),
        out_shape=jax.ShapeDtypeStruct((NT * ROW_TILES, LANES), F32),
        scratch_shapes=[pltpu.VMEM((CAP * ROW_TILES, LANES), F32),
                        pltpu.VMEM((CAP, D_MODEL), BF16),
                        pltpu.VMEM((CAP, D_MODEL), BF16),
                        pltpu.VMEM((CAP, D_MODEL), F32),
                        pltpu.VMEM((CAP * ROW_TILES, LANES), F32),
                        pltpu.VMEM((CAP * ROW_TILES, LANES), F32)],
        compiler_params=_cparams(("arbitrary", "arbitrary", "arbitrary")),
        name="moe_experts",
    )(idx, idx, idx, aff_rows, aff_rows, h3, wg, wu, wd)


def _moe_post_kernel(f3_ref, x_ref, mod_ref, gpost_ref, *o_refs):
    i = pl.program_id(0)
    m = mod_ref[0]
    fx = jnp.concatenate([f3_ref[pl.ds(c, TM, stride=ROW_TILES), :] for c in range(ROW_TILES)], axis=1)
    out = x_ref[...] + _rms(fx) * gpost_ref[...] * m[5:6]
    if len(o_refs) == 1:
        o_refs[0][...] = out
    else:
        @pl.when(i < SAMPLE_TILES)
        def _():
            o_refs[0][...] = out

        @pl.when(i >= SAMPLE_TILES)
        def _():
            o_refs[1][...] = out


def _moe_post(f3, x, mods, gpost, split_out):
    if split_out:
        out_specs = _part_specs(D_MODEL, joint=False)
        out_shape = [jax.ShapeDtypeStruct((N_SAMPLE, D_MODEL), F32),
                     jax.ShapeDtypeStruct((N_PROMPT, D_MODEL), F32)]
    else:
        out_specs = pl.BlockSpec((TM, D_MODEL), lambda i: (i, 0))
        out_shape = jax.ShapeDtypeStruct((NT, D_MODEL), F32)
    return pl.pallas_call(
        _moe_post_kernel,
        grid=(NT // TM,),
        in_specs=[pl.BlockSpec((TM * ROW_TILES, LANES), lambda i: (i, 0)),
                  pl.BlockSpec((TM, D_MODEL), lambda i: (i, 0)),
                  _mod_spec(), _const_spec((1, D_MODEL))],
        out_specs=out_specs,
        out_shape=out_shape,
        compiler_params=_cparams(("arbitrary",)),
        name="moe_post",
    )(f3, x, mods, gpost)


def _route(aff_t):
    aff_s = jnp.swapaxes(aff_t[:, :N_SAMPLE].reshape(N_EXPERTS, DEC_BATCH, DEC_SEQ), 0, 1)
    aff_p = aff_t[:, N_SAMPLE:].reshape(N_EXPERTS, BATCH, SEQ)
    _, idx_s = lax.top_k(aff_s, 2 * DEC_SEQ // N_EXPERTS)
    _, idx_p = lax.top_k(aff_p, 2 * SEQ // N_EXPERTS)
    idx_p = idx_p + (jnp.arange(BATCH, dtype=jnp.int32) * SEQ)[None, :, None]
    idx = jnp.concatenate([idx_s, idx_p.reshape(1, N_EXPERTS, CAP)], axis=0).astype(jnp.int32)
    return idx.reshape(NG * N_EXPERTS, 1, CAP)


HD = MLA_HEADS * LANES
Q_SCALE = math.log2(math.e) / math.sqrt(MLA_NOPE + MLA_ROPE)


def _mla_proj_kernel(xlo_ref, xhi_ref, mod_ref, gain_ref, win_ref, qn_g_ref, kvn_g_ref, wq_ref, wkv_ref,
                     cos_ref, sin_ref, q_ref, kn_ref, v_ref, krp_ref, ckv_ref, kraw_ref):
    i = pl.program_id(0)
    m = mod_ref[0]
    h = (_rms(_load_rows(xlo_ref, xhi_ref, i)) * gain_ref[...] * (1.0 + m[1:2]) + m[0:1]).astype(BF16)
    lat = jnp.dot(h, win_ref[...], preferred_element_type=F32)
    q_lat = lat[:, :MLA_Q_LORA]
    kv_lat = lat[:, MLA_Q_LORA:MLA_Q_LORA + MLA_KV_LORA]
    kr = lat[:, MLA_Q_LORA + MLA_KV_LORA:MLA_Q_LORA + MLA_KV_LORA + LANES]
    kr_rot = lat[:, MLA_Q_LORA + MLA_KV_LORA + LANES:]
    is_latent = i < SAMPLE_TILES
    cos = jnp.where(is_latent, cos_ref[...], 1.0)
    sin = jnp.where(is_latent, sin_ref[...], 0.0)

    qln = (_rms(q_lat) * qn_g_ref[...]).astype(BF16)
    q = jnp.dot(qln, wq_ref[...], preferred_element_type=F32) * Q_SCALE
    for hd in range(MLA_HEADS):
        lo = HD + hd * LANES
        q_ref[:, 2 * hd * LANES:(2 * hd + 1) * LANES] = q[:, hd * LANES:(hd + 1) * LANES].astype(BF16)
        q_ref[:, (2 * hd + 1) * LANES:(2 * hd + 2) * LANES] = (
            q[:, lo:lo + LANES] * cos + q[:, HD + lo:HD + lo + LANES] * sin).astype(BF16)

    ckv = _rms(kv_lat) * kvn_g_ref[...]
    kv = jnp.dot(ckv.astype(BF16), wkv_ref[...], preferred_element_type=F32)
    kn_ref[...] = kv[:, :HD].astype(BF16)
    v_ref[...] = kv[:, HD:].astype(BF16)
    krp_ref[...] = (kr * cos + kr_rot * sin).astype(BF16)

    @pl.when(i >= SAMPLE_TILES)
    def _():
        ckv_ref[...] = ckv
        kraw_ref[...] = kr


def _mla_proj(x_parts, joint, mods, gain, win_ext, qn_g, kvn_g, wq_ext, wkv_ext, cos128, sin128):
    row = lambda w: pl.BlockSpec((TM, w), lambda i: (i, 0))
    ctx_row = lambda w: pl.BlockSpec((TM, w), lambda i: (jnp.maximum(i - SAMPLE_TILES, 0), 0))
    tab = pl.BlockSpec((TM, LANES), lambda i: (i % TILES_PER_GROUP, 0))
    return pl.pallas_call(
        _mla_proj_kernel,
        grid=(NT // TM,),
        in_specs=[*_part_specs(D_MODEL, joint), _mod_spec(),
                  _const_spec(gain.shape), _const_spec(win_ext.shape), _const_spec(qn_g.shape),
                  _const_spec(kvn_g.shape), _const_spec(wq_ext.shape), _const_spec(wkv_ext.shape),
                  tab, tab],
        out_specs=[row(2 * HD), row(HD), row(HD), row(LANES), ctx_row(MLA_KV_LORA), ctx_row(LANES)],
        out_shape=[jax.ShapeDtypeStruct((NT, 2 * HD), BF16),
                   jax.ShapeDtypeStruct((NT, HD), BF16),
                   jax.ShapeDtypeStruct((NT, HD), BF16),
                   jax.ShapeDtypeStruct((NT, LANES), BF16),
                   jax.ShapeDtypeStruct((N_PROMPT, MLA_KV_LORA), F32),
                   jax.ShapeDtypeStruct((N_PROMPT, LANES), F32)],
        compiler_params=_cparams(("arbitrary",)),
        name="mla_proj",
    )(*x_parts, mods, gain, win_ext, qn_g, kvn_g, wq_ext, wkv_ext, cos128, sin128)


def _matmul_kernel(a_ref, w_ref, o_ref):
    o_ref[...] = jnp.dot(a_ref[...].astype(BF16), w_ref[...],
                         preferred_element_type=F32).astype(o_ref.dtype)


def _ctx_expand(ckv_ctx, wkv_ext):
    n = ckv_ctx.shape[0]
    return pl.pallas_call(
        _matmul_kernel,
        grid=(n // TM,),
        in_specs=[pl.BlockSpec((TM, MLA_KV_LORA), lambda i: (i, 0)),
                  pl.BlockSpec(wkv_ext.shape, lambda i: (0, 0))],
        out_specs=pl.BlockSpec((TM, 2 * HD), lambda i: (i, 0)),
        out_shape=jax.ShapeDtypeStruct((n, 2 * HD), BF16),
        compiler_params=_cparams(("parallel",)),
        name="ctx_expand",
    )(ckv_ctx, wkv_ext)


ATTN_HEADS_PER_STEP = 2
ATTN_CHUNK_UNROLL = 8


def _attn_kernel(q_ref, kn_ref, krp_ref, v_ref, *rest, tk, n_chunks, has_ctx):
    if has_ctx:
        knc_ref, krpc_ref, vc_ref, o_ref = rest
    else:
        _, o_ref = rest
    tq = q_ref.shape[0]
    contract1 = (((1,), (1,)), ((), ()))
    heads = range(ATTN_HEADS_PER_STEP)
    head_cols = lambda hd: slice(hd * LANES, (hd + 1) * LANES)
    qs = [q_ref[:, 2 * hd * LANES:2 * (hd + 1) * LANES] for hd in heads]

    def ones_column(rows):
        return (lax.broadcasted_iota(jnp.int32, (rows, LANES), 1) == 0).astype(BF16)

    def step(q, kn, krp, v, ones, carry):
        m, acc = carry
        kc = jnp.concatenate([kn, krp], axis=1)
        s = lax.dot_general(q, kc, contract1, preferred_element_type=F32)
        m_new = jnp.maximum(m, jnp.max(s, axis=-1, keepdims=True))
        p = jnp.exp2(s - m_new).astype(BF16)
        pv = jnp.dot(p, jnp.concatenate([v, ones], axis=1), preferred_element_type=F32)
        return m_new, jnp.exp2(m - m_new) * acc + pv

    carry = tuple((jnp.full((tq, 1), -jnp.inf, F32), jnp.zeros((tq, 2 * LANES), F32)) for _ in heads)
    if has_ctx:
        ones = ones_column(PAST_LEN)
        carry = tuple(step(qs[hd], knc_ref[0, :, head_cols(hd)], krpc_ref[0], vc_ref[0, :, head_cols(hd)],
                           ones, carry[hd]) for hd in heads)
    ones = ones_column(tk)

    def chunk(c, carry):
        rows = pl.ds(pl.multiple_of(c * tk, tk), tk)
        krp = krp_ref[rows, :]
        return tuple(step(qs[hd], kn_ref[rows, head_cols(hd)], krp, v_ref[rows, head_cols(hd)],
                          ones, carry[hd]) for hd in heads)

    unroll = math.gcd(n_chunks, ATTN_CHUNK_UNROLL)

    def body(c, carry):
        for u in range(unroll):
            carry = chunk(c * unroll + u, carry)
        return carry

    carry = lax.fori_loop(0, n_chunks // unroll, body, carry)
    for hd in heads:
        acc = carry[hd][1]
        o_ref[:, head_cols(hd)] = (acc[:, :MLA_V] / acc[:, MLA_V:MLA_V + 1]).astype(o_ref.dtype)


def _attention(q, kn, krp, v, ctx, o_prev, *, n_req, seq_len, row0, tq, tk):
    rb0 = row0 // seq_len
    qb0 = row0 // tq
    nq = seq_len // tq
    hw = ATTN_HEADS_PER_STEP * LANES
    o_spec = pl.BlockSpec((tq, hw), lambda b, h, i: (qb0 + b * nq + i, h))
    k_spec = pl.BlockSpec((seq_len, hw), lambda b, h, i: (rb0 + b, h))
    in_specs = [pl.BlockSpec((tq, 2 * hw), lambda b, h, i: (qb0 + b * nq + i, h)), k_spec,
                pl.BlockSpec((seq_len, LANES), lambda b, h, i: (rb0 + b, 0)), k_spec]
    args = [q, kn, krp, v]
    if ctx is not None:
        c_spec = pl.BlockSpec((1, PAST_LEN, hw), lambda b, h, i: (b, 0, h))
        in_specs += [c_spec, pl.BlockSpec((1, PAST_LEN, LANES), lambda b, h, i: (b, 0, 0)), c_spec]
        args += list(ctx)
        aliases = {}
    else:
        in_specs.append(pl.BlockSpec(memory_space=pl.ANY))
        args.append(o_prev)
        aliases = {len(args) - 1: 0}
    return pl.pallas_call(
        functools.partial(_attn_kernel, tk=tk, n_chunks=seq_len // tk, has_ctx=ctx is not None),
        grid=(n_req, MLA_HEADS // ATTN_HEADS_PER_STEP, nq),
        in_specs=in_specs,
        out_specs=o_spec,
        out_shape=jax.ShapeDtypeStruct((NT, HD), BF16),
        input_output_aliases=aliases,
        compiler_params=_cparams(("parallel", "parallel", "arbitrary")),
        name=f"attention_{seq_len}",
    )(*args)


def _rope_rot_cols(w):
    w1, w2, w3, w4 = jnp.split(w, 4, axis=-1)
    return jnp.concatenate([-w2, w1, -w4, w3], axis=-1)


def _pad_cols(w, width):
    return jnp.pad(w, ((0, 0), (0, width - w.shape[1])))


def _mla_weights(w_in, w_q_b, w_kv_b):
    w_rope = w_in[:, MLA_Q_LORA + MLA_KV_LORA:]
    win_ext = jnp.concatenate([w_in[:, :MLA_Q_LORA + MLA_KV_LORA], _pad_cols(w_rope, LANES),
                               _pad_cols(_rope_rot_cols(w_rope), LANES)], axis=1).astype(BF16)
    wq = w_q_b.reshape(MLA_Q_LORA, MLA_HEADS, MLA_NOPE + MLA_ROPE)
    wq_nope = wq[:, :, :MLA_NOPE].reshape(MLA_Q_LORA, HD)
    wq_rope = wq[:, :, MLA_NOPE:]
    pad = ((0, 0), (0, 0), (0, LANES - MLA_ROPE))
    wq_rope_p = jnp.pad(wq_rope, pad).reshape(MLA_Q_LORA, HD)
    wq_rot_p = jnp.pad(_rope_rot_cols(wq_rope), pad).reshape(MLA_Q_LORA, HD)
    wq_ext = jnp.concatenate([wq_nope, wq_rope_p, wq_rot_p], axis=1).astype(BF16)
    wkv = w_kv_b.reshape(MLA_KV_LORA, MLA_HEADS, MLA_NOPE + MLA_V)
    wkv_ext = jnp.concatenate([wkv[:, :, :MLA_NOPE].reshape(MLA_KV_LORA, HD),
                               wkv[:, :, MLA_NOPE:].reshape(MLA_KV_LORA, HD)], axis=1).astype(BF16)
    return win_ext, wq_ext, wkv_ext


def _rope_tables():
    rows = DEC_SEQ // GRID_W
    row = jnp.repeat(jnp.arange(rows, dtype=F32), GRID_W)
    col = jnp.tile(jnp.arange(GRID_W, dtype=F32), rows)
    half = MLA_ROPE // 2
    inv = 1.0 / (ROPE_BASE ** (jnp.arange(0, half, 2, dtype=F32) / half))
    ar = row[:, None] * inv
    ac = col[:, None] * inv
    ang = jnp.concatenate([ar, ar, ac, ac] * 2, axis=-1)
    return jnp.cos(ang), jnp.sin(ang)


def kernel(x_prompt, x_sample, state_ret_fwd, state_ret_bwd, cache_mla_ckv, cache_mla_krope, c, c_ctx,
           ada_w, ada_b, norm_pre, norm_post, ret_w_in, ret_decay_fwd, ret_decay_bwd, ret_w_out,
           mla_w_in, mla_q_norm, mla_kv_norm, mla_w_q_b, mla_w_kv_b, mla_w_out,
           moe_w_router, moe_w_gate, moe_w_up, moe_w_down):
    x_parts = (x_sample.reshape(N_SAMPLE, D_MODEL), x_prompt.reshape(N_PROMPT, D_MODEL))
    joint = False
    cond8 = jnp.concatenate([c, c_ctx[None, :], jnp.zeros((8 - NG, D_MODEL), F32)], axis=0)
    cos128, sin128 = _rope_tables()
    outs = {}

    for i in range(DEPTH):
        mods = _adaln(cond8, ada_w, ada_b.reshape(DEPTH, 1, 6 * D_MODEL), i)[:NG].reshape(NG, 6, D_MODEL)
        gpre1 = norm_pre[i, 0][None, :]
        gpre2 = norm_pre[i, 1][None, :]
        gpost1 = norm_post[i, 0][None, :]
        gpost2 = norm_post[i, 1][None, :]
        if i % 2 == 0:
            r = i // 2
            qkvg = _norm_mod_matmul(x_parts, joint, mods, gpre1, ret_w_in[r], 1536)
            lg = jnp.stack([jax.nn.log_sigmoid(ret_decay_fwd[r].astype(F32)),
                            jax.nn.log_sigmoid(ret_decay_bwd[r].astype(F32))])
            yg, _, _ = _retention(qkvg, lg, state_ret_fwd[:, r], state_ret_bwd[:, r], None,
                                  n_req=DEC_BATCH, seq_len=DEC_SEQ, row0=0)
            yg, s_f, s_b = _retention(qkvg, lg, None, None, yg,
                                      n_req=BATCH, seq_len=SEQ, row0=N_SAMPLE)
            outs["fwd"], outs["bwd"] = s_f[:, None], s_b[:, None]
            mix, w_out = yg, ret_w_out[r]
        else:
            mi = i // 2
            win_ext, wq_ext, wkv_ext = _mla_weights(mla_w_in[mi], mla_w_q_b[mi], mla_w_kv_b[mi])
            q, kn, v, krp, ckv, kraw = _mla_proj(
                x_parts, joint, mods, gpre1, win_ext, mla_q_norm[mi][None, :], mla_kv_norm[mi][None, :],
                wq_ext, wkv_ext, cos128, sin128)
            outs["ckv"] = ckv.reshape(BATCH, 1, SEQ, MLA_KV_LORA)
            outs["krope"] = kraw[:, :MLA_ROPE].reshape(BATCH, 1, SEQ, MLA_ROPE)
            kvc = _ctx_expand(cache_mla_ckv[:, mi].reshape(DEC_BATCH * PAST_LEN, MLA_KV_LORA), wkv_ext)
            kvc = kvc.reshape(DEC_BATCH, PAST_LEN, 2 * HD)
            krpc = jnp.pad(cache_mla_krope[:, mi], ((0, 0), (0, 0), (0, LANES - MLA_ROPE))).astype(BF16)
            o = _attention(q, kn, krp, v, (kvc[:, :, :HD], krpc, kvc[:, :, HD:]), None,
                           n_req=DEC_BATCH, seq_len=DEC_SEQ, row0=0, tq=512, tk=512)
            o = _attention(q, kn, krp, v, None, o,
                           n_req=BATCH, seq_len=SEQ, row0=N_SAMPLE, tq=SEQ, tk=SEQ)
            mix, w_out = o, mla_w_out[mi]
        x1, h3, aff_t = _out_post(mix, w_out, x_parts, joint, mods, gpost1, gpre2, moe_w_router[i].T)
        idx = _route(aff_t)
        f3 = _moe(idx, aff_t.reshape(N_EXPERTS * NG, 1, GT), h3, moe_w_gate, moe_w_up, moe_w_down, i)
        last = i == DEPTH - 1
        x_new = _moe_post(f3, x1, mods, gpost2, split_out=last)
        if not last:
            x_parts, joint = (x_new, x_new), True

    y_sample, y_prompt = x_new
    return (y_prompt.reshape(BATCH, SEQ, D_MODEL), y_sample.reshape(DEC_BATCH, DEC_SEQ, D_MODEL),
            outs["fwd"], outs["bwd"], outs["ckv"], outs["krope"])
```

```python
import functools
import math

import jax
import jax.numpy as jnp
from jax import lax
from jax.experimental import pallas as pl
from jax.experimental.pallas import tpu as pltpu

F32 = jnp.float32
BF16 = jnp.bfloat16

D_MODEL = 1024
BATCH = 16
SEQ = 256
DEPTH = 2
DEC_BATCH = 4
DEC_SEQ = 4096
PAST_LEN = 256
GRID_W = 64
RET_HEADS = 4
RET_DK = 256
RET_DV = 512
RET_QK_WIDTH = RET_HEADS * RET_DK
RET_V_WIDTH = RET_HEADS * RET_DV
RET_CHUNK = 128
MLA_HEADS = 8
MLA_NOPE = 128
MLA_ROPE = 64
MLA_V = 128
MLA_Q_LORA = 384
MLA_KV_LORA = 256
ROPE_BASE = 10000.0
N_EXPERTS = 16
EXPERT_FF = 1024
NORM_EPS = 1e-6

GT = DEC_SEQ
NG = DEC_BATCH + 1
NT = NG * GT
N_SAMPLE = DEC_BATCH * DEC_SEQ
N_PROMPT = BATCH * SEQ
CAP = 2 * GT // N_EXPERTS

LANES = 128
SUBLANES = 8
ROW_TILES = D_MODEL // LANES
TM = 512
TILES_PER_GROUP = GT // TM
SAMPLE_TILES = N_SAMPLE // TM
VMEM_LIMIT = 60 * 1024 * 1024


def _cparams(sem):
    return pltpu.CompilerParams(dimension_semantics=sem, vmem_limit_bytes=VMEM_LIMIT)


def _rms(x):
    return x * lax.rsqrt(jnp.mean(x * x, axis=-1, keepdims=True) + NORM_EPS)


def _sigmoid(x):
    return 1.0 / (1.0 + jnp.exp(-x))


def _part_specs(width, joint, split=SAMPLE_TILES, rows=TM):
    lo = pl.BlockSpec((rows, width), lambda *g: (jnp.minimum(g[-1], split - 1), 0))
    if joint:
        hi = pl.BlockSpec((rows, width), lambda *g: (jnp.maximum(g[-1], split), 0))
    else:
        hi = pl.BlockSpec((rows, width), lambda *g: (jnp.maximum(g[-1] - split, 0), 0))
    return [lo, hi]


def _load_rows(lo_ref, hi_ref, i, split=SAMPLE_TILES):
    return jnp.where(i < split, lo_ref[...], hi_ref[...])


def _mod_spec():
    return pl.BlockSpec((1, 6, D_MODEL), lambda *g: (g[-1] // TILES_PER_GROUP, 0, 0))


def _const_spec(shape):
    return pl.BlockSpec(shape, lambda *g: (0,) * len(shape))


def _adaln_kernel(c_ref, w_ref, b_ref, o_ref):
    c = c_ref[...]
    s = (c * _sigmoid(c)).astype(BF16)
    o_ref[...] = jnp.dot(s, w_ref[0].astype(BF16), preferred_element_type=F32) + b_ref[0]


def _adaln(cond8, w, b, layer):
    tn = 1536
    return pl.pallas_call(
        _adaln_kernel,
        grid=(6 * D_MODEL // tn,),
        in_specs=[pl.BlockSpec((8, D_MODEL), lambda j: (0, 0)),
                  pl.BlockSpec((1, D_MODEL, tn), lambda j: (layer, 0, j)),
                  pl.BlockSpec((1, 1, tn), lambda j: (layer, 0, j))],
        out_specs=pl.BlockSpec((8, tn), lambda j: (0, j)),
        out_shape=jax.ShapeDtypeStruct((8, 6 * D_MODEL), F32),
        compiler_params=_cparams(("arbitrary",)),
        name="adaln",
    )(cond8, w, b)


def _nmm_kernel(xlo_ref, xhi_ref, mod_ref, gain_ref, w_ref, o_ref, w_scr):
    i = pl.program_id(1)

    @pl.when(i == 0)
    def _():
        w_scr[...] = w_ref[...].astype(BF16)

    m = mod_ref[0]
    h = _rms(_load_rows(xlo_ref, xhi_ref, i)) * gain_ref[...] * (1.0 + m[1:2]) + m[0:1]
    o_ref[...] = jnp.dot(h.astype(BF16), w_scr[...], preferred_element_type=F32).astype(o_ref.dtype)


def _norm_mod_matmul(x_parts, joint, mods, gain, w, tn):
    n_out = w.shape[1]
    return pl.pallas_call(
        _nmm_kernel,
        grid=(n_out // tn, NT // TM),
        in_specs=[*_part_specs(D_MODEL, joint), _mod_spec(), _const_spec((1, D_MODEL)),
                  pl.BlockSpec((D_MODEL, tn), lambda j, i: (0, j))],
        out_specs=pl.BlockSpec((TM, tn), lambda j, i: (i, j)),
        out_shape=jax.ShapeDtypeStruct((NT, n_out), BF16),
        scratch_shapes=[pltpu.VMEM((D_MODEL, tn), BF16)],
        compiler_params=_cparams(("arbitrary", "arbitrary")),
        name="norm_mod_matmul",
    )(*x_parts, mods, gain, w)


def _ret_kernel(lg_ref, q_ref, k_ref, v_ref, g_ref, *rest, seq_len, has_init):
    if has_init:
        s0f_ref, s0b_ref, *rest = rest
    yg_ref, sf_ref, sb_ref, ycross, sf_scr, sb_scr = rest
    C = RET_CHUNK
    nc = seq_len // C
    head = pl.program_id(1)
    lgf = lg_ref[0, head]
    lgb = lg_ref[1, head]
    k_scale = RET_DK ** -0.5

    ii = lax.broadcasted_iota(jnp.int32, (C, C), 0).astype(F32)
    jj = lax.broadcasted_iota(jnp.int32, (C, C), 1).astype(F32)
    diff = ii - jj
    inner = jnp.where(diff >= 0, jnp.exp(diff * lgf), jnp.exp(-diff * lgb)) * k_scale
    pos = lax.broadcasted_iota(jnp.int32, (C, 1), 0).astype(F32)
    cross_f = jnp.exp((pos + 1.0) * lgf)
    cross_b = jnp.exp((C - pos) * lgb)
    sdec_f = jnp.exp((C - 1.0 - pos) * lgf) * k_scale
    sdec_b = jnp.exp(pos * lgb) * k_scale
    one = jnp.ones((1, 1), F32)
    cdec_f = jnp.exp(one * (C * lgf))
    cdec_b = jnp.exp(one * (C * lgb))

    if has_init:
        sf_scr[...] = s0f_ref[0, 0]
        sb_scr[...] = s0b_ref[0, 0]
    else:
        sf_scr[...] = jnp.zeros_like(sf_scr)
        sb_scr[...] = jnp.zeros_like(sb_scr)

    contract0 = (((0,), (0,)), ((), ()))
    contract1 = (((1,), (1,)), ((), ()))

    def bwd(t, carry):
        rows = pl.ds(pl.multiple_of((nc - 1 - t) * C, C), C)
        q = q_ref[rows, :]
        s = sb_scr[...]
        ycross[rows, :] = jnp.dot(q, s.astype(BF16), preferred_element_type=F32) * cross_b
        kd = (k_ref[rows, :].astype(F32) * sdec_b).astype(BF16)
        sb_scr[...] = s * cdec_b + lax.dot_general(kd, v_ref[rows, :], contract0,
                                                   preferred_element_type=F32)
        return carry

    lax.fori_loop(0, nc, bwd, 0)

    def fwd(c, carry):
        rows = pl.ds(pl.multiple_of(c * C, C), C)
        q = q_ref[rows, :]
        k = k_ref[rows, :]
        v = v_ref[rows, :]
        s = sf_scr[...]
        scores = lax.dot_general(q, k, contract1, preferred_element_type=F32) * inner
        y = (jnp.dot(scores.astype(BF16), v, preferred_element_type=F32)
             + jnp.dot(q, s.astype(BF16), preferred_element_type=F32) * cross_f
             + ycross[rows, :])
        mu = jnp.mean(y, axis=-1, keepdims=True)
        yc = y - mu
        yn = yc * lax.rsqrt(jnp.mean(yc * yc, axis=-1, keepdims=True) + NORM_EPS)
        g = g_ref[rows, :].astype(F32)
        yg_ref[rows, :] = (g * _sigmoid(g) * yn).astype(yg_ref.dtype)
        kd = (k.astype(F32) * sdec_f).astype(BF16)
        sf_scr[...] = s * cdec_f + lax.dot_general(kd, v, contract0, preferred_element_type=F32)
        return carry

    lax.fori_loop(0, nc, fwd, 0)
    sf_ref[0, 0] = sf_scr[...]
    sb_ref[0, 0] = sb_scr[...]


def _retention(qkvg, lg, s0f, s0b, *, n_req, seq_len, row0):
    rb0 = row0 // seq_len
    has_init = s0f is not None
    qk_spec = lambda off: pl.BlockSpec((seq_len, RET_DK), lambda b, h, lg_: (rb0 + b, off + h))
    vg_spec = lambda off: pl.BlockSpec((seq_len, RET_DV), lambda b, h, lg_: (rb0 + b, off + h))
    st_spec = pl.BlockSpec((1, 1, RET_DK, RET_DV), lambda b, h, lg_: (b, h, 0, 0))
    in_specs = [qk_spec(0), qk_spec(RET_HEADS), vg_spec(RET_HEADS), vg_spec(2 * RET_HEADS)]
    args = [qkvg, qkvg, qkvg, qkvg]
    if has_init:
        in_specs += [st_spec, st_spec]
        args += [s0f, s0b]
    st_shape = jax.ShapeDtypeStruct((n_req, RET_HEADS, RET_DK, RET_DV), F32)
    return pl.pallas_call(
        functools.partial(_ret_kernel, seq_len=seq_len, has_init=has_init),
        grid_spec=pltpu.PrefetchScalarGridSpec(
            num_scalar_prefetch=1,
            grid=(n_req, RET_HEADS),
            in_specs=in_specs,
            out_specs=[pl.BlockSpec((seq_len, RET_DV), lambda b, h, lg_: (b, h)), st_spec, st_spec],
            scratch_shapes=[pltpu.VMEM((seq_len, RET_DV), F32),
                            pltpu.VMEM((RET_DK, RET_DV), F32),
                            pltpu.VMEM((RET_DK, RET_DV), F32)]),
        out_shape=[jax.ShapeDtypeStruct((n_req * seq_len, RET_V_WIDTH), BF16), st_shape, st_shape],
        compiler_params=_cparams(("parallel", "arbitrary")),
        name=f"retention_{seq_len}",
    )(lg, *args)


def _split_bf16(x):
    hi = x.astype(BF16)
    return hi, (x - hi.astype(F32)).astype(BF16)


def _out_post_kernel(ylo_ref, yhi_ref, w_ref, xlo_ref, xhi_ref, mod_ref, gpost_ref, gpre_ref, wr_ref,
                     x1_ref, h3_ref, aff_ref, w_scr):
    i = pl.program_id(0)

    @pl.when(i == 0)
    def _():
        w_scr[...] = w_ref[...].astype(BF16)

    m = mod_ref[0]
    y = jnp.dot(_load_rows(ylo_ref, yhi_ref, i), w_scr[...], preferred_element_type=F32)
    x1 = _load_rows(xlo_ref, xhi_ref, i) + _rms(y) * gpost_ref[...] * m[2:3]
    x1_ref[...] = x1
    h = _rms(x1) * gpre_ref[...] * (1.0 + m[4:5]) + m[3:4]
    for c in range(ROW_TILES):
        h3_ref[pl.ds(c, TM, stride=ROW_TILES), :] = h[:, c * LANES:(c + 1) * LANES]
    contract1 = (((1,), (1,)), ((), ()))
    w_hi, w_lo = _split_bf16(wr_ref[...])
    h_hi, h_lo = _split_bf16(h)
    part = lax.dot_general(jnp.concatenate([w_hi, w_lo], axis=0), h_hi, contract1,
                           preferred_element_type=F32)
    logits = (part[:N_EXPERTS] + part[N_EXPERTS:]
              + lax.dot_general(w_hi, h_lo, contract1, preferred_element_type=F32))
    e = jnp.exp(logits - jnp.max(logits, axis=0, keepdims=True))
    aff_ref[...] = e / jnp.sum(e, axis=0, keepdims=True)


def _out_post(y_parts, w, x_parts, joint, mods, gpost, gpre, w_router_t):
    kdim = w.shape[0]
    return pl.pallas_call(
        _out_post_kernel,
        grid=(NT // TM,),
        in_specs=[*_part_specs(kdim, joint=False),
                  _const_spec((kdim, D_MODEL)),
                  *_part_specs(D_MODEL, joint), _mod_spec(),
                  _const_spec((1, D_MODEL)), _const_spec((1, D_MODEL)),
                  _const_spec((N_EXPERTS, D_MODEL))],
        out_specs=[pl.BlockSpec((TM, D_MODEL), lambda i: (i, 0)),
                   pl.BlockSpec((TM * ROW_TILES, LANES), lambda i: (i, 0)),
                   pl.BlockSpec((N_EXPERTS, TM), lambda i: (0, i))],
        out_shape=[jax.ShapeDtypeStruct((NT, D_MODEL), F32),
                   jax.ShapeDtypeStruct((NT * ROW_TILES, LANES), F32),
                   jax.ShapeDtypeStruct((N_EXPERTS, NT), F32)],
        scratch_shapes=[pltpu.VMEM((kdim, D_MODEL), BF16)],
        compiler_params=_cparams(("arbitrary",)),
        name="out_post",
    )(*y_parts, w, *x_parts, mods, gpost, gpre, w_router_t)


SCATTER_BATCH = 8
MOE_FIRST_GROUPS = 1
MOE_FIRST_FF_SPLIT = 4
MOE_REST_FF_SPLIT = 2


def _moe_kernel(idx_prev_ref, idx_ref, idx_next_ref, aff_prev_ref, aff_ref, h3_ref, wg_ref, wu_ref, wd_ref,
                f3_ref, *rest, emit_bf16, ff_split):
    if emit_bf16:
        wg16_ref, wu16_ref, wd16_ref, *rest = rest
    xg, x_a, x_b, yacc, y_a, y_b = rest
    e = pl.program_id(1)
    f = pl.program_id(2)

    def tile(r):
        return pl.ds(pl.multiple_of(r * SUBLANES, SUBLANES), SUBLANES)

    def gather(ids_ref, x_dst):
        for r in range(CAP):
            xg[r * SUBLANES:(r + 1) * SUBLANES, :] = h3_ref[tile(ids_ref[0, 0, r]), :]
        x_dst[...] = jnp.concatenate(
            [xg[pl.ds(c, CAP, stride=ROW_TILES), :] for c in range(ROW_TILES)], axis=1).astype(BF16)

    def scatter(ids_ref, gate_ref, y_src):
        for r0 in range(0, CAP, SCATTER_BATCH):
            dst, vals = [], []
            for r in range(r0, r0 + SCATTER_BATCH):
                t = ids_ref[0, 0, r]
                dst.append(tile(t))
                vals.append(f3_ref[tile(t), :] + gate_ref[0, 0, t] * y_src[r * SUBLANES:(r + 1) * SUBLANES, :])
            for d, v in zip(dst, vals):
                f3_ref[d, :] = v

    def half_ffn(x_src):
        x = x_src[...]
        wg = wg_ref[0, 0].astype(BF16)
        wu = wu_ref[0, 0].astype(BF16)
        wd = wd_ref[0, 0].astype(BF16)
        if emit_bf16:
            wg16_ref[0, 0] = wg
            wu16_ref[0, 0] = wu
            wd16_ref[0, 0] = wd
        a = jnp.dot(x, wg, preferred_element_type=F32)
        u = jnp.dot(x, wu, preferred_element_type=F32)
        act = (a * _sigmoid(a) * u).astype(BF16)
        return jnp.dot(act, wd, preferred_element_type=F32)

    first = f == 0
    last = f == ff_split - 1

    @pl.when(jnp.logical_and(e == 0, first))
    def _():
        f3_ref[...] = jnp.zeros_like(f3_ref)
        y_b[...] = jnp.zeros_like(y_b)
        gather(idx_ref, x_a)

    if ff_split == 1:
        phases = (("only", first),)
    elif ff_split == 2:
        phases = (("first", first), ("last", last))
    else:
        phases = (("first", first), ("middle", jnp.logical_not(jnp.logical_or(first, last))), ("last", last))
    for parity, (x_cur, x_nxt, y_cur, y_prv) in enumerate(((x_a, x_b, y_a, y_b), (x_b, x_a, y_b, y_a))):
        for phase, cond in phases:
            @pl.when(jnp.logical_and(e % 2 == parity, cond))
            def _():
                if phase in ("only", "first"):
                    gather(idx_next_ref, x_nxt)
                part = half_ffn(x_cur)
                if phase == "first":
                    yacc[...] = part
                elif phase == "middle":
                    yacc[...] += part
                else:
                    y = part if phase == "only" else yacc[...] + part
                    for c in range(ROW_TILES):
                        y_cur[pl.ds(c, CAP, stride=ROW_TILES), :] = y[:, c * LANES:(c + 1) * LANES]
                    scatter(idx_prev_ref, aff_prev_ref, y_prv)

    @pl.when(jnp.logical_and(e == N_EXPERTS - 1, last))
    def _():
        scatter(idx_ref, aff_ref, y_b)


def _moe(idx, aff_rows, h3, wg, wu, wd, layer, *, group0, n_groups, emit_bf16, ff_split):
    grp_rows = GT * ROW_TILES
    ff_tile = EXPERT_FF // ff_split
    clamp = lambda e: jnp.clip(e, 0, N_EXPERTS - 1)
    idx_spec = lambda d: pl.BlockSpec(
        (1, 1, CAP), lambda b, e, f: ((group0 + b) * N_EXPERTS + clamp(e + d), 0, 0), memory_space=pltpu.SMEM)
    aff_spec = lambda d: pl.BlockSpec(
        (1, 1, GT), lambda b, e, f: (clamp(e + d) * NG + group0 + b, 0, 0), memory_space=pltpu.SMEM)
    w_in_spec = lambda l: pl.BlockSpec((1, 1, D_MODEL, ff_tile), lambda b, e, f: (l, e, 0, f))
    w_out_spec = lambda l: pl.BlockSpec((1, 1, ff_tile, D_MODEL), lambda b, e, f: (l, e, f, 0))
    out_specs = [pl.BlockSpec((grp_rows, LANES), lambda b, e, f: (b, 0), pipeline_mode=pl.Buffered(1))]
    out_shape = [jax.ShapeDtypeStruct((n_groups * grp_rows, LANES), F32)]
    if emit_bf16:
        out_specs += [w_in_spec(0), w_in_spec(0), w_out_spec(0)]
        out_shape += [jax.ShapeDtypeStruct((1, N_EXPERTS, D_MODEL, EXPERT_FF), BF16),
                      jax.ShapeDtypeStruct((1, N_EXPERTS, D_MODEL, EXPERT_FF), BF16),
                      jax.ShapeDtypeStruct((1, N_EXPERTS, EXPERT_FF, D_MODEL), BF16)]
    return pl.pallas_call(
        functools.partial(_moe_kernel, emit_bf16=emit_bf16, ff_split=ff_split),
        grid=(n_groups, N_EXPERTS, ff_split),
        in_specs=[idx_spec(-1), idx_spec(0), idx_spec(1), aff_spec(-1), aff_spec(0),
                  pl.BlockSpec((grp_rows, LANES), lambda b, e, f: (group0 + b, 0),
                               pipeline_mode=pl.Buffered(1)),
                  w_in_spec(layer), w_in_spec(layer), w_out_spec(layer)],
        out_specs=out_specs,
        out_shape=out_shape,
        scratch_shapes=[pltpu.VMEM((CAP * ROW_TILES, LANES), F32),
                        pltpu.VMEM((CAP, D_MODEL), BF16),
                        pltpu.VMEM((CAP, D_MODEL), BF16),
                        pltpu.VMEM((CAP, D_MODEL), F32),
                        pltpu.VMEM((CAP * ROW_TILES, LANES), F32),
                        pltpu.VMEM((CAP * ROW_TILES, LANES), F32)],
        compiler_params=_cparams(("arbitrary", "arbitrary", "arbitrary")),
        name="moe_experts",
    )(idx, idx, idx, aff_rows, aff_rows, h3, wg, wu, wd)


def _moe_post_kernel(f3lo_ref, f3hi_ref, x_ref, mod_ref, gpost_ref, *o_refs):
    i = pl.program_id(0)
    m = mod_ref[0]
    rows = lambda ref: jnp.concatenate(
        [ref[pl.ds(c, TM, stride=ROW_TILES), :] for c in range(ROW_TILES)], axis=1)
    fx = jnp.where(i < MOE_FIRST_GROUPS * TILES_PER_GROUP, rows(f3lo_ref), rows(f3hi_ref))
    out = x_ref[...] + _rms(fx) * gpost_ref[...] * m[5:6]
    if len(o_refs) == 1:
        o_refs[0][...] = out
    else:
        @pl.when(i < SAMPLE_TILES)
        def _():
            o_refs[0][...] = out

        @pl.when(i >= SAMPLE_TILES)
        def _():
            o_refs[1][...] = out


def _moe_post(f3_parts, x, mods, gpost, split_out):
    if split_out:
        out_specs = _part_specs(D_MODEL, joint=False)
        out_shape = [jax.ShapeDtypeStruct((N_SAMPLE, D_MODEL), F32),
                     jax.ShapeDtypeStruct((N_PROMPT, D_MODEL), F32)]
    else:
        out_specs = pl.BlockSpec((TM, D_MODEL), lambda i: (i, 0))
        out_shape = jax.ShapeDtypeStruct((NT, D_MODEL), F32)
    return pl.pallas_call(
        _moe_post_kernel,
        grid=(NT // TM,),
        in_specs=[*_part_specs(LANES, joint=False, split=MOE_FIRST_GROUPS * TILES_PER_GROUP,
                               rows=TM * ROW_TILES),
                  pl.BlockSpec((TM, D_MODEL), lambda i: (i, 0)),
                  _mod_spec(), _const_spec((1, D_MODEL))],
        out_specs=out_specs,
        out_shape=out_shape,
        compiler_params=_cparams(("arbitrary",)),
        name="moe_post",
    )(*f3_parts, x, mods, gpost)


def _route(aff_t):
    aff_s = jnp.swapaxes(aff_t[:, :N_SAMPLE].reshape(N_EXPERTS, DEC_BATCH, DEC_SEQ), 0, 1)
    aff_p = aff_t[:, N_SAMPLE:].reshape(N_EXPERTS, BATCH, SEQ)
    _, idx_s = lax.top_k(aff_s, 2 * DEC_SEQ // N_EXPERTS)
    _, idx_p = lax.top_k(aff_p, 2 * SEQ // N_EXPERTS)
    idx_p = idx_p + (jnp.arange(BATCH, dtype=jnp.int32) * SEQ)[None, :, None]
    idx = jnp.concatenate([idx_s, idx_p.reshape(1, N_EXPERTS, CAP)], axis=0).astype(jnp.int32)
    return idx.reshape(NG * N_EXPERTS, 1, CAP)


HD = MLA_HEADS * LANES
Q_SCALE = math.log2(math.e) / math.sqrt(MLA_NOPE + MLA_ROPE)


def _mla_proj_kernel(xlo_ref, xhi_ref, mod_ref, gain_ref, win_ref, qn_g_ref, kvn_g_ref, wq_ref, wkv_ref,
                     cos_ref, sin_ref, q_ref, kn_ref, v_ref, krp_ref, ckv_ref, kraw_ref):
    i = pl.program_id(0)
    m = mod_ref[0]
    h = (_rms(_load_rows(xlo_ref, xhi_ref, i)) * gain_ref[...] * (1.0 + m[1:2]) + m[0:1]).astype(BF16)
    lat = jnp.dot(h, win_ref[...], preferred_element_type=F32)
    q_lat = lat[:, :MLA_Q_LORA]
    kv_lat = lat[:, MLA_Q_LORA:MLA_Q_LORA + MLA_KV_LORA]
    kr = lat[:, MLA_Q_LORA + MLA_KV_LORA:MLA_Q_LORA + MLA_KV_LORA + LANES]
    kr_rot = lat[:, MLA_Q_LORA + MLA_KV_LORA + LANES:]
    is_latent = i < SAMPLE_TILES
    cos = jnp.where(is_latent, cos_ref[...], 1.0)
    sin = jnp.where(is_latent, sin_ref[...], 0.0)

    qln = (_rms(q_lat) * qn_g_ref[...]).astype(BF16)
    q = jnp.dot(qln, wq_ref[...], preferred_element_type=F32) * Q_SCALE
    for hd in range(MLA_HEADS):
        lo = HD + hd * LANES
        q_ref[:, 2 * hd * LANES:(2 * hd + 1) * LANES] = q[:, hd * LANES:(hd + 1) * LANES].astype(BF16)
        q_ref[:, (2 * hd + 1) * LANES:(2 * hd + 2) * LANES] = (
            q[:, lo:lo + LANES] * cos + q[:, HD + lo:HD + lo + LANES] * sin).astype(BF16)

    ckv = _rms(kv_lat) * kvn_g_ref[...]
    kv = jnp.dot(ckv.astype(BF16), wkv_ref[...], preferred_element_type=F32)
    kn_ref[...] = kv[:, :HD].astype(BF16)
    v_ref[...] = kv[:, HD:].astype(BF16)
    krp_ref[...] = (kr * cos + kr_rot * sin).astype(BF16)

    @pl.when(i >= SAMPLE_TILES)
    def _():
        ckv_ref[...] = ckv
        kraw_ref[...] = kr


def _mla_proj(x_parts, joint, mods, gain, win_ext, qn_g, kvn_g, wq_ext, wkv_ext, cos128, sin128):
    row = lambda w: pl.BlockSpec((TM, w), lambda i: (i, 0))
    ctx_row = lambda w: pl.BlockSpec((TM, w), lambda i: (jnp.maximum(i - SAMPLE_TILES, 0), 0))
    tab = pl.BlockSpec((TM, LANES), lambda i: (i % TILES_PER_GROUP, 0))
    return pl.pallas_call(
        _mla_proj_kernel,
        grid=(NT // TM,),
        in_specs=[*_part_specs(D_MODEL, joint), _mod_spec(),
                  _const_spec(gain.shape), _const_spec(win_ext.shape), _const_spec(qn_g.shape),
                  _const_spec(kvn_g.shape), _const_spec(wq_ext.shape), _const_spec(wkv_ext.shape),
                  tab, tab],
        out_specs=[row(2 * HD), row(HD), row(HD), row(LANES), ctx_row(MLA_KV_LORA), ctx_row(LANES)],
        out_shape=[jax.ShapeDtypeStruct((NT, 2 * HD), BF16),
                   jax.ShapeDtypeStruct((NT, HD), BF16),
                   jax.ShapeDtypeStruct((NT, HD), BF16),
                   jax.ShapeDtypeStruct((NT, LANES), BF16),
                   jax.ShapeDtypeStruct((N_PROMPT, MLA_KV_LORA), F32),
                   jax.ShapeDtypeStruct((N_PROMPT, LANES), F32)],
        compiler_params=_cparams(("arbitrary",)),
        name="mla_proj",
    )(*x_parts, mods, gain, win_ext, qn_g, kvn_g, wq_ext, wkv_ext, cos128, sin128)


def _matmul_kernel(a_ref, w_ref, o_ref):
    o_ref[...] = jnp.dot(a_ref[...].astype(BF16), w_ref[...],
                         preferred_element_type=F32).astype(o_ref.dtype)


def _ctx_expand(ckv_ctx, wkv_ext):
    n = ckv_ctx.shape[0]
    return pl.pallas_call(
        _matmul_kernel,
        grid=(n // TM,),
        in_specs=[pl.BlockSpec((TM, MLA_KV_LORA), lambda i: (i, 0)),
                  pl.BlockSpec(wkv_ext.shape, lambda i: (0, 0))],
        out_specs=pl.BlockSpec((TM, 2 * HD), lambda i: (i, 0)),
        out_shape=jax.ShapeDtypeStruct((n, 2 * HD), BF16),
        compiler_params=_cparams(("parallel",)),
        name="ctx_expand",
    )(ckv_ctx, wkv_ext)


ATTN_HEADS_PER_STEP = 2
ATTN_CHUNK_UNROLL = 8


def _attn_kernel(q_ref, kn_ref, krp_ref, v_ref, *rest, tk, n_chunks, has_ctx):
    if has_ctx:
        knc_ref, krpc_ref, vc_ref, o_ref = rest
    else:
        o_ref, = rest
    tq = q_ref.shape[0]
    contract1 = (((1,), (1,)), ((), ()))
    heads = range(ATTN_HEADS_PER_STEP)
    head_cols = lambda hd: slice(hd * LANES, (hd + 1) * LANES)
    qs = [q_ref[:, 2 * hd * LANES:2 * (hd + 1) * LANES] for hd in heads]

    def ones_column(rows):
        return (lax.broadcasted_iota(jnp.int32, (rows, LANES), 1) == 0).astype(BF16)

    def step(q, kn, krp, v, ones, carry):
        m, acc = carry
        kc = jnp.concatenate([kn, krp], axis=1)
        s = lax.dot_general(q, kc, contract1, preferred_element_type=F32)
        m_new = jnp.maximum(m, jnp.max(s, axis=-1, keepdims=True))
        p = jnp.exp2(s - m_new).astype(BF16)
        pv = jnp.dot(p, jnp.concatenate([v, ones], axis=1), preferred_element_type=F32)
        return m_new, jnp.exp2(m - m_new) * acc + pv

    carry = tuple((jnp.full((tq, 1), -jnp.inf, F32), jnp.zeros((tq, 2 * LANES), F32)) for _ in heads)
    if has_ctx:
        ones = ones_column(PAST_LEN)
        carry = tuple(step(qs[hd], knc_ref[0, :, head_cols(hd)], krpc_ref[0], vc_ref[0, :, head_cols(hd)],
                           ones, carry[hd]) for hd in heads)
    ones = ones_column(tk)

    def chunk(c, carry):
        rows = pl.ds(pl.multiple_of(c * tk, tk), tk)
        krp = krp_ref[rows, :]
        return tuple(step(qs[hd], kn_ref[rows, head_cols(hd)], krp, v_ref[rows, head_cols(hd)],
                          ones, carry[hd]) for hd in heads)

    unroll = math.gcd(n_chunks, ATTN_CHUNK_UNROLL)

    def body(c, carry):
        for u in range(unroll):
            carry = chunk(c * unroll + u, carry)
        return carry

    carry = lax.fori_loop(0, n_chunks // unroll, body, carry)
    for hd in heads:
        acc = carry[hd][1]
        o_ref[:, head_cols(hd)] = (acc[:, :MLA_V] / acc[:, MLA_V:MLA_V + 1]).astype(o_ref.dtype)


def _attention(q, kn, krp, v, ctx, *, n_req, seq_len, row0, tq, tk):
    rb0 = row0 // seq_len
    qb0 = row0 // tq
    nq = seq_len // tq
    hw = ATTN_HEADS_PER_STEP * LANES
    o_spec = pl.BlockSpec((tq, hw), lambda b, h, i: (b * nq + i, h))
    k_spec = pl.BlockSpec((seq_len, hw), lambda b, h, i: (rb0 + b, h))
    in_specs = [pl.BlockSpec((tq, 2 * hw), lambda b, h, i: (qb0 + b * nq + i, h)), k_spec,
                pl.BlockSpec((seq_len, LANES), lambda b, h, i: (rb0 + b, 0)), k_spec]
    args = [q, kn, krp, v]
    if ctx is not None:
        c_spec = pl.BlockSpec((1, PAST_LEN, hw), lambda b, h, i: (b, 0, h))
        in_specs += [c_spec, pl.BlockSpec((1, PAST_LEN, LANES), lambda b, h, i: (b, 0, 0)), c_spec]
        args += list(ctx)
    return pl.pallas_call(
        functools.partial(_attn_kernel, tk=tk, n_chunks=seq_len // tk, has_ctx=ctx is not None),
        grid=(n_req, MLA_HEADS // ATTN_HEADS_PER_STEP, nq),
        in_specs=in_specs,
        out_specs=o_spec,
        out_shape=jax.ShapeDtypeStruct((n_req * seq_len, HD), BF16),
        compiler_params=_cparams(("parallel", "parallel", "arbitrary")),
        name=f"attention_{seq_len}",
    )(*args)


def _rope_rot_cols(w):
    w1, w2, w3, w4 = jnp.split(w, 4, axis=-1)
    return jnp.concatenate([-w2, w1, -w4, w3], axis=-1)


def _pad_cols(w, width):
    return jnp.pad(w, ((0, 0), (0, width - w.shape[1])))


def _mla_weights(w_in, w_q_b, w_kv_b):
    w_rope = w_in[:, MLA_Q_LORA + MLA_KV_LORA:]
    win_ext = jnp.concatenate([w_in[:, :MLA_Q_LORA + MLA_KV_LORA], _pad_cols(w_rope, LANES),
                               _pad_cols(_rope_rot_cols(w_rope), LANES)], axis=1).astype(BF16)
    wq = w_q_b.reshape(MLA_Q_LORA, MLA_HEADS, MLA_NOPE + MLA_ROPE)
    wq_nope = wq[:, :, :MLA_NOPE].reshape(MLA_Q_LORA, HD)
    wq_rope = wq[:, :, MLA_NOPE:]
    pad = ((0, 0), (0, 0), (0, LANES - MLA_ROPE))
    wq_rope_p = jnp.pad(wq_rope, pad).reshape(MLA_Q_LORA, HD)
    wq_rot_p = jnp.pad(_rope_rot_cols(wq_rope), pad).reshape(MLA_Q_LORA, HD)
    wq_ext = jnp.concatenate([wq_nope, wq_rope_p, wq_rot_p], axis=1).astype(BF16)
    wkv = w_kv_b.reshape(MLA_KV_LORA, MLA_HEADS, MLA_NOPE + MLA_V)
    wkv_ext = jnp.concatenate([wkv[:, :, :MLA_NOPE].reshape(MLA_KV_LORA, HD),
                               wkv[:, :, MLA_NOPE:].reshape(MLA_KV_LORA, HD)], axis=1).astype(BF16)
    return win_ext, wq_ext, wkv_ext


def _rope_tables():
    rows = DEC_SEQ // GRID_W
    row = jnp.repeat(jnp.arange(rows, dtype=F32), GRID_W)
    col = jnp.tile(jnp.arange(GRID_W, dtype=F32), rows)
    half = MLA_ROPE // 2
    inv = 1.0 / (ROPE_BASE ** (jnp.arange(0, half, 2, dtype=F32) / half))
    ar = row[:, None] * inv
    ac = col[:, None] * inv
    ang = jnp.concatenate([ar, ar, ac, ac] * 2, axis=-1)
    return jnp.cos(ang), jnp.sin(ang)


def kernel(x_prompt, x_sample, state_ret_fwd, state_ret_bwd, cache_mla_ckv, cache_mla_krope, c, c_ctx,
           ada_w, ada_b, norm_pre, norm_post, ret_w_in, ret_decay_fwd, ret_decay_bwd, ret_w_out,
           mla_w_in, mla_q_norm, mla_kv_norm, mla_w_q_b, mla_w_kv_b, mla_w_out,
           moe_w_router, moe_w_gate, moe_w_up, moe_w_down):
    x_parts = (x_sample.reshape(N_SAMPLE, D_MODEL), x_prompt.reshape(N_PROMPT, D_MODEL))
    joint = False
    cond8 = jnp.concatenate([c, c_ctx[None, :], jnp.zeros((8 - NG, D_MODEL), F32)], axis=0)
    cos128, sin128 = _rope_tables()
    outs = {}

    for i in range(DEPTH):
        mods = _adaln(cond8, ada_w, ada_b.reshape(DEPTH, 1, 6 * D_MODEL), i)[:NG].reshape(NG, 6, D_MODEL)
        gpre1 = norm_pre[i, 0][None, :]
        gpre2 = norm_pre[i, 1][None, :]
        gpost1 = norm_post[i, 0][None, :]
        gpost2 = norm_post[i, 1][None, :]
        if i % 2 == 0:
            r = i // 2
            qkvg = _norm_mod_matmul(x_parts, joint, mods, gpre1, ret_w_in[r], 1536)
            lg = jnp.stack([jax.nn.log_sigmoid(ret_decay_fwd[r].astype(F32)),
                            jax.nn.log_sigmoid(ret_decay_bwd[r].astype(F32))])
            yg_s, _, _ = _retention(qkvg, lg, state_ret_fwd[:, r], state_ret_bwd[:, r],
                                    n_req=DEC_BATCH, seq_len=DEC_SEQ, row0=0)
            yg_p, s_f, s_b = _retention(qkvg, lg, None, None, n_req=BATCH, seq_len=SEQ, row0=N_SAMPLE)
            outs["fwd"], outs["bwd"] = s_f[:, None], s_b[:, None]
            mix, w_out = (yg_s, yg_p), ret_w_out[r]
        else:
            mi = i // 2
            win_ext, wq_ext, wkv_ext = _mla_weights(mla_w_in[mi], mla_w_q_b[mi], mla_w_kv_b[mi])
            q, kn, v, krp, ckv, kraw = _mla_proj(
                x_parts, joint, mods, gpre1, win_ext, mla_q_norm[mi][None, :], mla_kv_norm[mi][None, :],
                wq_ext, wkv_ext, cos128, sin128)
            outs["ckv"] = ckv.reshape(BATCH, 1, SEQ, MLA_KV_LORA)
            outs["krope"] = kraw[:, :MLA_ROPE].reshape(BATCH, 1, SEQ, MLA_ROPE)
            kvc = _ctx_expand(cache_mla_ckv[:, mi].reshape(DEC_BATCH * PAST_LEN, MLA_KV_LORA), wkv_ext)
            kvc = kvc.reshape(DEC_BATCH, PAST_LEN, 2 * HD)
            krpc = jnp.pad(cache_mla_krope[:, mi], ((0, 0), (0, 0), (0, LANES - MLA_ROPE))).astype(BF16)
            o_s = _attention(q, kn, krp, v, (kvc[:, :, :HD], krpc, kvc[:, :, HD:]),
                             n_req=DEC_BATCH, seq_len=DEC_SEQ, row0=0, tq=512, tk=512)
            o_p = _attention(q, kn, krp, v, None, n_req=BATCH, seq_len=SEQ, row0=N_SAMPLE, tq=SEQ, tk=SEQ)
            mix, w_out = (o_s, o_p), mla_w_out[mi]
        x1, h3, aff_t = _out_post(mix, w_out, x_parts, joint, mods, gpost1, gpre2, moe_w_router[i].T)
        idx = _route(aff_t)
        aff_rows = aff_t.reshape(N_EXPERTS * NG, 1, GT)
        f3_lo, wg16, wu16, wd16 = _moe(idx, aff_rows, h3, moe_w_gate, moe_w_up, moe_w_down, i,
                                       group0=0, n_groups=MOE_FIRST_GROUPS, emit_bf16=True,
                                       ff_split=MOE_FIRST_FF_SPLIT)
        f3_hi, = _moe(idx, aff_rows, h3, wg16, wu16, wd16, 0,
                      group0=MOE_FIRST_GROUPS, n_groups=NG - MOE_FIRST_GROUPS, emit_bf16=False,
                      ff_split=MOE_REST_FF_SPLIT)
        last = i == DEPTH - 1
        x_new = _moe_post((f3_lo, f3_hi), x1, mods, gpost2, split_out=last)
        if not last:
            x_parts, joint = (x_new, x_new), True

    y_sample, y_prompt = x_new
    return (y_prompt.reshape(BATCH, SEQ, D_MODEL), y_sample.reshape(DEC_BATCH, DEC_SEQ, D_MODEL),
            outs["fwd"], outs["bwd"], outs["ckv"], outs["krope"])
```

```python
import functools
import math

import jax
import jax.numpy as jnp
from jax import lax
from jax.experimental import pallas as pl
from jax.experimental.pallas import tpu as pltpu

F32 = jnp.float32
BF16 = jnp.bfloat16

D_MODEL = 1024
BATCH = 16
SEQ = 256
DEPTH = 2
DEC_BATCH = 4
DEC_SEQ = 4096
PAST_LEN = 256
GRID_W = 64
RET_HEADS = 4
RET_DK = 256
RET_DV = 512
RET_QK_WIDTH = RET_HEADS * RET_DK
RET_V_WIDTH = RET_HEADS * RET_DV
RET_CHUNK = 128
MLA_HEADS = 8
MLA_NOPE = 128
MLA_ROPE = 64
MLA_V = 128
MLA_Q_LORA = 384
MLA_KV_LORA = 256
ROPE_BASE = 10000.0
N_EXPERTS = 16
EXPERT_FF = 1024
NORM_EPS = 1e-6

GT = DEC_SEQ
NG = DEC_BATCH + 1
NT = NG * GT
N_SAMPLE = DEC_BATCH * DEC_SEQ
N_PROMPT = BATCH * SEQ
CAP = 2 * GT // N_EXPERTS

LANES = 128
SUBLANES = 8
ROW_TILES = D_MODEL // LANES
TM = 512
TILES_PER_GROUP = GT // TM
SAMPLE_TILES = N_SAMPLE // TM
VMEM_LIMIT = 60 * 1024 * 1024


def _cparams(sem):
    return pltpu.CompilerParams(dimension_semantics=sem, vmem_limit_bytes=VMEM_LIMIT)


def _rms(x):
    return x * lax.rsqrt(jnp.mean(x * x, axis=-1, keepdims=True) + NORM_EPS)


def _sigmoid(x):
    return 1.0 / (1.0 + jnp.exp(-x))


def _part_specs(width, joint, split=SAMPLE_TILES, rows=TM):
    lo = pl.BlockSpec((rows, width), lambda *g: (jnp.minimum(g[-1], split - 1), 0))
    if joint:
        hi = pl.BlockSpec((rows, width), lambda *g: (jnp.maximum(g[-1], split), 0))
    else:
        hi = pl.BlockSpec((rows, width), lambda *g: (jnp.maximum(g[-1] - split, 0), 0))
    return [lo, hi]


def _load_rows(lo_ref, hi_ref, i, split=SAMPLE_TILES):
    return jnp.where(i < split, lo_ref[...], hi_ref[...])


def _mod_spec():
    return pl.BlockSpec((1, 6, D_MODEL), lambda *g: (g[-1] // TILES_PER_GROUP, 0, 0))


def _const_spec(shape):
    return pl.BlockSpec(shape, lambda *g: (0,) * len(shape))


def _adaln_kernel(c_ref, w_ref, b_ref, o_ref):
    c = c_ref[...]
    s = (c * _sigmoid(c)).astype(BF16)
    o_ref[...] = jnp.dot(s, w_ref[0].astype(BF16), preferred_element_type=F32) + b_ref[0]


def _adaln(cond8, w, b, layer):
    tn = 1536
    return pl.pallas_call(
        _adaln_kernel,
        grid=(6 * D_MODEL // tn,),
        in_specs=[pl.BlockSpec((8, D_MODEL), lambda j: (0, 0)),
                  pl.BlockSpec((1, D_MODEL, tn), lambda j: (layer, 0, j)),
                  pl.BlockSpec((1, 1, tn), lambda j: (layer, 0, j))],
        out_specs=pl.BlockSpec((8, tn), lambda j: (0, j)),
        out_shape=jax.ShapeDtypeStruct((8, 6 * D_MODEL), F32),
        compiler_params=_cparams(("arbitrary",)),
        name="adaln",
    )(cond8, w, b)


def _nmm_kernel(xlo_ref, xhi_ref, mod_ref, gain_ref, w_ref, o_ref, w_scr):
    i = pl.program_id(1)

    @pl.when(i == 0)
    def _():
        w_scr[...] = w_ref[...].astype(BF16)

    m = mod_ref[0]
    h = _rms(_load_rows(xlo_ref, xhi_ref, i)) * gain_ref[...] * (1.0 + m[1:2]) + m[0:1]
    o_ref[...] = jnp.dot(h.astype(BF16), w_scr[...], preferred_element_type=F32).astype(o_ref.dtype)


def _norm_mod_matmul(x_parts, joint, mods, gain, w, tn):
    n_out = w.shape[1]
    return pl.pallas_call(
        _nmm_kernel,
        grid=(n_out // tn, NT // TM),
        in_specs=[*_part_specs(D_MODEL, joint), _mod_spec(), _const_spec((1, D_MODEL)),
                  pl.BlockSpec((D_MODEL, tn), lambda j, i: (0, j))],
        out_specs=pl.BlockSpec((TM, tn), lambda j, i: (i, j)),
        out_shape=jax.ShapeDtypeStruct((NT, n_out), BF16),
        scratch_shapes=[pltpu.VMEM((D_MODEL, tn), BF16)],
        compiler_params=_cparams(("arbitrary", "arbitrary")),
        name="norm_mod_matmul",
    )(*x_parts, mods, gain, w)


def _ret_kernel(lg_ref, q_ref, k_ref, v_ref, g_ref, *rest, seq_len, has_init):
    if has_init:
        s0f_ref, s0b_ref, *rest = rest
    yg_ref, sf_ref, sb_ref, ycross, sf_scr, sb_scr = rest
    C = RET_CHUNK
    nc = seq_len // C
    head = pl.program_id(1)
    lgf = lg_ref[0, head]
    lgb = lg_ref[1, head]
    k_scale = RET_DK ** -0.5

    ii = lax.broadcasted_iota(jnp.int32, (C, C), 0).astype(F32)
    jj = lax.broadcasted_iota(jnp.int32, (C, C), 1).astype(F32)
    diff = ii - jj
    inner = jnp.where(diff >= 0, jnp.exp(diff * lgf), jnp.exp(-diff * lgb)) * k_scale
    pos = lax.broadcasted_iota(jnp.int32, (C, 1), 0).astype(F32)
    cross_f = jnp.exp((pos + 1.0) * lgf)
    cross_b = jnp.exp((C - pos) * lgb)
    sdec_f = jnp.exp((C - 1.0 - pos) * lgf) * k_scale
    sdec_b = jnp.exp(pos * lgb) * k_scale
    one = jnp.ones((1, 1), F32)
    cdec_f = jnp.exp(one * (C * lgf))
    cdec_b = jnp.exp(one * (C * lgb))

    if has_init:
        sf_scr[...] = s0f_ref[0, 0]
        sb_scr[...] = s0b_ref[0, 0]
    else:
        sf_scr[...] = jnp.zeros_like(sf_scr)
        sb_scr[...] = jnp.zeros_like(sb_scr)

    contract0 = (((0,), (0,)), ((), ()))
    contract1 = (((1,), (1,)), ((), ()))

    def bwd(t, carry):
        rows = pl.ds(pl.multiple_of((nc - 1 - t) * C, C), C)
        q = q_ref[rows, :]
        s = sb_scr[...]
        ycross[rows, :] = jnp.dot(q, s.astype(BF16), preferred_element_type=F32) * cross_b
        kd = (k_ref[rows, :].astype(F32) * sdec_b).astype(BF16)
        sb_scr[...] = s * cdec_b + lax.dot_general(kd, v_ref[rows, :], contract0,
                                                   preferred_element_type=F32)
        return carry

    lax.fori_loop(0, nc, bwd, 0)

    def fwd(c, carry):
        rows = pl.ds(pl.multiple_of(c * C, C), C)
        q = q_ref[rows, :]
        k = k_ref[rows, :]
        v = v_ref[rows, :]
        s = sf_scr[...]
        scores = lax.dot_general(q, k, contract1, preferred_element_type=F32) * inner
        y = (jnp.dot(scores.astype(BF16), v, preferred_element_type=F32)
             + jnp.dot(q, s.astype(BF16), preferred_element_type=F32) * cross_f
             + ycross[rows, :])
        mu = jnp.mean(y, axis=-1, keepdims=True)
        yc = y - mu
        yn = yc * lax.rsqrt(jnp.mean(yc * yc, axis=-1, keepdims=True) + NORM_EPS)
        g = g_ref[rows, :].astype(F32)
        yg_ref[rows, :] = (g * _sigmoid(g) * yn).astype(yg_ref.dtype)
        kd = (k.astype(F32) * sdec_f).astype(BF16)
        sf_scr[...] = s * cdec_f + lax.dot_general(kd, v, contract0, preferred_element_type=F32)
        return carry

    lax.fori_loop(0, nc, fwd, 0)
    sf_ref[0, 0] = sf_scr[...]
    sb_ref[0, 0] = sb_scr[...]


def _retention(qkvg, lg, s0f, s0b, *, n_req, seq_len, row0):
    rb0 = row0 // seq_len
    has_init = s0f is not None
    qk_spec = lambda off: pl.BlockSpec((seq_len, RET_DK), lambda b, h, lg_: (rb0 + b, off + h))
    vg_spec = lambda off: pl.BlockSpec((seq_len, RET_DV), lambda b, h, lg_: (rb0 + b, off + h))
    st_spec = pl.BlockSpec((1, 1, RET_DK, RET_DV), lambda b, h, lg_: (b, h, 0, 0))
    in_specs = [qk_spec(0), qk_spec(RET_HEADS), vg_spec(RET_HEADS), vg_spec(2 * RET_HEADS)]
    args = [qkvg, qkvg, qkvg, qkvg]
    if has_init:
        in_specs += [st_spec, st_spec]
        args += [s0f, s0b]
    st_shape = jax.ShapeDtypeStruct((n_req, RET_HEADS, RET_DK, RET_DV), F32)
    return pl.pallas_call(
        functools.partial(_ret_kernel, seq_len=seq_len, has_init=has_init),
        grid_spec=pltpu.PrefetchScalarGridSpec(
            num_scalar_prefetch=1,
            grid=(n_req, RET_HEADS),
            in_specs=in_specs,
            out_specs=[pl.BlockSpec((seq_len, RET_DV), lambda b, h, lg_: (b, h)), st_spec, st_spec],
            scratch_shapes=[pltpu.VMEM((seq_len, RET_DV), F32),
                            pltpu.VMEM((RET_DK, RET_DV), F32),
                            pltpu.VMEM((RET_DK, RET_DV), F32)]),
        out_shape=[jax.ShapeDtypeStruct((n_req * seq_len, RET_V_WIDTH), BF16), st_shape, st_shape],
        compiler_params=_cparams(("parallel", "arbitrary")),
        name=f"retention_{seq_len}",
    )(lg, *args)


def _split_bf16(x):
    hi = x.astype(BF16)
    return hi, (x - hi.astype(F32)).astype(BF16)


def _out_post_kernel(ylo_ref, yhi_ref, w_ref, xlo_ref, xhi_ref, mod_ref, gpost_ref, gpre_ref, wr_ref,
                     x1_ref, h3_ref, aff_ref, w_scr):
    i = pl.program_id(0)

    @pl.when(i == 0)
    def _():
        w_scr[...] = w_ref[...].astype(BF16)

    m = mod_ref[0]
    y = jnp.dot(_load_rows(ylo_ref, yhi_ref, i), w_scr[...], preferred_element_type=F32)
    x1 = _load_rows(xlo_ref, xhi_ref, i) + _rms(y) * gpost_ref[...] * m[2:3]
    x1_ref[...] = x1
    h = _rms(x1) * gpre_ref[...] * (1.0 + m[4:5]) + m[3:4]
    for c in range(ROW_TILES):
        h3_ref[pl.ds(c, TM, stride=ROW_TILES), :] = h[:, c * LANES:(c + 1) * LANES]
    contract1 = (((1,), (1,)), ((), ()))
    w_hi, w_lo = _split_bf16(wr_ref[...])
    h_hi, h_lo = _split_bf16(h)
    part = lax.dot_general(jnp.concatenate([w_hi, w_lo], axis=0), h_hi, contract1,
                           preferred_element_type=F32)
    logits = (part[:N_EXPERTS] + part[N_EXPERTS:]
              + lax.dot_general(w_hi, h_lo, contract1, preferred_element_type=F32))
    e = jnp.exp(logits - jnp.max(logits, axis=0, keepdims=True))
    aff_ref[...] = e / jnp.sum(e, axis=0, keepdims=True)


def _out_post(y_parts, w, x_parts, joint, mods, gpost, gpre, w_router_t):
    kdim = w.shape[0]
    return pl.pallas_call(
        _out_post_kernel,
        grid=(NT // TM,),
        in_specs=[*_part_specs(kdim, joint=False),
                  _const_spec((kdim, D_MODEL)),
                  *_part_specs(D_MODEL, joint), _mod_spec(),
                  _const_spec((1, D_MODEL)), _const_spec((1, D_MODEL)),
                  _const_spec((N_EXPERTS, D_MODEL))],
        out_specs=[pl.BlockSpec((TM, D_MODEL), lambda i: (i, 0)),
                   pl.BlockSpec((TM * ROW_TILES, LANES), lambda i: (i, 0)),
                   pl.BlockSpec((N_EXPERTS, TM), lambda i: (0, i))],
        out_shape=[jax.ShapeDtypeStruct((NT, D_MODEL), F32),
                   jax.ShapeDtypeStruct((NT * ROW_TILES, LANES), F32),
                   jax.ShapeDtypeStruct((N_EXPERTS, NT), F32)],
        scratch_shapes=[pltpu.VMEM((kdim, D_MODEL), BF16)],
        compiler_params=_cparams(("arbitrary",)),
        name="out_post",
    )(*y_parts, w, *x_parts, mods, gpost, gpre, w_router_t)


SCATTER_BATCH = 8
MOE_FIRST_GROUPS = 1
MOE_FIRST_FF_SPLIT = 4
MOE_REST_FF_SPLIT = 1


def _moe_kernel(idx_prev_ref, idx_ref, idx_next_ref, aff_prev_ref, aff_ref, h3_ref, wg_ref, wu_ref, wd_ref,
                f3_ref, *rest, emit_bf16, ff_split):
    if emit_bf16:
        wg16_ref, wu16_ref, wd16_ref, *rest = rest
    xg, x_a, x_b, yacc, y_a, y_b = rest
    e = pl.program_id(1)
    f = pl.program_id(2)

    def tile(r):
        return pl.ds(pl.multiple_of(r * SUBLANES, SUBLANES), SUBLANES)

    def gather(ids_ref, x_dst):
        for r in range(CAP):
            xg[r * SUBLANES:(r + 1) * SUBLANES, :] = h3_ref[tile(ids_ref[0, 0, r]), :]
        x_dst[...] = jnp.concatenate(
            [xg[pl.ds(c, CAP, stride=ROW_TILES), :] for c in range(ROW_TILES)], axis=1).astype(BF16)

    def scatter(ids_ref, gate_ref, y_src):
        for r0 in range(0, CAP, SCATTER_BATCH):
            dst, vals = [], []
            for r in range(r0, r0 + SCATTER_BATCH):
                t = ids_ref[0, 0, r]
                dst.append(tile(t))
                vals.append(f3_ref[tile(t), :] + gate_ref[0, 0, t] * y_src[r * SUBLANES:(r + 1) * SUBLANES, :])
            for d, v in zip(dst, vals):
                f3_ref[d, :] = v

    def half_ffn(x_src):
        x = x_src[...]
        wg = wg_ref[0, 0].astype(BF16)
        wu = wu_ref[0, 0].astype(BF16)
        wd = wd_ref[0, 0].astype(BF16)
        if emit_bf16:
            wg16_ref[0, 0] = wg
            wu16_ref[0, 0] = wu
            wd16_ref[0, 0] = wd
        a = jnp.dot(x, wg, preferred_element_type=F32)
        u = jnp.dot(x, wu, preferred_element_type=F32)
        act = (a * _sigmoid(a) * u).astype(BF16)
        return jnp.dot(act, wd, preferred_element_type=F32)

    first = f == 0
    last = f == ff_split - 1

    @pl.when(jnp.logical_and(e == 0, first))
    def _():
        f3_ref[...] = jnp.zeros_like(f3_ref)
        y_b[...] = jnp.zeros_like(y_b)
        gather(idx_ref, x_a)

    if ff_split == 1:
        phases = (("only", first),)
    elif ff_split == 2:
        phases = (("first", first), ("last", last))
    else:
        phases = (("first", first), ("middle", jnp.logical_not(jnp.logical_or(first, last))), ("last", last))
    for parity, (x_cur, x_nxt, y_cur, y_prv) in enumerate(((x_a, x_b, y_a, y_b), (x_b, x_a, y_b, y_a))):
        for phase, cond in phases:
            @pl.when(jnp.logical_and(e % 2 == parity, cond))
            def _():
                if phase in ("only", "first"):
                    gather(idx_next_ref, x_nxt)
                part = half_ffn(x_cur)
                if phase == "first":
                    yacc[...] = part
                elif phase == "middle":
                    yacc[...] += part
                else:
                    y = part if phase == "only" else yacc[...] + part
                    for c in range(ROW_TILES):
                        y_cur[pl.ds(c, CAP, stride=ROW_TILES), :] = y[:, c * LANES:(c + 1) * LANES]
                    scatter(idx_prev_ref, aff_prev_ref, y_prv)

    @pl.when(jnp.logical_and(e == N_EXPERTS - 1, last))
    def _():
        scatter(idx_ref, aff_ref, y_b)


def _moe(idx, aff_rows, h3, wg, wu, wd, layer, *, group0, n_groups, emit_bf16, ff_split):
    grp_rows = GT * ROW_TILES
    ff_tile = EXPERT_FF // ff_split
    clamp = lambda e: jnp.clip(e, 0, N_EXPERTS - 1)
    idx_spec = lambda d: pl.BlockSpec(
        (1, 1, CAP), lambda b, e, f: ((group0 + b) * N_EXPERTS + clamp(e + d), 0, 0), memory_space=pltpu.SMEM)
    aff_spec = lambda d: pl.BlockSpec(
        (1, 1, GT), lambda b, e, f: (clamp(e + d) * NG + group0 + b, 0, 0), memory_space=pltpu.SMEM)
    w_in_spec = lambda l: pl.BlockSpec((1, 1, D_MODEL, ff_tile), lambda b, e, f: (l, e, 0, f))
    w_out_spec = lambda l: pl.BlockSpec((1, 1, ff_tile, D_MODEL), lambda b, e, f: (l, e, f, 0))
    out_specs = [pl.BlockSpec((grp_rows, LANES), lambda b, e, f: (b, 0), pipeline_mode=pl.Buffered(1))]
    out_shape = [jax.ShapeDtypeStruct((n_groups * grp_rows, LANES), F32)]
    if emit_bf16:
        out_specs += [w_in_spec(0), w_in_spec(0), w_out_spec(0)]
        out_shape += [jax.ShapeDtypeStruct((1, N_EXPERTS, D_MODEL, EXPERT_FF), BF16),
                      jax.ShapeDtypeStruct((1, N_EXPERTS, D_MODEL, EXPERT_FF), BF16),
                      jax.ShapeDtypeStruct((1, N_EXPERTS, EXPERT_FF, D_MODEL), BF16)]
    return pl.pallas_call(
        functools.partial(_moe_kernel, emit_bf16=emit_bf16, ff_split=ff_split),
        grid=(n_groups, N_EXPERTS, ff_split),
        in_specs=[idx_spec(-1), idx_spec(0), idx_spec(1), aff_spec(-1), aff_spec(0),
                  pl.BlockSpec((grp_rows, LANES), lambda b, e, f: (group0 + b, 0),
                               pipeline_mode=pl.Buffered(1)),
                  w_in_spec(layer), w_in_spec(layer), w_out_spec(layer)],
        out_specs=out_specs,
        out_shape=out_shape,
        scratch_shapes=[pltpu.VMEM((CAP * ROW_TILES, LANES), F32),
                        pltpu.VMEM((CAP, D_MODEL), BF16),
                        pltpu.VMEM((CAP, D_MODEL), BF16),
                        pltpu.VMEM((CAP, D_MODEL), F32),
                        pltpu.VMEM((CAP * ROW_TILES, LANES), F32),
                        pltpu.VMEM((CAP * ROW_TILES, LANES), F32)],
        compiler_params=_cparams(("arbitrary", "arbitrary", "arbitrary")),
        name="moe_experts",
    )(idx, idx, idx, aff_rows, aff_rows, h3, wg, wu, wd)


def _moe_post_kernel(f3lo_ref, f3hi_ref, x_ref, mod_ref, gpost_ref, *o_refs):
    i = pl.program_id(0)
    m = mod_ref[0]
    rows = lambda ref: jnp.concatenate(
        [ref[pl.ds(c, TM, stride=ROW_TILES), :] for c in range(ROW_TILES)], axis=1)
    fx = jnp.where(i < MOE_FIRST_GROUPS * TILES_PER_GROUP, rows(f3lo_ref), rows(f3hi_ref))
    out = x_ref[...] + _rms(fx) * gpost_ref[...] * m[5:6]
    if len(o_refs) == 1:
        o_refs[0][...] = out
    else:
        @pl.when(i < SAMPLE_TILES)
        def _():
            o_refs[0][...] = out

        @pl.when(i >= SAMPLE_TILES)
        def _():
            o_refs[1][...] = out


def _moe_post(f3_parts, x, mods, gpost, split_out):
    if split_out:
        out_specs = _part_specs(D_MODEL, joint=False)
        out_shape = [jax.ShapeDtypeStruct((N_SAMPLE, D_MODEL), F32),
                     jax.ShapeDtypeStruct((N_PROMPT, D_MODEL), F32)]
    else:
        out_specs = pl.BlockSpec((TM, D_MODEL), lambda i: (i, 0))
        out_shape = jax.ShapeDtypeStruct((NT, D_MODEL), F32)
    return pl.pallas_call(
        _moe_post_kernel,
        grid=(NT // TM,),
        in_specs=[*_part_specs(LANES, joint=False, split=MOE_FIRST_GROUPS * TILES_PER_GROUP,
                               rows=TM * ROW_TILES),
                  pl.BlockSpec((TM, D_MODEL), lambda i: (i, 0)),
                  _mod_spec(), _const_spec((1, D_MODEL))],
        out_specs=out_specs,
        out_shape=out_shape,
        compiler_params=_cparams(("arbitrary",)),
        name="moe_post",
    )(*f3_parts, x, mods, gpost)


def _route(aff_t):
    aff_s = jnp.swapaxes(aff_t[:, :N_SAMPLE].reshape(N_EXPERTS, DEC_BATCH, DEC_SEQ), 0, 1)
    aff_p = aff_t[:, N_SAMPLE:].reshape(N_EXPERTS, BATCH, SEQ)
    _, idx_s = lax.top_k(aff_s, 2 * DEC_SEQ // N_EXPERTS)
    _, idx_p = lax.top_k(aff_p, 2 * SEQ // N_EXPERTS)
    idx_p = idx_p + (jnp.arange(BATCH, dtype=jnp.int32) * SEQ)[None, :, None]
    idx = jnp.concatenate([idx_s, idx_p.reshape(1, N_EXPERTS, CAP)], axis=0).astype(jnp.int32)
    return idx.reshape(NG * N_EXPERTS, 1, CAP)


HD = MLA_HEADS * LANES
Q_SCALE = math.log2(math.e) / math.sqrt(MLA_NOPE + MLA_ROPE)


def _mla_proj_kernel(xlo_ref, xhi_ref, mod_ref, gain_ref, win_ref, qn_g_ref, kvn_g_ref, wq_ref, wkv_ref,
                     cos_ref, sin_ref, q_ref, kn_ref, v_ref, krp_ref, ckv_ref, kraw_ref):
    i = pl.program_id(0)
    m = mod_ref[0]
    h = (_rms(_load_rows(xlo_ref, xhi_ref, i)) * gain_ref[...] * (1.0 + m[1:2]) + m[0:1]).astype(BF16)
    lat = jnp.dot(h, win_ref[...], preferred_element_type=F32)
    q_lat = lat[:, :MLA_Q_LORA]
    kv_lat = lat[:, MLA_Q_LORA:MLA_Q_LORA + MLA_KV_LORA]
    kr = lat[:, MLA_Q_LORA + MLA_KV_LORA:MLA_Q_LORA + MLA_KV_LORA + LANES]
    kr_rot = lat[:, MLA_Q_LORA + MLA_KV_LORA + LANES:]
    is_latent = i < SAMPLE_TILES
    cos = jnp.where(is_latent, cos_ref[...], 1.0)
    sin = jnp.where(is_latent, sin_ref[...], 0.0)

    qln = (_rms(q_lat) * qn_g_ref[...]).astype(BF16)
    q = jnp.dot(qln, wq_ref[...], preferred_element_type=F32) * Q_SCALE
    for hd in range(MLA_HEADS):
        lo = HD + hd * LANES
        q_ref[:, 2 * hd * LANES:(2 * hd + 1) * LANES] = q[:, hd * LANES:(hd + 1) * LANES].astype(BF16)
        q_ref[:, (2 * hd + 1) * LANES:(2 * hd + 2) * LANES] = (
            q[:, lo:lo + LANES] * cos + q[:, HD + lo:HD + lo + LANES] * sin).astype(BF16)

    ckv = _rms(kv_lat) * kvn_g_ref[...]
    kv = jnp.dot(ckv.astype(BF16), wkv_ref[...], preferred_element_type=F32)
    kn_ref[...] = kv[:, :HD].astype(BF16)
    v_ref[...] = kv[:, HD:].astype(BF16)
    krp_ref[...] = (kr * cos + kr_rot * sin).astype(BF16)

    @pl.when(i >= SAMPLE_TILES)
    def _():
        ckv_ref[...] = ckv
        kraw_ref[...] = kr


def _mla_proj(x_parts, joint, mods, gain, win_ext, qn_g, kvn_g, wq_ext, wkv_ext, cos128, sin128):
    row = lambda w: pl.BlockSpec((TM, w), lambda i: (i, 0))
    ctx_row = lambda w: pl.BlockSpec((TM, w), lambda i: (jnp.maximum(i - SAMPLE_TILES, 0), 0))
    tab = pl.BlockSpec((TM, LANES), lambda i: (i % TILES_PER_GROUP, 0))
    return pl.pallas_call(
        _mla_proj_kernel,
        grid=(NT // TM,),
        in_specs=[*_part_specs(D_MODEL, joint), _mod_spec(),
                  _const_spec(gain.shape), _const_spec(win_ext.shape), _const_spec(qn_g.shape),
                  _const_spec(kvn_g.shape), _const_spec(wq_ext.shape), _const_spec(wkv_ext.shape),
                  tab, tab],
        out_specs=[row(2 * HD), row(HD), row(HD), row(LANES), ctx_row(MLA_KV_LORA), ctx_row(LANES)],
        out_shape=[jax.ShapeDtypeStruct((NT, 2 * HD), BF16),
                   jax.ShapeDtypeStruct((NT, HD), BF16),
                   jax.ShapeDtypeStruct((NT, HD), BF16),
                   jax.ShapeDtypeStruct((NT, LANES), BF16),
                   jax.ShapeDtypeStruct((N_PROMPT, MLA_KV_LORA), F32),
                   jax.ShapeDtypeStruct((N_PROMPT, LANES), F32)],
        compiler_params=_cparams(("arbitrary",)),
        name="mla_proj",
    )(*x_parts, mods, gain, win_ext, qn_g, kvn_g, wq_ext, wkv_ext, cos128, sin128)


def _matmul_kernel(a_ref, w_ref, o_ref):
    o_ref[...] = jnp.dot(a_ref[...].astype(BF16), w_ref[...],
                         preferred_element_type=F32).astype(o_ref.dtype)


def _ctx_expand(ckv_ctx, wkv_ext):
    n = ckv_ctx.shape[0]
    return pl.pallas_call(
        _matmul_kernel,
        grid=(n // TM,),
        in_specs=[pl.BlockSpec((TM, MLA_KV_LORA), lambda i: (i, 0)),
                  pl.BlockSpec(wkv_ext.shape, lambda i: (0, 0))],
        out_specs=pl.BlockSpec((TM, 2 * HD), lambda i: (i, 0)),
        out_shape=jax.ShapeDtypeStruct((n, 2 * HD), BF16),
        compiler_params=_cparams(("parallel",)),
        name="ctx_expand",
    )(ckv_ctx, wkv_ext)


ATTN_HEADS_PER_STEP = 2
ATTN_CHUNK_UNROLL = 8


def _attn_kernel(q_ref, kn_ref, krp_ref, v_ref, *rest, tk, n_chunks, has_ctx):
    if has_ctx:
        knc_ref, krpc_ref, vc_ref, o_ref = rest
    else:
        o_ref, = rest
    tq = q_ref.shape[0]
    contract1 = (((1,), (1,)), ((), ()))
    heads = range(ATTN_HEADS_PER_STEP)
    head_cols = lambda hd: slice(hd * LANES, (hd + 1) * LANES)
    qs = [q_ref[:, 2 * hd * LANES:2 * (hd + 1) * LANES] for hd in heads]

    def ones_column(rows):
        return (lax.broadcasted_iota(jnp.int32, (rows, LANES), 1) == 0).astype(BF16)

    def step(q, kn, krp, v, ones, carry):
        m, acc = carry
        kc = jnp.concatenate([kn, krp], axis=1)
        s = lax.dot_general(q, kc, contract1, preferred_element_type=F32)
        m_new = jnp.maximum(m, jnp.max(s, axis=-1, keepdims=True))
        p = jnp.exp2(s - m_new).astype(BF16)
        pv = jnp.dot(p, jnp.concatenate([v, ones], axis=1), preferred_element_type=F32)
        return m_new, jnp.exp2(m - m_new) * acc + pv

    carry = tuple((jnp.full((tq, 1), -jnp.inf, F32), jnp.zeros((tq, 2 * LANES), F32)) for _ in heads)
    if has_ctx:
        ones = ones_column(PAST_LEN)
        carry = tuple(step(qs[hd], knc_ref[0, :, head_cols(hd)], krpc_ref[0], vc_ref[0, :, head_cols(hd)],
                           ones, carry[hd]) for hd in heads)
    ones = ones_column(tk)

    def chunk(c, carry):
        rows = pl.ds(pl.multiple_of(c * tk, tk), tk)
        krp = krp_ref[rows, :]
        return tuple(step(qs[hd], kn_ref[rows, head_cols(hd)], krp, v_ref[rows, head_cols(hd)],
                          ones, carry[hd]) for hd in heads)

    unroll = math.gcd(n_chunks, ATTN_CHUNK_UNROLL)

    def body(c, carry):
        for u in range(unroll):
            carry = chunk(c * unroll + u, carry)
        return carry

    carry = lax.fori_loop(0, n_chunks // unroll, body, carry)
    for hd in heads:
        acc = carry[hd][1]
        o_ref[:, head_cols(hd)] = (acc[:, :MLA_V] / acc[:, MLA_V:MLA_V + 1]).astype(o_ref.dtype)


def _attention(q, kn, krp, v, ctx, *, n_req, seq_len, row0, tq, tk):
    rb0 = row0 // seq_len
    qb0 = row0 // tq
    nq = seq_len // tq
    hw = ATTN_HEADS_PER_STEP * LANES
    o_spec = pl.BlockSpec((tq, hw), lambda b, h, i: (b * nq + i, h))
    k_spec = pl.BlockSpec((seq_len, hw), lambda b, h, i: (rb0 + b, h))
    in_specs = [pl.BlockSpec((tq, 2 * hw), lambda b, h, i: (qb0 + b * nq + i, h)), k_spec,
                pl.BlockSpec((seq_len, LANES), lambda b, h, i: (rb0 + b, 0)), k_spec]
    args = [q, kn, krp, v]
    if ctx is not None:
        c_spec = pl.BlockSpec((1, PAST_LEN, hw), lambda b, h, i: (b, 0, h))
        in_specs += [c_spec, pl.BlockSpec((1, PAST_LEN, LANES), lambda b, h, i: (b, 0, 0)), c_spec]
        args += list(ctx)
    return pl.pallas_call(
        functools.partial(_attn_kernel, tk=tk, n_chunks=seq_len // tk, has_ctx=ctx is not None),
        grid=(n_req, MLA_HEADS // ATTN_HEADS_PER_STEP, nq),
        in_specs=in_specs,
        out_specs=o_spec,
        out_shape=jax.ShapeDtypeStruct((n_req * seq_len, HD), BF16),
        compiler_params=_cparams(("parallel", "parallel", "arbitrary")),
        name=f"attention_{seq_len}",
    )(*args)


def _rope_rot_cols(w):
    w1, w2, w3, w4 = jnp.split(w, 4, axis=-1)
    return jnp.concatenate([-w2, w1, -w4, w3], axis=-1)


def _pad_cols(w, width):
    return jnp.pad(w, ((0, 0), (0, width - w.shape[1])))


def _mla_weights(w_in, w_q_b, w_kv_b):
    w_rope = w_in[:, MLA_Q_LORA + MLA_KV_LORA:]
    win_ext = jnp.concatenate([w_in[:, :MLA_Q_LORA + MLA_KV_LORA], _pad_cols(w_rope, LANES),
                               _pad_cols(_rope_rot_cols(w_rope), LANES)], axis=1).astype(BF16)
    wq = w_q_b.reshape(MLA_Q_LORA, MLA_HEADS, MLA_NOPE + MLA_ROPE)
    wq_nope = wq[:, :, :MLA_NOPE].reshape(MLA_Q_LORA, HD)
    wq_rope = wq[:, :, MLA_NOPE:]
    pad = ((0, 0), (0, 0), (0, LANES - MLA_ROPE))
    wq_rope_p = jnp.pad(wq_rope, pad).reshape(MLA_Q_LORA, HD)
    wq_rot_p = jnp.pad(_rope_rot_cols(wq_rope), pad).reshape(MLA_Q_LORA, HD)
    wq_ext = jnp.concatenate([wq_nope, wq_rope_p, wq_rot_p], axis=1).astype(BF16)
    wkv = w_kv_b.reshape(MLA_KV_LORA, MLA_HEADS, MLA_NOPE + MLA_V)
    wkv_ext = jnp.concatenate([wkv[:, :, :MLA_NOPE].reshape(MLA_KV_LORA, HD),
                               wkv[:, :, MLA_NOPE:].reshape(MLA_KV_LORA, HD)], axis=1).astype(BF16)
    return win_ext, wq_ext, wkv_ext


def _rope_tables():
    rows = DEC_SEQ // GRID_W
    row = jnp.repeat(jnp.arange(rows, dtype=F32), GRID_W)
    col = jnp.tile(jnp.arange(GRID_W, dtype=F32), rows)
    half = MLA_ROPE // 2
    inv = 1.0 / (ROPE_BASE ** (jnp.arange(0, half, 2, dtype=F32) / half))
    ar = row[:, None] * inv
    ac = col[:, None] * inv
    ang = jnp.concatenate([ar, ar, ac, ac] * 2, axis=-1)
    return jnp.cos(ang), jnp.sin(ang)


def kernel(x_prompt, x_sample, state_ret_fwd, state_ret_bwd, cache_mla_ckv, cache_mla_krope, c, c_ctx,
           ada_w, ada_b, norm_pre, norm_post, ret_w_in, ret_decay_fwd, ret_decay_bwd, ret_w_out,
           mla_w_in, mla_q_norm, mla_kv_norm, mla_w_q_b, mla_w_kv_b, mla_w_out,
           moe_w_router, moe_w_gate, moe_w_up, moe_w_down):
    x_parts = (x_sample.reshape(N_SAMPLE, D_MODEL), x_prompt.reshape(N_PROMPT, D_MODEL))
    joint = False
    cond8 = jnp.concatenate([c, c_ctx[None, :], jnp.zeros((8 - NG, D_MODEL), F32)], axis=0)
    cos128, sin128 = _rope_tables()
    outs = {}

    for i in range(DEPTH):
        mods = _adaln(cond8, ada_w, ada_b.reshape(DEPTH, 1, 6 * D_MODEL), i)[:NG].reshape(NG, 6, D_MODEL)
        gpre1 = norm_pre[i, 0][None, :]
        gpre2 = norm_pre[i, 1][None, :]
        gpost1 = norm_post[i, 0][None, :]
        gpost2 = norm_post[i, 1][None, :]
        if i % 2 == 0:
            r = i // 2
            qkvg = _norm_mod_matmul(x_parts, joint, mods, gpre1, ret_w_in[r], 1536)
            lg = jnp.stack([jax.nn.log_sigmoid(ret_decay_fwd[r].astype(F32)),
                            jax.nn.log_sigmoid(ret_decay_bwd[r].astype(F32))])
            yg_s, _, _ = _retention(qkvg, lg, state_ret_fwd[:, r], state_ret_bwd[:, r],
                                    n_req=DEC_BATCH, seq_len=DEC_SEQ, row0=0)
            yg_p, s_f, s_b = _retention(qkvg, lg, None, None, n_req=BATCH, seq_len=SEQ, row0=N_SAMPLE)
            outs["fwd"], outs["bwd"] = s_f[:, None], s_b[:, None]
            mix, w_out = (yg_s, yg_p), ret_w_out[r]
        else:
            mi = i // 2
            win_ext, wq_ext, wkv_ext = _mla_weights(mla_w_in[mi], mla_w_q_b[mi], mla_w_kv_b[mi])
            q, kn, v, krp, ckv, kraw = _mla_proj(
                x_parts, joint, mods, gpre1, win_ext, mla_q_norm[mi][None, :], mla_kv_norm[mi][None, :],
                wq_ext, wkv_ext, cos128, sin128)
            outs["ckv"] = ckv.reshape(BATCH, 1, SEQ, MLA_KV_LORA)
            outs["krope"] = kraw[:, :MLA_ROPE].reshape(BATCH, 1, SEQ, MLA_ROPE)
            kvc = _ctx_expand(cache_mla_ckv[:, mi].reshape(DEC_BATCH * PAST_LEN, MLA_KV_LORA), wkv_ext)
            kvc = kvc.reshape(DEC_BATCH, PAST_LEN, 2 * HD)
            krpc = jnp.pad(cache_mla_krope[:, mi], ((0, 0), (0, 0), (0, LANES - MLA_ROPE))).astype(BF16)
            o_s = _attention(q, kn, krp, v, (kvc[:, :, :HD], krpc, kvc[:, :, HD:]),
                             n_req=DEC_BATCH, seq_len=DEC_SEQ, row0=0, tq=512, tk=512)
            o_p = _attention(q, kn, krp, v, None, n_req=BATCH, seq_len=SEQ, row0=N_SAMPLE, tq=SEQ, tk=SEQ)
            mix, w_out = (o_s, o_p), mla_w_out[mi]
        x1, h3, aff_t = _out_post(mix, w_out, x_parts, joint, mods, gpost1, gpre2, moe_w_router[i].T)
        idx = _route(aff_t)
        aff_rows = aff_t.reshape(N_EXPERTS * NG, 1, GT)
        f3_lo, wg16, wu16, wd16 = _moe(idx, aff_rows, h3, moe_w_gate, moe_w_up, moe_w_down, i,
                                       group0=0, n_groups=MOE_FIRST_GROUPS, emit_bf16=True,
                                       ff_split=MOE_FIRST_FF_SPLIT)
        f3_hi, = _moe(idx, aff_rows, h3, wg16, wu16, wd16, 0,
                      group0=MOE_FIRST_GROUPS, n_groups=NG - MOE_FIRST_GROUPS, emit_bf16=False,
                      ff_split=MOE_REST_FF_SPLIT)
        last = i == DEPTH - 1
        x_new = _moe_post((f3_lo, f3_hi), x1, mods, gpost2, split_out=last)
        if not last:
            x_parts, joint = (x_new, x_new), True

    y_sample, y_prompt = x_new
    return (y_prompt.reshape(BATCH, SEQ, D_MODEL), y_sample.reshape(DEC_BATCH, DEC_SEQ, D_MODEL),
            outs["fwd"], outs["bwd"], outs["ckv"], outs["krope"])
```

```python
import functools
import math

import jax
import jax.numpy as jnp
from jax import lax
from jax.experimental import pallas as pl
from jax.experimental.pallas import tpu as pltpu

F32 = jnp.float32
BF16 = jnp.bfloat16

D_MODEL = 1024
BATCH = 16
SEQ = 256
DEPTH = 2
DEC_BATCH = 4
DEC_SEQ = 4096
PAST_LEN = 256
GRID_W = 64
RET_HEADS = 4
RET_DK = 256
RET_DV = 512
RET_QK_WIDTH = RET_HEADS * RET_DK
RET_V_WIDTH = RET_HEADS * RET_DV
RET_CHUNK = 128
MLA_HEADS = 8
MLA_NOPE = 128
MLA_ROPE = 64
MLA_V = 128
MLA_Q_LORA = 384
MLA_KV_LORA = 256
ROPE_BASE = 10000.0
N_EXPERTS = 16
EXPERT_FF = 1024
NORM_EPS = 1e-6

GT = DEC_SEQ
NG = DEC_BATCH + 1
NT = NG * GT
N_SAMPLE = DEC_BATCH * DEC_SEQ
N_PROMPT = BATCH * SEQ
CAP = 2 * GT // N_EXPERTS

LANES = 128
SUBLANES = 8
ROW_TILES = D_MODEL // LANES
TM = 512
TILES_PER_GROUP = GT // TM
SAMPLE_TILES = N_SAMPLE // TM
VMEM_LIMIT = 60 * 1024 * 1024


def _cparams(sem):
    return pltpu.CompilerParams(dimension_semantics=sem, vmem_limit_bytes=VMEM_LIMIT)


def _rms(x):
    return x * lax.rsqrt(jnp.mean(x * x, axis=-1, keepdims=True) + NORM_EPS)


def _sigmoid(x):
    return 1.0 / (1.0 + jnp.exp(-x))


def _part_specs(width, joint, split=SAMPLE_TILES, rows=TM):
    lo = pl.BlockSpec((rows, width), lambda *g: (jnp.minimum(g[-1], split - 1), 0))
    if joint:
        hi = pl.BlockSpec((rows, width), lambda *g: (jnp.maximum(g[-1], split), 0))
    else:
        hi = pl.BlockSpec((rows, width), lambda *g: (jnp.maximum(g[-1] - split, 0), 0))
    return [lo, hi]


def _load_rows(lo_ref, hi_ref, i, split=SAMPLE_TILES):
    return jnp.where(i < split, lo_ref[...], hi_ref[...])


def _mod_spec():
    return pl.BlockSpec((1, 6, D_MODEL), lambda *g: (g[-1] // TILES_PER_GROUP, 0, 0))


def _const_spec(shape):
    return pl.BlockSpec(shape, lambda *g: (0,) * len(shape))


def _adaln_kernel(c_ref, w_ref, b_ref, o_ref):
    c = c_ref[...]
    s = (c * _sigmoid(c)).astype(BF16)
    o_ref[...] = jnp.dot(s, w_ref[0].astype(BF16), preferred_element_type=F32) + b_ref[0]


def _adaln(cond8, w, b, layer):
    tn = 1536
    return pl.pallas_call(
        _adaln_kernel,
        grid=(6 * D_MODEL // tn,),
        in_specs=[pl.BlockSpec((8, D_MODEL), lambda j: (0, 0)),
                  pl.BlockSpec((1, D_MODEL, tn), lambda j: (layer, 0, j)),
                  pl.BlockSpec((1, 1, tn), lambda j: (layer, 0, j))],
        out_specs=pl.BlockSpec((8, tn), lambda j: (0, j)),
        out_shape=jax.ShapeDtypeStruct((8, 6 * D_MODEL), F32),
        compiler_params=_cparams(("arbitrary",)),
        name="adaln",
    )(cond8, w, b)


def _nmm_kernel(xlo_ref, xhi_ref, mod_ref, gain_ref, w_ref, o_ref, w_scr):
    i = pl.program_id(1)

    @pl.when(i == 0)
    def _():
        w_scr[...] = w_ref[...].astype(BF16)

    m = mod_ref[0]
    h = _rms(_load_rows(xlo_ref, xhi_ref, i)) * gain_ref[...] * (1.0 + m[1:2]) + m[0:1]
    o_ref[...] = jnp.dot(h.astype(BF16), w_scr[...], preferred_element_type=F32).astype(o_ref.dtype)


def _norm_mod_matmul(x_parts, joint, mods, gain, w, tn):
    n_out = w.shape[1]
    return pl.pallas_call(
        _nmm_kernel,
        grid=(n_out // tn, NT // TM),
        in_specs=[*_part_specs(D_MODEL, joint), _mod_spec(), _const_spec((1, D_MODEL)),
                  pl.BlockSpec((D_MODEL, tn), lambda j, i: (0, j))],
        out_specs=pl.BlockSpec((TM, tn), lambda j, i: (i, j)),
        out_shape=jax.ShapeDtypeStruct((NT, n_out), BF16),
        scratch_shapes=[pltpu.VMEM((D_MODEL, tn), BF16)],
        compiler_params=_cparams(("arbitrary", "arbitrary")),
        name="norm_mod_matmul",
    )(*x_parts, mods, gain, w)


def _ret_kernel(lg_ref, q_ref, k_ref, v_ref, g_ref, *rest, seq_len, has_init):
    if has_init:
        s0f_ref, s0b_ref, *rest = rest
    yg_ref, sf_ref, sb_ref, ycross, sf_scr, sb_scr = rest
    C = RET_CHUNK
    nc = seq_len // C
    head = pl.program_id(1)
    lgf = lg_ref[0, head]
    lgb = lg_ref[1, head]
    k_scale = RET_DK ** -0.5

    ii = lax.broadcasted_iota(jnp.int32, (C, C), 0).astype(F32)
    jj = lax.broadcasted_iota(jnp.int32, (C, C), 1).astype(F32)
    diff = ii - jj
    inner = jnp.where(diff >= 0, jnp.exp(diff * lgf), jnp.exp(-diff * lgb)) * k_scale
    pos = lax.broadcasted_iota(jnp.int32, (C, 1), 0).astype(F32)
    cross_f = jnp.exp((pos + 1.0) * lgf)
    cross_b = jnp.exp((C - pos) * lgb)
    sdec_f = jnp.exp((C - 1.0 - pos) * lgf) * k_scale
    sdec_b = jnp.exp(pos * lgb) * k_scale
    one = jnp.ones((1, 1), F32)
    cdec_f = jnp.exp(one * (C * lgf))
    cdec_b = jnp.exp(one * (C * lgb))

    if has_init:
        sf_scr[...] = s0f_ref[0, 0]
        sb_scr[...] = s0b_ref[0, 0]
    else:
        sf_scr[...] = jnp.zeros_like(sf_scr)
        sb_scr[...] = jnp.zeros_like(sb_scr)

    contract0 = (((0,), (0,)), ((), ()))
    contract1 = (((1,), (1,)), ((), ()))

    def bwd(t, carry):
        rows = pl.ds(pl.multiple_of((nc - 1 - t) * C, C), C)
        q = q_ref[rows, :]
        s = sb_scr[...]
        ycross[rows, :] = jnp.dot(q, s.astype(BF16), preferred_element_type=F32) * cross_b
        kd = (k_ref[rows, :].astype(F32) * sdec_b).astype(BF16)
        sb_scr[...] = s * cdec_b + lax.dot_general(kd, v_ref[rows, :], contract0,
                                                   preferred_element_type=F32)
        return carry

    lax.fori_loop(0, nc, bwd, 0)

    def fwd(c, carry):
        rows = pl.ds(pl.multiple_of(c * C, C), C)
        q = q_ref[rows, :]
        k = k_ref[rows, :]
        v = v_ref[rows, :]
        s = sf_scr[...]
        scores = lax.dot_general(q, k, contract1, preferred_element_type=F32) * inner
        y = (jnp.dot(scores.astype(BF16), v, preferred_element_type=F32)
             + jnp.dot(q, s.astype(BF16), preferred_element_type=F32) * cross_f
             + ycross[rows, :])
        mu = jnp.mean(y, axis=-1, keepdims=True)
        yc = y - mu
        yn = yc * lax.rsqrt(jnp.mean(yc * yc, axis=-1, keepdims=True) + NORM_EPS)
        g = g_ref[rows, :].astype(F32)
        yg_ref[rows, :] = (g * _sigmoid(g) * yn).astype(yg_ref.dtype)
        kd = (k.astype(F32) * sdec_f).astype(BF16)
        sf_scr[...] = s * cdec_f + lax.dot_general(kd, v, contract0, preferred_element_type=F32)
        return carry

    lax.fori_loop(0, nc, fwd, 0)
    sf_ref[0, 0] = sf_scr[...]
    sb_ref[0, 0] = sb_scr[...]


def _retention(qkvg, lg, s0f, s0b, *, n_req, seq_len, row0):
    rb0 = row0 // seq_len
    has_init = s0f is not None
    qk_spec = lambda off: pl.BlockSpec((seq_len, RET_DK), lambda b, h, lg_: (rb0 + b, off + h))
    vg_spec = lambda off: pl.BlockSpec((seq_len, RET_DV), lambda b, h, lg_: (rb0 + b, off + h))
    st_spec = pl.BlockSpec((1, 1, RET_DK, RET_DV), lambda b, h, lg_: (b, h, 0, 0))
    in_specs = [qk_spec(0), qk_spec(RET_HEADS), vg_spec(RET_HEADS), vg_spec(2 * RET_HEADS)]
    args = [qkvg, qkvg, qkvg, qkvg]
    if has_init:
        in_specs += [st_spec, st_spec]
        args += [s0f, s0b]
    st_shape = jax.ShapeDtypeStruct((n_req, RET_HEADS, RET_DK, RET_DV), F32)
    return pl.pallas_call(
        functools.partial(_ret_kernel, seq_len=seq_len, has_init=has_init),
        grid_spec=pltpu.PrefetchScalarGridSpec(
            num_scalar_prefetch=1,
            grid=(n_req, RET_HEADS),
            in_specs=in_specs,
            out_specs=[pl.BlockSpec((seq_len, RET_DV), lambda b, h, lg_: (b, h)), st_spec, st_spec],
            scratch_shapes=[pltpu.VMEM((seq_len, RET_DV), F32),
                            pltpu.VMEM((RET_DK, RET_DV), F32),
                            pltpu.VMEM((RET_DK, RET_DV), F32)]),
        out_shape=[jax.ShapeDtypeStruct((n_req * seq_len, RET_V_WIDTH), BF16), st_shape, st_shape],
        compiler_params=_cparams(("parallel", "arbitrary")),
        name=f"retention_{seq_len}",
    )(lg, *args)


def _split_bf16(x):
    hi = x.astype(BF16)
    return hi, (x - hi.astype(F32)).astype(BF16)


def _out_post_kernel(ylo_ref, yhi_ref, w_ref, xlo_ref, xhi_ref, mod_ref, gpost_ref, gpre_ref, wr_ref,
                     x1_ref, h3_ref, aff_ref, w_scr):
    i = pl.program_id(0)

    @pl.when(i == 0)
    def _():
        w_scr[...] = w_ref[...].astype(BF16)

    m = mod_ref[0]
    y = jnp.dot(_load_rows(ylo_ref, yhi_ref, i), w_scr[...], preferred_element_type=F32)
    x1 = _load_rows(xlo_ref, xhi_ref, i) + _rms(y) * gpost_ref[...] * m[2:3]
    x1_ref[...] = x1
    h = _rms(x1) * gpre_ref[...] * (1.0 + m[4:5]) + m[3:4]
    for c in range(ROW_TILES):
        h3_ref[pl.ds(c, TM, stride=ROW_TILES), :] = h[:, c * LANES:(c + 1) * LANES]
    contract1 = (((1,), (1,)), ((), ()))
    w_hi, w_lo = _split_bf16(wr_ref[...])
    h_hi, h_lo = _split_bf16(h)
    part = lax.dot_general(jnp.concatenate([w_hi, w_lo], axis=0), h_hi, contract1,
                           preferred_element_type=F32)
    logits = (part[:N_EXPERTS] + part[N_EXPERTS:]
              + lax.dot_general(w_hi, h_lo, contract1, preferred_element_type=F32))
    e = jnp.exp(logits - jnp.max(logits, axis=0, keepdims=True))
    aff_ref[...] = e / jnp.sum(e, axis=0, keepdims=True)


def _out_post(y_parts, w, x_parts, joint, mods, gpost, gpre, w_router_t):
    kdim = w.shape[0]
    return pl.pallas_call(
        _out_post_kernel,
        grid=(NT // TM,),
        in_specs=[*_part_specs(kdim, joint=False),
                  _const_spec((kdim, D_MODEL)),
                  *_part_specs(D_MODEL, joint), _mod_spec(),
                  _const_spec((1, D_MODEL)), _const_spec((1, D_MODEL)),
                  _const_spec((N_EXPERTS, D_MODEL))],
        out_specs=[pl.BlockSpec((TM, D_MODEL), lambda i: (i, 0)),
                   pl.BlockSpec((TM * ROW_TILES, LANES), lambda i: (i, 0)),
                   pl.BlockSpec((N_EXPERTS, TM), lambda i: (0, i))],
        out_shape=[jax.ShapeDtypeStruct((NT, D_MODEL), F32),
                   jax.ShapeDtypeStruct((NT * ROW_TILES, LANES), F32),
                   jax.ShapeDtypeStruct((N_EXPERTS, NT), F32)],
        scratch_shapes=[pltpu.VMEM((kdim, D_MODEL), BF16)],
        compiler_params=_cparams(("arbitrary",)),
        name="out_post",
    )(*y_parts, w, *x_parts, mods, gpost, gpre, w_router_t)


SCATTER_BATCH = 8
MOE_FIRST_GROUPS = 1
MOE_FIRST_FF_SPLIT = 4
MOE_REST_FF_SPLIT = 1


def _moe_kernel(idx_ref, gate_ref, h3_ref, wg_ref, wu_ref, wd_ref, f3_ref, *rest, emit_bf16, ff_split):
    if emit_bf16:
        wg16_ref, wu16_ref, wd16_ref, *rest = rest
    xg, x_a, x_b, yacc, y_a, y_b = rest
    e = pl.program_id(1)
    f = pl.program_id(2)

    def tile(r):
        return pl.ds(pl.multiple_of(r * SUBLANES, SUBLANES), SUBLANES)

    def gather(ex, x_dst):
        base = ex * CAP
        for r in range(CAP):
            xg[r * SUBLANES:(r + 1) * SUBLANES, :] = h3_ref[tile(idx_ref[0, 0, base + r]), :]
        x_dst[...] = jnp.concatenate(
            [xg[pl.ds(c, CAP, stride=ROW_TILES), :] for c in range(ROW_TILES)], axis=1).astype(BF16)

    def scatter(ex, y_src):
        base = ex * CAP
        for r0 in range(0, CAP, SCATTER_BATCH):
            dst, vals = [], []
            for r in range(r0, r0 + SCATTER_BATCH):
                t = idx_ref[0, 0, base + r]
                dst.append(tile(t))
                vals.append(f3_ref[tile(t), :]
                            + gate_ref[0, 0, base + r] * y_src[r * SUBLANES:(r + 1) * SUBLANES, :])
            for d, v in zip(dst, vals):
                f3_ref[d, :] = v

    def half_ffn(x_src):
        x = x_src[...]
        wg = wg_ref[0, 0].astype(BF16)
        wu = wu_ref[0, 0].astype(BF16)
        wd = wd_ref[0, 0].astype(BF16)
        if emit_bf16:
            wg16_ref[0, 0] = wg
            wu16_ref[0, 0] = wu
            wd16_ref[0, 0] = wd
        a = jnp.dot(x, wg, preferred_element_type=F32)
        u = jnp.dot(x, wu, preferred_element_type=F32)
        act = (a * _sigmoid(a) * u).astype(BF16)
        return jnp.dot(act, wd, preferred_element_type=F32)

    first = f == 0
    last = f == ff_split - 1

    @pl.when(jnp.logical_and(e == 0, first))
    def _():
        f3_ref[...] = jnp.zeros_like(f3_ref)
        y_b[...] = jnp.zeros_like(y_b)
        gather(0, x_a)

    if ff_split == 1:
        phases = (("only", first),)
    elif ff_split == 2:
        phases = (("first", first), ("last", last))
    else:
        phases = (("first", first), ("middle", jnp.logical_not(jnp.logical_or(first, last))), ("last", last))
    for parity, (x_cur, x_nxt, y_cur, y_prv) in enumerate(((x_a, x_b, y_a, y_b), (x_b, x_a, y_b, y_a))):
        for phase, cond in phases:
            @pl.when(jnp.logical_and(e % 2 == parity, cond))
            def _():
                if phase in ("only", "first"):
                    gather(jnp.minimum(e + 1, N_EXPERTS - 1), x_nxt)
                part = half_ffn(x_cur)
                if phase == "first":
                    yacc[...] = part
                elif phase == "middle":
                    yacc[...] += part
                else:
                    y = part if phase == "only" else yacc[...] + part
                    for c in range(ROW_TILES):
                        y_cur[pl.ds(c, CAP, stride=ROW_TILES), :] = y[:, c * LANES:(c + 1) * LANES]
                    scatter(jnp.maximum(e - 1, 0), y_prv)

    @pl.when(jnp.logical_and(e == N_EXPERTS - 1, last))
    def _():
        scatter(N_EXPERTS - 1, y_b)


def _moe(idx, gate, h3, wg, wu, wd, layer, *, group0, n_groups, emit_bf16, ff_split):
    grp_rows = GT * ROW_TILES
    ff_tile = EXPERT_FF // ff_split
    route_spec = pl.BlockSpec((1, 1, N_EXPERTS * CAP), lambda b, e, f: (group0 + b, 0, 0),
                              memory_space=pltpu.SMEM)
    w_in_spec = lambda l: pl.BlockSpec((1, 1, D_MODEL, ff_tile), lambda b, e, f: (l, e, 0, f))
    w_out_spec = lambda l: pl.BlockSpec((1, 1, ff_tile, D_MODEL), lambda b, e, f: (l, e, f, 0))
    out_specs = [pl.BlockSpec((grp_rows, LANES), lambda b, e, f: (b, 0), pipeline_mode=pl.Buffered(1))]
    out_shape = [jax.ShapeDtypeStruct((n_groups * grp_rows, LANES), F32)]
    if emit_bf16:
        out_specs += [w_in_spec(0), w_in_spec(0), w_out_spec(0)]
        out_shape += [jax.ShapeDtypeStruct((1, N_EXPERTS, D_MODEL, EXPERT_FF), BF16),
                      jax.ShapeDtypeStruct((1, N_EXPERTS, D_MODEL, EXPERT_FF), BF16),
                      jax.ShapeDtypeStruct((1, N_EXPERTS, EXPERT_FF, D_MODEL), BF16)]
    return pl.pallas_call(
        functools.partial(_moe_kernel, emit_bf16=emit_bf16, ff_split=ff_split),
        grid=(n_groups, N_EXPERTS, ff_split),
        in_specs=[route_spec, route_spec,
                  pl.BlockSpec((grp_rows, LANES), lambda b, e, f: (group0 + b, 0),
                               pipeline_mode=pl.Buffered(1)),
                  w_in_spec(layer), w_in_spec(layer), w_out_spec(layer)],
        out_specs=out_specs,
        out_shape=out_shape,
        scratch_shapes=[pltpu.VMEM((CAP * ROW_TILES, LANES), F32),
                        pltpu.VMEM((CAP, D_MODEL), BF16),
                        pltpu.VMEM((CAP, D_MODEL), BF16),
                        pltpu.VMEM((CAP, D_MODEL), F32),
                        pltpu.VMEM((CAP * ROW_TILES, LANES), F32),
                        pltpu.VMEM((CAP * ROW_TILES, LANES), F32)],
        compiler_params=_cparams(("arbitrary", "arbitrary", "arbitrary")),
        name="moe_experts",
    )(idx, gate, h3, wg, wu, wd)


def _moe_post_kernel(f3lo_ref, f3hi_ref, x_ref, mod_ref, gpost_ref, *o_refs):
    i = pl.program_id(0)
    m = mod_ref[0]
    rows = lambda ref: jnp.concatenate(
        [ref[pl.ds(c, TM, stride=ROW_TILES), :] for c in range(ROW_TILES)], axis=1)
    fx = jnp.where(i < MOE_FIRST_GROUPS * TILES_PER_GROUP, rows(f3lo_ref), rows(f3hi_ref))
    out = x_ref[...] + _rms(fx) * gpost_ref[...] * m[5:6]
    if len(o_refs) == 1:
        o_refs[0][...] = out
    else:
        @pl.when(i < SAMPLE_TILES)
        def _():
            o_refs[0][...] = out

        @pl.when(i >= SAMPLE_TILES)
        def _():
            o_refs[1][...] = out


def _moe_post(f3_parts, x, mods, gpost, split_out):
    if split_out:
        out_specs = _part_specs(D_MODEL, joint=False)
        out_shape = [jax.ShapeDtypeStruct((N_SAMPLE, D_MODEL), F32),
                     jax.ShapeDtypeStruct((N_PROMPT, D_MODEL), F32)]
    else:
        out_specs = pl.BlockSpec((TM, D_MODEL), lambda i: (i, 0))
        out_shape = jax.ShapeDtypeStruct((NT, D_MODEL), F32)
    return pl.pallas_call(
        _moe_post_kernel,
        grid=(NT // TM,),
        in_specs=[*_part_specs(LANES, joint=False, split=MOE_FIRST_GROUPS * TILES_PER_GROUP,
                               rows=TM * ROW_TILES),
                  pl.BlockSpec((TM, D_MODEL), lambda i: (i, 0)),
                  _mod_spec(), _const_spec((1, D_MODEL))],
        out_specs=out_specs,
        out_shape=out_shape,
        compiler_params=_cparams(("arbitrary",)),
        name="moe_post",
    )(*f3_parts, x, mods, gpost)


def _route(aff_t):
    aff_s = jnp.swapaxes(aff_t[:, :N_SAMPLE].reshape(N_EXPERTS, DEC_BATCH, DEC_SEQ), 0, 1)
    aff_p = aff_t[:, N_SAMPLE:].reshape(N_EXPERTS, BATCH, SEQ)
    gate_s, idx_s = lax.top_k(aff_s, 2 * DEC_SEQ // N_EXPERTS)
    gate_p, idx_p = lax.top_k(aff_p, 2 * SEQ // N_EXPERTS)
    idx_p = idx_p + (jnp.arange(BATCH, dtype=jnp.int32) * SEQ)[None, :, None]
    idx = jnp.concatenate([idx_s, idx_p.reshape(1, N_EXPERTS, CAP)], axis=0).astype(jnp.int32)
    gate = jnp.concatenate([gate_s, gate_p.reshape(1, N_EXPERTS, CAP)], axis=0)
    return idx.reshape(NG, 1, N_EXPERTS * CAP), gate.reshape(NG, 1, N_EXPERTS * CAP)


HD = MLA_HEADS * LANES
Q_SCALE = math.log2(math.e) / math.sqrt(MLA_NOPE + MLA_ROPE)


def _mla_proj_kernel(xlo_ref, xhi_ref, mod_ref, gain_ref, win_ref, qn_g_ref, kvn_g_ref, wq_ref, wkv_ref,
                     cos_ref, sin_ref, q_ref, kn_ref, v_ref, krp_ref, ckv_ref, kraw_ref):
    i = pl.program_id(0)
    m = mod_ref[0]
    h = (_rms(_load_rows(xlo_ref, xhi_ref, i)) * gain_ref[...] * (1.0 + m[1:2]) + m[0:1]).astype(BF16)
    lat = jnp.dot(h, win_ref[...], preferred_element_type=F32)
    q_lat = lat[:, :MLA_Q_LORA]
    kv_lat = lat[:, MLA_Q_LORA:MLA_Q_LORA + MLA_KV_LORA]
    kr = lat[:, MLA_Q_LORA + MLA_KV_LORA:MLA_Q_LORA + MLA_KV_LORA + LANES]
    kr_rot = lat[:, MLA_Q_LORA + MLA_KV_LORA + LANES:]
    is_latent = i < SAMPLE_TILES
    cos = jnp.where(is_latent, cos_ref[...], 1.0)
    sin = jnp.where(is_latent, sin_ref[...], 0.0)

    qln = (_rms(q_lat) * qn_g_ref[...]).astype(BF16)
    q = jnp.dot(qln, wq_ref[...], preferred_element_type=F32) * Q_SCALE
    for hd in range(MLA_HEADS):
        lo = HD + hd * LANES
        q_ref[:, 2 * hd * LANES:(2 * hd + 1) * LANES] = q[:, hd * LANES:(hd + 1) * LANES].astype(BF16)
        q_ref[:, (2 * hd + 1) * LANES:(2 * hd + 2) * LANES] = (
            q[:, lo:lo + LANES] * cos + q[:, HD + lo:HD + lo + LANES] * sin).astype(BF16)

    ckv = _rms(kv_lat) * kvn_g_ref[...]
    kv = jnp.dot(ckv.astype(BF16), wkv_ref[...], preferred_element_type=F32)
    kn_ref[...] = kv[:, :HD].astype(BF16)
    v_ref[...] = kv[:, HD:].astype(BF16)
    krp_ref[...] = (kr * cos + kr_rot * sin).astype(BF16)

    @pl.when(i >= SAMPLE_TILES)
    def _():
        ckv_ref[...] = ckv
        kraw_ref[...] = kr


def _mla_proj(x_parts, joint, mods, gain, win_ext, qn_g, kvn_g, wq_ext, wkv_ext, cos128, sin128):
    row = lambda w: pl.BlockSpec((TM, w), lambda i: (i, 0))
    ctx_row = lambda w: pl.BlockSpec((TM, w), lambda i: (jnp.maximum(i - SAMPLE_TILES, 0), 0))
    tab = pl.BlockSpec((TM, LANES), lambda i: (i % TILES_PER_GROUP, 0))
    return pl.pallas_call(
        _mla_proj_kernel,
        grid=(NT // TM,),
        in_specs=[*_part_specs(D_MODEL, joint), _mod_spec(),
                  _const_spec(gain.shape), _const_spec(win_ext.shape), _const_spec(qn_g.shape),
                  _const_spec(kvn_g.shape), _const_spec(wq_ext.shape), _const_spec(wkv_ext.shape),
                  tab, tab],
        out_specs=[row(2 * HD), row(HD), row(HD), row(LANES), ctx_row(MLA_KV_LORA), ctx_row(LANES)],
        out_shape=[jax.ShapeDtypeStruct((NT, 2 * HD), BF16),
                   jax.ShapeDtypeStruct((NT, HD), BF16),
                   jax.ShapeDtypeStruct((NT, HD), BF16),
                   jax.ShapeDtypeStruct((NT, LANES), BF16),
                   jax.ShapeDtypeStruct((N_PROMPT, MLA_KV_LORA), F32),
                   jax.ShapeDtypeStruct((N_PROMPT, LANES), F32)],
        compiler_params=_cparams(("arbitrary",)),
        name="mla_proj",
    )(*x_parts, mods, gain, win_ext, qn_g, kvn_g, wq_ext, wkv_ext, cos128, sin128)


def _matmul_kernel(a_ref, w_ref, o_ref):
    o_ref[...] = jnp.dot(a_ref[...].astype(BF16), w_ref[...],
                         preferred_element_type=F32).astype(o_ref.dtype)


def _ctx_expand(ckv_ctx, wkv_ext):
    n = ckv_ctx.shape[0]
    return pl.pallas_call(
        _matmul_kernel,
        grid=(n // TM,),
        in_specs=[pl.BlockSpec((TM, MLA_KV_LORA), lambda i: (i, 0)),
                  pl.BlockSpec(wkv_ext.shape, lambda i: (0, 0))],
        out_specs=pl.BlockSpec((TM, 2 * HD), lambda i: (i, 0)),
        out_shape=jax.ShapeDtypeStruct((n, 2 * HD), BF16),
        compiler_params=_cparams(("parallel",)),
        name="ctx_expand",
    )(ckv_ctx, wkv_ext)


ATTN_HEADS_PER_STEP = 2
ATTN_CHUNK_UNROLL = 8


def _attn_kernel(q_ref, kn_ref, krp_ref, v_ref, *rest, tk, n_chunks, has_ctx):
    if has_ctx:
        knc_ref, krpc_ref, vc_ref, o_ref = rest
    else:
        o_ref, = rest
    tq = q_ref.shape[0]
    contract1 = (((1,), (1,)), ((), ()))
    heads = range(ATTN_HEADS_PER_STEP)
    head_cols = lambda hd: slice(hd * LANES, (hd + 1) * LANES)
    qs = [q_ref[:, 2 * hd * LANES:2 * (hd + 1) * LANES] for hd in heads]

    def ones_column(rows):
        return (lax.broadcasted_iota(jnp.int32, (rows, LANES), 1) == 0).astype(BF16)

    def step(q, kn, krp, v, ones, carry):
        m, acc = carry
        kc = jnp.concatenate([kn, krp], axis=1)
        s = lax.dot_general(q, kc, contract1, preferred_element_type=F32)
        m_new = jnp.maximum(m, jnp.max(s, axis=-1, keepdims=True))
        p = jnp.exp2(s - m_new).astype(BF16)
        pv = jnp.dot(p, jnp.concatenate([v, ones], axis=1), preferred_element_type=F32)
        return m_new, jnp.exp2(m - m_new) * acc + pv

    carry = tuple((jnp.full((tq, 1), -jnp.inf, F32), jnp.zeros((tq, 2 * LANES), F32)) for _ in heads)
    if has_ctx:
        ones = ones_column(PAST_LEN)
        carry = tuple(step(qs[hd], knc_ref[0, :, head_cols(hd)], krpc_ref[0], vc_ref[0, :, head_cols(hd)],
                           ones, carry[hd]) for hd in heads)
    ones = ones_column(tk)

    def chunk(c, carry):
        rows = pl.ds(pl.multiple_of(c * tk, tk), tk)
        krp = krp_ref[rows, :]
        return tuple(step(qs[hd], kn_ref[rows, head_cols(hd)], krp, v_ref[rows, head_cols(hd)],
                          ones, carry[hd]) for hd in heads)

    unroll = math.gcd(n_chunks, ATTN_CHUNK_UNROLL)

    def body(c, carry):
        for u in range(unroll):
            carry = chunk(c * unroll + u, carry)
        return carry

    carry = lax.fori_loop(0, n_chunks // unroll, body, carry)
    for hd in heads:
        acc = carry[hd][1]
        o_ref[:, head_cols(hd)] = (acc[:, :MLA_V] / acc[:, MLA_V:MLA_V + 1]).astype(o_ref.dtype)


def _attention(q, kn, krp, v, ctx, *, n_req, seq_len, row0, tq, tk):
    rb0 = row0 // seq_len
    qb0 = row0 // tq
    nq = seq_len // tq
    hw = ATTN_HEADS_PER_STEP * LANES
    o_spec = pl.BlockSpec((tq, hw), lambda b, h, i: (b * nq + i, h))
    k_spec = pl.BlockSpec((seq_len, hw), lambda b, h, i: (rb0 + b, h))
    in_specs = [pl.BlockSpec((tq, 2 * hw), lambda b, h, i: (qb0 + b * nq + i, h)), k_spec,
                pl.BlockSpec((seq_len, LANES), lambda b, h, i: (rb0 + b, 0)), k_spec]
    args = [q, kn, krp, v]
    if ctx is not None:
        c_spec = pl.BlockSpec((1, PAST_LEN, hw), lambda b, h, i: (b, 0, h))
        in_specs += [c_spec, pl.BlockSpec((1, PAST_LEN, LANES), lambda b, h, i: (b, 0, 0)), c_spec]
        args += list(ctx)
    return pl.pallas_call(
        functools.partial(_attn_kernel, tk=tk, n_chunks=seq_len // tk, has_ctx=ctx is not None),
        grid=(n_req, MLA_HEADS // ATTN_HEADS_PER_STEP, nq),
        in_specs=in_specs,
        out_specs=o_spec,
        out_shape=jax.ShapeDtypeStruct((n_req * seq_len, HD), BF16),
        compiler_params=_cparams(("parallel", "parallel", "arbitrary")),
        name=f"attention_{seq_len}",
    )(*args)


def _rope_rot_cols(w):
    w1, w2, w3, w4 = jnp.split(w, 4, axis=-1)
    return jnp.concatenate([-w2, w1, -w4, w3], axis=-1)


def _pad_cols(w, width):
    return jnp.pad(w, ((0, 0), (0, width - w.shape[1])))


def _mla_weights(w_in, w_q_b, w_kv_b):
    w_rope = w_in[:, MLA_Q_LORA + MLA_KV_LORA:]
    win_ext = jnp.concatenate([w_in[:, :MLA_Q_LORA + MLA_KV_LORA], _pad_cols(w_rope, LANES),
                               _pad_cols(_rope_rot_cols(w_rope), LANES)], axis=1).astype(BF16)
    wq = w_q_b.reshape(MLA_Q_LORA, MLA_HEADS, MLA_NOPE + MLA_ROPE)
    wq_nope = wq[:, :, :MLA_NOPE].reshape(MLA_Q_LORA, HD)
    wq_rope = wq[:, :, MLA_NOPE:]
    pad = ((0, 0), (0, 0), (0, LANES - MLA_ROPE))
    wq_rope_p = jnp.pad(wq_rope, pad).reshape(MLA_Q_LORA, HD)
    wq_rot_p = jnp.pad(_rope_rot_cols(wq_rope), pad).reshape(MLA_Q_LORA, HD)
    wq_ext = jnp.concatenate([wq_nope, wq_rope_p, wq_rot_p], axis=1).astype(BF16)
    wkv = w_kv_b.reshape(MLA_KV_LORA, MLA_HEADS, MLA_NOPE + MLA_V)
    wkv_ext = jnp.concatenate([wkv[:, :, :MLA_NOPE].reshape(MLA_KV_LORA, HD),
                               wkv[:, :, MLA_NOPE:].reshape(MLA_KV_LORA, HD)], axis=1).astype(BF16)
    return win_ext, wq_ext, wkv_ext


def _rope_tables():
    rows = DEC_SEQ // GRID_W
    row = jnp.repeat(jnp.arange(rows, dtype=F32), GRID_W)
    col = jnp.tile(jnp.arange(GRID_W, dtype=F32), rows)
    half = MLA_ROPE // 2
    inv = 1.0 / (ROPE_BASE ** (jnp.arange(0, half, 2, dtype=F32) / half))
    ar = row[:, None] * inv
    ac = col[:, None] * inv
    ang = jnp.concatenate([ar, ar, ac, ac] * 2, axis=-1)
    return jnp.cos(ang), jnp.sin(ang)


def kernel(x_prompt, x_sample, state_ret_fwd, state_ret_bwd, cache_mla_ckv, cache_mla_krope, c, c_ctx,
           ada_w, ada_b, norm_pre, norm_post, ret_w_in, ret_decay_fwd, ret_decay_bwd, ret_w_out,
           mla_w_in, mla_q_norm, mla_kv_norm, mla_w_q_b, mla_w_kv_b, mla_w_out,
           moe_w_router, moe_w_gate, moe_w_up, moe_w_down):
    x_parts = (x_sample.reshape(N_SAMPLE, D_MODEL), x_prompt.reshape(N_PROMPT, D_MODEL))
    joint = False
    cond8 = jnp.concatenate([c, c_ctx[None, :], jnp.zeros((8 - NG, D_MODEL), F32)], axis=0)
    cos128, sin128 = _rope_tables()
    outs = {}

    for i in range(DEPTH):
        mods = _adaln(cond8, ada_w, ada_b.reshape(DEPTH, 1, 6 * D_MODEL), i)[:NG].reshape(NG, 6, D_MODEL)
        gpre1 = norm_pre[i, 0][None, :]
        gpre2 = norm_pre[i, 1][None, :]
        gpost1 = norm_post[i, 0][None, :]
        gpost2 = norm_post[i, 1][None, :]
        if i % 2 == 0:
            r = i // 2
            qkvg = _norm_mod_matmul(x_parts, joint, mods, gpre1, ret_w_in[r], 1536)
            lg = jnp.stack([jax.nn.log_sigmoid(ret_decay_fwd[r].astype(F32)),
                            jax.nn.log_sigmoid(ret_decay_bwd[r].astype(F32))])
            yg_s, _, _ = _retention(qkvg, lg, state_ret_fwd[:, r], state_ret_bwd[:, r],
                                    n_req=DEC_BATCH, seq_len=DEC_SEQ, row0=0)
            yg_p, s_f, s_b = _retention(qkvg, lg, None, None, n_req=BATCH, seq_len=SEQ, row0=N_SAMPLE)
            outs["fwd"], outs["bwd"] = s_f[:, None], s_b[:, None]
            mix, w_out = (yg_s, yg_p), ret_w_out[r]
        else:
            mi = i // 2
            win_ext, wq_ext, wkv_ext = _mla_weights(mla_w_in[mi], mla_w_q_b[mi], mla_w_kv_b[mi])
            q, kn, v, krp, ckv, kraw = _mla_proj(
                x_parts, joint, mods, gpre1, win_ext, mla_q_norm[mi][None, :], mla_kv_norm[mi][None, :],
                wq_ext, wkv_ext, cos128, sin128)
            outs["ckv"] = ckv.reshape(BATCH, 1, SEQ, MLA_KV_LORA)
            outs["krope"] = kraw[:, :MLA_ROPE].reshape(BATCH, 1, SEQ, MLA_ROPE)
            kvc = _ctx_expand(cache_mla_ckv[:, mi].reshape(DEC_BATCH * PAST_LEN, MLA_KV_LORA), wkv_ext)
            kvc = kvc.reshape(DEC_BATCH, PAST_LEN, 2 * HD)
            krpc = jnp.pad(cache_mla_krope[:, mi], ((0, 0), (0, 0), (0, LANES - MLA_ROPE))).astype(BF16)
            o_s = _attention(q, kn, krp, v, (kvc[:, :, :HD], krpc, kvc[:, :, HD:]),
                             n_req=DEC_BATCH, seq_len=DEC_SEQ, row0=0, tq=512, tk=512)
            o_p = _attention(q, kn, krp, v, None, n_req=BATCH, seq_len=SEQ, row0=N_SAMPLE, tq=SEQ, tk=SEQ)
            mix, w_out = (o_s, o_p), mla_w_out[mi]
        x1, h3, aff_t = _out_post(mix, w_out, x_parts, joint, mods, gpost1, gpre2, moe_w_router[i].T)
        idx, gate = _route(aff_t)
        f3_lo, wg16, wu16, wd16 = _moe(idx, gate, h3, moe_w_gate, moe_w_up, moe_w_down, i,
                                       group0=0, n_groups=MOE_FIRST_GROUPS, emit_bf16=True,
                                       ff_split=MOE_FIRST_FF_SPLIT)
        f3_hi, = _moe(idx, gate, h3, wg16, wu16, wd16, 0,
                      group0=MOE_FIRST_GROUPS, n_groups=NG - MOE_FIRST_GROUPS, emit_bf16=False,
                      ff_split=MOE_REST_FF_SPLIT)
        last = i == DEPTH - 1
        x_new = _moe_post((f3_lo, f3_hi), x1, mods, gpost2, split_out=last)
        if not last:
            x_parts, joint = (x_new, x_new), True

    y_sample, y_prompt = x_new
    return (y_prompt.reshape(BATCH, SEQ, D_MODEL), y_sample.reshape(DEC_BATCH, DEC_SEQ, D_MODEL),
            outs["fwd"], outs["bwd"], outs["ckv"], outs["krope"])
```

```python
import functools
import math

import jax
import jax.numpy as jnp
from jax import lax
from jax.experimental import pallas as pl
from jax.experimental.pallas import tpu as pltpu

F32 = jnp.float32
BF16 = jnp.bfloat16

D_MODEL = 1024
BATCH = 16
SEQ = 256
DEPTH = 2
DEC_BATCH = 4
DEC_SEQ = 4096
PAST_LEN = 256
GRID_W = 64
RET_HEADS = 4
RET_DK = 256
RET_DV = 512
RET_QK_WIDTH = RET_HEADS * RET_DK
RET_V_WIDTH = RET_HEADS * RET_DV
RET_CHUNK = 128
MLA_HEADS = 8
MLA_NOPE = 128
MLA_ROPE = 64
MLA_V = 128
MLA_Q_LORA = 384
MLA_KV_LORA = 256
ROPE_BASE = 10000.0
N_EXPERTS = 16
EXPERT_FF = 1024
NORM_EPS = 1e-6

GT = DEC_SEQ
NG = DEC_BATCH + 1
NT = NG * GT
N_SAMPLE = DEC_BATCH * DEC_SEQ
N_PROMPT = BATCH * SEQ
CAP = 2 * GT // N_EXPERTS

LANES = 128
SUBLANES = 8
ROW_TILES = D_MODEL // LANES
TM = 512
TILES_PER_GROUP = GT // TM
SAMPLE_TILES = N_SAMPLE // TM
VMEM_LIMIT = 60 * 1024 * 1024


def _cparams(sem):
    return pltpu.CompilerParams(dimension_semantics=sem, vmem_limit_bytes=VMEM_LIMIT)


def _rms(x):
    return x * lax.rsqrt(jnp.mean(x * x, axis=-1, keepdims=True) + NORM_EPS)


def _sigmoid(x):
    return 1.0 / (1.0 + jnp.exp(-x))


def _part_specs(width, joint, split=SAMPLE_TILES, rows=TM):
    lo = pl.BlockSpec((rows, width), lambda *g: (jnp.minimum(g[-1], split - 1), 0))
    if joint:
        hi = pl.BlockSpec((rows, width), lambda *g: (jnp.maximum(g[-1], split), 0))
    else:
        hi = pl.BlockSpec((rows, width), lambda *g: (jnp.maximum(g[-1] - split, 0), 0))
    return [lo, hi]


def _load_rows(lo_ref, hi_ref, i, split=SAMPLE_TILES):
    return jnp.where(i < split, lo_ref[...], hi_ref[...])


def _mod_spec():
    return pl.BlockSpec((1, 6, D_MODEL), lambda *g: (g[-1] // TILES_PER_GROUP, 0, 0))


def _const_spec(shape):
    return pl.BlockSpec(shape, lambda *g: (0,) * len(shape))


def _adaln_kernel(c_ref, w_ref, b_ref, o_ref):
    c = c_ref[...]
    s = (c * _sigmoid(c)).astype(BF16)
    o_ref[...] = jnp.dot(s, w_ref[0].astype(BF16), preferred_element_type=F32) + b_ref[0]


def _adaln(cond8, w, b, layer):
    tn = 1536
    return pl.pallas_call(
        _adaln_kernel,
        grid=(6 * D_MODEL // tn,),
        in_specs=[pl.BlockSpec((8, D_MODEL), lambda j: (0, 0)),
                  pl.BlockSpec((1, D_MODEL, tn), lambda j: (layer, 0, j)),
                  pl.BlockSpec((1, 1, tn), lambda j: (layer, 0, j))],
        out_specs=pl.BlockSpec((8, tn), lambda j: (0, j)),
        out_shape=jax.ShapeDtypeStruct((8, 6 * D_MODEL), F32),
        compiler_params=_cparams(("arbitrary",)),
        name="adaln",
    )(cond8, w, b)


def _nmm_kernel(xlo_ref, xhi_ref, mod_ref, gain_ref, w_ref, o_ref, w_scr):
    i = pl.program_id(1)

    @pl.when(i == 0)
    def _():
        w_scr[...] = w_ref[...].astype(BF16)

    m = mod_ref[0]
    h = _rms(_load_rows(xlo_ref, xhi_ref, i)) * gain_ref[...] * (1.0 + m[1:2]) + m[0:1]
    o_ref[...] = jnp.dot(h.astype(BF16), w_scr[...], preferred_element_type=F32).astype(o_ref.dtype)


def _norm_mod_matmul(x_parts, joint, mods, gain, w, tn):
    n_out = w.shape[1]
    return pl.pallas_call(
        _nmm_kernel,
        grid=(n_out // tn, NT // TM),
        in_specs=[*_part_specs(D_MODEL, joint), _mod_spec(), _const_spec((1, D_MODEL)),
                  pl.BlockSpec((D_MODEL, tn), lambda j, i: (0, j))],
        out_specs=pl.BlockSpec((TM, tn), lambda j, i: (i, j)),
        out_shape=jax.ShapeDtypeStruct((NT, n_out), BF16),
        scratch_shapes=[pltpu.VMEM((D_MODEL, tn), BF16)],
        compiler_params=_cparams(("arbitrary", "arbitrary")),
        name="norm_mod_matmul",
    )(*x_parts, mods, gain, w)


def _ret_kernel(lg_ref, q_ref, k_ref, v_ref, g_ref, *rest, seq_len, has_init):
    if has_init:
        s0f_ref, s0b_ref, *rest = rest
    yg_ref, sf_ref, sb_ref, ycross, sf_scr, sb_scr = rest
    C = RET_CHUNK
    nc = seq_len // C
    head = pl.program_id(1)
    lgf = lg_ref[0, head]
    lgb = lg_ref[1, head]
    k_scale = RET_DK ** -0.5

    ii = lax.broadcasted_iota(jnp.int32, (C, C), 0).astype(F32)
    jj = lax.broadcasted_iota(jnp.int32, (C, C), 1).astype(F32)
    diff = ii - jj
    inner = jnp.where(diff >= 0, jnp.exp(diff * lgf), jnp.exp(-diff * lgb)) * k_scale
    pos = lax.broadcasted_iota(jnp.int32, (C, 1), 0).astype(F32)
    cross_f = jnp.exp((pos + 1.0) * lgf)
    cross_b = jnp.exp((C - pos) * lgb)
    sdec_f = jnp.exp((C - 1.0 - pos) * lgf) * k_scale
    sdec_b = jnp.exp(pos * lgb) * k_scale
    one = jnp.ones((1, 1), F32)
    cdec_f = jnp.exp(one * (C * lgf))
    cdec_b = jnp.exp(one * (C * lgb))

    if has_init:
        sf_scr[...] = s0f_ref[0, 0]
        sb_scr[...] = s0b_ref[0, 0]
    else:
        sf_scr[...] = jnp.zeros_like(sf_scr)
        sb_scr[...] = jnp.zeros_like(sb_scr)

    contract0 = (((0,), (0,)), ((), ()))
    contract1 = (((1,), (1,)), ((), ()))

    def bwd(t, carry):
        rows = pl.ds(pl.multiple_of((nc - 1 - t) * C, C), C)
        q = q_ref[rows, :]
        s = sb_scr[...]
        ycross[rows, :] = jnp.dot(q, s.astype(BF16), preferred_element_type=F32) * cross_b
        kd = (k_ref[rows, :].astype(F32) * sdec_b).astype(BF16)
        sb_scr[...] = s * cdec_b + lax.dot_general(kd, v_ref[rows, :], contract0,
                                                   preferred_element_type=F32)
        return carry

    lax.fori_loop(0, nc, bwd, 0)

    def fwd(c, carry):
        rows = pl.ds(pl.multiple_of(c * C, C), C)
        q = q_ref[rows, :]
        k = k_ref[rows, :]
        v = v_ref[rows, :]
        s = sf_scr[...]
        scores = lax.dot_general(q, k, contract1, preferred_element_type=F32) * inner
        y = (jnp.dot(scores.astype(BF16), v, preferred_element_type=F32)
             + jnp.dot(q, s.astype(BF16), preferred_element_type=F32) * cross_f
             + ycross[rows, :])
        mu = jnp.mean(y, axis=-1, keepdims=True)
        yc = y - mu
        yn = yc * lax.rsqrt(jnp.mean(yc * yc, axis=-1, keepdims=True) + NORM_EPS)
        g = g_ref[rows, :].astype(F32)
        yg_ref[rows, :] = (g * _sigmoid(g) * yn).astype(yg_ref.dtype)
        kd = (k.astype(F32) * sdec_f).astype(BF16)
        sf_scr[...] = s * cdec_f + lax.dot_general(kd, v, contract0, preferred_element_type=F32)
        return carry

    lax.fori_loop(0, nc, fwd, 0)
    sf_ref[0, 0] = sf_scr[...]
    sb_ref[0, 0] = sb_scr[...]


def _retention(qkvg, lg, s0f, s0b, *, n_req, seq_len, row0):
    rb0 = row0 // seq_len
    has_init = s0f is not None
    qk_spec = lambda off: pl.BlockSpec((seq_len, RET_DK), lambda b, h, lg_: (rb0 + b, off + h))
    vg_spec = lambda off: pl.BlockSpec((seq_len, RET_DV), lambda b, h, lg_: (rb0 + b, off + h))
    st_spec = pl.BlockSpec((1, 1, RET_DK, RET_DV), lambda b, h, lg_: (b, h, 0, 0))
    in_specs = [qk_spec(0), qk_spec(RET_HEADS), vg_spec(RET_HEADS), vg_spec(2 * RET_HEADS)]
    args = [qkvg, qkvg, qkvg, qkvg]
    if has_init:
        in_specs += [st_spec, st_spec]
        args += [s0f, s0b]
    st_shape = jax.ShapeDtypeStruct((n_req, RET_HEADS, RET_DK, RET_DV), F32)
    return pl.pallas_call(
        functools.partial(_ret_kernel, seq_len=seq_len, has_init=has_init),
        grid_spec=pltpu.PrefetchScalarGridSpec(
            num_scalar_prefetch=1,
            grid=(n_req, RET_HEADS),
            in_specs=in_specs,
            out_specs=[pl.BlockSpec((seq_len, RET_DV), lambda b, h, lg_: (b, h)), st_spec, st_spec],
            scratch_shapes=[pltpu.VMEM((seq_len, RET_DV), F32),
                            pltpu.VMEM((RET_DK, RET_DV), F32),
                            pltpu.VMEM((RET_DK, RET_DV), F32)]),
        out_shape=[jax.ShapeDtypeStruct((n_req * seq_len, RET_V_WIDTH), BF16), st_shape, st_shape],
        compiler_params=_cparams(("parallel", "arbitrary")),
        name=f"retention_{seq_len}",
    )(lg, *args)


def _split_bf16(x):
    hi = x.astype(BF16)
    return hi, (x - hi.astype(F32)).astype(BF16)


def _out_post_kernel(ylo_ref, yhi_ref, w_ref, xlo_ref, xhi_ref, mod_ref, gpost_ref, gpre_ref, wr_ref,
                     x1_ref, h3_ref, aff_ref, w_scr):
    i = pl.program_id(0)

    @pl.when(i == 0)
    def _():
        w_scr[...] = w_ref[...].astype(BF16)

    m = mod_ref[0]
    y = jnp.dot(_load_rows(ylo_ref, yhi_ref, i), w_scr[...], preferred_element_type=F32)
    x1 = _load_rows(xlo_ref, xhi_ref, i) + _rms(y) * gpost_ref[...] * m[2:3]
    x1_ref[...] = x1
    h = _rms(x1) * gpre_ref[...] * (1.0 + m[4:5]) + m[3:4]
    for c in range(ROW_TILES):
        h3_ref[pl.ds(c, TM, stride=ROW_TILES), :] = h[:, c * LANES:(c + 1) * LANES]
    contract1 = (((1,), (1,)), ((), ()))
    w_hi, w_lo = _split_bf16(wr_ref[...])
    h_hi, h_lo = _split_bf16(h)
    part = lax.dot_general(jnp.concatenate([w_hi, w_lo], axis=0), h_hi, contract1,
                           preferred_element_type=F32)
    logits = (part[:N_EXPERTS] + part[N_EXPERTS:]
              + lax.dot_general(w_hi, h_lo, contract1, preferred_element_type=F32))
    e = jnp.exp(logits - jnp.max(logits, axis=0, keepdims=True))
    aff_ref[...] = e / jnp.sum(e, axis=0, keepdims=True)


def _out_post(y_parts, w, x_parts, joint, mods, gpost, gpre, w_router_t):
    kdim = w.shape[0]
    return pl.pallas_call(
        _out_post_kernel,
        grid=(NT // TM,),
        in_specs=[*_part_specs(kdim, joint=False),
                  _const_spec((kdim, D_MODEL)),
                  *_part_specs(D_MODEL, joint), _mod_spec(),
                  _const_spec((1, D_MODEL)), _const_spec((1, D_MODEL)),
                  _const_spec((N_EXPERTS, D_MODEL))],
        out_specs=[pl.BlockSpec((TM, D_MODEL), lambda i: (i, 0)),
                   pl.BlockSpec((TM * ROW_TILES, LANES), lambda i: (i, 0)),
                   pl.BlockSpec((N_EXPERTS, TM), lambda i: (0, i))],
        out_shape=[jax.ShapeDtypeStruct((NT, D_MODEL), F32),
                   jax.ShapeDtypeStruct((NT * ROW_TILES, LANES), F32),
                   jax.ShapeDtypeStruct((N_EXPERTS, NT), F32)],
        scratch_shapes=[pltpu.VMEM((kdim, D_MODEL), BF16)],
        compiler_params=_cparams(("arbitrary",)),
        name="out_post",
    )(*y_parts, w, *x_parts, mods, gpost, gpre, w_router_t)


SCATTER_BATCH = 8
MOE_FIRST_GROUPS = 1
MOE_FIRST_FF_SPLIT = 4
MOE_REST_FF_SPLIT = 1


def _moe_kernel(idx_ref, gate_ref, h3_ref, wg_ref, wu_ref, wd_ref, f3_ref, *rest, emit_bf16, ff_split):
    if emit_bf16:
        wg16_ref, wu16_ref, wd16_ref, *rest = rest
    xg, x_a, x_b, yacc, y_a, y_b = rest
    e = pl.program_id(1)
    f = pl.program_id(2)

    def tile(r):
        return pl.ds(pl.multiple_of(r * SUBLANES, SUBLANES), SUBLANES)

    def gather(ex, x_dst):
        base = ex * CAP
        for r in range(CAP):
            xg[r * SUBLANES:(r + 1) * SUBLANES, :] = h3_ref[tile(idx_ref[0, 0, base + r]), :]
        x_dst[...] = jnp.concatenate(
            [xg[pl.ds(c, CAP, stride=ROW_TILES), :] for c in range(ROW_TILES)], axis=1).astype(BF16)

    def scatter(ex, y_src):
        base = ex * CAP
        for r0 in range(0, CAP, SCATTER_BATCH):
            dst, vals = [], []
            for r in range(r0, r0 + SCATTER_BATCH):
                t = idx_ref[0, 0, base + r]
                dst.append(tile(t))
                vals.append(f3_ref[tile(t), :]
                            + gate_ref[0, 0, base + r] * y_src[r * SUBLANES:(r + 1) * SUBLANES, :])
            for d, v in zip(dst, vals):
                f3_ref[d, :] = v

    def half_ffn(x_src):
        x = x_src[...]
        wg = wg_ref[0, 0].astype(BF16)
        wu = wu_ref[0, 0].astype(BF16)
        wd = wd_ref[0, 0].astype(BF16)
        if emit_bf16:
            wg16_ref[0, 0] = wg
            wu16_ref[0, 0] = wu
            wd16_ref[0, 0] = wd
        a = jnp.dot(x, wg, preferred_element_type=F32)
        u = jnp.dot(x, wu, preferred_element_type=F32)
        act = (a * _sigmoid(a) * u).astype(BF16)
        return jnp.dot(act, wd, preferred_element_type=F32)

    first = f == 0
    last = f == ff_split - 1

    @pl.when(jnp.logical_and(e == 0, first))
    def _():
        f3_ref[...] = jnp.zeros_like(f3_ref)
        y_b[...] = jnp.zeros_like(y_b)
        gather(0, x_a)

    if ff_split == 1:
        phases = (("only", first),)
    elif ff_split == 2:
        phases = (("first", first), ("last", last))
    else:
        phases = (("first", first), ("middle", jnp.logical_not(jnp.logical_or(first, last))), ("last", last))
    for parity, (x_cur, x_nxt, y_cur, y_prv) in enumerate(((x_a, x_b, y_a, y_b), (x_b, x_a, y_b, y_a))):
        for phase, cond in phases:
            @pl.when(jnp.logical_and(e % 2 == parity, cond))
            def _():
                if phase in ("only", "first"):
                    gather(jnp.minimum(e + 1, N_EXPERTS - 1), x_nxt)
                part = half_ffn(x_cur)
                if phase == "first":
                    yacc[...] = part
                elif phase == "middle":
                    yacc[...] += part
                else:
                    y = part if phase == "only" else yacc[...] + part
                    for c in range(ROW_TILES):
                        y_cur[pl.ds(c, CAP, stride=ROW_TILES), :] = y[:, c * LANES:(c + 1) * LANES]
                    scatter(jnp.maximum(e - 1, 0), y_prv)

    @pl.when(jnp.logical_and(e == N_EXPERTS - 1, last))
    def _():
        scatter(N_EXPERTS - 1, y_b)


def _moe(idx, gate, h3, wg, wu, wd, layer, *, group0, n_groups, emit_bf16, ff_split):
    grp_rows = GT * ROW_TILES
    ff_tile = EXPERT_FF // ff_split
    route_spec = pl.BlockSpec((1, 1, N_EXPERTS * CAP), lambda b, e, f: (group0 + b, 0, 0),
                              memory_space=pltpu.SMEM)
    w_in_spec = lambda l: pl.BlockSpec((1, 1, D_MODEL, ff_tile), lambda b, e, f: (l, e, 0, f))
    w_out_spec = lambda l: pl.BlockSpec((1, 1, ff_tile, D_MODEL), lambda b, e, f: (l, e, f, 0))
    out_specs = [pl.BlockSpec((grp_rows, LANES), lambda b, e, f: (b, 0), pipeline_mode=pl.Buffered(1))]
    out_shape = [jax.ShapeDtypeStruct((n_groups * grp_rows, LANES), F32)]
    if emit_bf16:
        out_specs += [w_in_spec(0), w_in_spec(0), w_out_spec(0)]
        out_shape += [jax.ShapeDtypeStruct((1, N_EXPERTS, D_MODEL, EXPERT_FF), BF16),
                      jax.ShapeDtypeStruct((1, N_EXPERTS, D_MODEL, EXPERT_FF), BF16),
                      jax.ShapeDtypeStruct((1, N_EXPERTS, EXPERT_FF, D_MODEL), BF16)]
    return pl.pallas_call(
        functools.partial(_moe_kernel, emit_bf16=emit_bf16, ff_split=ff_split),
        grid=(n_groups, N_EXPERTS, ff_split),
        in_specs=[route_spec, route_spec,
                  pl.BlockSpec((grp_rows, LANES), lambda b, e, f: (group0 + b, 0),
                               pipeline_mode=pl.Buffered(1)),
                  w_in_spec(layer), w_in_spec(layer), w_out_spec(layer)],
        out_specs=out_specs,
        out_shape=out_shape,
        scratch_shapes=[pltpu.VMEM((CAP * ROW_TILES, LANES), F32),
                        pltpu.VMEM((CAP, D_MODEL), BF16),
                        pltpu.VMEM((CAP, D_MODEL), BF16),
                        pltpu.VMEM((CAP, D_MODEL), F32),
                        pltpu.VMEM((CAP * ROW_TILES, LANES), F32),
                        pltpu.VMEM((CAP * ROW_TILES, LANES), F32)],
        compiler_params=_cparams(("arbitrary", "arbitrary", "arbitrary")),
        name="moe_experts",
    )(idx, gate, h3, wg, wu, wd)


def _moe_post_kernel(f3lo_ref, f3hi_ref, x_ref, mod_ref, gpost_ref, *o_refs):
    i = pl.program_id(0)
    m = mod_ref[0]
    rows = lambda ref: jnp.concatenate(
        [ref[pl.ds(c, TM, stride=ROW_TILES), :] for c in range(ROW_TILES)], axis=1)
    fx = jnp.where(i < MOE_FIRST_GROUPS * TILES_PER_GROUP, rows(f3lo_ref), rows(f3hi_ref))
    out = x_ref[...] + _rms(fx) * gpost_ref[...] * m[5:6]
    if len(o_refs) == 1:
        o_refs[0][...] = out
    else:
        @pl.when(i < SAMPLE_TILES)
        def _():
            o_refs[0][...] = out

        @pl.when(i >= SAMPLE_TILES)
        def _():
            o_refs[1][...] = out


def _moe_post(f3_parts, x, mods, gpost, split_out):
    if split_out:
        out_specs = _part_specs(D_MODEL, joint=False)
        out_shape = [jax.ShapeDtypeStruct((N_SAMPLE, D_MODEL), F32),
                     jax.ShapeDtypeStruct((N_PROMPT, D_MODEL), F32)]
    else:
        out_specs = pl.BlockSpec((TM, D_MODEL), lambda i: (i, 0))
        out_shape = jax.ShapeDtypeStruct((NT, D_MODEL), F32)
    return pl.pallas_call(
        _moe_post_kernel,
        grid=(NT // TM,),
        in_specs=[*_part_specs(LANES, joint=False, split=MOE_FIRST_GROUPS * TILES_PER_GROUP,
                               rows=TM * ROW_TILES),
                  pl.BlockSpec((TM, D_MODEL), lambda i: (i, 0)),
                  _mod_spec(), _const_spec((1, D_MODEL))],
        out_specs=out_specs,
        out_shape=out_shape,
        compiler_params=_cparams(("arbitrary",)),
        name="moe_post",
    )(*f3_parts, x, mods, gpost)


ROUTE_SEG = SEQ
RANK_LO = 16
RANK_HI = CAP // RANK_LO
TOK_LO = 64
ROUTE_VALS = 5


def _route_kernel(aff_ref, idx_ref, gate_ref, sel_scr, a_scr, code_scr, g_scr, acc_scr):
    g = pl.program_id(0)
    n_seg = GT // ROUTE_SEG
    seg = lambda s: slice(s * ROUTE_SEG, (s + 1) * ROUTE_SEG)
    aff = aff_ref[...]
    tok = lax.broadcasted_iota(jnp.int32, (N_EXPERTS, GT), 1)
    ones_seg = jnp.ones((ROUTE_SEG, ROUTE_SEG), BF16)

    def row_count(mask):
        return jnp.sum(jnp.where(mask, 1.0, 0.0), axis=1, keepdims=True)

    def seg_count(mask):
        m = jnp.where(mask, 1.0, 0.0).astype(BF16)
        return jnp.concatenate(
            [jnp.dot(m[:, seg(s)], ones_seg, preferred_element_type=F32) for s in range(n_seg)], axis=1)

    def select(count, k, shape):
        as_float = lambda b: pltpu.bitcast(b, F32)

        def value_bit(i, t):
            cand = t | jnp.left_shift(jnp.int32(1), 30 - i)
            return jnp.where(count(aff >= as_float(cand)) >= k, cand, t)

        thr = lax.fori_loop(0, 31, value_bit, jnp.zeros(shape, jnp.int32))
        above = aff >= as_float(thr + 1)
        tied = jnp.logical_and(aff >= as_float(thr), jnp.logical_not(above))
        need = k - count(above)

        def index_bit(i, v):
            cand = v | jnp.left_shift(jnp.int32(1), 11 - i)
            return jnp.where(count(jnp.logical_and(tied, tok < cand)) < need, cand, v)

        last = lax.fori_loop(0, 12, index_bit, jnp.zeros(shape, jnp.int32))
        chosen = jnp.logical_or(above, jnp.logical_and(tied, tok <= last))
        sel_scr[...] = jnp.where(chosen, 1.0, 0.0)

    @pl.when(g < DEC_BATCH)
    def _():
        select(row_count, float(2 * DEC_SEQ // N_EXPERTS), (N_EXPERTS, 1))

    @pl.when(g >= DEC_BATCH)
    def _():
        select(seg_count, float(2 * SEQ // N_EXPERTS), (N_EXPERTS, GT))

    sel = sel_scr[...]
    r_i = lax.broadcasted_iota(jnp.int32, (ROUTE_SEG, ROUTE_SEG), 0)
    c_i = lax.broadcasted_iota(jnp.int32, (ROUTE_SEG, ROUTE_SEG), 1)
    upper = jnp.where(r_i <= c_i, 1.0, 0.0).astype(BF16)
    sel_b = sel.astype(BF16)
    off = jnp.zeros((N_EXPERTS, 1), F32)
    for s in range(n_seg):
        inc = jnp.dot(sel_b[:, seg(s)], upper, preferred_element_type=F32)
        rank = inc + off - sel[:, seg(s)]
        a = jnp.floor(rank * (1.0 / RANK_LO))
        a_scr[:, seg(s)] = jnp.where(sel[:, seg(s)] > 0, a, -1.0)
        code_scr[:, seg(s)] = jnp.where(sel[:, seg(s)] > 0, rank - RANK_LO * a + 1.0, 0.0).astype(BF16)
        off = off + inc[:, ROUTE_SEG - 1:ROUTE_SEG]
    g1 = aff.astype(BF16).astype(F32)
    g2 = (aff - g1).astype(BF16).astype(F32)
    g_scr[0] = g1
    g_scr[1] = g2
    g_scr[2] = (aff - g1 - g2).astype(BF16).astype(F32)

    lane = lax.broadcasted_iota(jnp.int32, (N_EXPERTS, N_EXPERTS * RANK_LO), 1)
    row = lax.broadcasted_iota(jnp.int32, (N_EXPERTS, N_EXPERTS * RANK_LO), 0)
    lo_bits = RANK_LO.bit_length() - 1
    expand = jnp.where(jnp.right_shift(lane, lo_bits) == row, 1.0, 0.0).astype(BF16)
    digit = (jnp.bitwise_and(lax.broadcasted_iota(jnp.int32, (1, N_EXPERTS * RANK_LO), 1), RANK_LO - 1)
             + 1).astype(F32)
    a_iota = lax.broadcasted_iota(jnp.int32, (RANK_HI, 1), 0).astype(F32)
    acc_scr[...] = jnp.zeros_like(acc_scr)
    contract0 = (((0,), (0,)), ((), ()))

    def chunk(c, carry):
        lanes = pl.ds(pl.multiple_of(c * ROUTE_SEG, ROUTE_SEG), ROUTE_SEG)
        a_c = a_scr[:, lanes]
        g_c = [g_scr[j, :, lanes] for j in range(3)]
        tok_c = lax.broadcasted_iota(jnp.int32, (1, ROUTE_SEG), 1) + c * ROUTE_SEG
        t_hi = jnp.right_shift(tok_c, TOK_LO.bit_length() - 1).astype(F32)
        t_lo = jnp.bitwise_and(tok_c, TOK_LO - 1).astype(F32)
        rows = []
        for e in range(N_EXPERTS):
            hit = a_c[e:e + 1, :] == a_iota
            for val in (t_hi, t_lo, g_c[0][e:e + 1, :], g_c[1][e:e + 1, :], g_c[2][e:e + 1, :]):
                rows.append(jnp.where(hit, val, 0.0))
        lhs = jnp.concatenate(rows, axis=0).astype(BF16)
        spread = lax.dot_general(code_scr[:, lanes], expand, contract0, preferred_element_type=F32)
        low_hot = jnp.where(spread == digit, 1.0, 0.0).astype(BF16)
        acc_scr[...] += jnp.dot(lhs, low_hot, preferred_element_type=F32)
        return carry

    lax.fori_loop(0, n_seg, chunk, 0)
    per_e = ROUTE_VALS * RANK_HI
    for e in range(N_EXPERTS):
        blk = acc_scr[e * per_e:(e + 1) * per_e, :][:, e * RANK_LO:(e + 1) * RANK_LO]
        part = lambda j: blk[j * RANK_HI:(j + 1) * RANK_HI]
        idx_ref[0, e] = (part(0) * TOK_LO + part(1)).astype(jnp.int32)
        gate_ref[0, e] = part(2) + part(3) + part(4)


def _route(aff_t):
    out_spec = pl.BlockSpec((1, N_EXPERTS, RANK_HI, RANK_LO), lambda g: (g, 0, 0, 0))
    idx, gate = pl.pallas_call(
        _route_kernel,
        grid=(NG,),
        in_specs=[pl.BlockSpec((N_EXPERTS, GT), lambda g: (0, g))],
        out_specs=[out_spec, out_spec],
        out_shape=[jax.ShapeDtypeStruct((NG, N_EXPERTS, RANK_HI, RANK_LO), jnp.int32),
                   jax.ShapeDtypeStruct((NG, N_EXPERTS, RANK_HI, RANK_LO), F32)],
        scratch_shapes=[pltpu.VMEM((N_EXPERTS, GT), F32),
                        pltpu.VMEM((N_EXPERTS, GT), F32),
                        pltpu.VMEM((N_EXPERTS, GT), BF16),
                        pltpu.VMEM((3, N_EXPERTS, GT), F32),
                        pltpu.VMEM((N_EXPERTS * ROUTE_VALS * RANK_HI, N_EXPERTS * RANK_LO), F32)],
        compiler_params=_cparams(("arbitrary",)),
        name="route",
    )(aff_t)
    return idx.reshape(NG, 1, N_EXPERTS * CAP), gate.reshape(NG, 1, N_EXPERTS * CAP)


HD = MLA_HEADS * LANES
Q_SCALE = math.log2(math.e) / math.sqrt(MLA_NOPE + MLA_ROPE)


def _mla_proj_kernel(xlo_ref, xhi_ref, mod_ref, gain_ref, win_ref, qn_g_ref, kvn_g_ref, wq_ref, wkv_ref,
                     cos_ref, sin_ref, q_ref, kn_ref, v_ref, krp_ref, ckv_ref, kraw_ref):
    i = pl.program_id(0)
    m = mod_ref[0]
    h = (_rms(_load_rows(xlo_ref, xhi_ref, i)) * gain_ref[...] * (1.0 + m[1:2]) + m[0:1]).astype(BF16)
    lat = jnp.dot(h, win_ref[...], preferred_element_type=F32)
    q_lat = lat[:, :MLA_Q_LORA]
    kv_lat = lat[:, MLA_Q_LORA:MLA_Q_LORA + MLA_KV_LORA]
    kr = lat[:, MLA_Q_LORA + MLA_KV_LORA:MLA_Q_LORA + MLA_KV_LORA + LANES]
    kr_rot = lat[:, MLA_Q_LORA + MLA_KV_LORA + LANES:]
    is_latent = i < SAMPLE_TILES
    cos = jnp.where(is_latent, cos_ref[...], 1.0)
    sin = jnp.where(is_latent, sin_ref[...], 0.0)

    qln = (_rms(q_lat) * qn_g_ref[...]).astype(BF16)
    q = jnp.dot(qln, wq_ref[...], preferred_element_type=F32) * Q_SCALE
    for hd in range(MLA_HEADS):
        lo = HD + hd * LANES
        q_ref[:, 2 * hd * LANES:(2 * hd + 1) * LANES] = q[:, hd * LANES:(hd + 1) * LANES].astype(BF16)
        q_ref[:, (2 * hd + 1) * LANES:(2 * hd + 2) * LANES] = (
            q[:, lo:lo + LANES] * cos + q[:, HD + lo:HD + lo + LANES] * sin).astype(BF16)

    ckv = _rms(kv_lat) * kvn_g_ref[...]
    kv = jnp.dot(ckv.astype(BF16), wkv_ref[...], preferred_element_type=F32)
    kn_ref[...] = kv[:, :HD].astype(BF16)
    v_ref[...] = kv[:, HD:].astype(BF16)
    krp_ref[...] = (kr * cos + kr_rot * sin).astype(BF16)

    @pl.when(i >= SAMPLE_TILES)
    def _():
        ckv_ref[...] = ckv
        kraw_ref[...] = kr


def _mla_proj(x_parts, joint, mods, gain, win_ext, qn_g, kvn_g, wq_ext, wkv_ext, cos128, sin128):
    row = lambda w: pl.BlockSpec((TM, w), lambda i: (i, 0))
    ctx_row = lambda w: pl.BlockSpec((TM, w), lambda i: (jnp.maximum(i - SAMPLE_TILES, 0), 0))
    tab = pl.BlockSpec((TM, LANES), lambda i: (i % TILES_PER_GROUP, 0))
    return pl.pallas_call(
        _mla_proj_kernel,
        grid=(NT // TM,),
        in_specs=[*_part_specs(D_MODEL, joint), _mod_spec(),
                  _const_spec(gain.shape), _const_spec(win_ext.shape), _const_spec(qn_g.shape),
                  _const_spec(kvn_g.shape), _const_spec(wq_ext.shape), _const_spec(wkv_ext.shape),
                  tab, tab],
        out_specs=[row(2 * HD), row(HD), row(HD), row(LANES), ctx_row(MLA_KV_LORA), ctx_row(LANES)],
        out_shape=[jax.ShapeDtypeStruct((NT, 2 * HD), BF16),
                   jax.ShapeDtypeStruct((NT, HD), BF16),
                   jax.ShapeDtypeStruct((NT, HD), BF16),
                   jax.ShapeDtypeStruct((NT, LANES), BF16),
                   jax.ShapeDtypeStruct((N_PROMPT, MLA_KV_LORA), F32),
                   jax.ShapeDtypeStruct((N_PROMPT, LANES), F32)],
        compiler_params=_cparams(("arbitrary",)),
        name="mla_proj",
    )(*x_parts, mods, gain, win_ext, qn_g, kvn_g, wq_ext, wkv_ext, cos128, sin128)


def _matmul_kernel(a_ref, w_ref, o_ref):
    o_ref[...] = jnp.dot(a_ref[...].astype(BF16), w_ref[...],
                         preferred_element_type=F32).astype(o_ref.dtype)


def _ctx_expand(ckv_ctx, wkv_ext):
    n = ckv_ctx.shape[0]
    return pl.pallas_call(
        _matmul_kernel,
        grid=(n // TM,),
        in_specs=[pl.BlockSpec((TM, MLA_KV_LORA), lambda i: (i, 0)),
                  pl.BlockSpec(wkv_ext.shape, lambda i: (0, 0))],
        out_specs=pl.BlockSpec((TM, 2 * HD), lambda i: (i, 0)),
        out_shape=jax.ShapeDtypeStruct((n, 2 * HD), BF16),
        compiler_params=_cparams(("parallel",)),
        name="ctx_expand",
    )(ckv_ctx, wkv_ext)


ATTN_HEADS_PER_STEP = 2
ATTN_CHUNK_UNROLL = 8


def _attn_kernel(q_ref, kn_ref, krp_ref, v_ref, *rest, tk, n_chunks, has_ctx):
    if has_ctx:
        knc_ref, krpc_ref, vc_ref, o_ref = rest
    else:
        o_ref, = rest
    tq = q_ref.shape[0]
    contract1 = (((1,), (1,)), ((), ()))
    heads = range(ATTN_HEADS_PER_STEP)
    head_cols = lambda hd: slice(hd * LANES, (hd + 1) * LANES)
    qs = [q_ref[:, 2 * hd * LANES:2 * (hd + 1) * LANES] for hd in heads]

    def ones_column(rows):
        return (lax.broadcasted_iota(jnp.int32, (rows, LANES), 1) == 0).astype(BF16)

    def step(q, kn, krp, v, ones, carry):
        m, acc = carry
        kc = jnp.concatenate([kn, krp], axis=1)
        s = lax.dot_general(q, kc, contract1, preferred_element_type=F32)
        m_new = jnp.maximum(m, jnp.max(s, axis=-1, keepdims=True))
        p = jnp.exp2(s - m_new).astype(BF16)
        pv = jnp.dot(p, jnp.concatenate([v, ones], axis=1), preferred_element_type=F32)
        return m_new, jnp.exp2(m - m_new) * acc + pv

    carry = tuple((jnp.full((tq, 1), -jnp.inf, F32), jnp.zeros((tq, 2 * LANES), F32)) for _ in heads)
    if has_ctx:
        ones = ones_column(PAST_LEN)
        carry = tuple(step(qs[hd], knc_ref[0, :, head_cols(hd)], krpc_ref[0], vc_ref[0, :, head_cols(hd)],
                           ones, carry[hd]) for hd in heads)
    ones = ones_column(tk)

    def chunk(c, carry):
        rows = pl.ds(pl.multiple_of(c * tk, tk), tk)
        krp = krp_ref[rows, :]
        return tuple(step(qs[hd], kn_ref[rows, head_cols(hd)], krp, v_ref[rows, head_cols(hd)],
                          ones, carry[hd]) for hd in heads)

    unroll = math.gcd(n_chunks, ATTN_CHUNK_UNROLL)

    def body(c, carry):
        for u in range(unroll):
            carry = chunk(c * unroll + u, carry)
        return carry

    carry = lax.fori_loop(0, n_chunks // unroll, body, carry)
    for hd in heads:
        acc = carry[hd][1]
        o_ref[:, head_cols(hd)] = (acc[:, :MLA_V] / acc[:, MLA_V:MLA_V + 1]).astype(o_ref.dtype)


def _attention(q, kn, krp, v, ctx, *, n_req, seq_len, row0, tq, tk):
    rb0 = row0 // seq_len
    qb0 = row0 // tq
    nq = seq_len // tq
    hw = ATTN_HEADS_PER_STEP * LANES
    o_spec = pl.BlockSpec((tq, hw), lambda b, h, i: (b * nq + i, h))
    k_spec = pl.BlockSpec((seq_len, hw), lambda b, h, i: (rb0 + b, h))
    in_specs = [pl.BlockSpec((tq, 2 * hw), lambda b, h, i: (qb0 + b * nq + i, h)), k_spec,
                pl.BlockSpec((seq_len, LANES), lambda b, h, i: (rb0 + b, 0)), k_spec]
    args = [q, kn, krp, v]
    if ctx is not None:
        c_spec = pl.BlockSpec((1, PAST_LEN, hw), lambda b, h, i: (b, 0, h))
        in_specs += [c_spec, pl.BlockSpec((1, PAST_LEN, LANES), lambda b, h, i: (b, 0, 0)), c_spec]
        args += list(ctx)
    return pl.pallas_call(
        functools.partial(_attn_kernel, tk=tk, n_chunks=seq_len // tk, has_ctx=ctx is not None),
        grid=(n_req, MLA_HEADS // ATTN_HEADS_PER_STEP, nq),
        in_specs=in_specs,
        out_specs=o_spec,
        out_shape=jax.ShapeDtypeStruct((n_req * seq_len, HD), BF16),
        compiler_params=_cparams(("parallel", "parallel", "arbitrary")),
        name=f"attention_{seq_len}",
    )(*args)


def _rope_rot_cols(w):
    w1, w2, w3, w4 = jnp.split(w, 4, axis=-1)
    return jnp.concatenate([-w2, w1, -w4, w3], axis=-1)


def _pad_cols(w, width):
    return jnp.pad(w, ((0, 0), (0, width - w.shape[1])))


def _mla_weights(w_in, w_q_b, w_kv_b):
    w_rope = w_in[:, MLA_Q_LORA + MLA_KV_LORA:]
    win_ext = jnp.concatenate([w_in[:, :MLA_Q_LORA + MLA_KV_LORA], _pad_cols(w_rope, LANES),
                               _pad_cols(_rope_rot_cols(w_rope), LANES)], axis=1).astype(BF16)
    wq = w_q_b.reshape(MLA_Q_LORA, MLA_HEADS, MLA_NOPE + MLA_ROPE)
    wq_nope = wq[:, :, :MLA_NOPE].reshape(MLA_Q_LORA, HD)
    wq_rope = wq[:, :, MLA_NOPE:]
    pad = ((0, 0), (0, 0), (0, LANES - MLA_ROPE))
    wq_rope_p = jnp.pad(wq_rope, pad).reshape(MLA_Q_LORA, HD)
    wq_rot_p = jnp.pad(_rope_rot_cols(wq_rope), pad).reshape(MLA_Q_LORA, HD)
    wq_ext = jnp.concatenate([wq_nope, wq_rope_p, wq_rot_p], axis=1).astype(BF16)
    wkv = w_kv_b.reshape(MLA_KV_LORA, MLA_HEADS, MLA_NOPE + MLA_V)
    wkv_ext = jnp.concatenate([wkv[:, :, :MLA_NOPE].reshape(MLA_KV_LORA, HD),
                               wkv[:, :, MLA_NOPE:].reshape(MLA_KV_LORA, HD)], axis=1).astype(BF16)
    return win_ext, wq_ext, wkv_ext


def _rope_tables():
    rows = DEC_SEQ // GRID_W
    row = jnp.repeat(jnp.arange(rows, dtype=F32), GRID_W)
    col = jnp.tile(jnp.arange(GRID_W, dtype=F32), rows)
    half = MLA_ROPE // 2
    inv = 1.0 / (ROPE_BASE ** (jnp.arange(0, half, 2, dtype=F32) / half))
    ar = row[:, None] * inv
    ac = col[:, None] * inv
    ang = jnp.concatenate([ar, ar, ac, ac] * 2, axis=-1)
    return jnp.cos(ang), jnp.sin(ang)


def kernel(x_prompt, x_sample, state_ret_fwd, state_ret_bwd, cache_mla_ckv, cache_mla_krope, c, c_ctx,
           ada_w, ada_b, norm_pre, norm_post, ret_w_in, ret_decay_fwd, ret_decay_bwd, ret_w_out,
           mla_w_in, mla_q_norm, mla_kv_norm, mla_w_q_b, mla_w_kv_b, mla_w_out,
           moe_w_router, moe_w_gate, moe_w_up, moe_w_down):
    x_parts = (x_sample.reshape(N_SAMPLE, D_MODEL), x_prompt.reshape(N_PROMPT, D_MODEL))
    joint = False
    cond8 = jnp.concatenate([c, c_ctx[None, :], jnp.zeros((8 - NG, D_MODEL), F32)], axis=0)
    cos128, sin128 = _rope_tables()
    outs = {}

    for i in range(DEPTH):
        mods = _adaln(cond8, ada_w, ada_b.reshape(DEPTH, 1, 6 * D_MODEL), i)[:NG].reshape(NG, 6, D_MODEL)
        gpre1 = norm_pre[i, 0][None, :]
        gpre2 = norm_pre[i, 1][None, :]
        gpost1 = norm_post[i, 0][None, :]
        gpost2 = norm_post[i, 1][None, :]
        if i % 2 == 0:
            r = i // 2
            qkvg = _norm_mod_matmul(x_parts, joint, mods, gpre1, ret_w_in[r], 1536)
            lg = jnp.stack([jax.nn.log_sigmoid(ret_decay_fwd[r].astype(F32)),
                            jax.nn.log_sigmoid(ret_decay_bwd[r].astype(F32))])
            yg_s, _, _ = _retention(qkvg, lg, state_ret_fwd[:, r], state_ret_bwd[:, r],
                                    n_req=DEC_BATCH, seq_len=DEC_SEQ, row0=0)
            yg_p, s_f, s_b = _retention(qkvg, lg, None, None, n_req=BATCH, seq_len=SEQ, row0=N_SAMPLE)
            outs["fwd"], outs["bwd"] = s_f[:, None], s_b[:, None]
            mix, w_out = (yg_s, yg_p), ret_w_out[r]
        else:
            mi = i // 2
            win_ext, wq_ext, wkv_ext = _mla_weights(mla_w_in[mi], mla_w_q_b[mi], mla_w_kv_b[mi])
            q, kn, v, krp, ckv, kraw = _mla_proj(
                x_parts, joint, mods, gpre1, win_ext, mla_q_norm[mi][None, :], mla_kv_norm[mi][None, :],
                wq_ext, wkv_ext, cos128, sin128)
            outs["ckv"] = ckv.reshape(BATCH, 1, SEQ, MLA_KV_LORA)
            outs["krope"] = kraw[:, :MLA_ROPE].reshape(BATCH, 1, SEQ, MLA_ROPE)
            kvc = _ctx_expand(cache_mla_ckv[:, mi].reshape(DEC_BATCH * PAST_LEN, MLA_KV_LORA), wkv_ext)
            kvc = kvc.reshape(DEC_BATCH, PAST_LEN, 2 * HD)
            krpc = jnp.pad(cache_mla_krope[:, mi], ((0, 0), (0, 0), (0, LANES - MLA_ROPE))).astype(BF16)
            o_s = _attention(q, kn, krp, v, (kvc[:, :, :HD], krpc, kvc[:, :, HD:]),
                             n_req=DEC_BATCH, seq_len=DEC_SEQ, row0=0, tq=512, tk=512)
            o_p = _attention(q, kn, krp, v, None, n_req=BATCH, seq_len=SEQ, row0=N_SAMPLE, tq=SEQ, tk=SEQ)
            mix, w_out = (o_s, o_p), mla_w_out[mi]
        x1, h3, aff_t = _out_post(mix, w_out, x_parts, joint, mods, gpost1, gpre2, moe_w_router[i].T)
        idx, gate = _route(aff_t)
        f3_lo, wg16, wu16, wd16 = _moe(idx, gate, h3, moe_w_gate, moe_w_up, moe_w_down, i,
                                       group0=0, n_groups=MOE_FIRST_GROUPS, emit_bf16=True,
                                       ff_split=MOE_FIRST_FF_SPLIT)
        f3_hi, = _moe(idx, gate, h3, wg16, wu16, wd16, 0,
                      group0=MOE_FIRST_GROUPS, n_groups=NG - MOE_FIRST_GROUPS, emit_bf16=False,
                      ff_split=MOE_REST_FF_SPLIT)
        last = i == DEPTH - 1
        x_new = _moe_post((f3_lo, f3_hi), x1, mods, gpost2, split_out=last)
        if not last:
            x_parts, joint = (x_new, x_new), True

    y_sample, y_prompt = x_new
    return (y_prompt.reshape(BATCH, SEQ, D_MODEL), y_sample.reshape(DEC_BATCH, DEC_SEQ, D_MODEL),
            outs["fwd"], outs["bwd"], outs["ckv"], outs["krope"])
```

```python
import functools
import math

import jax
import jax.numpy as jnp
from jax import lax
from jax.experimental import pallas as pl
from jax.experimental.pallas import tpu as pltpu

F32 = jnp.float32
BF16 = jnp.bfloat16

D_MODEL = 1024
BATCH = 16
SEQ = 256
DEPTH = 2
DEC_BATCH = 4
DEC_SEQ = 4096
PAST_LEN = 256
GRID_W = 64
RET_HEADS = 4
RET_DK = 256
RET_DV = 512
RET_QK_WIDTH = RET_HEADS * RET_DK
RET_V_WIDTH = RET_HEADS * RET_DV
RET_CHUNK = 128
MLA_HEADS = 8
MLA_NOPE = 128
MLA_ROPE = 64
MLA_V = 128
MLA_Q_LORA = 384
MLA_KV_LORA = 256
ROPE_BASE = 10000.0
N_EXPERTS = 16
EXPERT_FF = 1024
NORM_EPS = 1e-6

GT = DEC_SEQ
NG = DEC_BATCH + 1
NT = NG * GT
N_SAMPLE = DEC_BATCH * DEC_SEQ
N_PROMPT = BATCH * SEQ
CAP = 2 * GT // N_EXPERTS

LANES = 128
SUBLANES = 8
ROW_TILES = D_MODEL // LANES
TM = 512
TILES_PER_GROUP = GT // TM
SAMPLE_TILES = N_SAMPLE // TM
VMEM_LIMIT = 60 * 1024 * 1024


def _cparams(sem):
    return pltpu.CompilerParams(dimension_semantics=sem, vmem_limit_bytes=VMEM_LIMIT)


def _rms(x):
    return x * lax.rsqrt(jnp.mean(x * x, axis=-1, keepdims=True) + NORM_EPS)


def _sigmoid(x):
    return 1.0 / (1.0 + jnp.exp(-x))


def _part_specs(width, joint, split=SAMPLE_TILES, rows=TM):
    lo = pl.BlockSpec((rows, width), lambda *g: (jnp.minimum(g[-1], split - 1), 0))
    if joint:
        hi = pl.BlockSpec((rows, width), lambda *g: (jnp.maximum(g[-1], split), 0))
    else:
        hi = pl.BlockSpec((rows, width), lambda *g: (jnp.maximum(g[-1] - split, 0), 0))
    return [lo, hi]


def _load_rows(lo_ref, hi_ref, i, split=SAMPLE_TILES):
    return jnp.where(i < split, lo_ref[...], hi_ref[...])


def _mod_spec():
    return pl.BlockSpec((1, 6, D_MODEL), lambda *g: (g[-1] // TILES_PER_GROUP, 0, 0))


def _const_spec(shape):
    return pl.BlockSpec(shape, lambda *g: (0,) * len(shape))


def _adaln_kernel(c_ref, w_ref, b_ref, o_ref):
    c = c_ref[...]
    s = (c * _sigmoid(c)).astype(BF16)
    o_ref[...] = jnp.dot(s, w_ref[0].astype(BF16), preferred_element_type=F32) + b_ref[0]


def _adaln(cond8, w, b, layer):
    tn = 1536
    return pl.pallas_call(
        _adaln_kernel,
        grid=(6 * D_MODEL // tn,),
        in_specs=[pl.BlockSpec((8, D_MODEL), lambda j: (0, 0)),
                  pl.BlockSpec((1, D_MODEL, tn), lambda j: (layer, 0, j)),
                  pl.BlockSpec((1, 1, tn), lambda j: (layer, 0, j))],
        out_specs=pl.BlockSpec((8, tn), lambda j: (0, j)),
        out_shape=jax.ShapeDtypeStruct((8, 6 * D_MODEL), F32),
        compiler_params=_cparams(("arbitrary",)),
        name="adaln",
    )(cond8, w, b)


NMM_TM = 1024


def _nmm_kernel(xlo_ref, xhi_ref, mod_ref, gain_ref, w_ref, o_ref, w_scr):
    i = pl.program_id(1)

    @pl.when(i == 0)
    def _():
        w_scr[...] = w_ref[...].astype(BF16)

    m = mod_ref[0]
    x = _load_rows(xlo_ref, xhi_ref, i, split=N_SAMPLE // NMM_TM)
    h = _rms(x) * gain_ref[...] * (1.0 + m[1:2]) + m[0:1]
    o_ref[...] = jnp.dot(h.astype(BF16), w_scr[...], preferred_element_type=F32).astype(o_ref.dtype)


def _norm_mod_matmul(x_parts, joint, mods, gain, w, tn):
    n_out = w.shape[1]
    tm = NMM_TM
    return pl.pallas_call(
        _nmm_kernel,
        grid=(n_out // tn, NT // tm),
        in_specs=[*_part_specs(D_MODEL, joint, split=N_SAMPLE // tm, rows=tm),
                  pl.BlockSpec((1, 6, D_MODEL), lambda j, i: (i // (GT // tm), 0, 0)),
                  _const_spec((1, D_MODEL)),
                  pl.BlockSpec((D_MODEL, tn), lambda j, i: (0, j))],
        out_specs=pl.BlockSpec((tm, tn), lambda j, i: (i, j)),
        out_shape=jax.ShapeDtypeStruct((NT, n_out), BF16),
        scratch_shapes=[pltpu.VMEM((D_MODEL, tn), BF16)],
        compiler_params=_cparams(("arbitrary", "arbitrary")),
        name="norm_mod_matmul",
    )(*x_parts, mods, gain, w)


RET_SCAN_UNROLL = 2


def _ret_kernel(lg_ref, q_ref, k_ref, v_ref, g_ref, *rest, seq_len, has_init):
    if has_init:
        s0f_ref, s0b_ref, *rest = rest
    yg_ref, sf_ref, sb_ref, ypart, ycross, sf_scr, sb_scr = rest
    C = RET_CHUNK
    nc = seq_len // C
    head = pl.program_id(1)
    lgf = lg_ref[0, head]
    lgb = lg_ref[1, head]
    k_scale = RET_DK ** -0.5

    ii = lax.broadcasted_iota(jnp.int32, (C, C), 0).astype(F32)
    jj = lax.broadcasted_iota(jnp.int32, (C, C), 1).astype(F32)
    diff = ii - jj
    inner = jnp.where(diff >= 0, jnp.exp(diff * lgf), jnp.exp(-diff * lgb)) * k_scale
    pos = lax.broadcasted_iota(jnp.int32, (C, 1), 0).astype(F32)
    cross_f = jnp.exp((pos + 1.0) * lgf)
    cross_b = jnp.exp((C - pos) * lgb)
    sdec_f = jnp.exp((C - 1.0 - pos) * lgf) * k_scale
    sdec_b = jnp.exp(pos * lgb) * k_scale
    one = jnp.ones((1, 1), F32)
    cdec_f = jnp.exp(one * (C * lgf))
    cdec_b = jnp.exp(one * (C * lgb))

    if has_init:
        sf_scr[...] = s0f_ref[0, 0]
        sb_scr[...] = s0b_ref[0, 0]
    else:
        sf_scr[...] = jnp.zeros_like(sf_scr)
        sb_scr[...] = jnp.zeros_like(sb_scr)

    contract0 = (((0,), (0,)), ((), ()))
    contract1 = (((1,), (1,)), ((), ()))

    def chunk_rows(c):
        return pl.ds(pl.multiple_of(c * C, C), C)

    def bwd_chunk(c):
        rows = chunk_rows(c)
        q = q_ref[rows, :]
        s = sb_scr[...]
        ycross[rows, :] = jnp.dot(q, s.astype(BF16), preferred_element_type=F32) * cross_b
        kd = (k_ref[rows, :].astype(F32) * sdec_b).astype(BF16)
        sb_scr[...] = s * cdec_b + lax.dot_general(kd, v_ref[rows, :], contract0,
                                                   preferred_element_type=F32)

    def fwd_chunk(c):
        rows = chunk_rows(c)
        q = q_ref[rows, :]
        k = k_ref[rows, :]
        v = v_ref[rows, :]
        s = sf_scr[...]
        scores = lax.dot_general(q, k, contract1, preferred_element_type=F32) * inner
        ypart[rows, :] = (jnp.dot(scores.astype(BF16), v, preferred_element_type=F32)
                          + jnp.dot(q, s.astype(BF16), preferred_element_type=F32) * cross_f)
        kd = (k.astype(F32) * sdec_f).astype(BF16)
        sf_scr[...] = s * cdec_f + lax.dot_general(kd, v, contract0, preferred_element_type=F32)

    def scan(t, carry):
        for u in range(RET_SCAN_UNROLL):
            step = t * RET_SCAN_UNROLL + u
            fwd_chunk(step)
            bwd_chunk(nc - 1 - step)
        return carry

    lax.fori_loop(0, nc // RET_SCAN_UNROLL, scan, 0)
    sf_ref[0, 0] = sf_scr[...]
    sb_ref[0, 0] = sb_scr[...]

    def finish(t, carry):
        for u in range(RET_SCAN_UNROLL):
            rows = chunk_rows(t * RET_SCAN_UNROLL + u)
            y = ypart[rows, :] + ycross[rows, :]
            mu = jnp.mean(y, axis=-1, keepdims=True)
            yc = y - mu
            yn = yc * lax.rsqrt(jnp.mean(yc * yc, axis=-1, keepdims=True) + NORM_EPS)
            g = g_ref[rows, :].astype(F32)
            yg_ref[rows, :] = (g * _sigmoid(g) * yn).astype(yg_ref.dtype)
        return carry

    lax.fori_loop(0, nc // RET_SCAN_UNROLL, finish, 0)


def _retention(qkvg, lg, s0f, s0b, *, n_req, seq_len, row0):
    rb0 = row0 // seq_len
    has_init = s0f is not None
    qk_spec = lambda off: pl.BlockSpec((seq_len, RET_DK), lambda b, h, lg_: (rb0 + b, off + h))
    vg_spec = lambda off: pl.BlockSpec((seq_len, RET_DV), lambda b, h, lg_: (rb0 + b, off + h))
    st_spec = pl.BlockSpec((1, 1, RET_DK, RET_DV), lambda b, h, lg_: (b, h, 0, 0))
    in_specs = [qk_spec(0), qk_spec(RET_HEADS), vg_spec(RET_HEADS), vg_spec(2 * RET_HEADS)]
    args = [qkvg, qkvg, qkvg, qkvg]
    if has_init:
        in_specs += [st_spec, st_spec]
        args += [s0f, s0b]
    st_shape = jax.ShapeDtypeStruct((n_req, RET_HEADS, RET_DK, RET_DV), F32)
    return pl.pallas_call(
        functools.partial(_ret_kernel, seq_len=seq_len, has_init=has_init),
        grid_spec=pltpu.PrefetchScalarGridSpec(
            num_scalar_prefetch=1,
            grid=(n_req, RET_HEADS),
            in_specs=in_specs,
            out_specs=[pl.BlockSpec((seq_len, RET_DV), lambda b, h, lg_: (b, h)), st_spec, st_spec],
            scratch_shapes=[pltpu.VMEM((seq_len, RET_DV), F32),
                            pltpu.VMEM((seq_len, RET_DV), F32),
                            pltpu.VMEM((RET_DK, RET_DV), F32),
                            pltpu.VMEM((RET_DK, RET_DV), F32)]),
        out_shape=[jax.ShapeDtypeStruct((n_req * seq_len, RET_V_WIDTH), BF16), st_shape, st_shape],
        compiler_params=_cparams(("parallel", "arbitrary")),
        name=f"retention_{seq_len}",
    )(lg, *args)


def _split_bf16(x):
    hi = x.astype(BF16)
    return hi, (x - hi.astype(F32)).astype(BF16)


def _out_post_kernel(ylo_ref, yhi_ref, w_ref, xlo_ref, xhi_ref, mod_ref, gpost_ref, gpre_ref, wr_ref,
                     x1_ref, h3_ref, aff_ref, w_scr):
    i = pl.program_id(0)

    @pl.when(i == 0)
    def _():
        w_scr[...] = w_ref[...].astype(BF16)

    m = mod_ref[0]
    y = jnp.dot(_load_rows(ylo_ref, yhi_ref, i), w_scr[...], preferred_element_type=F32)
    x1 = _load_rows(xlo_ref, xhi_ref, i) + _rms(y) * gpost_ref[...] * m[2:3]
    x1_ref[...] = x1
    h = _rms(x1) * gpre_ref[...] * (1.0 + m[4:5]) + m[3:4]
    for c in range(ROW_TILES):
        h3_ref[pl.ds(c, TM, stride=ROW_TILES), :] = h[:, c * LANES:(c + 1) * LANES]
    contract1 = (((1,), (1,)), ((), ()))
    w_hi, w_lo = _split_bf16(wr_ref[...])
    h_hi, h_lo = _split_bf16(h)
    part = lax.dot_general(jnp.concatenate([w_hi, w_lo], axis=0), h_hi, contract1,
                           preferred_element_type=F32)
    logits = (part[:N_EXPERTS] + part[N_EXPERTS:]
              + lax.dot_general(w_hi, h_lo, contract1, preferred_element_type=F32))
    e = jnp.exp(logits - jnp.max(logits, axis=0, keepdims=True))
    aff_ref[...] = e / jnp.sum(e, axis=0, keepdims=True)


def _out_post(y_parts, w, x_parts, joint, mods, gpost, gpre, w_router_t):
    kdim = w.shape[0]
    return pl.pallas_call(
        _out_post_kernel,
        grid=(NT // TM,),
        in_specs=[*_part_specs(kdim, joint=False),
                  _const_spec((kdim, D_MODEL)),
                  *_part_specs(D_MODEL, joint), _mod_spec(),
                  _const_spec((1, D_MODEL)), _const_spec((1, D_MODEL)),
                  _const_spec((N_EXPERTS, D_MODEL))],
        out_specs=[pl.BlockSpec((TM, D_MODEL), lambda i: (i, 0)),
                   pl.BlockSpec((TM * ROW_TILES, LANES), lambda i: (i, 0)),
                   pl.BlockSpec((N_EXPERTS, TM), lambda i: (0, i))],
        out_shape=[jax.ShapeDtypeStruct((NT, D_MODEL), F32),
                   jax.ShapeDtypeStruct((NT * ROW_TILES, LANES), F32),
                   jax.ShapeDtypeStruct((N_EXPERTS, NT), F32)],
        scratch_shapes=[pltpu.VMEM((kdim, D_MODEL), BF16)],
        compiler_params=_cparams(("arbitrary",)),
        name="out_post",
    )(*y_parts, w, *x_parts, mods, gpost, gpre, w_router_t)


SCATTER_BATCH = 8
MOE_FIRST_GROUPS = 1
MOE_FIRST_FF_SPLIT = 4
MOE_REST_FF_SPLIT = 1


def _moe_kernel(idx_ref, gate_ref, h3_ref, wg_ref, wu_ref, wd_ref, f3_ref, *rest, emit_bf16, ff_split):
    if emit_bf16:
        wg16_ref, wu16_ref, wd16_ref, *rest = rest
    xg, x_a, x_b, yacc, y_a, y_b = rest
    e = pl.program_id(1)
    f = pl.program_id(2)

    def tile(r):
        return pl.ds(pl.multiple_of(r * SUBLANES, SUBLANES), SUBLANES)

    def gather(ex, x_dst):
        base = ex * CAP
        for r in range(CAP):
            xg[r * SUBLANES:(r + 1) * SUBLANES, :] = h3_ref[tile(idx_ref[0, 0, base + r]), :]
        x_dst[...] = jnp.concatenate(
            [xg[pl.ds(c, CAP, stride=ROW_TILES), :] for c in range(ROW_TILES)], axis=1).astype(BF16)

    def scatter(ex, y_src):
        base = ex * CAP
        for r0 in range(0, CAP, SCATTER_BATCH):
            dst, vals = [], []
            for r in range(r0, r0 + SCATTER_BATCH):
                t = idx_ref[0, 0, base + r]
                dst.append(tile(t))
                vals.append(f3_ref[tile(t), :]
                            + gate_ref[0, 0, base + r] * y_src[r * SUBLANES:(r + 1) * SUBLANES, :])
            for d, v in zip(dst, vals):
                f3_ref[d, :] = v

    def half_ffn(x_src):
        x = x_src[...]
        wg = wg_ref[0, 0].astype(BF16)
        wu = wu_ref[0, 0].astype(BF16)
        wd = wd_ref[0, 0].astype(BF16)
        if emit_bf16:
            wg16_ref[0, 0] = wg
            wu16_ref[0, 0] = wu
            wd16_ref[0, 0] = wd
        a = jnp.dot(x, wg, preferred_element_type=F32)
        u = jnp.dot(x, wu, preferred_element_type=F32)
        act = (a * _sigmoid(a) * u).astype(BF16)
        return jnp.dot(act, wd, preferred_element_type=F32)

    first = f == 0
    last = f == ff_split - 1

    @pl.when(jnp.logical_and(e == 0, first))
    def _():
        f3_ref[...] = jnp.zeros_like(f3_ref)
        y_b[...] = jnp.zeros_like(y_b)
        gather(0, x_a)

    if ff_split == 1:
        phases = (("only", first),)
    elif ff_split == 2:
        phases = (("first", first), ("last", last))
    else:
        phases = (("first", first), ("middle", jnp.logical_not(jnp.logical_or(first, last))), ("last", last))
    for parity, (x_cur, x_nxt, y_cur, y_prv) in enumerate(((x_a, x_b, y_a, y_b), (x_b, x_a, y_b, y_a))):
        for phase, cond in phases:
            @pl.when(jnp.logical_and(e % 2 == parity, cond))
            def _():
                if phase in ("only", "first"):
                    gather(jnp.minimum(e + 1, N_EXPERTS - 1), x_nxt)
                part = half_ffn(x_cur)
                if phase == "first":
                    yacc[...] = part
                elif phase == "middle":
                    yacc[...] += part
                else:
                    y = part if phase == "only" else yacc[...] + part
                    for c in range(ROW_TILES):
                        y_cur[pl.ds(c, CAP, stride=ROW_TILES), :] = y[:, c * LANES:(c + 1) * LANES]
                    scatter(jnp.maximum(e - 1, 0), y_prv)

    @pl.when(jnp.logical_and(e == N_EXPERTS - 1, last))
    def _():
        scatter(N_EXPERTS - 1, y_b)


def _moe(idx, gate, h3, wg, wu, wd, layer, *, group0, n_groups, emit_bf16, ff_split):
    grp_rows = GT * ROW_TILES
    ff_tile = EXPERT_FF // ff_split
    route_spec = pl.BlockSpec((1, 1, N_EXPERTS * CAP), lambda b, e, f: (group0 + b, 0, 0),
                              memory_space=pltpu.SMEM)
    w_in_spec = lambda l: pl.BlockSpec((1, 1, D_MODEL, ff_tile), lambda b, e, f: (l, e, 0, f))
    w_out_spec = lambda l: pl.BlockSpec((1, 1, ff_tile, D_MODEL), lambda b, e, f: (l, e, f, 0))
    out_specs = [pl.BlockSpec((grp_rows, LANES), lambda b, e, f: (b, 0), pipeline_mode=pl.Buffered(1))]
    out_shape = [jax.ShapeDtypeStruct((n_groups * grp_rows, LANES), F32)]
    if emit_bf16:
        out_specs += [w_in_spec(0), w_in_spec(0), w_out_spec(0)]
        out_shape += [jax.ShapeDtypeStruct((1, N_EXPERTS, D_MODEL, EXPERT_FF), BF16),
                      jax.ShapeDtypeStruct((1, N_EXPERTS, D_MODEL, EXPERT_FF), BF16),
                      jax.ShapeDtypeStruct((1, N_EXPERTS, EXPERT_FF, D_MODEL), BF16)]
    return pl.pallas_call(
        functools.partial(_moe_kernel, emit_bf16=emit_bf16, ff_split=ff_split),
        grid=(n_groups, N_EXPERTS, ff_split),
        in_specs=[route_spec, route_spec,
                  pl.BlockSpec((grp_rows, LANES), lambda b, e, f: (group0 + b, 0),
                               pipeline_mode=pl.Buffered(1)),
                  w_in_spec(layer), w_in_spec(layer), w_out_spec(layer)],
        out_specs=out_specs,
        out_shape=out_shape,
        scratch_shapes=[pltpu.VMEM((CAP * ROW_TILES, LANES), F32),
                        pltpu.VMEM((CAP, D_MODEL), BF16),
                        pltpu.VMEM((CAP, D_MODEL), BF16),
                        pltpu.VMEM((CAP, D_MODEL), F32),
                        pltpu.VMEM((CAP * ROW_TILES, LANES), F32),
                        pltpu.VMEM((CAP * ROW_TILES, LANES), F32)],
        compiler_params=_cparams(("arbitrary", "arbitrary", "arbitrary")),
        name="moe_experts",
    )(idx, gate, h3, wg, wu, wd)


def _moe_post_kernel(f3lo_ref, f3hi_ref, x_ref, mod_ref, gpost_ref, *o_refs):
    i = pl.program_id(0)
    m = mod_ref[0]
    rows = lambda ref: jnp.concatenate(
        [ref[pl.ds(c, TM, stride=ROW_TILES), :] for c in range(ROW_TILES)], axis=1)
    fx = jnp.where(i < MOE_FIRST_GROUPS * TILES_PER_GROUP, rows(f3lo_ref), rows(f3hi_ref))
    out = x_ref[...] + _rms(fx) * gpost_ref[...] * m[5:6]
    if len(o_refs) == 1:
        o_refs[0][...] = out
    else:
        @pl.when(i < SAMPLE_TILES)
        def _():
            o_refs[0][...] = out

        @pl.when(i >= SAMPLE_TILES)
        def _():
            o_refs[1][...] = out


def _moe_post(f3_parts, x, mods, gpost, split_out):
    if split_out:
        out_specs = _part_specs(D_MODEL, joint=False)
        out_shape = [jax.ShapeDtypeStruct((N_SAMPLE, D_MODEL), F32),
                     jax.ShapeDtypeStruct((N_PROMPT, D_MODEL), F32)]
    else:
        out_specs = pl.BlockSpec((TM, D_MODEL), lambda i: (i, 0))
        out_shape = jax.ShapeDtypeStruct((NT, D_MODEL), F32)
    return pl.pallas_call(
        _moe_post_kernel,
        grid=(NT // TM,),
        in_specs=[*_part_specs(LANES, joint=False, split=MOE_FIRST_GROUPS * TILES_PER_GROUP,
                               rows=TM * ROW_TILES),
                  pl.BlockSpec((TM, D_MODEL), lambda i: (i, 0)),
                  _mod_spec(), _const_spec((1, D_MODEL))],
        out_specs=out_specs,
        out_shape=out_shape,
        compiler_params=_cparams(("arbitrary",)),
        name="moe_post",
    )(*f3_parts, x, mods, gpost)


ROUTE_SEG = SEQ
RANK_LO = 16
RANK_HI = CAP // RANK_LO
TOK_LO = 64
ROUTE_VALS = 5


def _route_kernel(aff_ref, idx_ref, gate_ref, sel_scr, a_scr, code_scr, g_scr, acc_scr):
    g = pl.program_id(0)
    n_seg = GT // ROUTE_SEG
    seg = lambda s: slice(s * ROUTE_SEG, (s + 1) * ROUTE_SEG)
    aff = aff_ref[...]
    tok = lax.broadcasted_iota(jnp.int32, (N_EXPERTS, GT), 1)
    ones_seg = jnp.ones((ROUTE_SEG, ROUTE_SEG), BF16)

    def row_count(mask):
        return jnp.sum(jnp.where(mask, 1.0, 0.0), axis=1, keepdims=True)

    def seg_count(mask):
        m = jnp.where(mask, 1.0, 0.0).astype(BF16)
        return jnp.concatenate(
            [jnp.dot(m[:, seg(s)], ones_seg, preferred_element_type=F32) for s in range(n_seg)], axis=1)

    def select(count, k, shape):
        as_float = lambda b: pltpu.bitcast(b, F32)

        def value_bit(i, t):
            cand = t | jnp.left_shift(jnp.int32(1), 30 - i)
            return jnp.where(count(aff >= as_float(cand)) >= k, cand, t)

        thr = lax.fori_loop(0, 31, value_bit, jnp.zeros(shape, jnp.int32))
        above = aff >= as_float(thr + 1)
        tied = jnp.logical_and(aff >= as_float(thr), jnp.logical_not(above))
        need = k - count(above)

        def index_bit(i, v):
            cand = v | jnp.left_shift(jnp.int32(1), 11 - i)
            return jnp.where(count(jnp.logical_and(tied, tok < cand)) < need, cand, v)

        last = lax.fori_loop(0, 12, index_bit, jnp.zeros(shape, jnp.int32))
        chosen = jnp.logical_or(above, jnp.logical_and(tied, tok <= last))
        sel_scr[...] = jnp.where(chosen, 1.0, 0.0)

    @pl.when(g < DEC_BATCH)
    def _():
        select(row_count, float(2 * DEC_SEQ // N_EXPERTS), (N_EXPERTS, 1))

    @pl.when(g >= DEC_BATCH)
    def _():
        select(seg_count, float(2 * SEQ // N_EXPERTS), (N_EXPERTS, GT))

    sel = sel_scr[...]
    r_i = lax.broadcasted_iota(jnp.int32, (ROUTE_SEG, ROUTE_SEG), 0)
    c_i = lax.broadcasted_iota(jnp.int32, (ROUTE_SEG, ROUTE_SEG), 1)
    upper = jnp.where(r_i <= c_i, 1.0, 0.0).astype(BF16)
    sel_b = sel.astype(BF16)
    off = jnp.zeros((N_EXPERTS, 1), F32)
    for s in range(n_seg):
        inc = jnp.dot(sel_b[:, seg(s)], upper, preferred_element_type=F32)
        rank = inc + off - sel[:, seg(s)]
        a = jnp.floor(rank * (1.0 / RANK_LO))
        a_scr[:, seg(s)] = jnp.where(sel[:, seg(s)] > 0, a, -1.0)
        code_scr[:, seg(s)] = jnp.where(sel[:, seg(s)] > 0, rank - RANK_LO * a + 1.0, 0.0).astype(BF16)
        off = off + inc[:, ROUTE_SEG - 1:ROUTE_SEG]
    g1 = aff.astype(BF16).astype(F32)
    g2 = (aff - g1).astype(BF16).astype(F32)
    g_scr[0] = g1
    g_scr[1] = g2
    g_scr[2] = (aff - g1 - g2).astype(BF16).astype(F32)

    lane = lax.broadcasted_iota(jnp.int32, (N_EXPERTS, N_EXPERTS * RANK_LO), 1)
    row = lax.broadcasted_iota(jnp.int32, (N_EXPERTS, N_EXPERTS * RANK_LO), 0)
    lo_bits = RANK_LO.bit_length() - 1
    expand = jnp.where(jnp.right_shift(lane, lo_bits) == row, 1.0, 0.0).astype(BF16)
    digit = (jnp.bitwise_and(lax.broadcasted_iota(jnp.int32, (1, N_EXPERTS * RANK_LO), 1), RANK_LO - 1)
             + 1).astype(F32)
    a_iota = lax.broadcasted_iota(jnp.int32, (RANK_HI, 1), 0).astype(F32)
    acc_scr[...] = jnp.zeros_like(acc_scr)
    contract0 = (((0,), (0,)), ((), ()))

    def chunk(c, carry):
        lanes = pl.ds(pl.multiple_of(c * ROUTE_SEG, ROUTE_SEG), ROUTE_SEG)
        a_c = a_scr[:, lanes]
        g_c = [g_scr[j, :, lanes] for j in range(3)]
        tok_c = lax.broadcasted_iota(jnp.int32, (1, ROUTE_SEG), 1) + c * ROUTE_SEG
        t_hi = jnp.right_shift(tok_c, TOK_LO.bit_length() - 1).astype(F32)
        t_lo = jnp.bitwise_and(tok_c, TOK_LO - 1).astype(F32)
        rows = []
        for e in range(N_EXPERTS):
            hit = a_c[e:e + 1, :] == a_iota
            for val in (t_hi, t_lo, g_c[0][e:e + 1, :], g_c[1][e:e + 1, :], g_c[2][e:e + 1, :]):
                rows.append(jnp.where(hit, val, 0.0))
        lhs = jnp.concatenate(rows, axis=0).astype(BF16)
        spread = lax.dot_general(code_scr[:, lanes], expand, contract0, preferred_element_type=F32)
        low_hot = jnp.where(spread == digit, 1.0, 0.0).astype(BF16)
        acc_scr[...] += jnp.dot(lhs, low_hot, preferred_element_type=F32)
        return carry

    lax.fori_loop(0, n_seg, chunk, 0)
    per_e = ROUTE_VALS * RANK_HI
    for e in range(N_EXPERTS):
        blk = acc_scr[e * per_e:(e + 1) * per_e, :][:, e * RANK_LO:(e + 1) * RANK_LO]
        part = lambda j: blk[j * RANK_HI:(j + 1) * RANK_HI]
        idx_ref[0, e] = (part(0) * TOK_LO + part(1)).astype(jnp.int32)
        gate_ref[0, e] = part(2) + part(3) + part(4)


def _route(aff_t):
    out_spec = pl.BlockSpec((1, N_EXPERTS, RANK_HI, RANK_LO), lambda g: (g, 0, 0, 0))
    idx, gate = pl.pallas_call(
        _route_kernel,
        grid=(NG,),
        in_specs=[pl.BlockSpec((N_EXPERTS, GT), lambda g: (0, g))],
        out_specs=[out_spec, out_spec],
        out_shape=[jax.ShapeDtypeStruct((NG, N_EXPERTS, RANK_HI, RANK_LO), jnp.int32),
                   jax.ShapeDtypeStruct((NG, N_EXPERTS, RANK_HI, RANK_LO), F32)],
        scratch_shapes=[pltpu.VMEM((N_EXPERTS, GT), F32),
                        pltpu.VMEM((N_EXPERTS, GT), F32),
                        pltpu.VMEM((N_EXPERTS, GT), BF16),
                        pltpu.VMEM((3, N_EXPERTS, GT), F32),
                        pltpu.VMEM((N_EXPERTS * ROUTE_VALS * RANK_HI, N_EXPERTS * RANK_LO), F32)],
        compiler_params=_cparams(("arbitrary",)),
        name="route",
    )(aff_t)
    return idx.reshape(NG, 1, N_EXPERTS * CAP), gate.reshape(NG, 1, N_EXPERTS * CAP)


HD = MLA_HEADS * LANES
Q_SCALE = math.log2(math.e) / math.sqrt(MLA_NOPE + MLA_ROPE)


def _mla_proj_kernel(xlo_ref, xhi_ref, mod_ref, gain_ref, win_ref, qn_g_ref, kvn_g_ref, wq_ref, wkv_ref,
                     cos_ref, sin_ref, q_ref, kn_ref, v_ref, krp_ref, ckv_ref, kraw_ref):
    i = pl.program_id(0)
    m = mod_ref[0]
    h = (_rms(_load_rows(xlo_ref, xhi_ref, i)) * gain_ref[...] * (1.0 + m[1:2]) + m[0:1]).astype(BF16)
    lat = jnp.dot(h, win_ref[...], preferred_element_type=F32)
    q_lat = lat[:, :MLA_Q_LORA]
    kv_lat = lat[:, MLA_Q_LORA:MLA_Q_LORA + MLA_KV_LORA]
    kr = lat[:, MLA_Q_LORA + MLA_KV_LORA:MLA_Q_LORA + MLA_KV_LORA + LANES]
    kr_rot = lat[:, MLA_Q_LORA + MLA_KV_LORA + LANES:]
    is_latent = i < SAMPLE_TILES
    cos = jnp.where(is_latent, cos_ref[...], 1.0)
    sin = jnp.where(is_latent, sin_ref[...], 0.0)

    qln = (_rms(q_lat) * qn_g_ref[...]).astype(BF16)
    q = jnp.dot(qln, wq_ref[...], preferred_element_type=F32) * Q_SCALE
    for hd in range(MLA_HEADS):
        lo = HD + hd * LANES
        q_ref[:, 2 * hd * LANES:(2 * hd + 1) * LANES] = q[:, hd * LANES:(hd + 1) * LANES].astype(BF16)
        q_ref[:, (2 * hd + 1) * LANES:(2 * hd + 2) * LANES] = (
            q[:, lo:lo + LANES] * cos + q[:, HD + lo:HD + lo + LANES] * sin).astype(BF16)

    ckv = _rms(kv_lat) * kvn_g_ref[...]
    kv = jnp.dot(ckv.astype(BF16), wkv_ref[...], preferred_element_type=F32)
    kn_ref[...] = kv[:, :HD].astype(BF16)
    v_ref[...] = kv[:, HD:].astype(BF16)
    krp_ref[...] = (kr * cos + kr_rot * sin).astype(BF16)

    @pl.when(i >= SAMPLE_TILES)
    def _():
        ckv_ref[...] = ckv
        kraw_ref[...] = kr


def _mla_proj(x_parts, joint, mods, gain, win_ext, qn_g, kvn_g, wq_ext, wkv_ext, cos128, sin128):
    row = lambda w: pl.BlockSpec((TM, w), lambda i: (i, 0))
    ctx_row = lambda w: pl.BlockSpec((TM, w), lambda i: (jnp.maximum(i - SAMPLE_TILES, 0), 0))
    tab = pl.BlockSpec((TM, LANES), lambda i: (i % TILES_PER_GROUP, 0))
    return pl.pallas_call(
        _mla_proj_kernel,
        grid=(NT // TM,),
        in_specs=[*_part_specs(D_MODEL, joint), _mod_spec(),
                  _const_spec(gain.shape), _const_spec(win_ext.shape), _const_spec(qn_g.shape),
                  _const_spec(kvn_g.shape), _const_spec(wq_ext.shape), _const_spec(wkv_ext.shape),
                  tab, tab],
        out_specs=[row(2 * HD), row(HD), row(HD), row(LANES), ctx_row(MLA_KV_LORA), ctx_row(LANES)],
        out_shape=[jax.ShapeDtypeStruct((NT, 2 * HD), BF16),
                   jax.ShapeDtypeStruct((NT, HD), BF16),
                   jax.ShapeDtypeStruct((NT, HD), BF16),
                   jax.ShapeDtypeStruct((NT, LANES), BF16),
                   jax.ShapeDtypeStruct((N_PROMPT, MLA_KV_LORA), F32),
                   jax.ShapeDtypeStruct((N_PROMPT, LANES), F32)],
        compiler_params=_cparams(("arbitrary",)),
        name="mla_proj",
    )(*x_parts, mods, gain, win_ext, qn_g, kvn_g, wq_ext, wkv_ext, cos128, sin128)


def _matmul_kernel(a_ref, w_ref, o_ref):
    o_ref[...] = jnp.dot(a_ref[...].astype(BF16), w_ref[...],
                         preferred_element_type=F32).astype(o_ref.dtype)


def _ctx_expand(ckv_ctx, wkv_ext):
    n = ckv_ctx.shape[0]
    return pl.pallas_call(
        _matmul_kernel,
        grid=(n // TM,),
        in_specs=[pl.BlockSpec((TM, MLA_KV_LORA), lambda i: (i, 0)),
                  pl.BlockSpec(wkv_ext.shape, lambda i: (0, 0))],
        out_specs=pl.BlockSpec((TM, 2 * HD), lambda i: (i, 0)),
        out_shape=jax.ShapeDtypeStruct((n, 2 * HD), BF16),
        compiler_params=_cparams(("parallel",)),
        name="ctx_expand",
    )(ckv_ctx, wkv_ext)


ATTN_HEADS_PER_STEP = 2
ATTN_CHUNK_UNROLL = 8


def _attn_kernel(q_ref, kn_ref, krp_ref, v_ref, *rest, tk, n_chunks, has_ctx):
    if has_ctx:
        knc_ref, krpc_ref, vc_ref, o_ref = rest
    else:
        o_ref, = rest
    tq = q_ref.shape[0]
    contract1 = (((1,), (1,)), ((), ()))
    heads = range(ATTN_HEADS_PER_STEP)
    head_cols = lambda hd: slice(hd * LANES, (hd + 1) * LANES)
    qs = [q_ref[:, 2 * hd * LANES:2 * (hd + 1) * LANES] for hd in heads]

    def ones_column(rows):
        return (lax.broadcasted_iota(jnp.int32, (rows, LANES), 1) == 0).astype(BF16)

    def step(q, kn, krp, v, ones, carry):
        m, acc = carry
        kc = jnp.concatenate([kn, krp], axis=1)
        s = lax.dot_general(q, kc, contract1, preferred_element_type=F32)
        m_new = jnp.maximum(m, jnp.max(s, axis=-1, keepdims=True))
        p = jnp.exp2(s - m_new).astype(BF16)
        pv = jnp.dot(p, jnp.concatenate([v, ones], axis=1), preferred_element_type=F32)
        return m_new, jnp.exp2(m - m_new) * acc + pv

    carry = tuple((jnp.full((tq, 1), -jnp.inf, F32), jnp.zeros((tq, 2 * LANES), F32)) for _ in heads)
    if has_ctx:
        ones = ones_column(PAST_LEN)
        carry = tuple(step(qs[hd], knc_ref[0, :, head_cols(hd)], krpc_ref[0], vc_ref[0, :, head_cols(hd)],
                           ones, carry[hd]) for hd in heads)
    ones = ones_column(tk)

    def chunk(c, carry):
        rows = pl.ds(pl.multiple_of(c * tk, tk), tk)
        krp = krp_ref[rows, :]
        return tuple(step(qs[hd], kn_ref[rows, head_cols(hd)], krp, v_ref[rows, head_cols(hd)],
                          ones, carry[hd]) for hd in heads)

    unroll = math.gcd(n_chunks, ATTN_CHUNK_UNROLL)

    def body(c, carry):
        for u in range(unroll):
            carry = chunk(c * unroll + u, carry)
        return carry

    carry = lax.fori_loop(0, n_chunks // unroll, body, carry)
    for hd in heads:
        acc = carry[hd][1]
        o_ref[:, head_cols(hd)] = (acc[:, :MLA_V] / acc[:, MLA_V:MLA_V + 1]).astype(o_ref.dtype)


def _attention(q, kn, krp, v, ctx, *, n_req, seq_len, row0, tq, tk):
    rb0 = row0 // seq_len
    qb0 = row0 // tq
    nq = seq_len // tq
    hw = ATTN_HEADS_PER_STEP * LANES
    o_spec = pl.BlockSpec((tq, hw), lambda b, h, i: (b * nq + i, h))
    k_spec = pl.BlockSpec((seq_len, hw), lambda b, h, i: (rb0 + b, h))
    in_specs = [pl.BlockSpec((tq, 2 * hw), lambda b, h, i: (qb0 + b * nq + i, h)), k_spec,
                pl.BlockSpec((seq_len, LANES), lambda b, h, i: (rb0 + b, 0)), k_spec]
    args = [q, kn, krp, v]
    if ctx is not None:
        c_spec = pl.BlockSpec((1, PAST_LEN, hw), lambda b, h, i: (b, 0, h))
        in_specs += [c_spec, pl.BlockSpec((1, PAST_LEN, LANES), lambda b, h, i: (b, 0, 0)), c_spec]
        args += list(ctx)
    return pl.pallas_call(
        functools.partial(_attn_kernel, tk=tk, n_chunks=seq_len // tk, has_ctx=ctx is not None),
        grid=(n_req, MLA_HEADS // ATTN_HEADS_PER_STEP, nq),
        in_specs=in_specs,
        out_specs=o_spec,
        out_shape=jax.ShapeDtypeStruct((n_req * seq_len, HD), BF16),
        compiler_params=_cparams(("parallel", "parallel", "arbitrary")),
        name=f"attention_{seq_len}",
    )(*args)


def _rope_rot_cols(w):
    w1, w2, w3, w4 = jnp.split(w, 4, axis=-1)
    return jnp.concatenate([-w2, w1, -w4, w3], axis=-1)


def _pad_cols(w, width):
    return jnp.pad(w, ((0, 0), (0, width - w.shape[1])))


def _mla_weights(w_in, w_q_b, w_kv_b):
    w_rope = w_in[:, MLA_Q_LORA + MLA_KV_LORA:]
    win_ext = jnp.concatenate([w_in[:, :MLA_Q_LORA + MLA_KV_LORA], _pad_cols(w_rope, LANES),
                               _pad_cols(_rope_rot_cols(w_rope), LANES)], axis=1).astype(BF16)
    wq = w_q_b.reshape(MLA_Q_LORA, MLA_HEADS, MLA_NOPE + MLA_ROPE)
    wq_nope = wq[:, :, :MLA_NOPE].reshape(MLA_Q_LORA, HD)
    wq_rope = wq[:, :, MLA_NOPE:]
    pad = ((0, 0), (0, 0), (0, LANES - MLA_ROPE))
    wq_rope_p = jnp.pad(wq_rope, pad).reshape(MLA_Q_LORA, HD)
    wq_rot_p = jnp.pad(_rope_rot_cols(wq_rope), pad).reshape(MLA_Q_LORA, HD)
    wq_ext = jnp.concatenate([wq_nope, wq_rope_p, wq_rot_p], axis=1).astype(BF16)
    wkv = w_kv_b.reshape(MLA_KV_LORA, MLA_HEADS, MLA_NOPE + MLA_V)
    wkv_ext = jnp.concatenate([wkv[:, :, :MLA_NOPE].reshape(MLA_KV_LORA, HD),
                               wkv[:, :, MLA_NOPE:].reshape(MLA_KV_LORA, HD)], axis=1).astype(BF16)
    return win_ext, wq_ext, wkv_ext


def _rope_tables():
    rows = DEC_SEQ // GRID_W
    row = jnp.repeat(jnp.arange(rows, dtype=F32), GRID_W)
    col = jnp.tile(jnp.arange(GRID_W, dtype=F32), rows)
    half = MLA_ROPE // 2
    inv = 1.0 / (ROPE_BASE ** (jnp.arange(0, half, 2, dtype=F32) / half))
    ar = row[:, None] * inv
    ac = col[:, None] * inv
    ang = jnp.concatenate([ar, ar, ac, ac] * 2, axis=-1)
    return jnp.cos(ang), jnp.sin(ang)


def kernel(x_prompt, x_sample, state_ret_fwd, state_ret_bwd, cache_mla_ckv, cache_mla_krope, c, c_ctx,
           ada_w, ada_b, norm_pre, norm_post, ret_w_in, ret_decay_fwd, ret_decay_bwd, ret_w_out,
           mla_w_in, mla_q_norm, mla_kv_norm, mla_w_q_b, mla_w_kv_b, mla_w_out,
           moe_w_router, moe_w_gate, moe_w_up, moe_w_down):
    x_parts = (x_sample.reshape(N_SAMPLE, D_MODEL), x_prompt.reshape(N_PROMPT, D_MODEL))
    joint = False
    cond8 = jnp.concatenate([c, c_ctx[None, :], jnp.zeros((8 - NG, D_MODEL), F32)], axis=0)
    cos128, sin128 = _rope_tables()
    outs = {}

    for i in range(DEPTH):
        mods = _adaln(cond8, ada_w, ada_b.reshape(DEPTH, 1, 6 * D_MODEL), i)[:NG].reshape(NG, 6, D_MODEL)
        gpre1 = norm_pre[i, 0][None, :]
        gpre2 = norm_pre[i, 1][None, :]
        gpost1 = norm_post[i, 0][None, :]
        gpost2 = norm_post[i, 1][None, :]
        if i % 2 == 0:
            r = i // 2
            qkvg = _norm_mod_matmul(x_parts, joint, mods, gpre1, ret_w_in[r], 1536)
            lg = jnp.stack([jax.nn.log_sigmoid(ret_decay_fwd[r].astype(F32)),
                            jax.nn.log_sigmoid(ret_decay_bwd[r].astype(F32))])
            yg_s, _, _ = _retention(qkvg, lg, state_ret_fwd[:, r], state_ret_bwd[:, r],
                                    n_req=DEC_BATCH, seq_len=DEC_SEQ, row0=0)
            yg_p, s_f, s_b = _retention(qkvg, lg, None, None, n_req=BATCH, seq_len=SEQ, row0=N_SAMPLE)
            outs["fwd"], outs["bwd"] = s_f[:, None], s_b[:, None]
            mix, w_out = (yg_s, yg_p), ret_w_out[r]
        else:
            mi = i // 2
            win_ext, wq_ext, wkv_ext = _mla_weights(mla_w_in[mi], mla_w_q_b[mi], mla_w_kv_b[mi])
            q, kn, v, krp, ckv, kraw = _mla_proj(
                x_parts, joint, mods, gpre1, win_ext, mla_q_norm[mi][None, :], mla_kv_norm[mi][None, :],
                wq_ext, wkv_ext, cos128, sin128)
            outs["ckv"] = ckv.reshape(BATCH, 1, SEQ, MLA_KV_LORA)
            outs["krope"] = kraw[:, :MLA_ROPE].reshape(BATCH, 1, SEQ, MLA_ROPE)
            kvc = _ctx_expand(cache_mla_ckv[:, mi].reshape(DEC_BATCH * PAST_LEN, MLA_KV_LORA), wkv_ext)
            kvc = kvc.reshape(DEC_BATCH, PAST_LEN, 2 * HD)
            krpc = jnp.pad(cache_mla_krope[:, mi], ((0, 0), (0, 0), (0, LANES - MLA_ROPE))).astype(BF16)
            o_s = _attention(q, kn, krp, v, (kvc[:, :, :HD], krpc, kvc[:, :, HD:]),
                             n_req=DEC_BATCH, seq_len=DEC_SEQ, row0=0, tq=512, tk=512)
            o_p = _attention(q, kn, krp, v, None, n_req=BATCH, seq_len=SEQ, row0=N_SAMPLE, tq=SEQ, tk=SEQ)
            mix, w_out = (o_s, o_p), mla_w_out[mi]
        x1, h3, aff_t = _out_post(mix, w_out, x_parts, joint, mods, gpost1, gpre2, moe_w_router[i].T)
        idx, gate = _route(aff_t)
        f3_lo, wg16, wu16, wd16 = _moe(idx, gate, h3, moe_w_gate, moe_w_up, moe_w_down, i,
                                       group0=0, n_groups=MOE_FIRST_GROUPS, emit_bf16=True,
                                       ff_split=MOE_FIRST_FF_SPLIT)
        f3_hi, = _moe(idx, gate, h3, wg16, wu16, wd16, 0,
                      group0=MOE_FIRST_GROUPS, n_groups=NG - MOE_FIRST_GROUPS, emit_bf16=False,
                      ff_split=MOE_REST_FF_SPLIT)
        last = i == DEPTH - 1
        x_new = _moe_post((f3_lo, f3_hi), x1, mods, gpost2, split_out=last)
        if not last:
            x_parts, joint = (x_new, x_new), True

    y_sample, y_prompt = x_new
    return (y_prompt.reshape(BATCH, SEQ, D_MODEL), y_sample.reshape(DEC_BATCH, DEC_SEQ, D_MODEL),
            outs["fwd"], outs["bwd"], outs["ckv"], outs["krope"])
```

```python
import functools
import math

import jax
import jax.numpy as jnp
from jax import lax
from jax.experimental import pallas as pl
from jax.experimental.pallas import tpu as pltpu

F32 = jnp.float32
BF16 = jnp.bfloat16

D_MODEL = 1024
BATCH = 16
SEQ = 256
DEPTH = 2
DEC_BATCH = 4
DEC_SEQ = 4096
PAST_LEN = 256
GRID_W = 64
RET_HEADS = 4
RET_DK = 256
RET_DV = 512
RET_QK_WIDTH = RET_HEADS * RET_DK
RET_V_WIDTH = RET_HEADS * RET_DV
RET_CHUNK = 128
MLA_HEADS = 8
MLA_NOPE = 128
MLA_ROPE = 64
MLA_V = 128
MLA_Q_LORA = 384
MLA_KV_LORA = 256
ROPE_BASE = 10000.0
N_EXPERTS = 16
EXPERT_FF = 1024
NORM_EPS = 1e-6

GT = DEC_SEQ
NG = DEC_BATCH + 1
NT = NG * GT
N_SAMPLE = DEC_BATCH * DEC_SEQ
N_PROMPT = BATCH * SEQ
CAP = 2 * GT // N_EXPERTS

LANES = 128
SUBLANES = 8
ROW_TILES = D_MODEL // LANES
TM = 512
TILES_PER_GROUP = GT // TM
SAMPLE_TILES = N_SAMPLE // TM
VMEM_LIMIT = 60 * 1024 * 1024


def _cparams(sem):
    return pltpu.CompilerParams(dimension_semantics=sem, vmem_limit_bytes=VMEM_LIMIT)


def _rms(x):
    return x * lax.rsqrt(jnp.mean(x * x, axis=-1, keepdims=True) + NORM_EPS)


def _sigmoid(x):
    return 1.0 / (1.0 + jnp.exp(-x))


def _part_specs(width, joint, split=SAMPLE_TILES, rows=TM):
    lo = pl.BlockSpec((rows, width), lambda *g: (jnp.minimum(g[-1], split - 1), 0))
    if joint:
        hi = pl.BlockSpec((rows, width), lambda *g: (jnp.maximum(g[-1], split), 0))
    else:
        hi = pl.BlockSpec((rows, width), lambda *g: (jnp.maximum(g[-1] - split, 0), 0))
    return [lo, hi]


def _load_rows(lo_ref, hi_ref, i, split=SAMPLE_TILES):
    return jnp.where(i < split, lo_ref[...], hi_ref[...])


def _mod_spec():
    return pl.BlockSpec((1, 6, D_MODEL), lambda *g: (g[-1] // TILES_PER_GROUP, 0, 0))


def _const_spec(shape):
    return pl.BlockSpec(shape, lambda *g: (0,) * len(shape))


def _adaln_kernel(c_ref, w_ref, b_ref, o_ref):
    c = c_ref[...]
    s = (c * _sigmoid(c)).astype(BF16)
    o_ref[...] = jnp.dot(s, w_ref[0].astype(BF16), preferred_element_type=F32) + b_ref[0]


def _adaln(cond8, w, b, layer):
    tn = 1536
    return pl.pallas_call(
        _adaln_kernel,
        grid=(6 * D_MODEL // tn,),
        in_specs=[pl.BlockSpec((8, D_MODEL), lambda j: (0, 0)),
                  pl.BlockSpec((1, D_MODEL, tn), lambda j: (layer, 0, j)),
                  pl.BlockSpec((1, 1, tn), lambda j: (layer, 0, j))],
        out_specs=pl.BlockSpec((8, tn), lambda j: (0, j)),
        out_shape=jax.ShapeDtypeStruct((8, 6 * D_MODEL), F32),
        compiler_params=_cparams(("arbitrary",)),
        name="adaln",
    )(cond8, w, b)


NMM_TM = 1024


def _nmm_kernel(xlo_ref, xhi_ref, mod_ref, gain_ref, w_ref, o_ref, w_scr):
    i = pl.program_id(1)

    @pl.when(i == 0)
    def _():
        w_scr[...] = w_ref[...].astype(BF16)

    m = mod_ref[0]
    x = _load_rows(xlo_ref, xhi_ref, i, split=N_SAMPLE // NMM_TM)
    h = _rms(x) * gain_ref[...] * (1.0 + m[1:2]) + m[0:1]
    o_ref[...] = jnp.dot(h.astype(BF16), w_scr[...], preferred_element_type=F32).astype(o_ref.dtype)


def _norm_mod_matmul(x_parts, joint, mods, gain, w, tn):
    n_out = w.shape[1]
    tm = NMM_TM
    return pl.pallas_call(
        _nmm_kernel,
        grid=(n_out // tn, NT // tm),
        in_specs=[*_part_specs(D_MODEL, joint, split=N_SAMPLE // tm, rows=tm),
                  pl.BlockSpec((1, 6, D_MODEL), lambda j, i: (i // (GT // tm), 0, 0)),
                  _const_spec((1, D_MODEL)),
                  pl.BlockSpec((D_MODEL, tn), lambda j, i: (0, j))],
        out_specs=pl.BlockSpec((tm, tn), lambda j, i: (i, j)),
        out_shape=jax.ShapeDtypeStruct((NT, n_out), BF16),
        scratch_shapes=[pltpu.VMEM((D_MODEL, tn), BF16)],
        compiler_params=_cparams(("arbitrary", "arbitrary")),
        name="norm_mod_matmul",
    )(*x_parts, mods, gain, w)


RET_SCAN_UNROLL = 2


def _ret_kernel(lg_ref, q_ref, k_ref, v_ref, g_ref, *rest, seq_len, has_init):
    if has_init:
        s0f_ref, s0b_ref, *rest = rest
    yg_ref, sf_ref, sb_ref, ypart, ycross, sf_scr, sb_scr = rest
    C = RET_CHUNK
    nc = seq_len // C
    head = pl.program_id(1)
    lgf = lg_ref[0, head]
    lgb = lg_ref[1, head]
    k_scale = RET_DK ** -0.5

    ii = lax.broadcasted_iota(jnp.int32, (C, C), 0).astype(F32)
    jj = lax.broadcasted_iota(jnp.int32, (C, C), 1).astype(F32)
    diff = ii - jj
    inner = jnp.where(diff >= 0, jnp.exp(diff * lgf), jnp.exp(-diff * lgb)) * k_scale
    pos = lax.broadcasted_iota(jnp.int32, (C, 1), 0).astype(F32)
    cross_f = jnp.exp((pos + 1.0) * lgf)
    cross_b = jnp.exp((C - pos) * lgb)
    sdec_f = jnp.exp((C - 1.0 - pos) * lgf) * k_scale
    sdec_b = jnp.exp(pos * lgb) * k_scale
    one = jnp.ones((1, 1), F32)
    cdec_f = jnp.exp(one * (C * lgf))
    cdec_b = jnp.exp(one * (C * lgb))

    if has_init:
        sf_scr[...] = s0f_ref[0, 0]
        sb_scr[...] = s0b_ref[0, 0]
    else:
        sf_scr[...] = jnp.zeros_like(sf_scr)
        sb_scr[...] = jnp.zeros_like(sb_scr)

    contract0 = (((0,), (0,)), ((), ()))
    contract1 = (((1,), (1,)), ((), ()))

    def chunk_rows(c):
        return pl.ds(pl.multiple_of(c * C, C), C)

    def bwd_chunk(c):
        rows = chunk_rows(c)
        q = q_ref[rows, :]
        s = sb_scr[...]
        ycross[rows, :] = jnp.dot(q, s.astype(BF16), preferred_element_type=F32) * cross_b
        kd = (k_ref[rows, :].astype(F32) * sdec_b).astype(BF16)
        sb_scr[...] = s * cdec_b + lax.dot_general(kd, v_ref[rows, :], contract0,
                                                   preferred_element_type=F32)

    def fwd_chunk(c):
        rows = chunk_rows(c)
        q = q_ref[rows, :]
        k = k_ref[rows, :]
        v = v_ref[rows, :]
        s = sf_scr[...]
        scores = lax.dot_general(q, k, contract1, preferred_element_type=F32) * inner
        ypart[rows, :] = (jnp.dot(scores.astype(BF16), v, preferred_element_type=F32)
                          + jnp.dot(q, s.astype(BF16), preferred_element_type=F32) * cross_f)
        kd = (k.astype(F32) * sdec_f).astype(BF16)
        sf_scr[...] = s * cdec_f + lax.dot_general(kd, v, contract0, preferred_element_type=F32)

    def scan(t, carry):
        for u in range(RET_SCAN_UNROLL):
            step = t * RET_SCAN_UNROLL + u
            fwd_chunk(step)
            bwd_chunk(nc - 1 - step)
        return carry

    lax.fori_loop(0, nc // RET_SCAN_UNROLL, scan, 0)
    sf_ref[0, 0] = sf_scr[...]
    sb_ref[0, 0] = sb_scr[...]

    def finish(t, carry):
        for u in range(RET_SCAN_UNROLL):
            rows = chunk_rows(t * RET_SCAN_UNROLL + u)
            y = ypart[rows, :] + ycross[rows, :]
            mu = jnp.mean(y, axis=-1, keepdims=True)
            yc = y - mu
            yn = yc * lax.rsqrt(jnp.mean(yc * yc, axis=-1, keepdims=True) + NORM_EPS)
            g = g_ref[rows, :].astype(F32)
            yg_ref[rows, :] = (g * _sigmoid(g) * yn).astype(yg_ref.dtype)
        return carry

    lax.fori_loop(0, nc // RET_SCAN_UNROLL, finish, 0)


def _retention(qkvg, lg, s0f, s0b, *, n_req, seq_len, row0):
    rb0 = row0 // seq_len
    has_init = s0f is not None
    qk_spec = lambda off: pl.BlockSpec((seq_len, RET_DK), lambda b, h, lg_: (rb0 + b, off + h))
    vg_spec = lambda off: pl.BlockSpec((seq_len, RET_DV), lambda b, h, lg_: (rb0 + b, off + h))
    st_spec = pl.BlockSpec((1, 1, RET_DK, RET_DV), lambda b, h, lg_: (b, h, 0, 0))
    in_specs = [qk_spec(0), qk_spec(RET_HEADS), vg_spec(RET_HEADS), vg_spec(2 * RET_HEADS)]
    args = [qkvg, qkvg, qkvg, qkvg]
    if has_init:
        in_specs += [st_spec, st_spec]
        args += [s0f, s0b]
    st_shape = jax.ShapeDtypeStruct((n_req, RET_HEADS, RET_DK, RET_DV), F32)
    return pl.pallas_call(
        functools.partial(_ret_kernel, seq_len=seq_len, has_init=has_init),
        grid_spec=pltpu.PrefetchScalarGridSpec(
            num_scalar_prefetch=1,
            grid=(n_req, RET_HEADS),
            in_specs=in_specs,
            out_specs=[pl.BlockSpec((seq_len, RET_DV), lambda b, h, lg_: (b, h)), st_spec, st_spec],
            scratch_shapes=[pltpu.VMEM((seq_len, RET_DV), F32),
                            pltpu.VMEM((seq_len, RET_DV), F32),
                            pltpu.VMEM((RET_DK, RET_DV), F32),
                            pltpu.VMEM((RET_DK, RET_DV), F32)]),
        out_shape=[jax.ShapeDtypeStruct((n_req * seq_len, RET_V_WIDTH), BF16), st_shape, st_shape],
        compiler_params=_cparams(("parallel", "arbitrary")),
        name=f"retention_{seq_len}",
    )(lg, *args)


def _split_bf16(x):
    hi = x.astype(BF16)
    return hi, (x - hi.astype(F32)).astype(BF16)


def _out_post_kernel(ylo_ref, yhi_ref, w_ref, xlo_ref, xhi_ref, mod_ref, gpost_ref, gpre_ref, wr_ref,
                     x1_ref, h3_ref, aff_ref, w_scr):
    i = pl.program_id(0)

    @pl.when(i == 0)
    def _():
        w_scr[...] = w_ref[...].astype(BF16)

    m = mod_ref[0]
    y = jnp.dot(_load_rows(ylo_ref, yhi_ref, i), w_scr[...], preferred_element_type=F32)
    x1 = _load_rows(xlo_ref, xhi_ref, i) + _rms(y) * gpost_ref[...] * m[2:3]
    x1_ref[...] = x1
    h = _rms(x1) * gpre_ref[...] * (1.0 + m[4:5]) + m[3:4]
    for c in range(ROW_TILES):
        h3_ref[pl.ds(c, TM, stride=ROW_TILES), :] = h[:, c * LANES:(c + 1) * LANES]
    contract1 = (((1,), (1,)), ((), ()))
    w_hi, w_lo = _split_bf16(wr_ref[...])
    h_hi, h_lo = _split_bf16(h)
    part = lax.dot_general(jnp.concatenate([w_hi, w_lo], axis=0), h_hi, contract1,
                           preferred_element_type=F32)
    logits = (part[:N_EXPERTS] + part[N_EXPERTS:]
              + lax.dot_general(w_hi, h_lo, contract1, preferred_element_type=F32))
    e = jnp.exp(logits - jnp.max(logits, axis=0, keepdims=True))
    aff_ref[...] = e / jnp.sum(e, axis=0, keepdims=True)


def _out_post(y_parts, w, x_parts, joint, mods, gpost, gpre, w_router_t):
    kdim = w.shape[0]
    return pl.pallas_call(
        _out_post_kernel,
        grid=(NT // TM,),
        in_specs=[*_part_specs(kdim, joint=False),
                  _const_spec((kdim, D_MODEL)),
                  *_part_specs(D_MODEL, joint), _mod_spec(),
                  _const_spec((1, D_MODEL)), _const_spec((1, D_MODEL)),
                  _const_spec((N_EXPERTS, D_MODEL))],
        out_specs=[pl.BlockSpec((TM, D_MODEL), lambda i: (i, 0)),
                   pl.BlockSpec((TM * ROW_TILES, LANES), lambda i: (i, 0)),
                   pl.BlockSpec((N_EXPERTS, TM), lambda i: (0, i))],
        out_shape=[jax.ShapeDtypeStruct((NT, D_MODEL), F32),
                   jax.ShapeDtypeStruct((NT * ROW_TILES, LANES), F32),
                   jax.ShapeDtypeStruct((N_EXPERTS, NT), F32)],
        scratch_shapes=[pltpu.VMEM((kdim, D_MODEL), BF16)],
        compiler_params=_cparams(("arbitrary",)),
        name="out_post",
    )(*y_parts, w, *x_parts, mods, gpost, gpre, w_router_t)


SCATTER_BATCH = 8
MOE_FIRST_GROUPS = 1
MOE_FIRST_FF_SPLIT = 4
MOE_REST_FF_SPLIT = 1
MOE_REST_W_PARTS = 4


def _moe_kernel(idx_ref, gate_ref, h3_ref, *rest, emit_bf16, ff_split, w_parts):
    wg_refs, wu_refs, wd_refs = (rest[j * w_parts:(j + 1) * w_parts] for j in range(3))
    f3_ref, *rest = rest[3 * w_parts:]
    if emit_bf16:
        wg16_ref, wu16_ref, wd16_ref, *rest = rest
    xg, x_a, x_b, yacc, y_a, y_b = rest
    e = pl.program_id(1)
    f = pl.program_id(2)

    def tile(r):
        return pl.ds(pl.multiple_of(r * SUBLANES, SUBLANES), SUBLANES)

    def gather(ex, x_dst):
        base = ex * CAP
        for r in range(CAP):
            xg[r * SUBLANES:(r + 1) * SUBLANES, :] = h3_ref[tile(idx_ref[0, 0, base + r]), :]
        x_dst[...] = jnp.concatenate(
            [xg[pl.ds(c, CAP, stride=ROW_TILES), :] for c in range(ROW_TILES)], axis=1).astype(BF16)

    def scatter(ex, y_src):
        base = ex * CAP
        for r0 in range(0, CAP, SCATTER_BATCH):
            dst, vals = [], []
            for r in range(r0, r0 + SCATTER_BATCH):
                t = idx_ref[0, 0, base + r]
                dst.append(tile(t))
                vals.append(f3_ref[tile(t), :]
                            + gate_ref[0, 0, base + r] * y_src[r * SUBLANES:(r + 1) * SUBLANES, :])
            for d, v in zip(dst, vals):
                f3_ref[d, :] = v

    def half_ffn(x_src):
        def slab_dot(lhs, w_refs, w16_ref):
            rows = lhs.shape[1] // w_parts
            total = None
            for p, w_ref in enumerate(w_refs):
                w = w_ref[0, 0].astype(BF16)
                if emit_bf16:
                    w16_ref[0, 0] = w
                part = jnp.dot(lhs[:, p * rows:(p + 1) * rows], w, preferred_element_type=F32)
                total = part if total is None else total + part
            return total

        x = x_src[...]
        a = slab_dot(x, wg_refs, wg16_ref if emit_bf16 else None)
        u = slab_dot(x, wu_refs, wu16_ref if emit_bf16 else None)
        act = (a * _sigmoid(a) * u).astype(BF16)
        return slab_dot(act, wd_refs, wd16_ref if emit_bf16 else None)

    first = f == 0
    last = f == ff_split - 1

    @pl.when(jnp.logical_and(e == 0, first))
    def _():
        f3_ref[...] = jnp.zeros_like(f3_ref)
        y_b[...] = jnp.zeros_like(y_b)
        gather(0, x_a)

    if ff_split == 1:
        phases = (("only", first),)
    elif ff_split == 2:
        phases = (("first", first), ("last", last))
    else:
        phases = (("first", first), ("middle", jnp.logical_not(jnp.logical_or(first, last))), ("last", last))
    for parity, (x_cur, x_nxt, y_cur, y_prv) in enumerate(((x_a, x_b, y_a, y_b), (x_b, x_a, y_b, y_a))):
        for phase, cond in phases:
            @pl.when(jnp.logical_and(e % 2 == parity, cond))
            def _():
                if phase in ("only", "first"):
                    gather(jnp.minimum(e + 1, N_EXPERTS - 1), x_nxt)
                part = half_ffn(x_cur)
                if phase == "first":
                    yacc[...] = part
                elif phase == "middle":
                    yacc[...] += part
                else:
                    y = part if phase == "only" else yacc[...] + part
                    for c in range(ROW_TILES):
                        y_cur[pl.ds(c, CAP, stride=ROW_TILES), :] = y[:, c * LANES:(c + 1) * LANES]
                    scatter(jnp.maximum(e - 1, 0), y_prv)

    @pl.when(jnp.logical_and(e == N_EXPERTS - 1, last))
    def _():
        scatter(N_EXPERTS - 1, y_b)


def _moe(idx, gate, h3, wg, wu, wd, layer, *, group0, n_groups, emit_bf16, ff_split, w_parts):
    grp_rows = GT * ROW_TILES
    ff_tile = EXPERT_FF // ff_split
    route_spec = pl.BlockSpec((1, 1, N_EXPERTS * CAP), lambda b, e, f: (group0 + b, 0, 0),
                              memory_space=pltpu.SMEM)
    assert not emit_bf16 or w_parts == 1
    w_in_spec = lambda l, p=0: pl.BlockSpec((1, 1, D_MODEL // w_parts, ff_tile),
                                            lambda b, e, f: (l, e, p, f))
    w_out_spec = lambda l, p=0: pl.BlockSpec((1, 1, ff_tile // w_parts, D_MODEL),
                                             lambda b, e, f: (l, e, f * w_parts + p, 0))
    slabs = range(w_parts)
    out_specs = [pl.BlockSpec((grp_rows, LANES), lambda b, e, f: (b, 0), pipeline_mode=pl.Buffered(1))]
    out_shape = [jax.ShapeDtypeStruct((n_groups * grp_rows, LANES), F32)]
    if emit_bf16:
        out_specs += [w_in_spec(0), w_in_spec(0), w_out_spec(0)]
        out_shape += [jax.ShapeDtypeStruct((1, N_EXPERTS, D_MODEL, EXPERT_FF), BF16),
                      jax.ShapeDtypeStruct((1, N_EXPERTS, D_MODEL, EXPERT_FF), BF16),
                      jax.ShapeDtypeStruct((1, N_EXPERTS, EXPERT_FF, D_MODEL), BF16)]
    return pl.pallas_call(
        functools.partial(_moe_kernel, emit_bf16=emit_bf16, ff_split=ff_split, w_parts=w_parts),
        grid=(n_groups, N_EXPERTS, ff_split),
        in_specs=[route_spec, route_spec,
                  pl.BlockSpec((grp_rows, LANES), lambda b, e, f: (group0 + b, 0),
                               pipeline_mode=pl.Buffered(1)),
                  *[w_in_spec(layer, p) for p in slabs], *[w_in_spec(layer, p) for p in slabs],
                  *[w_out_spec(layer, p) for p in slabs]],
        out_specs=out_specs,
        out_shape=out_shape,
        scratch_shapes=[pltpu.VMEM((CAP * ROW_TILES, LANES), F32),
                        pltpu.VMEM((CAP, D_MODEL), BF16),
                        pltpu.VMEM((CAP, D_MODEL), BF16),
                        pltpu.VMEM((CAP, D_MODEL), F32),
                        pltpu.VMEM((CAP * ROW_TILES, LANES), F32),
                        pltpu.VMEM((CAP * ROW_TILES, LANES), F32)],
        compiler_params=_cparams(("arbitrary", "arbitrary", "arbitrary")),
        name="moe_experts",
    )(idx, gate, h3, *[wg] * w_parts, *[wu] * w_parts, *[wd] * w_parts)


def _moe_post_kernel(f3lo_ref, f3hi_ref, x_ref, mod_ref, gpost_ref, *o_refs):
    i = pl.program_id(0)
    m = mod_ref[0]
    rows = lambda ref: jnp.concatenate(
        [ref[pl.ds(c, TM, stride=ROW_TILES), :] for c in range(ROW_TILES)], axis=1)
    fx = jnp.where(i < MOE_FIRST_GROUPS * TILES_PER_GROUP, rows(f3lo_ref), rows(f3hi_ref))
    out = x_ref[...] + _rms(fx) * gpost_ref[...] * m[5:6]
    if len(o_refs) == 1:
        o_refs[0][...] = out
    else:
        @pl.when(i < SAMPLE_TILES)
        def _():
            o_refs[0][...] = out

        @pl.when(i >= SAMPLE_TILES)
        def _():
            o_refs[1][...] = out


def _moe_post(f3_parts, x, mods, gpost, split_out):
    if split_out:
        out_specs = _part_specs(D_MODEL, joint=False)
        out_shape = [jax.ShapeDtypeStruct((N_SAMPLE, D_MODEL), F32),
                     jax.ShapeDtypeStruct((N_PROMPT, D_MODEL), F32)]
    else:
        out_specs = pl.BlockSpec((TM, D_MODEL), lambda i: (i, 0))
        out_shape = jax.ShapeDtypeStruct((NT, D_MODEL), F32)
    return pl.pallas_call(
        _moe_post_kernel,
        grid=(NT // TM,),
        in_specs=[*_part_specs(LANES, joint=False, split=MOE_FIRST_GROUPS * TILES_PER_GROUP,
                               rows=TM * ROW_TILES),
                  pl.BlockSpec((TM, D_MODEL), lambda i: (i, 0)),
                  _mod_spec(), _const_spec((1, D_MODEL))],
        out_specs=out_specs,
        out_shape=out_shape,
        compiler_params=_cparams(("arbitrary",)),
        name="moe_post",
    )(*f3_parts, x, mods, gpost)


ROUTE_SEG = SEQ
RANK_LO = 16
RANK_HI = CAP // RANK_LO
TOK_LO = 64
ROUTE_VALS = 5


def _route_kernel(aff_ref, idx_ref, gate_ref, sel_scr, a_scr, code_scr, g_scr, acc_scr):
    g = pl.program_id(0)
    n_seg = GT // ROUTE_SEG
    seg = lambda s: slice(s * ROUTE_SEG, (s + 1) * ROUTE_SEG)
    aff = aff_ref[...]
    tok = lax.broadcasted_iota(jnp.int32, (N_EXPERTS, GT), 1)
    ones_seg = jnp.ones((ROUTE_SEG, ROUTE_SEG), BF16)

    def row_count(mask):
        return jnp.sum(jnp.where(mask, 1.0, 0.0), axis=1, keepdims=True)

    def seg_count(mask):
        m = jnp.where(mask, 1.0, 0.0).astype(BF16)
        return jnp.concatenate(
            [jnp.dot(m[:, seg(s)], ones_seg, preferred_element_type=F32) for s in range(n_seg)], axis=1)

    def select(count, k, shape):
        as_float = lambda b: pltpu.bitcast(b, F32)

        def value_bit(i, t):
            cand = t | jnp.left_shift(jnp.int32(1), 30 - i)
            return jnp.where(count(aff >= as_float(cand)) >= k, cand, t)

        thr = lax.fori_loop(0, 31, value_bit, jnp.zeros(shape, jnp.int32))
        above = aff >= as_float(thr + 1)
        tied = jnp.logical_and(aff >= as_float(thr), jnp.logical_not(above))
        need = k - count(above)

        def index_bit(i, v):
            cand = v | jnp.left_shift(jnp.int32(1), 11 - i)
            return jnp.where(count(jnp.logical_and(tied, tok < cand)) < need, cand, v)

        last = lax.fori_loop(0, 12, index_bit, jnp.zeros(shape, jnp.int32))
        chosen = jnp.logical_or(above, jnp.logical_and(tied, tok <= last))
        sel_scr[...] = jnp.where(chosen, 1.0, 0.0)

    @pl.when(g < DEC_BATCH)
    def _():
        select(row_count, float(2 * DEC_SEQ // N_EXPERTS), (N_EXPERTS, 1))

    @pl.when(g >= DEC_BATCH)
    def _():
        select(seg_count, float(2 * SEQ // N_EXPERTS), (N_EXPERTS, GT))

    sel = sel_scr[...]
    r_i = lax.broadcasted_iota(jnp.int32, (ROUTE_SEG, ROUTE_SEG), 0)
    c_i = lax.broadcasted_iota(jnp.int32, (ROUTE_SEG, ROUTE_SEG), 1)
    upper = jnp.where(r_i <= c_i, 1.0, 0.0).astype(BF16)
    sel_b = sel.astype(BF16)
    off = jnp.zeros((N_EXPERTS, 1), F32)
    for s in range(n_seg):
        inc = jnp.dot(sel_b[:, seg(s)], upper, preferred_element_type=F32)
        rank = inc + off - sel[:, seg(s)]
        a = jnp.floor(rank * (1.0 / RANK_LO))
        a_scr[:, seg(s)] = jnp.where(sel[:, seg(s)] > 0, a, -1.0)
        code_scr[:, seg(s)] = jnp.where(sel[:, seg(s)] > 0, rank - RANK_LO * a + 1.0, 0.0).astype(BF16)
        off = off + inc[:, ROUTE_SEG - 1:ROUTE_SEG]
    g1 = aff.astype(BF16).astype(F32)
    g2 = (aff - g1).astype(BF16).astype(F32)
    g_scr[0] = g1
    g_scr[1] = g2
    g_scr[2] = (aff - g1 - g2).astype(BF16).astype(F32)

    lane = lax.broadcasted_iota(jnp.int32, (N_EXPERTS, N_EXPERTS * RANK_LO), 1)
    row = lax.broadcasted_iota(jnp.int32, (N_EXPERTS, N_EXPERTS * RANK_LO), 0)
    lo_bits = RANK_LO.bit_length() - 1
    expand = jnp.where(jnp.right_shift(lane, lo_bits) == row, 1.0, 0.0).astype(BF16)
    digit = (jnp.bitwise_and(lax.broadcasted_iota(jnp.int32, (1, N_EXPERTS * RANK_LO), 1), RANK_LO - 1)
             + 1).astype(F32)
    a_iota = lax.broadcasted_iota(jnp.int32, (RANK_HI, 1), 0).astype(F32)
    acc_scr[...] = jnp.zeros_like(acc_scr)
    contract0 = (((0,), (0,)), ((), ()))

    def chunk(c, carry):
        lanes = pl.ds(pl.multiple_of(c * ROUTE_SEG, ROUTE_SEG), ROUTE_SEG)
        a_c = a_scr[:, lanes]
        g_c = [g_scr[j, :, lanes] for j in range(3)]
        tok_c = lax.broadcasted_iota(jnp.int32, (1, ROUTE_SEG), 1) + c * ROUTE_SEG
        t_hi = jnp.right_shift(tok_c, TOK_LO.bit_length() - 1).astype(F32)
        t_lo = jnp.bitwise_and(tok_c, TOK_LO - 1).astype(F32)
        rows = []
        for e in range(N_EXPERTS):
            hit = a_c[e:e + 1, :] == a_iota
            for val in (t_hi, t_lo, g_c[0][e:e + 1, :], g_c[1][e:e + 1, :], g_c[2][e:e + 1, :]):
                rows.append(jnp.where(hit, val, 0.0))
        lhs = jnp.concatenate(rows, axis=0).astype(BF16)
        spread = lax.dot_general(code_scr[:, lanes], expand, contract0, preferred_element_type=F32)
        low_hot = jnp.where(spread == digit, 1.0, 0.0).astype(BF16)
        acc_scr[...] += jnp.dot(lhs, low_hot, preferred_element_type=F32)
        return carry

    lax.fori_loop(0, n_seg, chunk, 0)
    per_e = ROUTE_VALS * RANK_HI
    for e in range(N_EXPERTS):
        blk = acc_scr[e * per_e:(e + 1) * per_e, :][:, e * RANK_LO:(e + 1) * RANK_LO]
        part = lambda j: blk[j * RANK_HI:(j + 1) * RANK_HI]
        idx_ref[0, e] = (part(0) * TOK_LO + part(1)).astype(jnp.int32)
        gate_ref[0, e] = part(2) + part(3) + part(4)


def _route(aff_t):
    out_spec = pl.BlockSpec((1, N_EXPERTS, RANK_HI, RANK_LO), lambda g: (g, 0, 0, 0))
    idx, gate = pl.pallas_call(
        _route_kernel,
        grid=(NG,),
        in_specs=[pl.BlockSpec((N_EXPERTS, GT), lambda g: (0, g))],
        out_specs=[out_spec, out_spec],
        out_shape=[jax.ShapeDtypeStruct((NG, N_EXPERTS, RANK_HI, RANK_LO), jnp.int32),
                   jax.ShapeDtypeStruct((NG, N_EXPERTS, RANK_HI, RANK_LO), F32)],
        scratch_shapes=[pltpu.VMEM((N_EXPERTS, GT), F32),
                        pltpu.VMEM((N_EXPERTS, GT), F32),
                        pltpu.VMEM((N_EXPERTS, GT), BF16),
                        pltpu.VMEM((3, N_EXPERTS, GT), F32),
                        pltpu.VMEM((N_EXPERTS * ROUTE_VALS * RANK_HI, N_EXPERTS * RANK_LO), F32)],
        compiler_params=_cparams(("arbitrary",)),
        name="route",
    )(aff_t)
    return idx.reshape(NG, 1, N_EXPERTS * CAP), gate.reshape(NG, 1, N_EXPERTS * CAP)


HD = MLA_HEADS * LANES
Q_SCALE = math.log2(math.e) / math.sqrt(MLA_NOPE + MLA_ROPE)


def _mla_proj_kernel(xlo_ref, xhi_ref, mod_ref, gain_ref, win_ref, qn_g_ref, kvn_g_ref, wq_ref, wkv_ref,
                     cos_ref, sin_ref, q_ref, kn_ref, v_ref, krp_ref, ckv_ref, kraw_ref):
    i = pl.program_id(0)
    m = mod_ref[0]
    h = (_rms(_load_rows(xlo_ref, xhi_ref, i)) * gain_ref[...] * (1.0 + m[1:2]) + m[0:1]).astype(BF16)
    lat = jnp.dot(h, win_ref[...], preferred_element_type=F32)
    q_lat = lat[:, :MLA_Q_LORA]
    kv_lat = lat[:, MLA_Q_LORA:MLA_Q_LORA + MLA_KV_LORA]
    kr = lat[:, MLA_Q_LORA + MLA_KV_LORA:MLA_Q_LORA + MLA_KV_LORA + LANES]
    kr_rot = lat[:, MLA_Q_LORA + MLA_KV_LORA + LANES:]
    is_latent = i < SAMPLE_TILES
    cos = jnp.where(is_latent, cos_ref[...], 1.0)
    sin = jnp.where(is_latent, sin_ref[...], 0.0)

    qln = (_rms(q_lat) * qn_g_ref[...]).astype(BF16)
    q = jnp.dot(qln, wq_ref[...], preferred_element_type=F32) * Q_SCALE
    for hd in range(MLA_HEADS):
        lo = HD + hd * LANES
        q_ref[:, 2 * hd * LANES:(2 * hd + 1) * LANES] = q[:, hd * LANES:(hd + 1) * LANES].astype(BF16)
        q_ref[:, (2 * hd + 1) * LANES:(2 * hd + 2) * LANES] = (
            q[:, lo:lo + LANES] * cos + q[:, HD + lo:HD + lo + LANES] * sin).astype(BF16)

    ckv = _rms(kv_lat) * kvn_g_ref[...]
    kv = jnp.dot(ckv.astype(BF16), wkv_ref[...], preferred_element_type=F32)
    kn_ref[...] = kv[:, :HD].astype(BF16)
    v_ref[...] = kv[:, HD:].astype(BF16)
    krp_ref[...] = (kr * cos + kr_rot * sin).astype(BF16)

    @pl.when(i >= SAMPLE_TILES)
    def _():
        ckv_ref[...] = ckv
        kraw_ref[...] = kr


def _mla_proj(x_parts, joint, mods, gain, win_ext, qn_g, kvn_g, wq_ext, wkv_ext, cos128, sin128):
    row = lambda w: pl.BlockSpec((TM, w), lambda i: (i, 0))
    ctx_row = lambda w: pl.BlockSpec((TM, w), lambda i: (jnp.maximum(i - SAMPLE_TILES, 0), 0))
    tab = pl.BlockSpec((TM, LANES), lambda i: (i % TILES_PER_GROUP, 0))
    return pl.pallas_call(
        _mla_proj_kernel,
        grid=(NT // TM,),
        in_specs=[*_part_specs(D_MODEL, joint), _mod_spec(),
                  _const_spec(gain.shape), _const_spec(win_ext.shape), _const_spec(qn_g.shape),
                  _const_spec(kvn_g.shape), _const_spec(wq_ext.shape), _const_spec(wkv_ext.shape),
                  tab, tab],
        out_specs=[row(2 * HD), row(HD), row(HD), row(LANES), ctx_row(MLA_KV_LORA), ctx_row(LANES)],
        out_shape=[jax.ShapeDtypeStruct((NT, 2 * HD), BF16),
                   jax.ShapeDtypeStruct((NT, HD), BF16),
                   jax.ShapeDtypeStruct((NT, HD), BF16),
                   jax.ShapeDtypeStruct((NT, LANES), BF16),
                   jax.ShapeDtypeStruct((N_PROMPT, MLA_KV_LORA), F32),
                   jax.ShapeDtypeStruct((N_PROMPT, LANES), F32)],
        compiler_params=_cparams(("arbitrary",)),
        name="mla_proj",
    )(*x_parts, mods, gain, win_ext, qn_g, kvn_g, wq_ext, wkv_ext, cos128, sin128)


def _matmul_kernel(a_ref, w_ref, o_ref):
    o_ref[...] = jnp.dot(a_ref[...].astype(BF16), w_ref[...],
                         preferred_element_type=F32).astype(o_ref.dtype)


def _ctx_expand(ckv_ctx, wkv_ext):
    n = ckv_ctx.shape[0]
    return pl.pallas_call(
        _matmul_kernel,
        grid=(n // TM,),
        in_specs=[pl.BlockSpec((TM, MLA_KV_LORA), lambda i: (i, 0)),
                  pl.BlockSpec(wkv_ext.shape, lambda i: (0, 0))],
        out_specs=pl.BlockSpec((TM, 2 * HD), lambda i: (i, 0)),
        out_shape=jax.ShapeDtypeStruct((n, 2 * HD), BF16),
        compiler_params=_cparams(("parallel",)),
        name="ctx_expand",
    )(ckv_ctx, wkv_ext)


ATTN_HEADS_PER_STEP = 2
ATTN_CHUNK_UNROLL = 8


def _attn_kernel(q_ref, kn_ref, krp_ref, v_ref, *rest, tk, n_chunks, has_ctx):
    if has_ctx:
        knc_ref, krpc_ref, vc_ref, o_ref = rest
    else:
        o_ref, = rest
    tq = q_ref.shape[0]
    contract1 = (((1,), (1,)), ((), ()))
    heads = range(ATTN_HEADS_PER_STEP)
    head_cols = lambda hd: slice(hd * LANES, (hd + 1) * LANES)
    qs = [q_ref[:, 2 * hd * LANES:2 * (hd + 1) * LANES] for hd in heads]

    def ones_column(rows):
        return (lax.broadcasted_iota(jnp.int32, (rows, LANES), 1) == 0).astype(BF16)

    def step(q, kn, krp, v, ones, carry):
        m, acc = carry
        kc = jnp.concatenate([kn, krp], axis=1)
        s = lax.dot_general(q, kc, contract1, preferred_element_type=F32)
        m_new = jnp.maximum(m, jnp.max(s, axis=-1, keepdims=True))
        p = jnp.exp2(s - m_new).astype(BF16)
        pv = jnp.dot(p, jnp.concatenate([v, ones], axis=1), preferred_element_type=F32)
        return m_new, jnp.exp2(m - m_new) * acc + pv

    carry = tuple((jnp.full((tq, 1), -jnp.inf, F32), jnp.zeros((tq, 2 * LANES), F32)) for _ in heads)
    if has_ctx:
        ones = ones_column(PAST_LEN)
        carry = tuple(step(qs[hd], knc_ref[0, :, head_cols(hd)], krpc_ref[0], vc_ref[0, :, head_cols(hd)],
                           ones, carry[hd]) for hd in heads)
    ones = ones_column(tk)

    def chunk(c, carry):
        rows = pl.ds(pl.multiple_of(c * tk, tk), tk)
        krp = krp_ref[rows, :]
        return tuple(step(qs[hd], kn_ref[rows, head_cols(hd)], krp, v_ref[rows, head_cols(hd)],
                          ones, carry[hd]) for hd in heads)

    unroll = math.gcd(n_chunks, ATTN_CHUNK_UNROLL)

    def body(c, carry):
        for u in range(unroll):
            carry = chunk(c * unroll + u, carry)
        return carry

    carry = lax.fori_loop(0, n_chunks // unroll, body, carry)
    for hd in heads:
        acc = carry[hd][1]
        o_ref[:, head_cols(hd)] = (acc[:, :MLA_V] / acc[:, MLA_V:MLA_V + 1]).astype(o_ref.dtype)


def _attention(q, kn, krp, v, ctx, *, n_req, seq_len, row0, tq, tk):
    rb0 = row0 // seq_len
    qb0 = row0 // tq
    nq = seq_len // tq
    hw = ATTN_HEADS_PER_STEP * LANES
    o_spec = pl.BlockSpec((tq, hw), lambda b, h, i: (b * nq + i, h))
    k_spec = pl.BlockSpec((seq_len, hw), lambda b, h, i: (rb0 + b, h))
    in_specs = [pl.BlockSpec((tq, 2 * hw), lambda b, h, i: (qb0 + b * nq + i, h)), k_spec,
                pl.BlockSpec((seq_len, LANES), lambda b, h, i: (rb0 + b, 0)), k_spec]
    args = [q, kn, krp, v]
    if ctx is not None:
        c_spec = pl.BlockSpec((1, PAST_LEN, hw), lambda b, h, i: (b, 0, h))
        in_specs += [c_spec, pl.BlockSpec((1, PAST_LEN, LANES), lambda b, h, i: (b, 0, 0)), c_spec]
        args += list(ctx)
    return pl.pallas_call(
        functools.partial(_attn_kernel, tk=tk, n_chunks=seq_len // tk, has_ctx=ctx is not None),
        grid=(n_req, MLA_HEADS // ATTN_HEADS_PER_STEP, nq),
        in_specs=in_specs,
        out_specs=o_spec,
        out_shape=jax.ShapeDtypeStruct((n_req * seq_len, HD), BF16),
        compiler_params=_cparams(("parallel", "parallel", "arbitrary")),
        name=f"attention_{seq_len}",
    )(*args)


def _rope_rot_cols(w):
    w1, w2, w3, w4 = jnp.split(w, 4, axis=-1)
    return jnp.concatenate([-w2, w1, -w4, w3], axis=-1)


def _pad_cols(w, width):
    return jnp.pad(w, ((0, 0), (0, width - w.shape[1])))


def _mla_weights(w_in, w_q_b, w_kv_b):
    w_rope = w_in[:, MLA_Q_LORA + MLA_KV_LORA:]
    win_ext = jnp.concatenate([w_in[:, :MLA_Q_LORA + MLA_KV_LORA], _pad_cols(w_rope, LANES),
                               _pad_cols(_rope_rot_cols(w_rope), LANES)], axis=1).astype(BF16)
    wq = w_q_b.reshape(MLA_Q_LORA, MLA_HEADS, MLA_NOPE + MLA_ROPE)
    wq_nope = wq[:, :, :MLA_NOPE].reshape(MLA_Q_LORA, HD)
    wq_rope = wq[:, :, MLA_NOPE:]
    pad = ((0, 0), (0, 0), (0, LANES - MLA_ROPE))
    wq_rope_p = jnp.pad(wq_rope, pad).reshape(MLA_Q_LORA, HD)
    wq_rot_p = jnp.pad(_rope_rot_cols(wq_rope), pad).reshape(MLA_Q_LORA, HD)
    wq_ext = jnp.concatenate([wq_nope, wq_rope_p, wq_rot_p], axis=1).astype(BF16)
    wkv = w_kv_b.reshape(MLA_KV_LORA, MLA_HEADS, MLA_NOPE + MLA_V)
    wkv_ext = jnp.concatenate([wkv[:, :, :MLA_NOPE].reshape(MLA_KV_LORA, HD),
                               wkv[:, :, MLA_NOPE:].reshape(MLA_KV_LORA, HD)], axis=1).astype(BF16)
    return win_ext, wq_ext, wkv_ext


def _rope_tables():
    rows = DEC_SEQ // GRID_W
    row = jnp.repeat(jnp.arange(rows, dtype=F32), GRID_W)
    col = jnp.tile(jnp.arange(GRID_W, dtype=F32), rows)
    half = MLA_ROPE // 2
    inv = 1.0 / (ROPE_BASE ** (jnp.arange(0, half, 2, dtype=F32) / half))
    ar = row[:, None] * inv
    ac = col[:, None] * inv
    ang = jnp.concatenate([ar, ar, ac, ac] * 2, axis=-1)
    return jnp.cos(ang), jnp.sin(ang)


def kernel(x_prompt, x_sample, state_ret_fwd, state_ret_bwd, cache_mla_ckv, cache_mla_krope, c, c_ctx,
           ada_w, ada_b, norm_pre, norm_post, ret_w_in, ret_decay_fwd, ret_decay_bwd, ret_w_out,
           mla_w_in, mla_q_norm, mla_kv_norm, mla_w_q_b, mla_w_kv_b, mla_w_out,
           moe_w_router, moe_w_gate, moe_w_up, moe_w_down):
    x_parts = (x_sample.reshape(N_SAMPLE, D_MODEL), x_prompt.reshape(N_PROMPT, D_MODEL))
    joint = False
    cond8 = jnp.concatenate([c, c_ctx[None, :], jnp.zeros((8 - NG, D_MODEL), F32)], axis=0)
    cos128, sin128 = _rope_tables()
    outs = {}

    for i in range(DEPTH):
        mods = _adaln(cond8, ada_w, ada_b.reshape(DEPTH, 1, 6 * D_MODEL), i)[:NG].reshape(NG, 6, D_MODEL)
        gpre1 = norm_pre[i, 0][None, :]
        gpre2 = norm_pre[i, 1][None, :]
        gpost1 = norm_post[i, 0][None, :]
        gpost2 = norm_post[i, 1][None, :]
        if i % 2 == 0:
            r = i // 2
            qkvg = _norm_mod_matmul(x_parts, joint, mods, gpre1, ret_w_in[r], 1536)
            lg = jnp.stack([jax.nn.log_sigmoid(ret_decay_fwd[r].astype(F32)),
                            jax.nn.log_sigmoid(ret_decay_bwd[r].astype(F32))])
            yg_s, _, _ = _retention(qkvg, lg, state_ret_fwd[:, r], state_ret_bwd[:, r],
                                    n_req=DEC_BATCH, seq_len=DEC_SEQ, row0=0)
            yg_p, s_f, s_b = _retention(qkvg, lg, None, None, n_req=BATCH, seq_len=SEQ, row0=N_SAMPLE)
            outs["fwd"], outs["bwd"] = s_f[:, None], s_b[:, None]
            mix, w_out = (yg_s, yg_p), ret_w_out[r]
        else:
            mi = i // 2
            win_ext, wq_ext, wkv_ext = _mla_weights(mla_w_in[mi], mla_w_q_b[mi], mla_w_kv_b[mi])
            q, kn, v, krp, ckv, kraw = _mla_proj(
                x_parts, joint, mods, gpre1, win_ext, mla_q_norm[mi][None, :], mla_kv_norm[mi][None, :],
                wq_ext, wkv_ext, cos128, sin128)
            outs["ckv"] = ckv.reshape(BATCH, 1, SEQ, MLA_KV_LORA)
            outs["krope"] = kraw[:, :MLA_ROPE].reshape(BATCH, 1, SEQ, MLA_ROPE)
            kvc = _ctx_expand(cache_mla_ckv[:, mi].reshape(DEC_BATCH * PAST_LEN, MLA_KV_LORA), wkv_ext)
            kvc = kvc.reshape(DEC_BATCH, PAST_LEN, 2 * HD)
            krpc = jnp.pad(cache_mla_krope[:, mi], ((0, 0), (0, 0), (0, LANES - MLA_ROPE))).astype(BF16)
            o_s = _attention(q, kn, krp, v, (kvc[:, :, :HD], krpc, kvc[:, :, HD:]),
                             n_req=DEC_BATCH, seq_len=DEC_SEQ, row0=0, tq=512, tk=512)
            o_p = _attention(q, kn, krp, v, None, n_req=BATCH, seq_len=SEQ, row0=N_SAMPLE, tq=SEQ, tk=SEQ)
            mix, w_out = (o_s, o_p), mla_w_out[mi]
        x1, h3, aff_t = _out_post(mix, w_out, x_parts, joint, mods, gpost1, gpre2, moe_w_router[i].T)
        idx, gate = _route(aff_t)
        f3_lo, wg16, wu16, wd16 = _moe(idx, gate, h3, moe_w_gate, moe_w_up, moe_w_down, i,
                                       group0=0, n_groups=MOE_FIRST_GROUPS, emit_bf16=True,
                                       ff_split=MOE_FIRST_FF_SPLIT, w_parts=1)
        f3_hi, = _moe(idx, gate, h3, wg16, wu16, wd16, 0,
                      group0=MOE_FIRST_GROUPS, n_groups=NG - MOE_FIRST_GROUPS, emit_bf16=False,
                      ff_split=MOE_REST_FF_SPLIT, w_parts=MOE_REST_W_PARTS)
        last = i == DEPTH - 1
        x_new = _moe_post((f3_lo, f3_hi), x1, mods, gpost2, split_out=last)
        if not last:
            x_parts, joint = (x_new, x_new), True

    y_sample, y_prompt = x_new
    return (y_prompt.reshape(BATCH, SEQ, D_MODEL), y_sample.reshape(DEC_BATCH, DEC_SEQ, D_MODEL),
            outs["fwd"], outs["bwd"], outs["ckv"], outs["krope"])
```

```python
import functools
import math

import jax
import jax.numpy as jnp
from jax import lax
from jax.experimental import pallas as pl
from jax.experimental.pallas import tpu as pltpu

F32 = jnp.float32
BF16 = jnp.bfloat16

D_MODEL = 1024
BATCH = 16
SEQ = 256
DEPTH = 2
DEC_BATCH = 4
DEC_SEQ = 4096
PAST_LEN = 256
GRID_W = 64
RET_HEADS = 4
RET_DK = 256
RET_DV = 512
RET_QK_WIDTH = RET_HEADS * RET_DK
RET_V_WIDTH = RET_HEADS * RET_DV
RET_CHUNK = 128
MLA_HEADS = 8
MLA_NOPE = 128
MLA_ROPE = 64
MLA_V = 128
MLA_Q_LORA = 384
MLA_KV_LORA = 256
ROPE_BASE = 10000.0
N_EXPERTS = 16
EXPERT_FF = 1024
NORM_EPS = 1e-6

GT = DEC_SEQ
NG = DEC_BATCH + 1
NT = NG * GT
N_SAMPLE = DEC_BATCH * DEC_SEQ
N_PROMPT = BATCH * SEQ
CAP = 2 * GT // N_EXPERTS

LANES = 128
SUBLANES = 8
ROW_TILES = D_MODEL // LANES
TM = 512
TILES_PER_GROUP = GT // TM
SAMPLE_TILES = N_SAMPLE // TM
VMEM_LIMIT = 60 * 1024 * 1024


def _cparams(sem):
    return pltpu.CompilerParams(dimension_semantics=sem, vmem_limit_bytes=VMEM_LIMIT)


def _rms(x):
    return x * lax.rsqrt(jnp.mean(x * x, axis=-1, keepdims=True) + NORM_EPS)


def _sigmoid(x):
    return 1.0 / (1.0 + jnp.exp(-x))


def _part_specs(width, joint, split=SAMPLE_TILES, rows=TM):
    lo = pl.BlockSpec((rows, width), lambda *g: (jnp.minimum(g[-1], split - 1), 0))
    if joint:
        hi = pl.BlockSpec((rows, width), lambda *g: (jnp.maximum(g[-1], split), 0))
    else:
        hi = pl.BlockSpec((rows, width), lambda *g: (jnp.maximum(g[-1] - split, 0), 0))
    return [lo, hi]


def _load_rows(lo_ref, hi_ref, i, split=SAMPLE_TILES):
    return jnp.where(i < split, lo_ref[...], hi_ref[...])


def _mod_spec():
    return pl.BlockSpec((1, 6, D_MODEL), lambda *g: (g[-1] // TILES_PER_GROUP, 0, 0))


def _const_spec(shape):
    return pl.BlockSpec(shape, lambda *g: (0,) * len(shape))


def _adaln_kernel(c_ref, w_ref, b_ref, o_ref):
    c = c_ref[...]
    s = (c * _sigmoid(c)).astype(BF16)
    o_ref[...] = jnp.dot(s, w_ref[0].astype(BF16), preferred_element_type=F32) + b_ref[0]


def _adaln(cond8, w, b, layer):
    tn = 1536
    return pl.pallas_call(
        _adaln_kernel,
        grid=(6 * D_MODEL // tn,),
        in_specs=[pl.BlockSpec((8, D_MODEL), lambda j: (0, 0)),
                  pl.BlockSpec((1, D_MODEL, tn), lambda j: (layer, 0, j)),
                  pl.BlockSpec((1, 1, tn), lambda j: (layer, 0, j))],
        out_specs=pl.BlockSpec((8, tn), lambda j: (0, j)),
        out_shape=jax.ShapeDtypeStruct((8, 6 * D_MODEL), F32),
        compiler_params=_cparams(("arbitrary",)),
        name="adaln",
    )(cond8, w, b)


NMM_TM = 1024


def _nmm_kernel(xlo_ref, xhi_ref, mod_ref, gain_ref, w_ref, o_ref, w_scr):
    i = pl.program_id(1)

    @pl.when(i == 0)
    def _():
        w_scr[...] = w_ref[...].astype(BF16)

    m = mod_ref[0]
    x = _load_rows(xlo_ref, xhi_ref, i, split=N_SAMPLE // NMM_TM)
    h = _rms(x) * gain_ref[...] * (1.0 + m[1:2]) + m[0:1]
    o_ref[...] = jnp.dot(h.astype(BF16), w_scr[...], preferred_element_type=F32).astype(o_ref.dtype)


def _norm_mod_matmul(x_parts, joint, mods, gain, w, tn):
    n_out = w.shape[1]
    tm = NMM_TM
    return pl.pallas_call(
        _nmm_kernel,
        grid=(n_out // tn, NT // tm),
        in_specs=[*_part_specs(D_MODEL, joint, split=N_SAMPLE // tm, rows=tm),
                  pl.BlockSpec((1, 6, D_MODEL), lambda j, i: (i // (GT // tm), 0, 0)),
                  _const_spec((1, D_MODEL)),
                  pl.BlockSpec((D_MODEL, tn), lambda j, i: (0, j))],
        out_specs=pl.BlockSpec((tm, tn), lambda j, i: (i, j)),
        out_shape=jax.ShapeDtypeStruct((NT, n_out), BF16),
        scratch_shapes=[pltpu.VMEM((D_MODEL, tn), BF16)],
        compiler_params=_cparams(("arbitrary", "arbitrary")),
        name="norm_mod_matmul",
    )(*x_parts, mods, gain, w)


RET_SCAN_UNROLL = 2


def _ret_kernel(lg_ref, q_ref, k_ref, v_ref, g_ref, *rest, seq_len, has_init, heads_per_step):
    s0f_ref = s0b_ref = None
    if has_init:
        s0f_ref, s0b_ref, *rest = rest
    yg_ref, sf_ref, sb_ref, *scratch = rest
    for hh in range(heads_per_step):
        dk = slice(hh * RET_DK, (hh + 1) * RET_DK)
        dv = slice(hh * RET_DV, (hh + 1) * RET_DV)
        _ret_head(lg_ref, pl.program_id(1) * heads_per_step + hh,
                  q_ref.at[:, dk], k_ref.at[:, dk], v_ref.at[:, dv], g_ref.at[:, dv],
                  s0f_ref.at[0, hh] if has_init else None, s0b_ref.at[0, hh] if has_init else None,
                  yg_ref.at[:, dv], sf_ref.at[0, hh], sb_ref.at[0, hh], *scratch, seq_len=seq_len)


def _ret_head(lg_ref, head, q_ref, k_ref, v_ref, g_ref, s0f_ref, s0b_ref, yg_ref, sf_ref, sb_ref,
              ypart, ycross, sf_scr, sb_scr, *, seq_len):
    C = RET_CHUNK
    nc = seq_len // C
    lgf = lg_ref[0, head]
    lgb = lg_ref[1, head]
    k_scale = RET_DK ** -0.5

    ii = lax.broadcasted_iota(jnp.int32, (C, C), 0).astype(F32)
    jj = lax.broadcasted_iota(jnp.int32, (C, C), 1).astype(F32)
    diff = ii - jj
    inner = jnp.where(diff >= 0, jnp.exp(diff * lgf), jnp.exp(-diff * lgb)) * k_scale
    pos = lax.broadcasted_iota(jnp.int32, (C, 1), 0).astype(F32)
    cross_f = jnp.exp((pos + 1.0) * lgf)
    cross_b = jnp.exp((C - pos) * lgb)
    sdec_f = jnp.exp((C - 1.0 - pos) * lgf) * k_scale
    sdec_b = jnp.exp(pos * lgb) * k_scale
    one = jnp.ones((1, 1), F32)
    cdec_f = jnp.exp(one * (C * lgf))
    cdec_b = jnp.exp(one * (C * lgb))

    if s0f_ref is not None:
        sf_scr[...] = s0f_ref[...]
        sb_scr[...] = s0b_ref[...]
    else:
        sf_scr[...] = jnp.zeros_like(sf_scr)
        sb_scr[...] = jnp.zeros_like(sb_scr)

    contract0 = (((0,), (0,)), ((), ()))
    contract1 = (((1,), (1,)), ((), ()))

    def chunk_rows(c):
        return pl.ds(pl.multiple_of(c * C, C), C)

    def bwd_chunk(c):
        rows = chunk_rows(c)
        q = q_ref[rows, :]
        s = sb_scr[...]
        ycross[rows, :] = jnp.dot(q, s.astype(BF16), preferred_element_type=F32) * cross_b
        kd = (k_ref[rows, :].astype(F32) * sdec_b).astype(BF16)
        sb_scr[...] = s * cdec_b + lax.dot_general(kd, v_ref[rows, :], contract0,
                                                   preferred_element_type=F32)

    def fwd_chunk(c):
        rows = chunk_rows(c)
        q = q_ref[rows, :]
        k = k_ref[rows, :]
        v = v_ref[rows, :]
        s = sf_scr[...]
        scores = lax.dot_general(q, k, contract1, preferred_element_type=F32) * inner
        ypart[rows, :] = (jnp.dot(scores.astype(BF16), v, preferred_element_type=F32)
                          + jnp.dot(q, s.astype(BF16), preferred_element_type=F32) * cross_f)
        kd = (k.astype(F32) * sdec_f).astype(BF16)
        sf_scr[...] = s * cdec_f + lax.dot_general(kd, v, contract0, preferred_element_type=F32)

    def scan(t, carry):
        for u in range(RET_SCAN_UNROLL):
            step = t * RET_SCAN_UNROLL + u
            fwd_chunk(step)
            bwd_chunk(nc - 1 - step)
        return carry

    lax.fori_loop(0, nc // RET_SCAN_UNROLL, scan, 0)
    sf_ref[...] = sf_scr[...]
    sb_ref[...] = sb_scr[...]

    def finish(t, carry):
        for u in range(RET_SCAN_UNROLL):
            rows = chunk_rows(t * RET_SCAN_UNROLL + u)
            y = ypart[rows, :] + ycross[rows, :]
            mu = jnp.mean(y, axis=-1, keepdims=True)
            yc = y - mu
            yn = yc * lax.rsqrt(jnp.mean(yc * yc, axis=-1, keepdims=True) + NORM_EPS)
            g = g_ref[rows, :].astype(F32)
            yg_ref[rows, :] = (g * _sigmoid(g) * yn).astype(yg_ref.dtype)
        return carry

    lax.fori_loop(0, nc // RET_SCAN_UNROLL, finish, 0)


def _retention(qkvg, lg, s0f, s0b, *, n_req, seq_len, row0, heads_per_step):
    rb0 = row0 // seq_len
    has_init = s0f is not None
    hps = heads_per_step
    n_hb = RET_HEADS // hps
    qk_spec = lambda off: pl.BlockSpec((seq_len, hps * RET_DK), lambda b, h, lg_: (rb0 + b, off + h))
    vg_spec = lambda off: pl.BlockSpec((seq_len, hps * RET_DV), lambda b, h, lg_: (rb0 + b, off + h))
    st_spec = pl.BlockSpec((1, hps, RET_DK, RET_DV), lambda b, h, lg_: (b, h, 0, 0))
    in_specs = [qk_spec(0), qk_spec(n_hb), vg_spec(n_hb), vg_spec(2 * n_hb)]
    args = [qkvg, qkvg, qkvg, qkvg]
    if has_init:
        in_specs += [st_spec, st_spec]
        args += [s0f, s0b]
    st_shape = jax.ShapeDtypeStruct((n_req, RET_HEADS, RET_DK, RET_DV), F32)
    return pl.pallas_call(
        functools.partial(_ret_kernel, seq_len=seq_len, has_init=has_init, heads_per_step=hps),
        grid_spec=pltpu.PrefetchScalarGridSpec(
            num_scalar_prefetch=1,
            grid=(n_req, n_hb),
            in_specs=in_specs,
            out_specs=[pl.BlockSpec((seq_len, hps * RET_DV), lambda b, h, lg_: (b, h)), st_spec, st_spec],
            scratch_shapes=[pltpu.VMEM((seq_len, RET_DV), F32),
                            pltpu.VMEM((seq_len, RET_DV), F32),
                            pltpu.VMEM((RET_DK, RET_DV), F32),
                            pltpu.VMEM((RET_DK, RET_DV), F32)]),
        out_shape=[jax.ShapeDtypeStruct((n_req * seq_len, RET_V_WIDTH), BF16), st_shape, st_shape],
        compiler_params=_cparams(("parallel", "arbitrary")),
        name=f"retention_{seq_len}",
    )(lg, *args)


def _split_bf16(x):
    hi = x.astype(BF16)
    return hi, (x - hi.astype(F32)).astype(BF16)


def _out_post_kernel(ylo_ref, yhi_ref, w_ref, xlo_ref, xhi_ref, mod_ref, gpost_ref, gpre_ref, wr_ref,
                     x1_ref, h3_ref, aff_ref, w_scr):
    i = pl.program_id(0)

    @pl.when(i == 0)
    def _():
        w_scr[...] = w_ref[...].astype(BF16)

    m = mod_ref[0]
    y = jnp.dot(_load_rows(ylo_ref, yhi_ref, i), w_scr[...], preferred_element_type=F32)
    x1 = _load_rows(xlo_ref, xhi_ref, i) + _rms(y) * gpost_ref[...] * m[2:3]
    x1_ref[...] = x1
    h = _rms(x1) * gpre_ref[...] * (1.0 + m[4:5]) + m[3:4]
    for c in range(ROW_TILES):
        h3_ref[pl.ds(c, TM, stride=ROW_TILES), :] = h[:, c * LANES:(c + 1) * LANES]
    contract1 = (((1,), (1,)), ((), ()))
    w_hi, w_lo = _split_bf16(wr_ref[...])
    h_hi, h_lo = _split_bf16(h)
    part = lax.dot_general(jnp.concatenate([w_hi, w_lo], axis=0), h_hi, contract1,
                           preferred_element_type=F32)
    logits = (part[:N_EXPERTS] + part[N_EXPERTS:]
              + lax.dot_general(w_hi, h_lo, contract1, preferred_element_type=F32))
    e = jnp.exp(logits - jnp.max(logits, axis=0, keepdims=True))
    aff_ref[...] = e / jnp.sum(e, axis=0, keepdims=True)


def _out_post(y_parts, w, x_parts, joint, mods, gpost, gpre, w_router_t):
    kdim = w.shape[0]
    return pl.pallas_call(
        _out_post_kernel,
        grid=(NT // TM,),
        in_specs=[*_part_specs(kdim, joint=False),
                  _const_spec((kdim, D_MODEL)),
                  *_part_specs(D_MODEL, joint), _mod_spec(),
                  _const_spec((1, D_MODEL)), _const_spec((1, D_MODEL)),
                  _const_spec((N_EXPERTS, D_MODEL))],
        out_specs=[pl.BlockSpec((TM, D_MODEL), lambda i: (i, 0)),
                   pl.BlockSpec((TM * ROW_TILES, LANES), lambda i: (i, 0)),
                   pl.BlockSpec((N_EXPERTS, TM), lambda i: (0, i))],
        out_shape=[jax.ShapeDtypeStruct((NT, D_MODEL), F32),
                   jax.ShapeDtypeStruct((NT * ROW_TILES, LANES), F32),
                   jax.ShapeDtypeStruct((N_EXPERTS, NT), F32)],
        scratch_shapes=[pltpu.VMEM((kdim, D_MODEL), BF16)],
        compiler_params=_cparams(("arbitrary",)),
        name="out_post",
    )(*y_parts, w, *x_parts, mods, gpost, gpre, w_router_t)


SCATTER_BATCH = 8
MOE_FIRST_GROUPS = 1
MOE_FIRST_FF_SPLIT = 4
MOE_REST_FF_SPLIT = 1
MOE_REST_W_PARTS = 1


def _moe_kernel(idx_ref, gate_ref, h3_ref, *rest, emit_bf16, ff_split, w_parts):
    wg_refs, wu_refs, wd_refs = (rest[j * w_parts:(j + 1) * w_parts] for j in range(3))
    f3_ref, *rest = rest[3 * w_parts:]
    if emit_bf16:
        wg16_ref, wu16_ref, wd16_ref, *rest = rest
    xg, x_a, x_b, yacc, y_a, y_b = rest
    e = pl.program_id(1)
    f = pl.program_id(2)

    def tile(r):
        return pl.ds(pl.multiple_of(r * SUBLANES, SUBLANES), SUBLANES)

    def gather(ex, x_dst):
        base = ex * CAP
        for r in range(CAP):
            xg[r * SUBLANES:(r + 1) * SUBLANES, :] = h3_ref[tile(idx_ref[0, 0, base + r]), :]
        x_dst[...] = jnp.concatenate(
            [xg[pl.ds(c, CAP, stride=ROW_TILES), :] for c in range(ROW_TILES)], axis=1).astype(BF16)

    def scatter(ex, y_src):
        base = ex * CAP
        for r0 in range(0, CAP, SCATTER_BATCH):
            dst, vals = [], []
            for r in range(r0, r0 + SCATTER_BATCH):
                t = idx_ref[0, 0, base + r]
                dst.append(tile(t))
                vals.append(f3_ref[tile(t), :]
                            + gate_ref[0, 0, base + r] * y_src[r * SUBLANES:(r + 1) * SUBLANES, :])
            for d, v in zip(dst, vals):
                f3_ref[d, :] = v

    def half_ffn(x_src):
        def slab_dot(lhs, w_refs, w16_ref):
            rows = lhs.shape[1] // w_parts
            total = None
            for p, w_ref in enumerate(w_refs):
                w = w_ref[0, 0].astype(BF16)
                if emit_bf16:
                    w16_ref[0, 0] = w
                part = jnp.dot(lhs[:, p * rows:(p + 1) * rows], w, preferred_element_type=F32)
                total = part if total is None else total + part
            return total

        x = x_src[...]
        a = slab_dot(x, wg_refs, wg16_ref if emit_bf16 else None)
        u = slab_dot(x, wu_refs, wu16_ref if emit_bf16 else None)
        act = (a * _sigmoid(a) * u).astype(BF16)
        return slab_dot(act, wd_refs, wd16_ref if emit_bf16 else None)

    first = f == 0
    last = f == ff_split - 1

    @pl.when(jnp.logical_and(e == 0, first))
    def _():
        f3_ref[...] = jnp.zeros_like(f3_ref)
        y_b[...] = jnp.zeros_like(y_b)
        gather(0, x_a)

    if ff_split == 1:
        phases = (("only", first),)
    elif ff_split == 2:
        phases = (("first", first), ("last", last))
    else:
        phases = (("first", first), ("middle", jnp.logical_not(jnp.logical_or(first, last))), ("last", last))
    for parity, (x_cur, x_nxt, y_cur, y_prv) in enumerate(((x_a, x_b, y_a, y_b), (x_b, x_a, y_b, y_a))):
        for phase, cond in phases:
            @pl.when(jnp.logical_and(e % 2 == parity, cond))
            def _():
                if phase in ("only", "first"):
                    gather(jnp.minimum(e + 1, N_EXPERTS - 1), x_nxt)
                part = half_ffn(x_cur)
                if phase == "first":
                    yacc[...] = part
                elif phase == "middle":
                    yacc[...] += part
                else:
                    y = part if phase == "only" else yacc[...] + part
                    for c in range(ROW_TILES):
                        y_cur[pl.ds(c, CAP, stride=ROW_TILES), :] = y[:, c * LANES:(c + 1) * LANES]
                    scatter(jnp.maximum(e - 1, 0), y_prv)

    @pl.when(jnp.logical_and(e == N_EXPERTS - 1, last))
    def _():
        scatter(N_EXPERTS - 1, y_b)


def _moe(idx, gate, h3, wg, wu, wd, layer, *, group0, n_groups, emit_bf16, ff_split, w_parts):
    grp_rows = GT * ROW_TILES
    ff_tile = EXPERT_FF // ff_split
    route_spec = pl.BlockSpec((1, 1, N_EXPERTS * CAP), lambda b, e, f: (group0 + b, 0, 0),
                              memory_space=pltpu.SMEM)
    assert not emit_bf16 or w_parts == 1
    w_in_spec = lambda l, p=0: pl.BlockSpec((1, 1, D_MODEL // w_parts, ff_tile),
                                            lambda b, e, f: (l, e, p, f))
    w_out_spec = lambda l, p=0: pl.BlockSpec((1, 1, ff_tile // w_parts, D_MODEL),
                                             lambda b, e, f: (l, e, f * w_parts + p, 0))
    slabs = range(w_parts)
    out_specs = [pl.BlockSpec((grp_rows, LANES), lambda b, e, f: (b, 0), pipeline_mode=pl.Buffered(1))]
    out_shape = [jax.ShapeDtypeStruct((n_groups * grp_rows, LANES), F32)]
    if emit_bf16:
        out_specs += [w_in_spec(0), w_in_spec(0), w_out_spec(0)]
        out_shape += [jax.ShapeDtypeStruct((1, N_EXPERTS, D_MODEL, EXPERT_FF), BF16),
                      jax.ShapeDtypeStruct((1, N_EXPERTS, D_MODEL, EXPERT_FF), BF16),
                      jax.ShapeDtypeStruct((1, N_EXPERTS, EXPERT_FF, D_MODEL), BF16)]
    return pl.pallas_call(
        functools.partial(_moe_kernel, emit_bf16=emit_bf16, ff_split=ff_split, w_parts=w_parts),
        grid=(n_groups, N_EXPERTS, ff_split),
        in_specs=[route_spec, route_spec,
                  pl.BlockSpec((grp_rows, LANES), lambda b, e, f: (group0 + b, 0),
                               pipeline_mode=pl.Buffered(1)),
                  *[w_in_spec(layer, p) for p in slabs], *[w_in_spec(layer, p) for p in slabs],
                  *[w_out_spec(layer, p) for p in slabs]],
        out_specs=out_specs,
        out_shape=out_shape,
        scratch_shapes=[pltpu.VMEM((CAP * ROW_TILES, LANES), F32),
                        pltpu.VMEM((CAP, D_MODEL), BF16),
                        pltpu.VMEM((CAP, D_MODEL), BF16),
                        pltpu.VMEM((CAP, D_MODEL), F32),
                        pltpu.VMEM((CAP * ROW_TILES, LANES), F32),
                        pltpu.VMEM((CAP * ROW_TILES, LANES), F32)],
        compiler_params=_cparams(("arbitrary", "arbitrary", "arbitrary")),
        name="moe_experts",
    )(idx, gate, h3, *[wg] * w_parts, *[wu] * w_parts, *[wd] * w_parts)


def _moe_post_kernel(f3lo_ref, f3hi_ref, x_ref, mod_ref, gpost_ref, *o_refs):
    i = pl.program_id(0)
    m = mod_ref[0]
    rows = lambda ref: jnp.concatenate(
        [ref[pl.ds(c, TM, stride=ROW_TILES), :] for c in range(ROW_TILES)], axis=1)
    fx = jnp.where(i < MOE_FIRST_GROUPS * TILES_PER_GROUP, rows(f3lo_ref), rows(f3hi_ref))
    out = x_ref[...] + _rms(fx) * gpost_ref[...] * m[5:6]
    if len(o_refs) == 1:
        o_refs[0][...] = out
    else:
        @pl.when(i < SAMPLE_TILES)
        def _():
            o_refs[0][...] = out

        @pl.when(i >= SAMPLE_TILES)
        def _():
            o_refs[1][...] = out


def _moe_post(f3_parts, x, mods, gpost, split_out):
    if split_out:
        out_specs = _part_specs(D_MODEL, joint=False)
        out_shape = [jax.ShapeDtypeStruct((N_SAMPLE, D_MODEL), F32),
                     jax.ShapeDtypeStruct((N_PROMPT, D_MODEL), F32)]
    else:
        out_specs = pl.BlockSpec((TM, D_MODEL), lambda i: (i, 0))
        out_shape = jax.ShapeDtypeStruct((NT, D_MODEL), F32)
    return pl.pallas_call(
        _moe_post_kernel,
        grid=(NT // TM,),
        in_specs=[*_part_specs(LANES, joint=False, split=MOE_FIRST_GROUPS * TILES_PER_GROUP,
                               rows=TM * ROW_TILES),
                  pl.BlockSpec((TM, D_MODEL), lambda i: (i, 0)),
                  _mod_spec(), _const_spec((1, D_MODEL))],
        out_specs=out_specs,
        out_shape=out_shape,
        compiler_params=_cparams(("arbitrary",)),
        name="moe_post",
    )(*f3_parts, x, mods, gpost)


ROUTE_SEG = SEQ
RANK_LO = 16
RANK_HI = CAP // RANK_LO
TOK_LO = 64
ROUTE_VALS = 5


def _route_kernel(aff_ref, idx_ref, gate_ref, sel_scr, a_scr, code_scr, g_scr, acc_scr):
    g = pl.program_id(0)
    n_seg = GT // ROUTE_SEG
    seg = lambda s: slice(s * ROUTE_SEG, (s + 1) * ROUTE_SEG)
    aff = aff_ref[...]
    tok = lax.broadcasted_iota(jnp.int32, (N_EXPERTS, GT), 1)
    ones_seg = jnp.ones((ROUTE_SEG, ROUTE_SEG), BF16)

    def row_count(mask):
        return jnp.sum(jnp.where(mask, 1.0, 0.0), axis=1, keepdims=True)

    def seg_count(mask):
        m = jnp.where(mask, 1.0, 0.0).astype(BF16)
        return jnp.concatenate(
            [jnp.dot(m[:, seg(s)], ones_seg, preferred_element_type=F32) for s in range(n_seg)], axis=1)

    def select(count, k, shape):
        as_float = lambda b: pltpu.bitcast(b, F32)

        def value_bit(i, t):
            cand = t | jnp.left_shift(jnp.int32(1), 30 - i)
            return jnp.where(count(aff >= as_float(cand)) >= k, cand, t)

        thr = lax.fori_loop(0, 31, value_bit, jnp.zeros(shape, jnp.int32))
        above = aff >= as_float(thr + 1)
        tied = jnp.logical_and(aff >= as_float(thr), jnp.logical_not(above))
        need = k - count(above)

        def index_bit(i, v):
            cand = v | jnp.left_shift(jnp.int32(1), 11 - i)
            return jnp.where(count(jnp.logical_and(tied, tok < cand)) < need, cand, v)

        last = lax.fori_loop(0, 12, index_bit, jnp.zeros(shape, jnp.int32))
        chosen = jnp.logical_or(above, jnp.logical_and(tied, tok <= last))
        sel_scr[...] = jnp.where(chosen, 1.0, 0.0)

    @pl.when(g < DEC_BATCH)
    def _():
        select(row_count, float(2 * DEC_SEQ // N_EXPERTS), (N_EXPERTS, 1))

    @pl.when(g >= DEC_BATCH)
    def _():
        select(seg_count, float(2 * SEQ // N_EXPERTS), (N_EXPERTS, GT))

    sel = sel_scr[...]
    r_i = lax.broadcasted_iota(jnp.int32, (ROUTE_SEG, ROUTE_SEG), 0)
    c_i = lax.broadcasted_iota(jnp.int32, (ROUTE_SEG, ROUTE_SEG), 1)
    upper = jnp.where(r_i <= c_i, 1.0, 0.0).astype(BF16)
    sel_b = sel.astype(BF16)
    off = jnp.zeros((N_EXPERTS, 1), F32)
    for s in range(n_seg):
        inc = jnp.dot(sel_b[:, seg(s)], upper, preferred_element_type=F32)
        rank = inc + off - sel[:, seg(s)]
        a = jnp.floor(rank * (1.0 / RANK_LO))
        a_scr[:, seg(s)] = jnp.where(sel[:, seg(s)] > 0, a, -1.0)
        code_scr[:, seg(s)] = jnp.where(sel[:, seg(s)] > 0, rank - RANK_LO * a + 1.0, 0.0).astype(BF16)
        off = off + inc[:, ROUTE_SEG - 1:ROUTE_SEG]
    g1 = aff.astype(BF16).astype(F32)
    g2 = (aff - g1).astype(BF16).astype(F32)
    g_scr[0] = g1
    g_scr[1] = g2
    g_scr[2] = (aff - g1 - g2).astype(BF16).astype(F32)

    lane = lax.broadcasted_iota(jnp.int32, (N_EXPERTS, N_EXPERTS * RANK_LO), 1)
    row = lax.broadcasted_iota(jnp.int32, (N_EXPERTS, N_EXPERTS * RANK_LO), 0)
    lo_bits = RANK_LO.bit_length() - 1
    expand = jnp.where(jnp.right_shift(lane, lo_bits) == row, 1.0, 0.0).astype(BF16)
    digit = (jnp.bitwise_and(lax.broadcasted_iota(jnp.int32, (1, N_EXPERTS * RANK_LO), 1), RANK_LO - 1)
             + 1).astype(F32)
    a_iota = lax.broadcasted_iota(jnp.int32, (RANK_HI, 1), 0).astype(F32)
    acc_scr[...] = jnp.zeros_like(acc_scr)
    contract0 = (((0,), (0,)), ((), ()))

    def chunk(c, carry):
        lanes = pl.ds(pl.multiple_of(c * ROUTE_SEG, ROUTE_SEG), ROUTE_SEG)
        a_c = a_scr[:, lanes]
        g_c = [g_scr[j, :, lanes] for j in range(3)]
        tok_c = lax.broadcasted_iota(jnp.int32, (1, ROUTE_SEG), 1) + c * ROUTE_SEG
        t_hi = jnp.right_shift(tok_c, TOK_LO.bit_length() - 1).astype(F32)
        t_lo = jnp.bitwise_and(tok_c, TOK_LO - 1).astype(F32)
        rows = []
        for e in range(N_EXPERTS):
            hit = a_c[e:e + 1, :] == a_iota
            for val in (t_hi, t_lo, g_c[0][e:e + 1, :], g_c[1][e:e + 1, :], g_c[2][e:e + 1, :]):
                rows.append(jnp.where(hit, val, 0.0))
        lhs = jnp.concatenate(rows, axis=0).astype(BF16)
        spread = lax.dot_general(code_scr[:, lanes], expand, contract0, preferred_element_type=F32)
        low_hot = jnp.where(spread == digit, 1.0, 0.0).astype(BF16)
        acc_scr[...] += jnp.dot(lhs, low_hot, preferred_element_type=F32)
        return carry

    lax.fori_loop(0, n_seg, chunk, 0)
    per_e = ROUTE_VALS * RANK_HI
    for e in range(N_EXPERTS):
        blk = acc_scr[e * per_e:(e + 1) * per_e, :][:, e * RANK_LO:(e + 1) * RANK_LO]
        part = lambda j: blk[j * RANK_HI:(j + 1) * RANK_HI]
        idx_ref[0, e] = (part(0) * TOK_LO + part(1)).astype(jnp.int32)
        gate_ref[0, e] = part(2) + part(3) + part(4)


def _route(aff_t):
    out_spec = pl.BlockSpec((1, N_EXPERTS, RANK_HI, RANK_LO), lambda g: (g, 0, 0, 0))
    idx, gate = pl.pallas_call(
        _route_kernel,
        grid=(NG,),
        in_specs=[pl.BlockSpec((N_EXPERTS, GT), lambda g: (0, g))],
        out_specs=[out_spec, out_spec],
        out_shape=[jax.ShapeDtypeStruct((NG, N_EXPERTS, RANK_HI, RANK_LO), jnp.int32),
                   jax.ShapeDtypeStruct((NG, N_EXPERTS, RANK_HI, RANK_LO), F32)],
        scratch_shapes=[pltpu.VMEM((N_EXPERTS, GT), F32),
                        pltpu.VMEM((N_EXPERTS, GT), F32),
                        pltpu.VMEM((N_EXPERTS, GT), BF16),
                        pltpu.VMEM((3, N_EXPERTS, GT), F32),
                        pltpu.VMEM((N_EXPERTS * ROUTE_VALS * RANK_HI, N_EXPERTS * RANK_LO), F32)],
        compiler_params=_cparams(("arbitrary",)),
        name="route",
    )(aff_t)
    return idx.reshape(NG, 1, N_EXPERTS * CAP), gate.reshape(NG, 1, N_EXPERTS * CAP)


HD = MLA_HEADS * LANES
Q_SCALE = math.log2(math.e) / math.sqrt(MLA_NOPE + MLA_ROPE)


def _mla_proj_kernel(xlo_ref, xhi_ref, mod_ref, gain_ref, win_ref, qn_g_ref, kvn_g_ref, wq_ref, wkv_ref,
                     cos_ref, sin_ref, q_ref, kn_ref, v_ref, krp_ref, ckv_ref, kraw_ref):
    i = pl.program_id(0)
    m = mod_ref[0]
    h = (_rms(_load_rows(xlo_ref, xhi_ref, i)) * gain_ref[...] * (1.0 + m[1:2]) + m[0:1]).astype(BF16)
    lat = jnp.dot(h, win_ref[...], preferred_element_type=F32)
    q_lat = lat[:, :MLA_Q_LORA]
    kv_lat = lat[:, MLA_Q_LORA:MLA_Q_LORA + MLA_KV_LORA]
    kr = lat[:, MLA_Q_LORA + MLA_KV_LORA:MLA_Q_LORA + MLA_KV_LORA + LANES]
    kr_rot = lat[:, MLA_Q_LORA + MLA_KV_LORA + LANES:]
    is_latent = i < SAMPLE_TILES
    cos = jnp.where(is_latent, cos_ref[...], 1.0)
    sin = jnp.where(is_latent, sin_ref[...], 0.0)

    qln = (_rms(q_lat) * qn_g_ref[...]).astype(BF16)
    q = jnp.dot(qln, wq_ref[...], preferred_element_type=F32) * Q_SCALE
    for hd in range(MLA_HEADS):
        lo = HD + hd * LANES
        q_ref[:, 2 * hd * LANES:(2 * hd + 1) * LANES] = q[:, hd * LANES:(hd + 1) * LANES].astype(BF16)
        q_ref[:, (2 * hd + 1) * LANES:(2 * hd + 2) * LANES] = (
            q[:, lo:lo + LANES] * cos + q[:, HD + lo:HD + lo + LANES] * sin).astype(BF16)

    ckv = _rms(kv_lat) * kvn_g_ref[...]
    kv = jnp.dot(ckv.astype(BF16), wkv_ref[...], preferred_element_type=F32)
    kn_ref[...] = kv[:, :HD].astype(BF16)
    v_ref[...] = kv[:, HD:].astype(BF16)
    krp_ref[...] = (kr * cos + kr_rot * sin).astype(BF16)

    @pl.when(i >= SAMPLE_TILES)
    def _():
        ckv_ref[...] = ckv
        kraw_ref[...] = kr


def _mla_proj(x_parts, joint, mods, gain, win_ext, qn_g, kvn_g, wq_ext, wkv_ext, cos128, sin128):
    row = lambda w: pl.BlockSpec((TM, w), lambda i: (i, 0))
    ctx_row = lambda w: pl.BlockSpec((TM, w), lambda i: (jnp.maximum(i - SAMPLE_TILES, 0), 0))
    tab = pl.BlockSpec((TM, LANES), lambda i: (i % TILES_PER_GROUP, 0))
    return pl.pallas_call(
        _mla_proj_kernel,
        grid=(NT // TM,),
        in_specs=[*_part_specs(D_MODEL, joint), _mod_spec(),
                  _const_spec(gain.shape), _const_spec(win_ext.shape), _const_spec(qn_g.shape),
                  _const_spec(kvn_g.shape), _const_spec(wq_ext.shape), _const_spec(wkv_ext.shape),
                  tab, tab],
        out_specs=[row(2 * HD), row(HD), row(HD), row(LANES), ctx_row(MLA_KV_LORA), ctx_row(LANES)],
        out_shape=[jax.ShapeDtypeStruct((NT, 2 * HD), BF16),
                   jax.ShapeDtypeStruct((NT, HD), BF16),
                   jax.ShapeDtypeStruct((NT, HD), BF16),
                   jax.ShapeDtypeStruct((NT, LANES), BF16),
                   jax.ShapeDtypeStruct((N_PROMPT, MLA_KV_LORA), F32),
                   jax.ShapeDtypeStruct((N_PROMPT, LANES), F32)],
        compiler_params=_cparams(("arbitrary",)),
        name="mla_proj",
    )(*x_parts, mods, gain, win_ext, qn_g, kvn_g, wq_ext, wkv_ext, cos128, sin128)


def _matmul_kernel(a_ref, w_ref, o_ref):
    o_ref[...] = jnp.dot(a_ref[...].astype(BF16), w_ref[...],
                         preferred_element_type=F32).astype(o_ref.dtype)


def _ctx_expand(ckv_ctx, wkv_ext):
    n = ckv_ctx.shape[0]
    return pl.pallas_call(
        _matmul_kernel,
        grid=(n // TM,),
        in_specs=[pl.BlockSpec((TM, MLA_KV_LORA), lambda i: (i, 0)),
                  pl.BlockSpec(wkv_ext.shape, lambda i: (0, 0))],
        out_specs=pl.BlockSpec((TM, 2 * HD), lambda i: (i, 0)),
        out_shape=jax.ShapeDtypeStruct((n, 2 * HD), BF16),
        compiler_params=_cparams(("parallel",)),
        name="ctx_expand",
    )(ckv_ctx, wkv_ext)


ATTN_CHUNK_UNROLL = 8


def _attn_kernel(q_ref, kn_ref, krp_ref, v_ref, *rest, tk, n_chunks, has_ctx, heads_per_step):
    if has_ctx:
        knc_ref, krpc_ref, vc_ref, o_ref = rest
    else:
        o_ref, = rest
    tq = q_ref.shape[0]
    contract1 = (((1,), (1,)), ((), ()))
    heads = range(heads_per_step)
    head_cols = lambda hd: slice(hd * LANES, (hd + 1) * LANES)
    qs = [q_ref[:, 2 * hd * LANES:2 * (hd + 1) * LANES] for hd in heads]

    def ones_column(rows):
        return (lax.broadcasted_iota(jnp.int32, (rows, LANES), 1) == 0).astype(BF16)

    def step(q, kn, krp, v, ones, carry):
        m, acc = carry
        kc = jnp.concatenate([kn, krp], axis=1)
        s = lax.dot_general(q, kc, contract1, preferred_element_type=F32)
        m_new = jnp.maximum(m, jnp.max(s, axis=-1, keepdims=True))
        p = jnp.exp2(s - m_new).astype(BF16)
        pv = jnp.dot(p, jnp.concatenate([v, ones], axis=1), preferred_element_type=F32)
        return m_new, jnp.exp2(m - m_new) * acc + pv

    carry = tuple((jnp.full((tq, 1), -jnp.inf, F32), jnp.zeros((tq, 2 * LANES), F32)) for _ in heads)
    if has_ctx:
        ones = ones_column(PAST_LEN)
        carry = tuple(step(qs[hd], knc_ref[0, :, head_cols(hd)], krpc_ref[0], vc_ref[0, :, head_cols(hd)],
                           ones, carry[hd]) for hd in heads)
    ones = ones_column(tk)

    def chunk(c, carry):
        rows = pl.ds(pl.multiple_of(c * tk, tk), tk)
        krp = krp_ref[rows, :]
        return tuple(step(qs[hd], kn_ref[rows, head_cols(hd)], krp, v_ref[rows, head_cols(hd)],
                          ones, carry[hd]) for hd in heads)

    unroll = math.gcd(n_chunks, ATTN_CHUNK_UNROLL)

    def body(c, carry):
        for u in range(unroll):
            carry = chunk(c * unroll + u, carry)
        return carry

    carry = lax.fori_loop(0, n_chunks // unroll, body, carry)
    for hd in heads:
        acc = carry[hd][1]
        o_ref[:, head_cols(hd)] = (acc[:, :MLA_V] / acc[:, MLA_V:MLA_V + 1]).astype(o_ref.dtype)


def _attention(q, kn, krp, v, ctx, *, n_req, seq_len, row0, tq, tk, heads_per_step):
    rb0 = row0 // seq_len
    qb0 = row0 // tq
    nq = seq_len // tq
    hw = heads_per_step * LANES
    o_spec = pl.BlockSpec((tq, hw), lambda b, h, i: (b * nq + i, h))
    k_spec = pl.BlockSpec((seq_len, hw), lambda b, h, i: (rb0 + b, h))
    in_specs = [pl.BlockSpec((tq, 2 * hw), lambda b, h, i: (qb0 + b * nq + i, h)), k_spec,
                pl.BlockSpec((seq_len, LANES), lambda b, h, i: (rb0 + b, 0)), k_spec]
    args = [q, kn, krp, v]
    if ctx is not None:
        c_spec = pl.BlockSpec((1, PAST_LEN, hw), lambda b, h, i: (b, 0, h))
        in_specs += [c_spec, pl.BlockSpec((1, PAST_LEN, LANES), lambda b, h, i: (b, 0, 0)), c_spec]
        args += list(ctx)
    return pl.pallas_call(
        functools.partial(_attn_kernel, tk=tk, n_chunks=seq_len // tk, has_ctx=ctx is not None,
                          heads_per_step=heads_per_step),
        grid=(n_req, MLA_HEADS // heads_per_step, nq),
        in_specs=in_specs,
        out_specs=o_spec,
        out_shape=jax.ShapeDtypeStruct((n_req * seq_len, HD), BF16),
        compiler_params=_cparams(("parallel", "parallel", "arbitrary")),
        name=f"attention_{seq_len}",
    )(*args)


def _rope_rot_cols(w):
    w1, w2, w3, w4 = jnp.split(w, 4, axis=-1)
    return jnp.concatenate([-w2, w1, -w4, w3], axis=-1)


def _pad_cols(w, width):
    return jnp.pad(w, ((0, 0), (0, width - w.shape[1])))


def _mla_weights(w_in, w_q_b, w_kv_b):
    w_rope = w_in[:, MLA_Q_LORA + MLA_KV_LORA:]
    win_ext = jnp.concatenate([w_in[:, :MLA_Q_LORA + MLA_KV_LORA], _pad_cols(w_rope, LANES),
                               _pad_cols(_rope_rot_cols(w_rope), LANES)], axis=1).astype(BF16)
    wq = w_q_b.reshape(MLA_Q_LORA, MLA_HEADS, MLA_NOPE + MLA_ROPE)
    wq_nope = wq[:, :, :MLA_NOPE].reshape(MLA_Q_LORA, HD)
    wq_rope = wq[:, :, MLA_NOPE:]
    pad = ((0, 0), (0, 0), (0, LANES - MLA_ROPE))
    wq_rope_p = jnp.pad(wq_rope, pad).reshape(MLA_Q_LORA, HD)
    wq_rot_p = jnp.pad(_rope_rot_cols(wq_rope), pad).reshape(MLA_Q_LORA, HD)
    wq_ext = jnp.concatenate([wq_nope, wq_rope_p, wq_rot_p], axis=1).astype(BF16)
    wkv = w_kv_b.reshape(MLA_KV_LORA, MLA_HEADS, MLA_NOPE + MLA_V)
    wkv_ext = jnp.concatenate([wkv[:, :, :MLA_NOPE].reshape(MLA_KV_LORA, HD),
                               wkv[:, :, MLA_NOPE:].reshape(MLA_KV_LORA, HD)], axis=1).astype(BF16)
    return win_ext, wq_ext, wkv_ext


def _rope_tables():
    rows = DEC_SEQ // GRID_W
    row = jnp.repeat(jnp.arange(rows, dtype=F32), GRID_W)
    col = jnp.tile(jnp.arange(GRID_W, dtype=F32), rows)
    half = MLA_ROPE // 2
    inv = 1.0 / (ROPE_BASE ** (jnp.arange(0, half, 2, dtype=F32) / half))
    ar = row[:, None] * inv
    ac = col[:, None] * inv
    ang = jnp.concatenate([ar, ar, ac, ac] * 2, axis=-1)
    return jnp.cos(ang), jnp.sin(ang)


def kernel(x_prompt, x_sample, state_ret_fwd, state_ret_bwd, cache_mla_ckv, cache_mla_krope, c, c_ctx,
           ada_w, ada_b, norm_pre, norm_post, ret_w_in, ret_decay_fwd, ret_decay_bwd, ret_w_out,
           mla_w_in, mla_q_norm, mla_kv_norm, mla_w_q_b, mla_w_kv_b, mla_w_out,
           moe_w_router, moe_w_gate, moe_w_up, moe_w_down):
    x_parts = (x_sample.reshape(N_SAMPLE, D_MODEL), x_prompt.reshape(N_PROMPT, D_MODEL))
    joint = False
    cond8 = jnp.concatenate([c, c_ctx[None, :], jnp.zeros((8 - NG, D_MODEL), F32)], axis=0)
    cos128, sin128 = _rope_tables()
    outs = {}

    for i in range(DEPTH):
        mods = _adaln(cond8, ada_w, ada_b.reshape(DEPTH, 1, 6 * D_MODEL), i)[:NG].reshape(NG, 6, D_MODEL)
        gpre1 = norm_pre[i, 0][None, :]
        gpre2 = norm_pre[i, 1][None, :]
        gpost1 = norm_post[i, 0][None, :]
        gpost2 = norm_post[i, 1][None, :]
        if i % 2 == 0:
            r = i // 2
            qkvg = _norm_mod_matmul(x_parts, joint, mods, gpre1, ret_w_in[r], 1536)
            lg = jnp.stack([jax.nn.log_sigmoid(ret_decay_fwd[r].astype(F32)),
                            jax.nn.log_sigmoid(ret_decay_bwd[r].astype(F32))])
            yg_s, _, _ = _retention(qkvg, lg, state_ret_fwd[:, r], state_ret_bwd[:, r],
                                    n_req=DEC_BATCH, seq_len=DEC_SEQ, row0=0, heads_per_step=1)
            yg_p, s_f, s_b = _retention(qkvg, lg, None, None, n_req=BATCH, seq_len=SEQ, row0=N_SAMPLE,
                                        heads_per_step=RET_HEADS)
            outs["fwd"], outs["bwd"] = s_f[:, None], s_b[:, None]
            mix, w_out = (yg_s, yg_p), ret_w_out[r]
        else:
            mi = i // 2
            win_ext, wq_ext, wkv_ext = _mla_weights(mla_w_in[mi], mla_w_q_b[mi], mla_w_kv_b[mi])
            q, kn, v, krp, ckv, kraw = _mla_proj(
                x_parts, joint, mods, gpre1, win_ext, mla_q_norm[mi][None, :], mla_kv_norm[mi][None, :],
                wq_ext, wkv_ext, cos128, sin128)
            outs["ckv"] = ckv.reshape(BATCH, 1, SEQ, MLA_KV_LORA)
            outs["krope"] = kraw[:, :MLA_ROPE].reshape(BATCH, 1, SEQ, MLA_ROPE)
            kvc = _ctx_expand(cache_mla_ckv[:, mi].reshape(DEC_BATCH * PAST_LEN, MLA_KV_LORA), wkv_ext)
            kvc = kvc.reshape(DEC_BATCH, PAST_LEN, 2 * HD)
            krpc = jnp.pad(cache_mla_krope[:, mi], ((0, 0), (0, 0), (0, LANES - MLA_ROPE))).astype(BF16)
            o_s = _attention(q, kn, krp, v, (kvc[:, :, :HD], krpc, kvc[:, :, HD:]),
                             n_req=DEC_BATCH, seq_len=DEC_SEQ, row0=0, tq=512, tk=512, heads_per_step=4)
            o_p = _attention(q, kn, krp, v, None, n_req=BATCH, seq_len=SEQ, row0=N_SAMPLE, tq=SEQ, tk=SEQ,
                             heads_per_step=MLA_HEADS)
            mix, w_out = (o_s, o_p), mla_w_out[mi]
        x1, h3, aff_t = _out_post(mix, w_out, x_parts, joint, mods, gpost1, gpre2, moe_w_router[i].T)
        idx, gate = _route(aff_t)
        f3_lo, wg16, wu16, wd16 = _moe(idx, gate, h3, moe_w_gate, moe_w_up, moe_w_down, i,
                                       group0=0, n_groups=MOE_FIRST_GROUPS, emit_bf16=True,
                                       ff_split=MOE_FIRST_FF_SPLIT, w_parts=1)
        f3_hi, = _moe(idx, gate, h3, wg16, wu16, wd16, 0,
                      group0=MOE_FIRST_GROUPS, n_groups=NG - MOE_FIRST_GROUPS, emit_bf16=False,
                      ff_split=MOE_REST_FF_SPLIT, w_parts=MOE_REST_W_PARTS)
        last = i == DEPTH - 1
        x_new = _moe_post((f3_lo, f3_hi), x1, mods, gpost2, split_out=last)
        if not last:
            x_parts, joint = (x_new, x_new), True

    y_sample, y_prompt = x_new
    return (y_prompt.reshape(BATCH, SEQ, D_MODEL), y_sample.reshape(DEC_BATCH, DEC_SEQ, D_MODEL),
            outs["fwd"], outs["bwd"], outs["ckv"], outs["krope"])
```

```python
import functools
import math

import jax
import jax.numpy as jnp
from jax import lax
from jax.experimental import pallas as pl
from jax.experimental.pallas import tpu as pltpu

F32 = jnp.float32
BF16 = jnp.bfloat16

D_MODEL = 1024
BATCH = 16
SEQ = 256
DEPTH = 2
DEC_BATCH = 4
DEC_SEQ = 4096
PAST_LEN = 256
GRID_W = 64
RET_HEADS = 4
RET_DK = 256
RET_DV = 512
RET_QK_WIDTH = RET_HEADS * RET_DK
RET_V_WIDTH = RET_HEADS * RET_DV
RET_CHUNK = 128
MLA_HEADS = 8
MLA_NOPE = 128
MLA_ROPE = 64
MLA_V = 128
MLA_Q_LORA = 384
MLA_KV_LORA = 256
ROPE_BASE = 10000.0
N_EXPERTS = 16
EXPERT_FF = 1024
NORM_EPS = 1e-6

GT = DEC_SEQ
NG = DEC_BATCH + 1
NT = NG * GT
N_SAMPLE = DEC_BATCH * DEC_SEQ
N_PROMPT = BATCH * SEQ
CAP = 2 * GT // N_EXPERTS

LANES = 128
SUBLANES = 8
ROW_TILES = D_MODEL // LANES
TM = 512
TILES_PER_GROUP = GT // TM
SAMPLE_TILES = N_SAMPLE // TM
VMEM_LIMIT = 60 * 1024 * 1024


def _cparams(sem):
    return pltpu.CompilerParams(dimension_semantics=sem, vmem_limit_bytes=VMEM_LIMIT)


def _rms(x):
    return x * lax.rsqrt(jnp.mean(x * x, axis=-1, keepdims=True) + NORM_EPS)


def _sigmoid(x):
    return 1.0 / (1.0 + jnp.exp(-x))


def _part_specs(width, joint, split=SAMPLE_TILES, rows=TM):
    lo = pl.BlockSpec((rows, width), lambda *g: (jnp.minimum(g[-1], split - 1), 0))
    if joint:
        hi = pl.BlockSpec((rows, width), lambda *g: (jnp.maximum(g[-1], split), 0))
    else:
        hi = pl.BlockSpec((rows, width), lambda *g: (jnp.maximum(g[-1] - split, 0), 0))
    return [lo, hi]


def _load_rows(lo_ref, hi_ref, i, split=SAMPLE_TILES):
    return jnp.where(i < split, lo_ref[...], hi_ref[...])


def _mod_spec():
    return pl.BlockSpec((1, 6, D_MODEL), lambda *g: (g[-1] // TILES_PER_GROUP, 0, 0))


def _const_spec(shape):
    return pl.BlockSpec(shape, lambda *g: (0,) * len(shape))


def _adaln_kernel(c_ref, w_ref, b_ref, o_ref):
    c = c_ref[...]
    s = (c * _sigmoid(c)).astype(BF16)
    o_ref[...] = jnp.dot(s, w_ref[0].astype(BF16), preferred_element_type=F32) + b_ref[0]


def _adaln(cond8, w, b, layer):
    tn = 1536
    return pl.pallas_call(
        _adaln_kernel,
        grid=(6 * D_MODEL // tn,),
        in_specs=[pl.BlockSpec((8, D_MODEL), lambda j: (0, 0)),
                  pl.BlockSpec((1, D_MODEL, tn), lambda j: (layer, 0, j)),
                  pl.BlockSpec((1, 1, tn), lambda j: (layer, 0, j))],
        out_specs=pl.BlockSpec((8, tn), lambda j: (0, j)),
        out_shape=jax.ShapeDtypeStruct((8, 6 * D_MODEL), F32),
        compiler_params=_cparams(("arbitrary",)),
        name="adaln",
    )(cond8, w, b)


NMM_TM = 1024


def _nmm_kernel(xlo_ref, xhi_ref, mod_ref, gain_ref, w_ref, o_ref, w_scr):
    i = pl.program_id(1)

    @pl.when(i == 0)
    def _():
        w_scr[...] = w_ref[...].astype(BF16)

    m = mod_ref[0]
    x = _load_rows(xlo_ref, xhi_ref, i, split=N_SAMPLE // NMM_TM)
    h = _rms(x) * gain_ref[...] * (1.0 + m[1:2]) + m[0:1]
    o_ref[...] = jnp.dot(h.astype(BF16), w_scr[...], preferred_element_type=F32).astype(o_ref.dtype)


def _norm_mod_matmul(x_parts, joint, mods, gain, w, tn):
    n_out = w.shape[1]
    tm = NMM_TM
    return pl.pallas_call(
        _nmm_kernel,
        grid=(n_out // tn, NT // tm),
        in_specs=[*_part_specs(D_MODEL, joint, split=N_SAMPLE // tm, rows=tm),
                  pl.BlockSpec((1, 6, D_MODEL), lambda j, i: (i // (GT // tm), 0, 0)),
                  _const_spec((1, D_MODEL)),
                  pl.BlockSpec((D_MODEL, tn), lambda j, i: (0, j))],
        out_specs=pl.BlockSpec((tm, tn), lambda j, i: (i, j)),
        out_shape=jax.ShapeDtypeStruct((NT, n_out), BF16),
        scratch_shapes=[pltpu.VMEM((D_MODEL, tn), BF16)],
        compiler_params=_cparams(("arbitrary", "arbitrary")),
        name="norm_mod_matmul",
    )(*x_parts, mods, gain, w)


RET_SCAN_UNROLL = 4


def _ret_kernel(lg_ref, q_ref, k_ref, v_ref, g_ref, *rest, seq_len, has_init, heads_per_step):
    s0f_ref = s0b_ref = None
    if has_init:
        s0f_ref, s0b_ref, *rest = rest
    yg_ref, sf_ref, sb_ref, *scratch = rest
    for hh in range(heads_per_step):
        dk = slice(hh * RET_DK, (hh + 1) * RET_DK)
        dv = slice(hh * RET_DV, (hh + 1) * RET_DV)
        _ret_head(lg_ref, pl.program_id(1) * heads_per_step + hh,
                  q_ref.at[:, dk], k_ref.at[:, dk], v_ref.at[:, dv], g_ref.at[:, dv],
                  s0f_ref.at[0, hh] if has_init else None, s0b_ref.at[0, hh] if has_init else None,
                  yg_ref.at[:, dv], sf_ref.at[0, hh], sb_ref.at[0, hh], *scratch, seq_len=seq_len)


def _ret_head(lg_ref, head, q_ref, k_ref, v_ref, g_ref, s0f_ref, s0b_ref, yg_ref, sf_ref, sb_ref,
              ypart, ycross, sf_scr, sb_scr, *, seq_len):
    C = RET_CHUNK
    nc = seq_len // C
    lgf = lg_ref[0, head]
    lgb = lg_ref[1, head]
    k_scale = RET_DK ** -0.5

    ii = lax.broadcasted_iota(jnp.int32, (C, C), 0).astype(F32)
    jj = lax.broadcasted_iota(jnp.int32, (C, C), 1).astype(F32)
    diff = ii - jj
    inner = jnp.where(diff >= 0, jnp.exp(diff * lgf), jnp.exp(-diff * lgb)) * k_scale
    pos = lax.broadcasted_iota(jnp.int32, (C, 1), 0).astype(F32)
    cross_f = jnp.exp((pos + 1.0) * lgf)
    cross_b = jnp.exp((C - pos) * lgb)
    sdec_f = jnp.exp((C - 1.0 - pos) * lgf) * k_scale
    sdec_b = jnp.exp(pos * lgb) * k_scale
    one = jnp.ones((1, 1), F32)
    cdec_f = jnp.exp(one * (C * lgf))
    cdec_b = jnp.exp(one * (C * lgb))

    if s0f_ref is not None:
        sf_scr[...] = s0f_ref[...]
        sb_scr[...] = s0b_ref[...]
    else:
        sf_scr[...] = jnp.zeros_like(sf_scr)
        sb_scr[...] = jnp.zeros_like(sb_scr)

    contract0 = (((0,), (0,)), ((), ()))
    contract1 = (((1,), (1,)), ((), ()))

    def chunk_rows(c):
        return pl.ds(pl.multiple_of(c * C, C), C)

    def bwd_chunk(c):
        rows = chunk_rows(c)
        q = q_ref[rows, :]
        s = sb_scr[...]
        ycross[rows, :] = jnp.dot(q, s.astype(BF16), preferred_element_type=F32) * cross_b
        kd = (k_ref[rows, :].astype(F32) * sdec_b).astype(BF16)
        sb_scr[...] = s * cdec_b + lax.dot_general(kd, v_ref[rows, :], contract0,
                                                   preferred_element_type=F32)

    def fwd_chunk(c):
        rows = chunk_rows(c)
        q = q_ref[rows, :]
        k = k_ref[rows, :]
        v = v_ref[rows, :]
        s = sf_scr[...]
        scores = lax.dot_general(q, k, contract1, preferred_element_type=F32) * inner
        ypart[rows, :] = (jnp.dot(scores.astype(BF16), v, preferred_element_type=F32)
                          + jnp.dot(q, s.astype(BF16), preferred_element_type=F32) * cross_f)
        kd = (k.astype(F32) * sdec_f).astype(BF16)
        sf_scr[...] = s * cdec_f + lax.dot_general(kd, v, contract0, preferred_element_type=F32)

    unroll = math.gcd(nc, RET_SCAN_UNROLL)

    def scan(t, carry):
        for u in range(unroll):
            step = t * unroll + u
            fwd_chunk(step)
            bwd_chunk(nc - 1 - step)
        return carry

    lax.fori_loop(0, nc // unroll, scan, 0)
    sf_ref[...] = sf_scr[...]
    sb_ref[...] = sb_scr[...]

    def finish(t, carry):
        for u in range(unroll):
            rows = chunk_rows(t * unroll + u)
            y = ypart[rows, :] + ycross[rows, :]
            mu = jnp.mean(y, axis=-1, keepdims=True)
            yc = y - mu
            yn = yc * lax.rsqrt(jnp.mean(yc * yc, axis=-1, keepdims=True) + NORM_EPS)
            g = g_ref[rows, :].astype(F32)
            yg_ref[rows, :] = (g * _sigmoid(g) * yn).astype(yg_ref.dtype)
        return carry

    lax.fori_loop(0, nc // unroll, finish, 0)


def _retention(qkvg, lg, s0f, s0b, *, n_req, seq_len, row0, heads_per_step):
    rb0 = row0 // seq_len
    has_init = s0f is not None
    hps = heads_per_step
    n_hb = RET_HEADS // hps
    qk_spec = lambda off: pl.BlockSpec((seq_len, hps * RET_DK), lambda b, h, lg_: (rb0 + b, off + h))
    vg_spec = lambda off: pl.BlockSpec((seq_len, hps * RET_DV), lambda b, h, lg_: (rb0 + b, off + h))
    st_spec = pl.BlockSpec((1, hps, RET_DK, RET_DV), lambda b, h, lg_: (b, h, 0, 0))
    in_specs = [qk_spec(0), qk_spec(n_hb), vg_spec(n_hb), vg_spec(2 * n_hb)]
    args = [qkvg, qkvg, qkvg, qkvg]
    if has_init:
        in_specs += [st_spec, st_spec]
        args += [s0f, s0b]
    st_shape = jax.ShapeDtypeStruct((n_req, RET_HEADS, RET_DK, RET_DV), F32)
    return pl.pallas_call(
        functools.partial(_ret_kernel, seq_len=seq_len, has_init=has_init, heads_per_step=hps),
        grid_spec=pltpu.PrefetchScalarGridSpec(
            num_scalar_prefetch=1,
            grid=(n_req, n_hb),
            in_specs=in_specs,
            out_specs=[pl.BlockSpec((seq_len, hps * RET_DV), lambda b, h, lg_: (b, h)), st_spec, st_spec],
            scratch_shapes=[pltpu.VMEM((seq_len, RET_DV), F32),
                            pltpu.VMEM((seq_len, RET_DV), F32),
                            pltpu.VMEM((RET_DK, RET_DV), F32),
                            pltpu.VMEM((RET_DK, RET_DV), F32)]),
        out_shape=[jax.ShapeDtypeStruct((n_req * seq_len, RET_V_WIDTH), BF16), st_shape, st_shape],
        compiler_params=_cparams(("parallel", "arbitrary")),
        name=f"retention_{seq_len}",
    )(lg, *args)


def _split_bf16(x):
    hi = x.astype(BF16)
    return hi, (x - hi.astype(F32)).astype(BF16)


def _out_post_kernel(ylo_ref, yhi_ref, w_ref, xlo_ref, xhi_ref, mod_ref, gpost_ref, gpre_ref, wr_ref,
                     x1_ref, h3_ref, aff_ref, w_scr):
    i = pl.program_id(0)

    @pl.when(i == 0)
    def _():
        w_scr[...] = w_ref[...].astype(BF16)

    m = mod_ref[0]
    y = jnp.dot(_load_rows(ylo_ref, yhi_ref, i), w_scr[...], preferred_element_type=F32)
    x1 = _load_rows(xlo_ref, xhi_ref, i) + _rms(y) * gpost_ref[...] * m[2:3]
    x1_ref[...] = x1
    h = _rms(x1) * gpre_ref[...] * (1.0 + m[4:5]) + m[3:4]
    for c in range(ROW_TILES):
        h3_ref[pl.ds(c, TM, stride=ROW_TILES), :] = h[:, c * LANES:(c + 1) * LANES]
    contract1 = (((1,), (1,)), ((), ()))
    w_hi, w_lo = _split_bf16(wr_ref[...])
    h_hi, h_lo = _split_bf16(h)
    part = lax.dot_general(jnp.concatenate([w_hi, w_lo], axis=0), h_hi, contract1,
                           preferred_element_type=F32)
    logits = (part[:N_EXPERTS] + part[N_EXPERTS:]
              + lax.dot_general(w_hi, h_lo, contract1, preferred_element_type=F32))
    e = jnp.exp(logits - jnp.max(logits, axis=0, keepdims=True))
    aff_ref[...] = e / jnp.sum(e, axis=0, keepdims=True)


def _out_post(y_parts, w, x_parts, joint, mods, gpost, gpre, w_router_t):
    kdim = w.shape[0]
    return pl.pallas_call(
        _out_post_kernel,
        grid=(NT // TM,),
        in_specs=[*_part_specs(kdim, joint=False),
                  _const_spec((kdim, D_MODEL)),
                  *_part_specs(D_MODEL, joint), _mod_spec(),
                  _const_spec((1, D_MODEL)), _const_spec((1, D_MODEL)),
                  _const_spec((N_EXPERTS, D_MODEL))],
        out_specs=[pl.BlockSpec((TM, D_MODEL), lambda i: (i, 0)),
                   pl.BlockSpec((TM * ROW_TILES, LANES), lambda i: (i, 0)),
                   pl.BlockSpec((N_EXPERTS, TM), lambda i: (0, i))],
        out_shape=[jax.ShapeDtypeStruct((NT, D_MODEL), F32),
                   jax.ShapeDtypeStruct((NT * ROW_TILES, LANES), F32),
                   jax.ShapeDtypeStruct((N_EXPERTS, NT), F32)],
        scratch_shapes=[pltpu.VMEM((kdim, D_MODEL), BF16)],
        compiler_params=_cparams(("arbitrary",)),
        name="out_post",
    )(*y_parts, w, *x_parts, mods, gpost, gpre, w_router_t)


SCATTER_BATCH = 8
MOE_FIRST_GROUPS = 1
MOE_FIRST_FF_SPLIT = 4
MOE_REST_FF_SPLIT = 1
MOE_REST_W_PARTS = 1


def _moe_kernel(idx_ref, gate_ref, h3_ref, *rest, emit_bf16, ff_split, w_parts):
    wg_refs, wu_refs, wd_refs = (rest[j * w_parts:(j + 1) * w_parts] for j in range(3))
    f3_ref, *rest = rest[3 * w_parts:]
    if emit_bf16:
        wg16_ref, wu16_ref, wd16_ref, *rest = rest
    xg, x_a, x_b, yacc, y_a, y_b = rest
    e = pl.program_id(1)
    f = pl.program_id(2)

    def tile(r):
        return pl.ds(pl.multiple_of(r * SUBLANES, SUBLANES), SUBLANES)

    def gather(ex, x_dst):
        base = ex * CAP
        for r in range(CAP):
            xg[r * SUBLANES:(r + 1) * SUBLANES, :] = h3_ref[tile(idx_ref[0, 0, base + r]), :]
        x_dst[...] = jnp.concatenate(
            [xg[pl.ds(c, CAP, stride=ROW_TILES), :] for c in range(ROW_TILES)], axis=1).astype(BF16)

    def scatter(ex, y_src):
        base = ex * CAP
        for r0 in range(0, CAP, SCATTER_BATCH):
            dst, vals = [], []
            for r in range(r0, r0 + SCATTER_BATCH):
                t = idx_ref[0, 0, base + r]
                dst.append(tile(t))
                vals.append(f3_ref[tile(t), :]
                            + gate_ref[0, 0, base + r] * y_src[r * SUBLANES:(r + 1) * SUBLANES, :])
            for d, v in zip(dst, vals):
                f3_ref[d, :] = v

    def half_ffn(x_src):
        def slab_dot(lhs, w_refs, w16_ref):
            rows = lhs.shape[1] // w_parts
            total = None
            for p, w_ref in enumerate(w_refs):
                w = w_ref[0, 0].astype(BF16)
                if emit_bf16:
                    w16_ref[0, 0] = w
                part = jnp.dot(lhs[:, p * rows:(p + 1) * rows], w, preferred_element_type=F32)
                total = part if total is None else total + part
            return total

        x = x_src[...]
        a = slab_dot(x, wg_refs, wg16_ref if emit_bf16 else None)
        u = slab_dot(x, wu_refs, wu16_ref if emit_bf16 else None)
        act = (a * _sigmoid(a) * u).astype(BF16)
        return slab_dot(act, wd_refs, wd16_ref if emit_bf16 else None)

    first = f == 0
    last = f == ff_split - 1

    @pl.when(jnp.logical_and(e == 0, first))
    def _():
        f3_ref[...] = jnp.zeros_like(f3_ref)
        y_b[...] = jnp.zeros_like(y_b)
        gather(0, x_a)

    if ff_split == 1:
        phases = (("only", first),)
    elif ff_split == 2:
        phases = (("first", first), ("last", last))
    else:
        phases = (("first", first), ("middle", jnp.logical_not(jnp.logical_or(first, last))), ("last", last))
    for parity, (x_cur, x_nxt, y_cur, y_prv) in enumerate(((x_a, x_b, y_a, y_b), (x_b, x_a, y_b, y_a))):
        for phase, cond in phases:
            @pl.when(jnp.logical_and(e % 2 == parity, cond))
            def _():
                if phase in ("only", "first"):
                    gather(jnp.minimum(e + 1, N_EXPERTS - 1), x_nxt)
                part = half_ffn(x_cur)
                if phase == "first":
                    yacc[...] = part
                elif phase == "middle":
                    yacc[...] += part
                else:
                    y = part if phase == "only" else yacc[...] + part
                    for c in range(ROW_TILES):
                        y_cur[pl.ds(c, CAP, stride=ROW_TILES), :] = y[:, c * LANES:(c + 1) * LANES]
                    scatter(jnp.maximum(e - 1, 0), y_prv)

    @pl.when(jnp.logical_and(e == N_EXPERTS - 1, last))
    def _():
        scatter(N_EXPERTS - 1, y_b)


def _moe(idx, gate, h3, wg, wu, wd, layer, *, group0, n_groups, emit_bf16, ff_split, w_parts):
    grp_rows = GT * ROW_TILES
    ff_tile = EXPERT_FF // ff_split
    route_spec = pl.BlockSpec((1, 1, N_EXPERTS * CAP), lambda b, e, f: (group0 + b, 0, 0),
                              memory_space=pltpu.SMEM)
    assert not emit_bf16 or w_parts == 1
    w_in_spec = lambda l, p=0: pl.BlockSpec((1, 1, D_MODEL // w_parts, ff_tile),
                                            lambda b, e, f: (l, e, p, f))
    w_out_spec = lambda l, p=0: pl.BlockSpec((1, 1, ff_tile // w_parts, D_MODEL),
                                             lambda b, e, f: (l, e, f * w_parts + p, 0))
    slabs = range(w_parts)
    out_specs = [pl.BlockSpec((grp_rows, LANES), lambda b, e, f: (b, 0), pipeline_mode=pl.Buffered(1))]
    out_shape = [jax.ShapeDtypeStruct((n_groups * grp_rows, LANES), F32)]
    if emit_bf16:
        out_specs += [w_in_spec(0), w_in_spec(0), w_out_spec(0)]
        out_shape += [jax.ShapeDtypeStruct((1, N_EXPERTS, D_MODEL, EXPERT_FF), BF16),
                      jax.ShapeDtypeStruct((1, N_EXPERTS, D_MODEL, EXPERT_FF), BF16),
                      jax.ShapeDtypeStruct((1, N_EXPERTS, EXPERT_FF, D_MODEL), BF16)]
    return pl.pallas_call(
        functools.partial(_moe_kernel, emit_bf16=emit_bf16, ff_split=ff_split, w_parts=w_parts),
        grid=(n_groups, N_EXPERTS, ff_split),
        in_specs=[route_spec, route_spec,
                  pl.BlockSpec((grp_rows, LANES), lambda b, e, f: (group0 + b, 0),
                               pipeline_mode=pl.Buffered(1)),
                  *[w_in_spec(layer, p) for p in slabs], *[w_in_spec(layer, p) for p in slabs],
                  *[w_out_spec(layer, p) for p in slabs]],
        out_specs=out_specs,
        out_shape=out_shape,
        scratch_shapes=[pltpu.VMEM((CAP * ROW_TILES, LANES), F32),
                        pltpu.VMEM((CAP, D_MODEL), BF16),
                        pltpu.VMEM((CAP, D_MODEL), BF16),
                        pltpu.VMEM((CAP, D_MODEL), F32),
                        pltpu.VMEM((CAP * ROW_TILES, LANES), F32),
                        pltpu.VMEM((CAP * ROW_TILES, LANES), F32)],
        compiler_params=_cparams(("arbitrary", "arbitrary", "arbitrary")),
        name="moe_experts",
    )(idx, gate, h3, *[wg] * w_parts, *[wu] * w_parts, *[wd] * w_parts)


def _moe_post_kernel(f3lo_ref, f3hi_ref, x_ref, mod_ref, gpost_ref, *o_refs):
    i = pl.program_id(0)
    m = mod_ref[0]
    rows = lambda ref: jnp.concatenate(
        [ref[pl.ds(c, TM, stride=ROW_TILES), :] for c in range(ROW_TILES)], axis=1)
    fx = jnp.where(i < MOE_FIRST_GROUPS * TILES_PER_GROUP, rows(f3lo_ref), rows(f3hi_ref))
    out = x_ref[...] + _rms(fx) * gpost_ref[...] * m[5:6]
    if len(o_refs) == 1:
        o_refs[0][...] = out
    else:
        @pl.when(i < SAMPLE_TILES)
        def _():
            o_refs[0][...] = out

        @pl.when(i >= SAMPLE_TILES)
        def _():
            o_refs[1][...] = out


def _moe_post(f3_parts, x, mods, gpost, split_out):
    if split_out:
        out_specs = _part_specs(D_MODEL, joint=False)
        out_shape = [jax.ShapeDtypeStruct((N_SAMPLE, D_MODEL), F32),
                     jax.ShapeDtypeStruct((N_PROMPT, D_MODEL), F32)]
    else:
        out_specs = pl.BlockSpec((TM, D_MODEL), lambda i: (i, 0))
        out_shape = jax.ShapeDtypeStruct((NT, D_MODEL), F32)
    return pl.pallas_call(
        _moe_post_kernel,
        grid=(NT // TM,),
        in_specs=[*_part_specs(LANES, joint=False, split=MOE_FIRST_GROUPS * TILES_PER_GROUP,
                               rows=TM * ROW_TILES),
                  pl.BlockSpec((TM, D_MODEL), lambda i: (i, 0)),
                  _mod_spec(), _const_spec((1, D_MODEL))],
        out_specs=out_specs,
        out_shape=out_shape,
        compiler_params=_cparams(("arbitrary",)),
        name="moe_post",
    )(*f3_parts, x, mods, gpost)


ROUTE_SEG = SEQ
RANK_LO = 16
RANK_HI = CAP // RANK_LO
TOK_LO = 64
ROUTE_VALS = 5


def _route_kernel(aff_ref, idx_ref, gate_ref, sel_scr, a_scr, code_scr, g_scr, acc_scr):
    g = pl.program_id(0)
    n_seg = GT // ROUTE_SEG
    seg = lambda s: slice(s * ROUTE_SEG, (s + 1) * ROUTE_SEG)
    aff = aff_ref[...]
    tok = lax.broadcasted_iota(jnp.int32, (N_EXPERTS, GT), 1)
    ones_seg = jnp.ones((ROUTE_SEG, ROUTE_SEG), BF16)

    def row_count(mask):
        return jnp.sum(jnp.where(mask, 1.0, 0.0), axis=1, keepdims=True)

    def seg_count(mask):
        m = jnp.where(mask, 1.0, 0.0).astype(BF16)
        return jnp.concatenate(
            [jnp.dot(m[:, seg(s)], ones_seg, preferred_element_type=F32) for s in range(n_seg)], axis=1)

    def select(count, k, shape):
        as_float = lambda b: pltpu.bitcast(b, F32)

        def value_bit(i, t):
            cand = t | jnp.left_shift(jnp.int32(1), 30 - i)
            return jnp.where(count(aff >= as_float(cand)) >= k, cand, t)

        thr = lax.fori_loop(0, 31, value_bit, jnp.zeros(shape, jnp.int32))
        above = aff >= as_float(thr + 1)
        tied = jnp.logical_and(aff >= as_float(thr), jnp.logical_not(above))
        need = k - count(above)

        def index_bit(i, v):
            cand = v | jnp.left_shift(jnp.int32(1), 11 - i)
            return jnp.where(count(jnp.logical_and(tied, tok < cand)) < need, cand, v)

        last = lax.fori_loop(0, 12, index_bit, jnp.zeros(shape, jnp.int32))
        chosen = jnp.logical_or(above, jnp.logical_and(tied, tok <= last))
        sel_scr[...] = jnp.where(chosen, 1.0, 0.0)

    @pl.when(g < DEC_BATCH)
    def _():
        select(row_count, float(2 * DEC_SEQ // N_EXPERTS), (N_EXPERTS, 1))

    @pl.when(g >= DEC_BATCH)
    def _():
        select(seg_count, float(2 * SEQ // N_EXPERTS), (N_EXPERTS, GT))

    sel = sel_scr[...]
    r_i = lax.broadcasted_iota(jnp.int32, (ROUTE_SEG, ROUTE_SEG), 0)
    c_i = lax.broadcasted_iota(jnp.int32, (ROUTE_SEG, ROUTE_SEG), 1)
    upper = jnp.where(r_i <= c_i, 1.0, 0.0).astype(BF16)
    sel_b = sel.astype(BF16)
    off = jnp.zeros((N_EXPERTS, 1), F32)
    for s in range(n_seg):
        inc = jnp.dot(sel_b[:, seg(s)], upper, preferred_element_type=F32)
        rank = inc + off - sel[:, seg(s)]
        a = jnp.floor(rank * (1.0 / RANK_LO))
        a_scr[:, seg(s)] = jnp.where(sel[:, seg(s)] > 0, a, -1.0)
        code_scr[:, seg(s)] = jnp.where(sel[:, seg(s)] > 0, rank - RANK_LO * a + 1.0, 0.0).astype(BF16)
        off = off + inc[:, ROUTE_SEG - 1:ROUTE_SEG]
    g1 = aff.astype(BF16).astype(F32)
    g2 = (aff - g1).astype(BF16).astype(F32)
    g_scr[0] = g1
    g_scr[1] = g2
    g_scr[2] = (aff - g1 - g2).astype(BF16).astype(F32)

    lane = lax.broadcasted_iota(jnp.int32, (N_EXPERTS, N_EXPERTS * RANK_LO), 1)
    row = lax.broadcasted_iota(jnp.int32, (N_EXPERTS, N_EXPERTS * RANK_LO), 0)
    lo_bits = RANK_LO.bit_length() - 1
    expand = jnp.where(jnp.right_shift(lane, lo_bits) == row, 1.0, 0.0).astype(BF16)
    digit = (jnp.bitwise_and(lax.broadcasted_iota(jnp.int32, (1, N_EXPERTS * RANK_LO), 1), RANK_LO - 1)
             + 1).astype(F32)
    a_iota = lax.broadcasted_iota(jnp.int32, (RANK_HI, 1), 0).astype(F32)
    acc_scr[...] = jnp.zeros_like(acc_scr)
    contract0 = (((0,), (0,)), ((), ()))

    def chunk(c):
        lanes = pl.ds(pl.multiple_of(c * ROUTE_SEG, ROUTE_SEG), ROUTE_SEG)
        a_c = a_scr[:, lanes]
        g_c = [g_scr[j, :, lanes] for j in range(3)]
        tok_c = lax.broadcasted_iota(jnp.int32, (1, ROUTE_SEG), 1) + c * ROUTE_SEG
        t_hi = jnp.right_shift(tok_c, TOK_LO.bit_length() - 1).astype(F32)
        t_lo = jnp.bitwise_and(tok_c, TOK_LO - 1).astype(F32)
        rows = []
        for e in range(N_EXPERTS):
            hit = a_c[e:e + 1, :] == a_iota
            for val in (t_hi, t_lo, g_c[0][e:e + 1, :], g_c[1][e:e + 1, :], g_c[2][e:e + 1, :]):
                rows.append(jnp.where(hit, val, 0.0))
        lhs = jnp.concatenate(rows, axis=0).astype(BF16)
        spread = lax.dot_general(code_scr[:, lanes], expand, contract0, preferred_element_type=F32)
        low_hot = jnp.where(spread == digit, 1.0, 0.0).astype(BF16)
        return jnp.dot(lhs, low_hot, preferred_element_type=F32)

    def chunk_pair(c, carry):
        acc_scr[...] += chunk(2 * c) + chunk(2 * c + 1)
        return carry

    lax.fori_loop(0, n_seg // 2, chunk_pair, 0)
    per_e = ROUTE_VALS * RANK_HI
    for e in range(N_EXPERTS):
        blk = acc_scr[e * per_e:(e + 1) * per_e, :][:, e * RANK_LO:(e + 1) * RANK_LO]
        part = lambda j: blk[j * RANK_HI:(j + 1) * RANK_HI]
        idx_ref[0, e] = (part(0) * TOK_LO + part(1)).astype(jnp.int32)
        gate_ref[0, e] = part(2) + part(3) + part(4)


def _route(aff_t):
    out_spec = pl.BlockSpec((1, N_EXPERTS, RANK_HI, RANK_LO), lambda g: (g, 0, 0, 0))
    idx, gate = pl.pallas_call(
        _route_kernel,
        grid=(NG,),
        in_specs=[pl.BlockSpec((N_EXPERTS, GT), lambda g: (0, g))],
        out_specs=[out_spec, out_spec],
        out_shape=[jax.ShapeDtypeStruct((NG, N_EXPERTS, RANK_HI, RANK_LO), jnp.int32),
                   jax.ShapeDtypeStruct((NG, N_EXPERTS, RANK_HI, RANK_LO), F32)],
        scratch_shapes=[pltpu.VMEM((N_EXPERTS, GT), F32),
                        pltpu.VMEM((N_EXPERTS, GT), F32),
                        pltpu.VMEM((N_EXPERTS, GT), BF16),
                        pltpu.VMEM((3, N_EXPERTS, GT), F32),
                        pltpu.VMEM((N_EXPERTS * ROUTE_VALS * RANK_HI, N_EXPERTS * RANK_LO), F32)],
        compiler_params=_cparams(("arbitrary",)),
        name="route",
    )(aff_t)
    return idx.reshape(NG, 1, N_EXPERTS * CAP), gate.reshape(NG, 1, N_EXPERTS * CAP)


HD = MLA_HEADS * LANES
Q_SCALE = math.log2(math.e) / math.sqrt(MLA_NOPE + MLA_ROPE)


def _mla_proj_kernel(xlo_ref, xhi_ref, mod_ref, gain_ref, win_ref, qn_g_ref, kvn_g_ref, wq_ref, wkv_ref,
                     cos_ref, sin_ref, q_ref, kn_ref, v_ref, krp_ref, ckv_ref, kraw_ref):
    i = pl.program_id(0)
    m = mod_ref[0]
    h = (_rms(_load_rows(xlo_ref, xhi_ref, i)) * gain_ref[...] * (1.0 + m[1:2]) + m[0:1]).astype(BF16)
    lat = jnp.dot(h, win_ref[...], preferred_element_type=F32)
    q_lat = lat[:, :MLA_Q_LORA]
    kv_lat = lat[:, MLA_Q_LORA:MLA_Q_LORA + MLA_KV_LORA]
    kr = lat[:, MLA_Q_LORA + MLA_KV_LORA:MLA_Q_LORA + MLA_KV_LORA + LANES]
    kr_rot = lat[:, MLA_Q_LORA + MLA_KV_LORA + LANES:]
    is_latent = i < SAMPLE_TILES
    cos = jnp.where(is_latent, cos_ref[...], 1.0)
    sin = jnp.where(is_latent, sin_ref[...], 0.0)

    qln = (_rms(q_lat) * qn_g_ref[...]).astype(BF16)
    q = jnp.dot(qln, wq_ref[...], preferred_element_type=F32) * Q_SCALE
    for hd in range(MLA_HEADS):
        lo = HD + hd * LANES
        q_ref[:, 2 * hd * LANES:(2 * hd + 1) * LANES] = q[:, hd * LANES:(hd + 1) * LANES].astype(BF16)
        q_ref[:, (2 * hd + 1) * LANES:(2 * hd + 2) * LANES] = (
            q[:, lo:lo + LANES] * cos + q[:, HD + lo:HD + lo + LANES] * sin).astype(BF16)

    ckv = _rms(kv_lat) * kvn_g_ref[...]
    kv = jnp.dot(ckv.astype(BF16), wkv_ref[...], preferred_element_type=F32)
    kn_ref[...] = kv[:, :HD].astype(BF16)
    v_ref[...] = kv[:, HD:].astype(BF16)
    krp_ref[...] = (kr * cos + kr_rot * sin).astype(BF16)

    @pl.when(i >= SAMPLE_TILES)
    def _():
        ckv_ref[...] = ckv
        kraw_ref[...] = kr


def _mla_proj(x_parts, joint, mods, gain, win_ext, qn_g, kvn_g, wq_ext, wkv_ext, cos128, sin128):
    row = lambda w: pl.BlockSpec((TM, w), lambda i: (i, 0))
    ctx_row = lambda w: pl.BlockSpec((TM, w), lambda i: (jnp.maximum(i - SAMPLE_TILES, 0), 0))
    tab = pl.BlockSpec((TM, LANES), lambda i: (i % TILES_PER_GROUP, 0))
    return pl.pallas_call(
        _mla_proj_kernel,
        grid=(NT // TM,),
        in_specs=[*_part_specs(D_MODEL, joint), _mod_spec(),
                  _const_spec(gain.shape), _const_spec(win_ext.shape), _const_spec(qn_g.shape),
                  _const_spec(kvn_g.shape), _const_spec(wq_ext.shape), _const_spec(wkv_ext.shape),
                  tab, tab],
        out_specs=[row(2 * HD), row(HD), row(HD), row(LANES), ctx_row(MLA_KV_LORA), ctx_row(LANES)],
        out_shape=[jax.ShapeDtypeStruct((NT, 2 * HD), BF16),
                   jax.ShapeDtypeStruct((NT, HD), BF16),
                   jax.ShapeDtypeStruct((NT, HD), BF16),
                   jax.ShapeDtypeStruct((NT, LANES), BF16),
                   jax.ShapeDtypeStruct((N_PROMPT, MLA_KV_LORA), F32),
                   jax.ShapeDtypeStruct((N_PROMPT, LANES), F32)],
        compiler_params=_cparams(("arbitrary",)),
        name="mla_proj",
    )(*x_parts, mods, gain, win_ext, qn_g, kvn_g, wq_ext, wkv_ext, cos128, sin128)


def _matmul_kernel(a_ref, w_ref, o_ref):
    o_ref[...] = jnp.dot(a_ref[...].astype(BF16), w_ref[...],
                         preferred_element_type=F32).astype(o_ref.dtype)


def _ctx_expand(ckv_ctx, wkv_ext):
    n = ckv_ctx.shape[0]
    return pl.pallas_call(
        _matmul_kernel,
        grid=(n // TM,),
        in_specs=[pl.BlockSpec((TM, MLA_KV_LORA), lambda i: (i, 0)),
                  pl.BlockSpec(wkv_ext.shape, lambda i: (0, 0))],
        out_specs=pl.BlockSpec((TM, 2 * HD), lambda i: (i, 0)),
        out_shape=jax.ShapeDtypeStruct((n, 2 * HD), BF16),
        compiler_params=_cparams(("parallel",)),
        name="ctx_expand",
    )(ckv_ctx, wkv_ext)


ATTN_CHUNK_UNROLL = 8


def _attn_kernel(q_ref, kn_ref, krp_ref, v_ref, *rest, tk, n_chunks, has_ctx, heads_per_step):
    if has_ctx:
        knc_ref, krpc_ref, vc_ref, o_ref = rest
    else:
        o_ref, = rest
    tq = q_ref.shape[0]
    contract1 = (((1,), (1,)), ((), ()))
    heads = range(heads_per_step)
    head_cols = lambda hd: slice(hd * LANES, (hd + 1) * LANES)
    qs = [q_ref[:, 2 * hd * LANES:2 * (hd + 1) * LANES] for hd in heads]

    def ones_column(rows):
        return (lax.broadcasted_iota(jnp.int32, (rows, LANES), 1) == 0).astype(BF16)

    def step(q, kn, krp, v, ones, carry):
        m, acc = carry
        kc = jnp.concatenate([kn, krp], axis=1)
        s = lax.dot_general(q, kc, contract1, preferred_element_type=F32)
        m_new = jnp.maximum(m, jnp.max(s, axis=-1, keepdims=True))
        p = jnp.exp2(s - m_new).astype(BF16)
        pv = jnp.dot(p, jnp.concatenate([v, ones], axis=1), preferred_element_type=F32)
        return m_new, jnp.exp2(m - m_new) * acc + pv

    carry = tuple((jnp.full((tq, 1), -jnp.inf, F32), jnp.zeros((tq, 2 * LANES), F32)) for _ in heads)
    if has_ctx:
        ones = ones_column(PAST_LEN)
        carry = tuple(step(qs[hd], knc_ref[0, :, head_cols(hd)], krpc_ref[0], vc_ref[0, :, head_cols(hd)],
                           ones, carry[hd]) for hd in heads)
    ones = ones_column(tk)

    def chunk(c, carry):
        rows = pl.ds(pl.multiple_of(c * tk, tk), tk)
        krp = krp_ref[rows, :]
        return tuple(step(qs[hd], kn_ref[rows, head_cols(hd)], krp, v_ref[rows, head_cols(hd)],
                          ones, carry[hd]) for hd in heads)

    unroll = math.gcd(n_chunks, ATTN_CHUNK_UNROLL)

    def body(c, carry):
        for u in range(unroll):
            carry = chunk(c * unroll + u, carry)
        return carry

    carry = lax.fori_loop(0, n_chunks // unroll, body, carry)
    for hd in heads:
        acc = carry[hd][1]
        o_ref[:, head_cols(hd)] = (acc[:, :MLA_V] / acc[:, MLA_V:MLA_V + 1]).astype(o_ref.dtype)


def _attention(q, kn, krp, v, ctx, *, n_req, seq_len, row0, tq, tk, heads_per_step):
    rb0 = row0 // seq_len
    qb0 = row0 // tq
    nq = seq_len // tq
    hw = heads_per_step * LANES
    o_spec = pl.BlockSpec((tq, hw), lambda b, h, i: (b * nq + i, h))
    k_spec = pl.BlockSpec((seq_len, hw), lambda b, h, i: (rb0 + b, h))
    in_specs = [pl.BlockSpec((tq, 2 * hw), lambda b, h, i: (qb0 + b * nq + i, h)), k_spec,
                pl.BlockSpec((seq_len, LANES), lambda b, h, i: (rb0 + b, 0)), k_spec]
    args = [q, kn, krp, v]
    if ctx is not None:
        c_spec = pl.BlockSpec((1, PAST_LEN, hw), lambda b, h, i: (b, 0, h))
        in_specs += [c_spec, pl.BlockSpec((1, PAST_LEN, LANES), lambda b, h, i: (b, 0, 0)), c_spec]
        args += list(ctx)
    return pl.pallas_call(
        functools.partial(_attn_kernel, tk=tk, n_chunks=seq_len // tk, has_ctx=ctx is not None,
                          heads_per_step=heads_per_step),
        grid=(n_req, MLA_HEADS // heads_per_step, nq),
        in_specs=in_specs,
        out_specs=o_spec,
        out_shape=jax.ShapeDtypeStruct((n_req * seq_len, HD), BF16),
        compiler_params=_cparams(("parallel", "parallel", "arbitrary")),
        name=f"attention_{seq_len}",
    )(*args)


def _rope_rot_cols(w):
    w1, w2, w3, w4 = jnp.split(w, 4, axis=-1)
    return jnp.concatenate([-w2, w1, -w4, w3], axis=-1)


def _pad_cols(w, width):
    return jnp.pad(w, ((0, 0), (0, width - w.shape[1])))


def _mla_weights(w_in, w_q_b, w_kv_b):
    w_rope = w_in[:, MLA_Q_LORA + MLA_KV_LORA:]
    win_ext = jnp.concatenate([w_in[:, :MLA_Q_LORA + MLA_KV_LORA], _pad_cols(w_rope, LANES),
                               _pad_cols(_rope_rot_cols(w_rope), LANES)], axis=1).astype(BF16)
    wq = w_q_b.reshape(MLA_Q_LORA, MLA_HEADS, MLA_NOPE + MLA_ROPE)
    wq_nope = wq[:, :, :MLA_NOPE].reshape(MLA_Q_LORA, HD)
    wq_rope = wq[:, :, MLA_NOPE:]
    pad = ((0, 0), (0, 0), (0, LANES - MLA_ROPE))
    wq_rope_p = jnp.pad(wq_rope, pad).reshape(MLA_Q_LORA, HD)
    wq_rot_p = jnp.pad(_rope_rot_cols(wq_rope), pad).reshape(MLA_Q_LORA, HD)
    wq_ext = jnp.concatenate([wq_nope, wq_rope_p, wq_rot_p], axis=1).astype(BF16)
    wkv = w_kv_b.reshape(MLA_KV_LORA, MLA_HEADS, MLA_NOPE + MLA_V)
    wkv_ext = jnp.concatenate([wkv[:, :, :MLA_NOPE].reshape(MLA_KV_LORA, HD),
                               wkv[:, :, MLA_NOPE:].reshape(MLA_KV_LORA, HD)], axis=1).astype(BF16)
    return win_ext, wq_ext, wkv_ext


def _rope_tables():
    rows = DEC_SEQ // GRID_W
    row = jnp.repeat(jnp.arange(rows, dtype=F32), GRID_W)
    col = jnp.tile(jnp.arange(GRID_W, dtype=F32), rows)
    half = MLA_ROPE // 2
    inv = 1.0 / (ROPE_BASE ** (jnp.arange(0, half, 2, dtype=F32) / half))
    ar = row[:, None] * inv
    ac = col[:, None] * inv
    ang = jnp.concatenate([ar, ar, ac, ac] * 2, axis=-1)
    return jnp.cos(ang), jnp.sin(ang)


def kernel(x_prompt, x_sample, state_ret_fwd, state_ret_bwd, cache_mla_ckv, cache_mla_krope, c, c_ctx,
           ada_w, ada_b, norm_pre, norm_post, ret_w_in, ret_decay_fwd, ret_decay_bwd, ret_w_out,
           mla_w_in, mla_q_norm, mla_kv_norm, mla_w_q_b, mla_w_kv_b, mla_w_out,
           moe_w_router, moe_w_gate, moe_w_up, moe_w_down):
    x_parts = (x_sample.reshape(N_SAMPLE, D_MODEL), x_prompt.reshape(N_PROMPT, D_MODEL))
    joint = False
    cond8 = jnp.concatenate([c, c_ctx[None, :], jnp.zeros((8 - NG, D_MODEL), F32)], axis=0)
    cos128, sin128 = _rope_tables()
    outs = {}

    for i in range(DEPTH):
        mods = _adaln(cond8, ada_w, ada_b.reshape(DEPTH, 1, 6 * D_MODEL), i)[:NG].reshape(NG, 6, D_MODEL)
        gpre1 = norm_pre[i, 0][None, :]
        gpre2 = norm_pre[i, 1][None, :]
        gpost1 = norm_post[i, 0][None, :]
        gpost2 = norm_post[i, 1][None, :]
        if i % 2 == 0:
            r = i // 2
            qkvg = _norm_mod_matmul(x_parts, joint, mods, gpre1, ret_w_in[r], 1536)
            lg = jnp.stack([jax.nn.log_sigmoid(ret_decay_fwd[r].astype(F32)),
                            jax.nn.log_sigmoid(ret_decay_bwd[r].astype(F32))])
            yg_s, _, _ = _retention(qkvg, lg, state_ret_fwd[:, r], state_ret_bwd[:, r],
                                    n_req=DEC_BATCH, seq_len=DEC_SEQ, row0=0, heads_per_step=1)
            yg_p, s_f, s_b = _retention(qkvg, lg, None, None, n_req=BATCH, seq_len=SEQ, row0=N_SAMPLE,
                                        heads_per_step=RET_HEADS)
            outs["fwd"], outs["bwd"] = s_f[:, None], s_b[:, None]
            mix, w_out = (yg_s, yg_p), ret_w_out[r]
        else:
            mi = i // 2
            win_ext, wq_ext, wkv_ext = _mla_weights(mla_w_in[mi], mla_w_q_b[mi], mla_w_kv_b[mi])
            q, kn, v, krp, ckv, kraw = _mla_proj(
                x_parts, joint, mods, gpre1, win_ext, mla_q_norm[mi][None, :], mla_kv_norm[mi][None, :],
                wq_ext, wkv_ext, cos128, sin128)
            outs["ckv"] = ckv.reshape(BATCH, 1, SEQ, MLA_KV_LORA)
            outs["krope"] = kraw[:, :MLA_ROPE].reshape(BATCH, 1, SEQ, MLA_ROPE)
            kvc = _ctx_expand(cache_mla_ckv[:, mi].reshape(DEC_BATCH * PAST_LEN, MLA_KV_LORA), wkv_ext)
            kvc = kvc.reshape(DEC_BATCH, PAST_LEN, 2 * HD)
            krpc = jnp.pad(cache_mla_krope[:, mi], ((0, 0), (0, 0), (0, LANES - MLA_ROPE))).astype(BF16)
            o_s = _attention(q, kn, krp, v, (kvc[:, :, :HD], krpc, kvc[:, :, HD:]),
                             n_req=DEC_BATCH, seq_len=DEC_SEQ, row0=0, tq=512, tk=512, heads_per_step=4)
            o_p = _attention(q, kn, krp, v, None, n_req=BATCH, seq_len=SEQ, row0=N_SAMPLE, tq=SEQ, tk=SEQ,
                             heads_per_step=MLA_HEADS)
            mix, w_out = (o_s, o_p), mla_w_out[mi]
        x1, h3, aff_t = _out_post(mix, w_out, x_parts, joint, mods, gpost1, gpre2, moe_w_router[i].T)
        idx, gate = _route(aff_t)
        f3_lo, wg16, wu16, wd16 = _moe(idx, gate, h3, moe_w_gate, moe_w_up, moe_w_down, i,
                                       group0=0, n_groups=MOE_FIRST_GROUPS, emit_bf16=True,
                                       ff_split=MOE_FIRST_FF_SPLIT, w_parts=1)
        f3_hi, = _moe(idx, gate, h3, wg16, wu16, wd16, 0,
                      group0=MOE_FIRST_GROUPS, n_groups=NG - MOE_FIRST_GROUPS, emit_bf16=False,
                      ff_split=MOE_REST_FF_SPLIT, w_parts=MOE_REST_W_PARTS)
        last = i == DEPTH - 1
        x_new = _moe_post((f3_lo, f3_hi), x1, mods, gpost2, split_out=last)
        if not last:
            x_parts, joint = (x_new, x_new), True

    y_sample, y_prompt = x_new
    return (y_prompt.reshape(BATCH, SEQ, D_MODEL), y_sample.reshape(DEC_BATCH, DEC_SEQ, D_MODEL),
            outs["fwd"], outs["bwd"], outs["ckv"], outs["krope"])
```

```python
import functools
import math

import jax
import jax.numpy as jnp
from jax import lax
from jax.experimental import pallas as pl
from jax.experimental.pallas import tpu as pltpu

F32 = jnp.float32
BF16 = jnp.bfloat16

D_MODEL = 1024
BATCH = 16
SEQ = 256
DEPTH = 2
DEC_BATCH = 4
DEC_SEQ = 4096
PAST_LEN = 256
GRID_W = 64
RET_HEADS = 4
RET_DK = 256
RET_DV = 512
RET_QK_WIDTH = RET_HEADS * RET_DK
RET_V_WIDTH = RET_HEADS * RET_DV
RET_CHUNK = 128
MLA_HEADS = 8
MLA_NOPE = 128
MLA_ROPE = 64
MLA_V = 128
MLA_Q_LORA = 384
MLA_KV_LORA = 256
ROPE_BASE = 10000.0
N_EXPERTS = 16
EXPERT_FF = 1024
NORM_EPS = 1e-6

GT = DEC_SEQ
NG = DEC_BATCH + 1
NT = NG * GT
N_SAMPLE = DEC_BATCH * DEC_SEQ
N_PROMPT = BATCH * SEQ
CAP = 2 * GT // N_EXPERTS

LANES = 128
SUBLANES = 8
ROW_TILES = D_MODEL // LANES
TM = 512
TILES_PER_GROUP = GT // TM
SAMPLE_TILES = N_SAMPLE // TM
VMEM_LIMIT = 60 * 1024 * 1024


def _cparams(sem):
    return pltpu.CompilerParams(dimension_semantics=sem, vmem_limit_bytes=VMEM_LIMIT)


def _rms(x):
    return x * lax.rsqrt(jnp.mean(x * x, axis=-1, keepdims=True) + NORM_EPS)


def _sigmoid(x):
    return 1.0 / (1.0 + jnp.exp(-x))


def _part_specs(width, joint, split=SAMPLE_TILES, rows=TM):
    lo = pl.BlockSpec((rows, width), lambda *g: (jnp.minimum(g[-1], split - 1), 0))
    if joint:
        hi = pl.BlockSpec((rows, width), lambda *g: (jnp.maximum(g[-1], split), 0))
    else:
        hi = pl.BlockSpec((rows, width), lambda *g: (jnp.maximum(g[-1] - split, 0), 0))
    return [lo, hi]


def _load_rows(lo_ref, hi_ref, i, split=SAMPLE_TILES):
    return jnp.where(i < split, lo_ref[...], hi_ref[...])


def _mod_spec():
    return pl.BlockSpec((1, 6, D_MODEL), lambda *g: (g[-1] // TILES_PER_GROUP, 0, 0))


def _const_spec(shape):
    return pl.BlockSpec(shape, lambda *g: (0,) * len(shape))


def _adaln_kernel(c_ref, w_ref, b_ref, o_ref):
    c = c_ref[...]
    s = (c * _sigmoid(c)).astype(BF16)
    o_ref[...] = jnp.dot(s, w_ref[0].astype(BF16), preferred_element_type=F32) + b_ref[0]


def _adaln(cond8, w, b, layer):
    tn = 1536
    return pl.pallas_call(
        _adaln_kernel,
        grid=(6 * D_MODEL // tn,),
        in_specs=[pl.BlockSpec((8, D_MODEL), lambda j: (0, 0)),
                  pl.BlockSpec((1, D_MODEL, tn), lambda j: (layer, 0, j)),
                  pl.BlockSpec((1, 1, tn), lambda j: (layer, 0, j))],
        out_specs=pl.BlockSpec((8, tn), lambda j: (0, j)),
        out_shape=jax.ShapeDtypeStruct((8, 6 * D_MODEL), F32),
        compiler_params=_cparams(("arbitrary",)),
        name="adaln",
    )(cond8, w, b)


NMM_TM = 1024


def _nmm_kernel(xlo_ref, xhi_ref, mod_ref, gain_ref, w_ref, o_ref, w_scr):
    i = pl.program_id(1)

    @pl.when(i == 0)
    def _():
        w_scr[...] = w_ref[...].astype(BF16)

    m = mod_ref[0]
    x = _load_rows(xlo_ref, xhi_ref, i, split=N_SAMPLE // NMM_TM)
    h = _rms(x) * gain_ref[...] * (1.0 + m[1:2]) + m[0:1]
    o_ref[...] = jnp.dot(h.astype(BF16), w_scr[...], preferred_element_type=F32).astype(o_ref.dtype)


def _norm_mod_matmul(x_parts, joint, mods, gain, w, tn):
    n_out = w.shape[1]
    tm = NMM_TM
    return pl.pallas_call(
        _nmm_kernel,
        grid=(n_out // tn, NT // tm),
        in_specs=[*_part_specs(D_MODEL, joint, split=N_SAMPLE // tm, rows=tm),
                  pl.BlockSpec((1, 6, D_MODEL), lambda j, i: (i // (GT // tm), 0, 0)),
                  _const_spec((1, D_MODEL)),
                  pl.BlockSpec((D_MODEL, tn), lambda j, i: (0, j))],
        out_specs=pl.BlockSpec((tm, tn), lambda j, i: (i, j)),
        out_shape=jax.ShapeDtypeStruct((NT, n_out), BF16),
        scratch_shapes=[pltpu.VMEM((D_MODEL, tn), BF16)],
        compiler_params=_cparams(("arbitrary", "arbitrary")),
        name="norm_mod_matmul",
    )(*x_parts, mods, gain, w)


RET_SCAN_UNROLL = 8


def _ret_kernel(lg_ref, q_ref, k_ref, v_ref, g_ref, *rest, seq_len, has_init, heads_per_step):
    s0f_ref = s0b_ref = None
    if has_init:
        s0f_ref, s0b_ref, *rest = rest
    yg_ref, sf_ref, sb_ref, *scratch = rest
    for hh in range(heads_per_step):
        dk = slice(hh * RET_DK, (hh + 1) * RET_DK)
        dv = slice(hh * RET_DV, (hh + 1) * RET_DV)
        _ret_head(lg_ref, pl.program_id(1) * heads_per_step + hh,
                  q_ref.at[:, dk], k_ref.at[:, dk], v_ref.at[:, dv], g_ref.at[:, dv],
                  s0f_ref.at[0, hh] if has_init else None, s0b_ref.at[0, hh] if has_init else None,
                  yg_ref.at[:, dv], sf_ref.at[0, hh], sb_ref.at[0, hh], *scratch, seq_len=seq_len)


def _ret_head(lg_ref, head, q_ref, k_ref, v_ref, g_ref, s0f_ref, s0b_ref, yg_ref, sf_ref, sb_ref,
              ypart, ycross, sf_scr, sb_scr, *, seq_len):
    C = RET_CHUNK
    nc = seq_len // C
    lgf = lg_ref[0, head]
    lgb = lg_ref[1, head]
    k_scale = RET_DK ** -0.5

    ii = lax.broadcasted_iota(jnp.int32, (C, C), 0).astype(F32)
    jj = lax.broadcasted_iota(jnp.int32, (C, C), 1).astype(F32)
    diff = ii - jj
    inner = jnp.where(diff >= 0, jnp.exp(diff * lgf), jnp.exp(-diff * lgb)) * k_scale
    pos = lax.broadcasted_iota(jnp.int32, (C, 1), 0).astype(F32)
    cross_f = jnp.exp((pos + 1.0) * lgf)
    cross_b = jnp.exp((C - pos) * lgb)
    sdec_f = jnp.exp((C - 1.0 - pos) * lgf) * k_scale
    sdec_b = jnp.exp(pos * lgb) * k_scale
    one = jnp.ones((1, 1), F32)
    cdec_f = jnp.exp(one * (C * lgf))
    cdec_b = jnp.exp(one * (C * lgb))

    if s0f_ref is not None:
        sf_scr[...] = s0f_ref[...]
        sb_scr[...] = s0b_ref[...]
    else:
        sf_scr[...] = jnp.zeros_like(sf_scr)
        sb_scr[...] = jnp.zeros_like(sb_scr)

    contract0 = (((0,), (0,)), ((), ()))
    contract1 = (((1,), (1,)), ((), ()))

    def chunk_rows(c):
        return pl.ds(pl.multiple_of(c * C, C), C)

    def bwd_chunk(c):
        rows = chunk_rows(c)
        q = q_ref[rows, :]
        s = sb_scr[...]
        ycross[rows, :] = jnp.dot(q, s.astype(BF16), preferred_element_type=F32) * cross_b
        kd = (k_ref[rows, :].astype(F32) * sdec_b).astype(BF16)
        sb_scr[...] = s * cdec_b + lax.dot_general(kd, v_ref[rows, :], contract0,
                                                   preferred_element_type=F32)

    def fwd_chunk(c):
        rows = chunk_rows(c)
        q = q_ref[rows, :]
        k = k_ref[rows, :]
        v = v_ref[rows, :]
        s = sf_scr[...]
        scores = lax.dot_general(q, k, contract1, preferred_element_type=F32) * inner
        ypart[rows, :] = (jnp.dot(scores.astype(BF16), v, preferred_element_type=F32)
                          + jnp.dot(q, s.astype(BF16), preferred_element_type=F32) * cross_f)
        kd = (k.astype(F32) * sdec_f).astype(BF16)
        sf_scr[...] = s * cdec_f + lax.dot_general(kd, v, contract0, preferred_element_type=F32)

    unroll = math.gcd(nc, RET_SCAN_UNROLL)

    def scan(t, carry):
        for u in range(unroll):
            step = t * unroll + u
            fwd_chunk(step)
            bwd_chunk(nc - 1 - step)
        return carry

    lax.fori_loop(0, nc // unroll, scan, 0)
    sf_ref[...] = sf_scr[...]
    sb_ref[...] = sb_scr[...]

    def finish(t, carry):
        for u in range(unroll):
            rows = chunk_rows(t * unroll + u)
            y = ypart[rows, :] + ycross[rows, :]
            mu = jnp.mean(y, axis=-1, keepdims=True)
            yc = y - mu
            yn = yc * lax.rsqrt(jnp.mean(yc * yc, axis=-1, keepdims=True) + NORM_EPS)
            g = g_ref[rows, :].astype(F32)
            yg_ref[rows, :] = (g * _sigmoid(g) * yn).astype(yg_ref.dtype)
        return carry

    lax.fori_loop(0, nc // unroll, finish, 0)


def _retention(qkvg, lg, s0f, s0b, *, n_req, seq_len, row0, heads_per_step):
    rb0 = row0 // seq_len
    has_init = s0f is not None
    hps = heads_per_step
    n_hb = RET_HEADS // hps
    qk_spec = lambda off: pl.BlockSpec((seq_len, hps * RET_DK), lambda b, h, lg_: (rb0 + b, off + h))
    vg_spec = lambda off: pl.BlockSpec((seq_len, hps * RET_DV), lambda b, h, lg_: (rb0 + b, off + h))
    st_spec = pl.BlockSpec((1, hps, RET_DK, RET_DV), lambda b, h, lg_: (b, h, 0, 0))
    in_specs = [qk_spec(0), qk_spec(n_hb), vg_spec(n_hb), vg_spec(2 * n_hb)]
    args = [qkvg, qkvg, qkvg, qkvg]
    if has_init:
        in_specs += [st_spec, st_spec]
        args += [s0f, s0b]
    st_shape = jax.ShapeDtypeStruct((n_req, RET_HEADS, RET_DK, RET_DV), F32)
    return pl.pallas_call(
        functools.partial(_ret_kernel, seq_len=seq_len, has_init=has_init, heads_per_step=hps),
        grid_spec=pltpu.PrefetchScalarGridSpec(
            num_scalar_prefetch=1,
            grid=(n_req, n_hb),
            in_specs=in_specs,
            out_specs=[pl.BlockSpec((seq_len, hps * RET_DV), lambda b, h, lg_: (b, h)), st_spec, st_spec],
            scratch_shapes=[pltpu.VMEM((seq_len, RET_DV), F32),
                            pltpu.VMEM((seq_len, RET_DV), F32),
                            pltpu.VMEM((RET_DK, RET_DV), F32),
                            pltpu.VMEM((RET_DK, RET_DV), F32)]),
        out_shape=[jax.ShapeDtypeStruct((n_req * seq_len, RET_V_WIDTH), BF16), st_shape, st_shape],
        compiler_params=_cparams(("parallel", "arbitrary")),
        name=f"retention_{seq_len}",
    )(lg, *args)


def _split_bf16(x):
    hi = x.astype(BF16)
    return hi, (x - hi.astype(F32)).astype(BF16)


def _out_post_kernel(ylo_ref, yhi_ref, w_ref, xlo_ref, xhi_ref, mod_ref, gpost_ref, gpre_ref, wr_ref,
                     x1_ref, h3_ref, aff_ref, w_scr):
    i = pl.program_id(0)

    @pl.when(i == 0)
    def _():
        w_scr[...] = w_ref[...].astype(BF16)

    m = mod_ref[0]
    y = jnp.dot(_load_rows(ylo_ref, yhi_ref, i), w_scr[...], preferred_element_type=F32)
    x1 = _load_rows(xlo_ref, xhi_ref, i) + _rms(y) * gpost_ref[...] * m[2:3]
    x1_ref[...] = x1
    h = _rms(x1) * gpre_ref[...] * (1.0 + m[4:5]) + m[3:4]
    for c in range(ROW_TILES):
        h3_ref[pl.ds(c, TM, stride=ROW_TILES), :] = h[:, c * LANES:(c + 1) * LANES]
    contract1 = (((1,), (1,)), ((), ()))
    w_hi, w_lo = _split_bf16(wr_ref[...])
    h_hi, h_lo = _split_bf16(h)
    part = lax.dot_general(jnp.concatenate([w_hi, w_lo], axis=0), h_hi, contract1,
                           preferred_element_type=F32)
    logits = (part[:N_EXPERTS] + part[N_EXPERTS:]
              + lax.dot_general(w_hi, h_lo, contract1, preferred_element_type=F32))
    e = jnp.exp(logits - jnp.max(logits, axis=0, keepdims=True))
    aff_ref[...] = e / jnp.sum(e, axis=0, keepdims=True)


def _out_post(y_parts, w, x_parts, joint, mods, gpost, gpre, w_router_t):
    kdim = w.shape[0]
    return pl.pallas_call(
        _out_post_kernel,
        grid=(NT // TM,),
        in_specs=[*_part_specs(kdim, joint=False),
                  _const_spec((kdim, D_MODEL)),
                  *_part_specs(D_MODEL, joint), _mod_spec(),
                  _const_spec((1, D_MODEL)), _const_spec((1, D_MODEL)),
                  _const_spec((N_EXPERTS, D_MODEL))],
        out_specs=[pl.BlockSpec((TM, D_MODEL), lambda i: (i, 0)),
                   pl.BlockSpec((TM * ROW_TILES, LANES), lambda i: (i, 0)),
                   pl.BlockSpec((N_EXPERTS, TM), lambda i: (0, i))],
        out_shape=[jax.ShapeDtypeStruct((NT, D_MODEL), F32),
                   jax.ShapeDtypeStruct((NT * ROW_TILES, LANES), F32),
                   jax.ShapeDtypeStruct((N_EXPERTS, NT), F32)],
        scratch_shapes=[pltpu.VMEM((kdim, D_MODEL), BF16)],
        compiler_params=_cparams(("arbitrary",)),
        name="out_post",
    )(*y_parts, w, *x_parts, mods, gpost, gpre, w_router_t)


SCATTER_BATCH = 8
MOE_FIRST_GROUPS = 1
MOE_FIRST_FF_SPLIT = 4
MOE_REST_FF_SPLIT = 1


def _moe_kernel(idx_ref, gate_ref, h3_ref, wg_ref, wu_ref, wd_ref, f3_ref, *rest, emit_bf16, ff_split):
    if emit_bf16:
        wg16_ref, wu16_ref, wd16_ref, *rest = rest
    xg, x_a, x_b, yacc, y_a, y_b = rest
    e = pl.program_id(1)
    f = pl.program_id(2)

    def tile(r):
        return pl.ds(pl.multiple_of(r * SUBLANES, SUBLANES), SUBLANES)

    def gather(ex, x_dst):
        base = ex * CAP
        for r in range(CAP):
            xg[r * SUBLANES:(r + 1) * SUBLANES, :] = h3_ref[tile(idx_ref[0, 0, base + r]), :]
        x_dst[...] = jnp.concatenate(
            [xg[pl.ds(c, CAP, stride=ROW_TILES), :] for c in range(ROW_TILES)], axis=1).astype(BF16)

    def scatter(ex, y_src):
        base = ex * CAP
        for r0 in range(0, CAP, SCATTER_BATCH):
            dst, vals = [], []
            for r in range(r0, r0 + SCATTER_BATCH):
                t = idx_ref[0, 0, base + r]
                dst.append(tile(t))
                vals.append(f3_ref[tile(t), :]
                            + gate_ref[0, 0, base + r] * y_src[r * SUBLANES:(r + 1) * SUBLANES, :])
            for d, v in zip(dst, vals):
                f3_ref[d, :] = v

    def half_ffn(x_src):
        x = x_src[...]
        wg = wg_ref[0, 0].astype(BF16)
        wu = wu_ref[0, 0].astype(BF16)
        wd = wd_ref[0, 0].astype(BF16)
        if emit_bf16:
            wg16_ref[0, 0] = wg
            wu16_ref[0, 0] = wu
            wd16_ref[0, 0] = wd
        a = jnp.dot(x, wg, preferred_element_type=F32)
        u = jnp.dot(x, wu, preferred_element_type=F32)
        act = (a * _sigmoid(a) * u).astype(BF16)
        return jnp.dot(act, wd, preferred_element_type=F32)

    first = f == 0
    last = f == ff_split - 1

    @pl.when(jnp.logical_and(e == 0, first))
    def _():
        f3_ref[...] = jnp.zeros_like(f3_ref)
        y_b[...] = jnp.zeros_like(y_b)
        gather(0, x_a)

    if ff_split == 1:
        phases = (("only", first),)
    elif ff_split == 2:
        phases = (("first", first), ("last", last))
    else:
        phases = (("first", first), ("middle", jnp.logical_not(jnp.logical_or(first, last))), ("last", last))
    for parity, (x_cur, x_nxt, y_cur, y_prv) in enumerate(((x_a, x_b, y_a, y_b), (x_b, x_a, y_b, y_a))):
        for phase, cond in phases:
            @pl.when(jnp.logical_and(e % 2 == parity, cond))
            def _():
                if phase in ("only", "first"):
                    gather(jnp.minimum(e + 1, N_EXPERTS - 1), x_nxt)
                part = half_ffn(x_cur)
                if phase == "first":
                    yacc[...] = part
                elif phase == "middle":
                    yacc[...] += part
                else:
                    y = part if phase == "only" else yacc[...] + part
                    for c in range(ROW_TILES):
                        y_cur[pl.ds(c, CAP, stride=ROW_TILES), :] = y[:, c * LANES:(c + 1) * LANES]
                    scatter(jnp.maximum(e - 1, 0), y_prv)

    @pl.when(jnp.logical_and(e == N_EXPERTS - 1, last))
    def _():
        scatter(N_EXPERTS - 1, y_b)


def _moe(idx, gate, h3, wg, wu, wd, layer, *, group0, n_groups, emit_bf16, ff_split):
    grp_rows = GT * ROW_TILES
    ff_tile = EXPERT_FF // ff_split
    route_spec = pl.BlockSpec((1, 1, N_EXPERTS * CAP), lambda b, e, f: (group0 + b, 0, 0),
                              memory_space=pltpu.SMEM)
    w_in_spec = lambda l: pl.BlockSpec((1, 1, D_MODEL, ff_tile), lambda b, e, f: (l, e, 0, f))
    w_out_spec = lambda l: pl.BlockSpec((1, 1, ff_tile, D_MODEL), lambda b, e, f: (l, e, f, 0))
    out_specs = [pl.BlockSpec((grp_rows, LANES), lambda b, e, f: (b, 0), pipeline_mode=pl.Buffered(1))]
    out_shape = [jax.ShapeDtypeStruct((n_groups * grp_rows, LANES), F32)]
    if emit_bf16:
        out_specs += [w_in_spec(0), w_in_spec(0), w_out_spec(0)]
        out_shape += [jax.ShapeDtypeStruct((1, N_EXPERTS, D_MODEL, EXPERT_FF), BF16),
                      jax.ShapeDtypeStruct((1, N_EXPERTS, D_MODEL, EXPERT_FF), BF16),
                      jax.ShapeDtypeStruct((1, N_EXPERTS, EXPERT_FF, D_MODEL), BF16)]
    return pl.pallas_call(
        functools.partial(_moe_kernel, emit_bf16=emit_bf16, ff_split=ff_split),
        grid=(n_groups, N_EXPERTS, ff_split),
        in_specs=[route_spec, route_spec,
                  pl.BlockSpec((grp_rows, LANES), lambda b, e, f: (group0 + b, 0),
                               pipeline_mode=pl.Buffered(1)),
                  w_in_spec(layer), w_in_spec(layer), w_out_spec(layer)],
        out_specs=out_specs,
        out_shape=out_shape,
        scratch_shapes=[pltpu.VMEM((CAP * ROW_TILES, LANES), F32),
                        pltpu.VMEM((CAP, D_MODEL), BF16),
                        pltpu.VMEM((CAP, D_MODEL), BF16),
                        pltpu.VMEM((CAP, D_MODEL), F32),
                        pltpu.VMEM((CAP * ROW_TILES, LANES), F32),
                        pltpu.VMEM((CAP * ROW_TILES, LANES), F32)],
        compiler_params=_cparams(("arbitrary", "arbitrary", "arbitrary")),
        name="moe_experts",
    )(idx, gate, h3, wg, wu, wd)


def _moe_post_kernel(f3lo_ref, f3hi_ref, x_ref, mod_ref, gpost_ref, *o_refs):
    i = pl.program_id(0)
    m = mod_ref[0]
    rows = lambda ref: jnp.concatenate(
        [ref[pl.ds(c, TM, stride=ROW_TILES), :] for c in range(ROW_TILES)], axis=1)
    fx = jnp.where(i < MOE_FIRST_GROUPS * TILES_PER_GROUP, rows(f3lo_ref), rows(f3hi_ref))
    out = x_ref[...] + _rms(fx) * gpost_ref[...] * m[5:6]
    if len(o_refs) == 1:
        o_refs[0][...] = out
    else:
        @pl.when(i < SAMPLE_TILES)
        def _():
            o_refs[0][...] = out

        @pl.when(i >= SAMPLE_TILES)
        def _():
            o_refs[1][...] = out


def _moe_post(f3_parts, x, mods, gpost, split_out):
    if split_out:
        out_specs = _part_specs(D_MODEL, joint=False)
        out_shape = [jax.ShapeDtypeStruct((N_SAMPLE, D_MODEL), F32),
                     jax.ShapeDtypeStruct((N_PROMPT, D_MODEL), F32)]
    else:
        out_specs = pl.BlockSpec((TM, D_MODEL), lambda i: (i, 0))
        out_shape = jax.ShapeDtypeStruct((NT, D_MODEL), F32)
    return pl.pallas_call(
        _moe_post_kernel,
        grid=(NT // TM,),
        in_specs=[*_part_specs(LANES, joint=False, split=MOE_FIRST_GROUPS * TILES_PER_GROUP,
                               rows=TM * ROW_TILES),
                  pl.BlockSpec((TM, D_MODEL), lambda i: (i, 0)),
                  _mod_spec(), _const_spec((1, D_MODEL))],
        out_specs=out_specs,
        out_shape=out_shape,
        compiler_params=_cparams(("arbitrary",)),
        name="moe_post",
    )(*f3_parts, x, mods, gpost)


ROUTE_SEG = SEQ
RANK_LO = 16
RANK_HI = CAP // RANK_LO
TOK_LO = 64
ROUTE_VALS = 5


def _route_kernel(aff_ref, idx_ref, gate_ref, sel_scr, a_scr, code_scr, g_scr, acc_scr):
    g = pl.program_id(0)
    n_seg = GT // ROUTE_SEG
    seg = lambda s: slice(s * ROUTE_SEG, (s + 1) * ROUTE_SEG)
    aff = aff_ref[...]
    tok = lax.broadcasted_iota(jnp.int32, (N_EXPERTS, GT), 1)
    ones_seg = jnp.ones((ROUTE_SEG, ROUTE_SEG), BF16)

    def row_count(mask):
        return jnp.sum(jnp.where(mask, 1.0, 0.0), axis=1, keepdims=True)

    def seg_count(mask):
        m = jnp.where(mask, 1.0, 0.0).astype(BF16)
        return jnp.concatenate(
            [jnp.dot(m[:, seg(s)], ones_seg, preferred_element_type=F32) for s in range(n_seg)], axis=1)

    def select(count, k, shape):
        as_float = lambda b: pltpu.bitcast(b, F32)

        def value_bit(i, t):
            cand = t | jnp.left_shift(jnp.int32(1), 30 - i)
            return jnp.where(count(aff >= as_float(cand)) >= k, cand, t)

        thr = lax.fori_loop(0, 31, value_bit, jnp.zeros(shape, jnp.int32))
        above = aff >= as_float(thr + 1)
        tied = jnp.logical_and(aff >= as_float(thr), jnp.logical_not(above))
        need = k - count(above)

        def index_bit(i, v):
            cand = v | jnp.left_shift(jnp.int32(1), 11 - i)
            return jnp.where(count(jnp.logical_and(tied, tok < cand)) < need, cand, v)

        last = lax.fori_loop(0, 12, index_bit, jnp.zeros(shape, jnp.int32))
        chosen = jnp.logical_or(above, jnp.logical_and(tied, tok <= last))
        sel_scr[...] = jnp.where(chosen, 1.0, 0.0)

    @pl.when(g < DEC_BATCH)
    def _():
        select(row_count, float(2 * DEC_SEQ // N_EXPERTS), (N_EXPERTS, 1))

    @pl.when(g >= DEC_BATCH)
    def _():
        select(seg_count, float(2 * SEQ // N_EXPERTS), (N_EXPERTS, GT))

    sel = sel_scr[...]
    r_i = lax.broadcasted_iota(jnp.int32, (ROUTE_SEG, ROUTE_SEG), 0)
    c_i = lax.broadcasted_iota(jnp.int32, (ROUTE_SEG, ROUTE_SEG), 1)
    upper = jnp.where(r_i <= c_i, 1.0, 0.0).astype(BF16)
    sel_b = sel.astype(BF16)
    off = jnp.zeros((N_EXPERTS, 1), F32)
    for s in range(n_seg):
        inc = jnp.dot(sel_b[:, seg(s)], upper, preferred_element_type=F32)
        rank = inc + off - sel[:, seg(s)]
        a = jnp.floor(rank * (1.0 / RANK_LO))
        a_scr[:, seg(s)] = jnp.where(sel[:, seg(s)] > 0, a, -1.0)
        code_scr[:, seg(s)] = jnp.where(sel[:, seg(s)] > 0, rank - RANK_LO * a + 1.0, 0.0).astype(BF16)
        off = off + inc[:, ROUTE_SEG - 1:ROUTE_SEG]
    g1 = aff.astype(BF16).astype(F32)
    g2 = (aff - g1).astype(BF16).astype(F32)
    g_scr[0] = g1
    g_scr[1] = g2
    g_scr[2] = (aff - g1 - g2).astype(BF16).astype(F32)

    lane = lax.broadcasted_iota(jnp.int32, (N_EXPERTS, N_EXPERTS * RANK_LO), 1)
    row = lax.broadcasted_iota(jnp.int32, (N_EXPERTS, N_EXPERTS * RANK_LO), 0)
    lo_bits = RANK_LO.bit_length() - 1
    expand = jnp.where(jnp.right_shift(lane, lo_bits) == row, 1.0, 0.0).astype(BF16)
    digit = (jnp.bitwise_and(lax.broadcasted_iota(jnp.int32, (1, N_EXPERTS * RANK_LO), 1), RANK_LO - 1)
             + 1).astype(F32)
    a_iota = lax.broadcasted_iota(jnp.int32, (RANK_HI, 1), 0).astype(F32)
    acc_scr[...] = jnp.zeros_like(acc_scr)
    contract0 = (((0,), (0,)), ((), ()))

    def chunk(c):
        lanes = pl.ds(pl.multiple_of(c * ROUTE_SEG, ROUTE_SEG), ROUTE_SEG)
        a_c = a_scr[:, lanes]
        g_c = [g_scr[j, :, lanes] for j in range(3)]
        tok_c = lax.broadcasted_iota(jnp.int32, (1, ROUTE_SEG), 1) + c * ROUTE_SEG
        t_hi = jnp.right_shift(tok_c, TOK_LO.bit_length() - 1).astype(F32)
        t_lo = jnp.bitwise_and(tok_c, TOK_LO - 1).astype(F32)
        rows = []
        for e in range(N_EXPERTS):
            hit = a_c[e:e + 1, :] == a_iota
            for val in (t_hi, t_lo, g_c[0][e:e + 1, :], g_c[1][e:e + 1, :], g_c[2][e:e + 1, :]):
                rows.append(jnp.where(hit, val, 0.0))
        lhs = jnp.concatenate(rows, axis=0).astype(BF16)
        spread = lax.dot_general(code_scr[:, lanes], expand, contract0, preferred_element_type=F32)
        low_hot = jnp.where(spread == digit, 1.0, 0.0).astype(BF16)
        return jnp.dot(lhs, low_hot, preferred_element_type=F32)

    def chunk_pair(c, carry):
        acc_scr[...] += chunk(2 * c) + chunk(2 * c + 1)
        return carry

    lax.fori_loop(0, n_seg // 2, chunk_pair, 0)
    per_e = ROUTE_VALS * RANK_HI
    for e in range(N_EXPERTS):
        blk = acc_scr[e * per_e:(e + 1) * per_e, :][:, e * RANK_LO:(e + 1) * RANK_LO]
        part = lambda j: blk[j * RANK_HI:(j + 1) * RANK_HI]
        idx_ref[0, e] = (part(0) * TOK_LO + part(1)).astype(jnp.int32)
        gate_ref[0, e] = part(2) + part(3) + part(4)


def _route(aff_t):
    out_spec = pl.BlockSpec((1, N_EXPERTS, RANK_HI, RANK_LO), lambda g: (g, 0, 0, 0))
    idx, gate = pl.pallas_call(
        _route_kernel,
        grid=(NG,),
        in_specs=[pl.BlockSpec((N_EXPERTS, GT), lambda g: (0, g))],
        out_specs=[out_spec, out_spec],
        out_shape=[jax.ShapeDtypeStruct((NG, N_EXPERTS, RANK_HI, RANK_LO), jnp.int32),
                   jax.ShapeDtypeStruct((NG, N_EXPERTS, RANK_HI, RANK_LO), F32)],
        scratch_shapes=[pltpu.VMEM((N_EXPERTS, GT), F32),
                        pltpu.VMEM((N_EXPERTS, GT), F32),
                        pltpu.VMEM((N_EXPERTS, GT), BF16),
                        pltpu.VMEM((3, N_EXPERTS, GT), F32),
                        pltpu.VMEM((N_EXPERTS * ROUTE_VALS * RANK_HI, N_EXPERTS * RANK_LO), F32)],
        compiler_params=_cparams(("arbitrary",)),
        name="route",
    )(aff_t)
    return idx.reshape(NG, 1, N_EXPERTS * CAP), gate.reshape(NG, 1, N_EXPERTS * CAP)


HD = MLA_HEADS * LANES
Q_SCALE = math.log2(math.e) / math.sqrt(MLA_NOPE + MLA_ROPE)


def _mla_proj_kernel(xlo_ref, xhi_ref, mod_ref, gain_ref, win_ref, qn_g_ref, kvn_g_ref, wq_ref, wkv_ref,
                     cos_ref, sin_ref, q_ref, kn_ref, v_ref, krp_ref, ckv_ref, kraw_ref):
    i = pl.program_id(0)
    m = mod_ref[0]
    h = (_rms(_load_rows(xlo_ref, xhi_ref, i)) * gain_ref[...] * (1.0 + m[1:2]) + m[0:1]).astype(BF16)
    lat = jnp.dot(h, win_ref[...], preferred_element_type=F32)
    q_lat = lat[:, :MLA_Q_LORA]
    kv_lat = lat[:, MLA_Q_LORA:MLA_Q_LORA + MLA_KV_LORA]
    kr = lat[:, MLA_Q_LORA + MLA_KV_LORA:MLA_Q_LORA + MLA_KV_LORA + LANES]
    kr_rot = lat[:, MLA_Q_LORA + MLA_KV_LORA + LANES:]
    is_latent = i < SAMPLE_TILES
    cos = jnp.where(is_latent, cos_ref[...], 1.0)
    sin = jnp.where(is_latent, sin_ref[...], 0.0)

    qln = (_rms(q_lat) * qn_g_ref[...]).astype(BF16)
    q = jnp.dot(qln, wq_ref[...], preferred_element_type=F32) * Q_SCALE
    for hd in range(MLA_HEADS):
        lo = HD + hd * LANES
        q_ref[:, 2 * hd * LANES:(2 * hd + 1) * LANES] = q[:, hd * LANES:(hd + 1) * LANES].astype(BF16)
        q_ref[:, (2 * hd + 1) * LANES:(2 * hd + 2) * LANES] = (
            q[:, lo:lo + LANES] * cos + q[:, HD + lo:HD + lo + LANES] * sin).astype(BF16)

    ckv = _rms(kv_lat) * kvn_g_ref[...]
    kv = jnp.dot(ckv.astype(BF16), wkv_ref[...], preferred_element_type=F32)
    kn_ref[...] = kv[:, :HD].astype(BF16)
    v_ref[...] = kv[:, HD:].astype(BF16)
    krp_ref[...] = (kr * cos + kr_rot * sin).astype(BF16)

    @pl.when(i >= SAMPLE_TILES)
    def _():
        ckv_ref[...] = ckv
        kraw_ref[...] = kr


def _mla_proj(x_parts, joint, mods, gain, win_ext, qn_g, kvn_g, wq_ext, wkv_ext, cos128, sin128):
    row = lambda w: pl.BlockSpec((TM, w), lambda i: (i, 0))
    ctx_row = lambda w: pl.BlockSpec((TM, w), lambda i: (jnp.maximum(i - SAMPLE_TILES, 0), 0))
    tab = pl.BlockSpec((TM, LANES), lambda i: (i % TILES_PER_GROUP, 0))
    return pl.pallas_call(
        _mla_proj_kernel,
        grid=(NT // TM,),
        in_specs=[*_part_specs(D_MODEL, joint), _mod_spec(),
                  _const_spec(gain.shape), _const_spec(win_ext.shape), _const_spec(qn_g.shape),
                  _const_spec(kvn_g.shape), _const_spec(wq_ext.shape), _const_spec(wkv_ext.shape),
                  tab, tab],
        out_specs=[row(2 * HD), row(HD), row(HD), row(LANES), ctx_row(MLA_KV_LORA), ctx_row(LANES)],
        out_shape=[jax.ShapeDtypeStruct((NT, 2 * HD), BF16),
                   jax.ShapeDtypeStruct((NT, HD), BF16),
                   jax.ShapeDtypeStruct((NT, HD), BF16),
                   jax.ShapeDtypeStruct((NT, LANES), BF16),
                   jax.ShapeDtypeStruct((N_PROMPT, MLA_KV_LORA), F32),
                   jax.ShapeDtypeStruct((N_PROMPT, LANES), F32)],
        compiler_params=_cparams(("arbitrary",)),
        name="mla_proj",
    )(*x_parts, mods, gain, win_ext, qn_g, kvn_g, wq_ext, wkv_ext, cos128, sin128)


def _matmul_kernel(a_ref, w_ref, o_ref):
    o_ref[...] = jnp.dot(a_ref[...].astype(BF16), w_ref[...],
                         preferred_element_type=F32).astype(o_ref.dtype)


def _ctx_expand(ckv_ctx, wkv_ext):
    n = ckv_ctx.shape[0]
    return pl.pallas_call(
        _matmul_kernel,
        grid=(n // TM,),
        in_specs=[pl.BlockSpec((TM, MLA_KV_LORA), lambda i: (i, 0)),
                  pl.BlockSpec(wkv_ext.shape, lambda i: (0, 0))],
        out_specs=pl.BlockSpec((TM, 2 * HD), lambda i: (i, 0)),
        out_shape=jax.ShapeDtypeStruct((n, 2 * HD), BF16),
        compiler_params=_cparams(("parallel",)),
        name="ctx_expand",
    )(ckv_ctx, wkv_ext)


ATTN_CHUNK_UNROLL = 8


def _attn_kernel(q_ref, kn_ref, krp_ref, v_ref, *rest, tk, n_chunks, has_ctx, heads_per_step):
    if has_ctx:
        knc_ref, krpc_ref, vc_ref, o_ref = rest
    else:
        o_ref, = rest
    tq = q_ref.shape[0]
    contract1 = (((1,), (1,)), ((), ()))
    heads = range(heads_per_step)
    head_cols = lambda hd: slice(hd * LANES, (hd + 1) * LANES)
    qs = [q_ref[:, 2 * hd * LANES:2 * (hd + 1) * LANES] for hd in heads]

    def ones_column(rows):
        return (lax.broadcasted_iota(jnp.int32, (rows, LANES), 1) == 0).astype(BF16)

    def step(q, kn, krp, v, ones, carry):
        m, acc = carry
        kc = jnp.concatenate([kn, krp], axis=1)
        s = lax.dot_general(q, kc, contract1, preferred_element_type=F32)
        m_new = jnp.maximum(m, jnp.max(s, axis=-1, keepdims=True))
        p = jnp.exp2(s - m_new).astype(BF16)
        pv = jnp.dot(p, jnp.concatenate([v, ones], axis=1), preferred_element_type=F32)
        return m_new, jnp.exp2(m - m_new) * acc + pv

    carry = tuple((jnp.full((tq, 1), -jnp.inf, F32), jnp.zeros((tq, 2 * LANES), F32)) for _ in heads)
    if has_ctx:
        ones = ones_column(PAST_LEN)
        carry = tuple(step(qs[hd], knc_ref[0, :, head_cols(hd)], krpc_ref[0], vc_ref[0, :, head_cols(hd)],
                           ones, carry[hd]) for hd in heads)
    ones = ones_column(tk)

    def chunk(c, carry):
        rows = pl.ds(pl.multiple_of(c * tk, tk), tk)
        krp = krp_ref[rows, :]
        return tuple(step(qs[hd], kn_ref[rows, head_cols(hd)], krp, v_ref[rows, head_cols(hd)],
                          ones, carry[hd]) for hd in heads)

    unroll = math.gcd(n_chunks, ATTN_CHUNK_UNROLL)

    def body(c, carry):
        for u in range(unroll):
            carry = chunk(c * unroll + u, carry)
        return carry

    carry = lax.fori_loop(0, n_chunks // unroll, body, carry)
    for hd in heads:
        acc = carry[hd][1]
        o_ref[:, head_cols(hd)] = (acc[:, :MLA_V] / acc[:, MLA_V:MLA_V + 1]).astype(o_ref.dtype)


def _attention(q, kn, krp, v, ctx, *, n_req, seq_len, row0, tq, tk, heads_per_step):
    rb0 = row0 // seq_len
    qb0 = row0 // tq
    nq = seq_len // tq
    hw = heads_per_step * LANES
    o_spec = pl.BlockSpec((tq, hw), lambda b, h, i: (b * nq + i, h))
    k_spec = pl.BlockSpec((seq_len, hw), lambda b, h, i: (rb0 + b, h))
    in_specs = [pl.BlockSpec((tq, 2 * hw), lambda b, h, i: (qb0 + b * nq + i, h)), k_spec,
                pl.BlockSpec((seq_len, LANES), lambda b, h, i: (rb0 + b, 0)), k_spec]
    args = [q, kn, krp, v]
    if ctx is not None:
        c_spec = pl.BlockSpec((1, PAST_LEN, hw), lambda b, h, i: (b, 0, h))
        in_specs += [c_spec, pl.BlockSpec((1, PAST_LEN, LANES), lambda b, h, i: (b, 0, 0)), c_spec]
        args += list(ctx)
    return pl.pallas_call(
        functools.partial(_attn_kernel, tk=tk, n_chunks=seq_len // tk, has_ctx=ctx is not None,
                          heads_per_step=heads_per_step),
        grid=(n_req, MLA_HEADS // heads_per_step, nq),
        in_specs=in_specs,
        out_specs=o_spec,
        out_shape=jax.ShapeDtypeStruct((n_req * seq_len, HD), BF16),
        compiler_params=_cparams(("parallel", "parallel", "arbitrary")),
        name=f"attention_{seq_len}",
    )(*args)


def _rope_rot_cols(w):
    w1, w2, w3, w4 = jnp.split(w, 4, axis=-1)
    return jnp.concatenate([-w2, w1, -w4, w3], axis=-1)


def _pad_cols(w, width):
    return jnp.pad(w, ((0, 0), (0, width - w.shape[1])))


def _mla_weights(w_in, w_q_b, w_kv_b):
    w_rope = w_in[:, MLA_Q_LORA + MLA_KV_LORA:]
    win_ext = jnp.concatenate([w_in[:, :MLA_Q_LORA + MLA_KV_LORA], _pad_cols(w_rope, LANES),
                               _pad_cols(_rope_rot_cols(w_rope), LANES)], axis=1).astype(BF16)
    wq = w_q_b.reshape(MLA_Q_LORA, MLA_HEADS, MLA_NOPE + MLA_ROPE)
    wq_nope = wq[:, :, :MLA_NOPE].reshape(MLA_Q_LORA, HD)
    wq_rope = wq[:, :, MLA_NOPE:]
    pad = ((0, 0), (0, 0), (0, LANES - MLA_ROPE))
    wq_rope_p = jnp.pad(wq_rope, pad).reshape(MLA_Q_LORA, HD)
    wq_rot_p = jnp.pad(_rope_rot_cols(wq_rope), pad).reshape(MLA_Q_LORA, HD)
    wq_ext = jnp.concatenate([wq_nope, wq_rope_p, wq_rot_p], axis=1).astype(BF16)
    wkv = w_kv_b.reshape(MLA_KV_LORA, MLA_HEADS, MLA_NOPE + MLA_V)
    wkv_ext = jnp.concatenate([wkv[:, :, :MLA_NOPE].reshape(MLA_KV_LORA, HD),
                               wkv[:, :, MLA_NOPE:].reshape(MLA_KV_LORA, HD)], axis=1).astype(BF16)
    return win_ext, wq_ext, wkv_ext


def _rope_tables():
    rows = DEC_SEQ // GRID_W
    row = jnp.repeat(jnp.arange(rows, dtype=F32), GRID_W)
    col = jnp.tile(jnp.arange(GRID_W, dtype=F32), rows)
    half = MLA_ROPE // 2
    inv = 1.0 / (ROPE_BASE ** (jnp.arange(0, half, 2, dtype=F32) / half))
    ar = row[:, None] * inv
    ac = col[:, None] * inv
    ang = jnp.concatenate([ar, ar, ac, ac] * 2, axis=-1)
    return jnp.cos(ang), jnp.sin(ang)


def kernel(x_prompt, x_sample, state_ret_fwd, state_ret_bwd, cache_mla_ckv, cache_mla_krope, c, c_ctx,
           ada_w, ada_b, norm_pre, norm_post, ret_w_in, ret_decay_fwd, ret_decay_bwd, ret_w_out,
           mla_w_in, mla_q_norm, mla_kv_norm, mla_w_q_b, mla_w_kv_b, mla_w_out,
           moe_w_router, moe_w_gate, moe_w_up, moe_w_down):
    x_parts = (x_sample.reshape(N_SAMPLE, D_MODEL), x_prompt.reshape(N_PROMPT, D_MODEL))
    joint = False
    cond8 = jnp.concatenate([c, c_ctx[None, :], jnp.zeros((8 - NG, D_MODEL), F32)], axis=0)
    cos128, sin128 = _rope_tables()
    outs = {}

    for i in range(DEPTH):
        mods = _adaln(cond8, ada_w, ada_b.reshape(DEPTH, 1, 6 * D_MODEL), i)[:NG].reshape(NG, 6, D_MODEL)
        gpre1 = norm_pre[i, 0][None, :]
        gpre2 = norm_pre[i, 1][None, :]
        gpost1 = norm_post[i, 0][None, :]
        gpost2 = norm_post[i, 1][None, :]
        if i % 2 == 0:
            r = i // 2
            qkvg = _norm_mod_matmul(x_parts, joint, mods, gpre1, ret_w_in[r], 1536)
            lg = jnp.stack([jax.nn.log_sigmoid(ret_decay_fwd[r].astype(F32)),
                            jax.nn.log_sigmoid(ret_decay_bwd[r].astype(F32))])
            yg_s, _, _ = _retention(qkvg, lg, state_ret_fwd[:, r], state_ret_bwd[:, r],
                                    n_req=DEC_BATCH, seq_len=DEC_SEQ, row0=0, heads_per_step=1)
            yg_p, s_f, s_b = _retention(qkvg, lg, None, None, n_req=BATCH, seq_len=SEQ, row0=N_SAMPLE,
                                        heads_per_step=RET_HEADS)
            outs["fwd"], outs["bwd"] = s_f[:, None], s_b[:, None]
            mix, w_out = (yg_s, yg_p), ret_w_out[r]
        else:
            mi = i // 2
            win_ext, wq_ext, wkv_ext = _mla_weights(mla_w_in[mi], mla_w_q_b[mi], mla_w_kv_b[mi])
            q, kn, v, krp, ckv, kraw = _mla_proj(
                x_parts, joint, mods, gpre1, win_ext, mla_q_norm[mi][None, :], mla_kv_norm[mi][None, :],
                wq_ext, wkv_ext, cos128, sin128)
            outs["ckv"] = ckv.reshape(BATCH, 1, SEQ, MLA_KV_LORA)
            outs["krope"] = kraw[:, :MLA_ROPE].reshape(BATCH, 1, SEQ, MLA_ROPE)
            kvc = _ctx_expand(cache_mla_ckv[:, mi].reshape(DEC_BATCH * PAST_LEN, MLA_KV_LORA), wkv_ext)
            kvc = kvc.reshape(DEC_BATCH, PAST_LEN, 2 * HD)
            krpc = jnp.pad(cache_mla_krope[:, mi], ((0, 0), (0, 0), (0, LANES - MLA_ROPE))).astype(BF16)
            o_s = _attention(q, kn, krp, v, (kvc[:, :, :HD], krpc, kvc[:, :, HD:]),
                             n_req=DEC_BATCH, seq_len=DEC_SEQ, row0=0, tq=512, tk=512, heads_per_step=4)
            o_p = _attention(q, kn, krp, v, None, n_req=BATCH, seq_len=SEQ, row0=N_SAMPLE, tq=SEQ, tk=SEQ,
                             heads_per_step=MLA_HEADS)
            mix, w_out = (o_s, o_p), mla_w_out[mi]
        x1, h3, aff_t = _out_post(mix, w_out, x_parts, joint, mods, gpost1, gpre2, moe_w_router[i].T)
        idx, gate = _route(aff_t)
        f3_lo, wg16, wu16, wd16 = _moe(idx, gate, h3, moe_w_gate, moe_w_up, moe_w_down, i,
                                       group0=0, n_groups=MOE_FIRST_GROUPS, emit_bf16=True,
                                       ff_split=MOE_FIRST_FF_SPLIT)
        f3_hi, = _moe(idx, gate, h3, wg16, wu16, wd16, 0,
                      group0=MOE_FIRST_GROUPS, n_groups=NG - MOE_FIRST_GROUPS, emit_bf16=False,
                      ff_split=MOE_REST_FF_SPLIT)
        last = i == DEPTH - 1
        x_new = _moe_post((f3_lo, f3_hi), x1, mods, gpost2, split_out=last)
        if not last:
            x_parts, joint = (x_new, x_new), True

    y_sample, y_prompt = x_new
    return (y_prompt.reshape(BATCH, SEQ, D_MODEL), y_sample.reshape(DEC_BATCH, DEC_SEQ, D_MODEL),
            outs["fwd"], outs["bwd"], outs["ckv"], outs["krope"])
```

```python
import functools
import math

import jax
import jax.numpy as jnp
from jax import lax
from jax.experimental import pallas as pl
from jax.experimental.pallas import tpu as pltpu

F32 = jnp.float32
BF16 = jnp.bfloat16

D_MODEL = 1024
BATCH = 16
SEQ = 256
DEPTH = 2
DEC_BATCH = 4
DEC_SEQ = 4096
PAST_LEN = 256
GRID_W = 64
RET_HEADS = 4
RET_DK = 256
RET_DV = 512
RET_QK_WIDTH = RET_HEADS * RET_DK
RET_V_WIDTH = RET_HEADS * RET_DV
RET_CHUNK = 128
MLA_HEADS = 8
MLA_NOPE = 128
MLA_ROPE = 64
MLA_V = 128
MLA_Q_LORA = 384
MLA_KV_LORA = 256
ROPE_BASE = 10000.0
N_EXPERTS = 16
EXPERT_FF = 1024
NORM_EPS = 1e-6

GT = DEC_SEQ
NG = DEC_BATCH + 1
NT = NG * GT
N_SAMPLE = DEC_BATCH * DEC_SEQ
N_PROMPT = BATCH * SEQ
CAP = 2 * GT // N_EXPERTS

LANES = 128
SUBLANES = 8
ROW_TILES = D_MODEL // LANES
TM = 512
TILES_PER_GROUP = GT // TM
SAMPLE_TILES = N_SAMPLE // TM
VMEM_LIMIT = 60 * 1024 * 1024


def _cparams(sem):
    return pltpu.CompilerParams(dimension_semantics=sem, vmem_limit_bytes=VMEM_LIMIT)


def _rms(x):
    return x * lax.rsqrt(jnp.mean(x * x, axis=-1, keepdims=True) + NORM_EPS)


def _sigmoid(x):
    return 1.0 / (1.0 + jnp.exp(-x))


def _part_specs(width, joint, split=SAMPLE_TILES, rows=TM):
    lo = pl.BlockSpec((rows, width), lambda *g: (jnp.minimum(g[-1], split - 1), 0))
    if joint:
        hi = pl.BlockSpec((rows, width), lambda *g: (jnp.maximum(g[-1], split), 0))
    else:
        hi = pl.BlockSpec((rows, width), lambda *g: (jnp.maximum(g[-1] - split, 0), 0))
    return [lo, hi]


def _load_rows(lo_ref, hi_ref, i, split=SAMPLE_TILES):
    return jnp.where(i < split, lo_ref[...], hi_ref[...])


def _mod_spec():
    return pl.BlockSpec((1, 6, D_MODEL), lambda *g: (g[-1] // TILES_PER_GROUP, 0, 0))


def _const_spec(shape):
    return pl.BlockSpec(shape, lambda *g: (0,) * len(shape))


def _adaln_kernel(c_ref, w_ref, b_ref, o_ref):
    c = c_ref[...]
    s = (c * _sigmoid(c)).astype(BF16)
    o_ref[...] = jnp.dot(s, w_ref[0].astype(BF16), preferred_element_type=F32) + b_ref[0]


def _adaln(cond8, w, b, layer):
    tn = 1536
    return pl.pallas_call(
        _adaln_kernel,
        grid=(6 * D_MODEL // tn,),
        in_specs=[pl.BlockSpec((8, D_MODEL), lambda j: (0, 0)),
                  pl.BlockSpec((1, D_MODEL, tn), lambda j: (layer, 0, j)),
                  pl.BlockSpec((1, 1, tn), lambda j: (layer, 0, j))],
        out_specs=pl.BlockSpec((8, tn), lambda j: (0, j)),
        out_shape=jax.ShapeDtypeStruct((8, 6 * D_MODEL), F32),
        compiler_params=_cparams(("arbitrary",)),
        name="adaln",
    )(cond8, w, b)


NMM_TM = 1024


def _nmm_kernel(xlo_ref, xhi_ref, mod_ref, gain_ref, w_ref, o_ref, w_scr):
    i = pl.program_id(1)

    @pl.when(i == 0)
    def _():
        w_scr[...] = w_ref[...].astype(BF16)

    m = mod_ref[0]
    x = _load_rows(xlo_ref, xhi_ref, i, split=N_SAMPLE // NMM_TM)
    h = _rms(x) * gain_ref[...] * (1.0 + m[1:2]) + m[0:1]
    o_ref[...] = jnp.dot(h.astype(BF16), w_scr[...], preferred_element_type=F32).astype(o_ref.dtype)


def _norm_mod_matmul(x_parts, joint, mods, gain, w, tn):
    n_out = w.shape[1]
    tm = NMM_TM
    return pl.pallas_call(
        _nmm_kernel,
        grid=(n_out // tn, NT // tm),
        in_specs=[*_part_specs(D_MODEL, joint, split=N_SAMPLE // tm, rows=tm),
                  pl.BlockSpec((1, 6, D_MODEL), lambda j, i: (i // (GT // tm), 0, 0)),
                  _const_spec((1, D_MODEL)),
                  pl.BlockSpec((D_MODEL, tn), lambda j, i: (0, j))],
        out_specs=pl.BlockSpec((tm, tn), lambda j, i: (i, j)),
        out_shape=jax.ShapeDtypeStruct((NT, n_out), BF16),
        scratch_shapes=[pltpu.VMEM((D_MODEL, tn), BF16)],
        compiler_params=_cparams(("arbitrary", "arbitrary")),
        name="norm_mod_matmul",
    )(*x_parts, mods, gain, w)


RET_SCAN_UNROLL = 8


def _ret_kernel(lg_ref, q_ref, k_ref, v_ref, g_ref, *rest, seq_len, has_init, heads_per_step):
    s0f_ref = s0b_ref = None
    if has_init:
        s0f_ref, s0b_ref, *rest = rest
    yg_ref, sf_ref, sb_ref, *scratch = rest
    for hh in range(heads_per_step):
        dk = slice(hh * RET_DK, (hh + 1) * RET_DK)
        dv = slice(hh * RET_DV, (hh + 1) * RET_DV)
        _ret_head(lg_ref, pl.program_id(1) * heads_per_step + hh,
                  q_ref.at[:, dk], k_ref.at[:, dk], v_ref.at[:, dv], g_ref.at[:, dv],
                  s0f_ref.at[0, hh] if has_init else None, s0b_ref.at[0, hh] if has_init else None,
                  yg_ref.at[:, dv], sf_ref.at[0, hh], sb_ref.at[0, hh], *scratch, seq_len=seq_len)


def _ret_head(lg_ref, head, q_ref, k_ref, v_ref, g_ref, s0f_ref, s0b_ref, yg_ref, sf_ref, sb_ref,
              ypart, ycross, sf_scr, sb_scr, *, seq_len):
    C = RET_CHUNK
    nc = seq_len // C
    lgf = lg_ref[0, head]
    lgb = lg_ref[1, head]
    k_scale = RET_DK ** -0.5

    ii = lax.broadcasted_iota(jnp.int32, (C, C), 0).astype(F32)
    jj = lax.broadcasted_iota(jnp.int32, (C, C), 1).astype(F32)
    diff = ii - jj
    inner = jnp.where(diff >= 0, jnp.exp(diff * lgf), jnp.exp(-diff * lgb)) * k_scale
    pos = lax.broadcasted_iota(jnp.int32, (C, 1), 0).astype(F32)
    cross_f = jnp.exp((pos + 1.0) * lgf)
    cross_b = jnp.exp((C - pos) * lgb)
    sdec_f = jnp.exp((C - 1.0 - pos) * lgf) * k_scale
    sdec_b = jnp.exp(pos * lgb) * k_scale
    one = jnp.ones((1, 1), F32)
    cdec_f = jnp.exp(one * (C * lgf))
    cdec_b = jnp.exp(one * (C * lgb))

    if s0f_ref is not None:
        sf_scr[...] = s0f_ref[...]
        sb_scr[...] = s0b_ref[...]
    else:
        sf_scr[...] = jnp.zeros_like(sf_scr)
        sb_scr[...] = jnp.zeros_like(sb_scr)

    contract0 = (((0,), (0,)), ((), ()))
    contract1 = (((1,), (1,)), ((), ()))

    def chunk_rows(c):
        return pl.ds(pl.multiple_of(c * C, C), C)

    def bwd_chunk(c):
        rows = chunk_rows(c)
        q = q_ref[rows, :]
        s = sb_scr[...]
        ycross[rows, :] = jnp.dot(q, s.astype(BF16), preferred_element_type=F32) * cross_b
        kd = (k_ref[rows, :].astype(F32) * sdec_b).astype(BF16)
        sb_scr[...] = s * cdec_b + lax.dot_general(kd, v_ref[rows, :], contract0,
                                                   preferred_element_type=F32)

    def fwd_chunk(c):
        rows = chunk_rows(c)
        q = q_ref[rows, :]
        k = k_ref[rows, :]
        v = v_ref[rows, :]
        s = sf_scr[...]
        scores = lax.dot_general(q, k, contract1, preferred_element_type=F32) * inner
        ypart[rows, :] = (jnp.dot(scores.astype(BF16), v, preferred_element_type=F32)
                          + jnp.dot(q, s.astype(BF16), preferred_element_type=F32) * cross_f)
        kd = (k.astype(F32) * sdec_f).astype(BF16)
        sf_scr[...] = s * cdec_f + lax.dot_general(kd, v, contract0, preferred_element_type=F32)

    unroll = math.gcd(nc, RET_SCAN_UNROLL)

    def scan(t, carry):
        for u in range(unroll):
            step = t * unroll + u
            fwd_chunk(step)
            bwd_chunk(nc - 1 - step)
        return carry

    lax.fori_loop(0, nc // unroll, scan, 0)
    sf_ref[...] = sf_scr[...]
    sb_ref[...] = sb_scr[...]

    def finish(t, carry):
        for u in range(unroll):
            rows = chunk_rows(t * unroll + u)
            y = ypart[rows, :] + ycross[rows, :]
            mu = jnp.mean(y, axis=-1, keepdims=True)
            yc = y - mu
            yn = yc * lax.rsqrt(jnp.mean(yc * yc, axis=-1, keepdims=True) + NORM_EPS)
            g = g_ref[rows, :].astype(F32)
            yg_ref[rows, :] = (g * _sigmoid(g) * yn).astype(yg_ref.dtype)
        return carry

    lax.fori_loop(0, nc // unroll, finish, 0)


def _retention(qkvg, lg, s0f, s0b, *, n_req, seq_len, row0, heads_per_step):
    rb0 = row0 // seq_len
    has_init = s0f is not None
    hps = heads_per_step
    n_hb = RET_HEADS // hps
    qk_spec = lambda off: pl.BlockSpec((seq_len, hps * RET_DK), lambda b, h, lg_: (rb0 + b, off + h))
    vg_spec = lambda off: pl.BlockSpec((seq_len, hps * RET_DV), lambda b, h, lg_: (rb0 + b, off + h))
    st_spec = pl.BlockSpec((1, hps, RET_DK, RET_DV), lambda b, h, lg_: (b, h, 0, 0))
    in_specs = [qk_spec(0), qk_spec(n_hb), vg_spec(n_hb), vg_spec(2 * n_hb)]
    args = [qkvg, qkvg, qkvg, qkvg]
    if has_init:
        in_specs += [st_spec, st_spec]
        args += [s0f, s0b]
    st_shape = jax.ShapeDtypeStruct((n_req, RET_HEADS, RET_DK, RET_DV), F32)
    return pl.pallas_call(
        functools.partial(_ret_kernel, seq_len=seq_len, has_init=has_init, heads_per_step=hps),
        grid_spec=pltpu.PrefetchScalarGridSpec(
            num_scalar_prefetch=1,
            grid=(n_req, n_hb),
            in_specs=in_specs,
            out_specs=[pl.BlockSpec((seq_len, hps * RET_DV), lambda b, h, lg_: (b, h)), st_spec, st_spec],
            scratch_shapes=[pltpu.VMEM((seq_len, RET_DV), F32),
                            pltpu.VMEM((seq_len, RET_DV), F32),
                            pltpu.VMEM((RET_DK, RET_DV), F32),
                            pltpu.VMEM((RET_DK, RET_DV), F32)]),
        out_shape=[jax.ShapeDtypeStruct((n_req * seq_len, RET_V_WIDTH), BF16), st_shape, st_shape],
        compiler_params=_cparams(("parallel", "arbitrary")),
        name=f"retention_{seq_len}",
    )(lg, *args)


def _split_bf16(x):
    hi = x.astype(BF16)
    return hi, (x - hi.astype(F32)).astype(BF16)


def _out_post_kernel(ylo_ref, yhi_ref, w_ref, xlo_ref, xhi_ref, mod_ref, gpost_ref, gpre_ref, wr_ref,
                     x1_ref, h3_ref, aff_ref, w_scr):
    i = pl.program_id(0)

    @pl.when(i == 0)
    def _():
        w_scr[...] = w_ref[...].astype(BF16)

    m = mod_ref[0]
    y = jnp.dot(_load_rows(ylo_ref, yhi_ref, i), w_scr[...], preferred_element_type=F32)
    x1 = _load_rows(xlo_ref, xhi_ref, i) + _rms(y) * gpost_ref[...] * m[2:3]
    x1_ref[...] = x1
    h = _rms(x1) * gpre_ref[...] * (1.0 + m[4:5]) + m[3:4]
    for c in range(ROW_TILES):
        h3_ref[pl.ds(c, TM, stride=ROW_TILES), :] = h[:, c * LANES:(c + 1) * LANES]
    contract1 = (((1,), (1,)), ((), ()))
    w_hi, w_lo = _split_bf16(wr_ref[...])
    h_hi, h_lo = _split_bf16(h)
    part = lax.dot_general(jnp.concatenate([w_hi, w_lo], axis=0), h_hi, contract1,
                           preferred_element_type=F32)
    logits = (part[:N_EXPERTS] + part[N_EXPERTS:]
              + lax.dot_general(w_hi, h_lo, contract1, preferred_element_type=F32))
    e = jnp.exp(logits - jnp.max(logits, axis=0, keepdims=True))
    aff_ref[...] = e / jnp.sum(e, axis=0, keepdims=True)


def _out_post(y_parts, w, x_parts, joint, mods, gpost, gpre, w_router_t):
    kdim = w.shape[0]
    return pl.pallas_call(
        _out_post_kernel,
        grid=(NT // TM,),
        in_specs=[*_part_specs(kdim, joint=False),
                  _const_spec((kdim, D_MODEL)),
                  *_part_specs(D_MODEL, joint), _mod_spec(),
                  _const_spec((1, D_MODEL)), _const_spec((1, D_MODEL)),
                  _const_spec((N_EXPERTS, D_MODEL))],
        out_specs=[pl.BlockSpec((TM, D_MODEL), lambda i: (i, 0)),
                   pl.BlockSpec((TM * ROW_TILES, LANES), lambda i: (i, 0)),
                   pl.BlockSpec((N_EXPERTS, TM), lambda i: (0, i))],
        out_shape=[jax.ShapeDtypeStruct((NT, D_MODEL), F32),
                   jax.ShapeDtypeStruct((NT * ROW_TILES, LANES), F32),
                   jax.ShapeDtypeStruct((N_EXPERTS, NT), F32)],
        scratch_shapes=[pltpu.VMEM((kdim, D_MODEL), BF16)],
        compiler_params=_cparams(("arbitrary",)),
        name="out_post",
    )(*y_parts, w, *x_parts, mods, gpost, gpre, w_router_t)


SCATTER_BATCH = 8
MOE_FIRST_GROUPS = 1
MOE_FIRST_FF_SPLIT = 4
MOE_REST_FF_SPLIT = 1


def _moe_kernel(idx_ref, gate_ref, h3_ref, wg_ref, wu_ref, wd_ref, f3_ref, *rest, emit_bf16, ff_split):
    if emit_bf16:
        wg16_ref, wu16_ref, wd16_ref, *rest = rest
    xg, x_a, x_b, yacc, y_a, y_b = rest
    e = pl.program_id(1)
    f = pl.program_id(2)

    def tile(r):
        return pl.ds(pl.multiple_of(r * SUBLANES, SUBLANES), SUBLANES)

    def gather(ex, x_dst):
        base = ex * CAP
        for r in range(CAP):
            xg[r * SUBLANES:(r + 1) * SUBLANES, :] = h3_ref[tile(idx_ref[0, 0, base + r]), :]
        x_dst[...] = jnp.concatenate(
            [xg[pl.ds(c, CAP, stride=ROW_TILES), :] for c in range(ROW_TILES)], axis=1).astype(BF16)

    def scatter(ex, y_src):
        base = ex * CAP
        for r0 in range(0, CAP, SCATTER_BATCH):
            dst, vals = [], []
            for r in range(r0, r0 + SCATTER_BATCH):
                t = idx_ref[0, 0, base + r]
                dst.append(tile(t))
                vals.append(f3_ref[tile(t), :]
                            + gate_ref[0, 0, base + r] * y_src[r * SUBLANES:(r + 1) * SUBLANES, :])
            for d, v in zip(dst, vals):
                f3_ref[d, :] = v

    def half_ffn(x_src):
        x = x_src[...]
        wg = wg_ref[0, 0].astype(BF16)
        wu = wu_ref[0, 0].astype(BF16)
        wd = wd_ref[0, 0].astype(BF16)
        if emit_bf16:
            wg16_ref[0, 0] = wg
            wu16_ref[0, 0] = wu
            wd16_ref[0, 0] = wd
        a = jnp.dot(x, wg, preferred_element_type=F32)
        u = jnp.dot(x, wu, preferred_element_type=F32)
        act = (a * _sigmoid(a) * u).astype(BF16)
        return jnp.dot(act, wd, preferred_element_type=F32)

    first = f == 0
    last = f == ff_split - 1

    @pl.when(jnp.logical_and(e == 0, first))
    def _():
        f3_ref[...] = jnp.zeros_like(f3_ref)
        y_b[...] = jnp.zeros_like(y_b)
        gather(0, x_a)

    if ff_split == 1:
        phases = (("only", first),)
    elif ff_split == 2:
        phases = (("first", first), ("last", last))
    else:
        phases = (("first", first), ("middle", jnp.logical_not(jnp.logical_or(first, last))), ("last", last))
    for parity, (x_cur, x_nxt, y_cur, y_prv) in enumerate(((x_a, x_b, y_a, y_b), (x_b, x_a, y_b, y_a))):
        for phase, cond in phases:
            @pl.when(jnp.logical_and(e % 2 == parity, cond))
            def _():
                if phase in ("only", "first"):
                    gather(jnp.minimum(e + 1, N_EXPERTS - 1), x_nxt)
                part = half_ffn(x_cur)
                if phase == "first":
                    yacc[...] = part
                elif phase == "middle":
                    yacc[...] += part
                else:
                    y = part if phase == "only" else yacc[...] + part
                    for c in range(ROW_TILES):
                        y_cur[pl.ds(c, CAP, stride=ROW_TILES), :] = y[:, c * LANES:(c + 1) * LANES]
                    scatter(jnp.maximum(e - 1, 0), y_prv)

    @pl.when(jnp.logical_and(e == N_EXPERTS - 1, last))
    def _():
        scatter(N_EXPERTS - 1, y_b)


def _moe(idx, gate, h3, wg, wu, wd, layer, *, group0, n_groups, emit_bf16, ff_split):
    grp_rows = GT * ROW_TILES
    ff_tile = EXPERT_FF // ff_split
    route_spec = pl.BlockSpec((1, 1, N_EXPERTS * CAP), lambda b, e, f: (group0 + b, 0, 0),
                              memory_space=pltpu.SMEM)
    w_in_spec = lambda l: pl.BlockSpec((1, 1, D_MODEL, ff_tile), lambda b, e, f: (l, e, 0, f))
    w_out_spec = lambda l: pl.BlockSpec((1, 1, ff_tile, D_MODEL), lambda b, e, f: (l, e, f, 0))
    out_specs = [pl.BlockSpec((grp_rows, LANES), lambda b, e, f: (b, 0), pipeline_mode=pl.Buffered(1))]
    out_shape = [jax.ShapeDtypeStruct((n_groups * grp_rows, LANES), F32)]
    if emit_bf16:
        out_specs += [w_in_spec(0), w_in_spec(0), w_out_spec(0)]
        out_shape += [jax.ShapeDtypeStruct((1, N_EXPERTS, D_MODEL, EXPERT_FF), BF16),
                      jax.ShapeDtypeStruct((1, N_EXPERTS, D_MODEL, EXPERT_FF), BF16),
                      jax.ShapeDtypeStruct((1, N_EXPERTS, EXPERT_FF, D_MODEL), BF16)]
    return pl.pallas_call(
        functools.partial(_moe_kernel, emit_bf16=emit_bf16, ff_split=ff_split),
        grid=(n_groups, N_EXPERTS, ff_split),
        in_specs=[route_spec, route_spec,
                  pl.BlockSpec((grp_rows, LANES), lambda b, e, f: (group0 + b, 0),
                               pipeline_mode=pl.Buffered(1)),
                  w_in_spec(layer), w_in_spec(layer), w_out_spec(layer)],
        out_specs=out_specs,
        out_shape=out_shape,
        scratch_shapes=[pltpu.VMEM((CAP * ROW_TILES, LANES), F32),
                        pltpu.VMEM((CAP, D_MODEL), BF16),
                        pltpu.VMEM((CAP, D_MODEL), BF16),
                        pltpu.VMEM((CAP, D_MODEL), F32),
                        pltpu.VMEM((CAP * ROW_TILES, LANES), F32),
                        pltpu.VMEM((CAP * ROW_TILES, LANES), F32)],
        compiler_params=_cparams(("arbitrary", "arbitrary", "arbitrary")),
        name="moe_experts",
    )(idx, gate, h3, wg, wu, wd)


def _moe_post_kernel(f3lo_ref, f3hi_ref, x_ref, mod_ref, gpost_ref, *o_refs):
    i = pl.program_id(0)
    m = mod_ref[0]
    rows = lambda ref: jnp.concatenate(
        [ref[pl.ds(c, TM, stride=ROW_TILES), :] for c in range(ROW_TILES)], axis=1)
    fx = jnp.where(i < MOE_FIRST_GROUPS * TILES_PER_GROUP, rows(f3lo_ref), rows(f3hi_ref))
    out = x_ref[...] + _rms(fx) * gpost_ref[...] * m[5:6]
    if len(o_refs) == 1:
        o_refs[0][...] = out
    else:
        @pl.when(i < SAMPLE_TILES)
        def _():
            o_refs[0][...] = out

        @pl.when(i >= SAMPLE_TILES)
        def _():
            o_refs[1][...] = out


def _moe_post(f3_parts, x, mods, gpost, split_out):
    if split_out:
        out_specs = _part_specs(D_MODEL, joint=False)
        out_shape = [jax.ShapeDtypeStruct((N_SAMPLE, D_MODEL), F32),
                     jax.ShapeDtypeStruct((N_PROMPT, D_MODEL), F32)]
    else:
        out_specs = pl.BlockSpec((TM, D_MODEL), lambda i: (i, 0))
        out_shape = jax.ShapeDtypeStruct((NT, D_MODEL), F32)
    return pl.pallas_call(
        _moe_post_kernel,
        grid=(NT // TM,),
        in_specs=[*_part_specs(LANES, joint=False, split=MOE_FIRST_GROUPS * TILES_PER_GROUP,
                               rows=TM * ROW_TILES),
                  pl.BlockSpec((TM, D_MODEL), lambda i: (i, 0)),
                  _mod_spec(), _const_spec((1, D_MODEL))],
        out_specs=out_specs,
        out_shape=out_shape,
        compiler_params=_cparams(("arbitrary",)),
        name="moe_post",
    )(*f3_parts, x, mods, gpost)


ROUTE_SEG = SEQ
RANK_LO = 16
RANK_HI = CAP // RANK_LO
TOK_LO = 64
ROUTE_VALS = 5


def _route_kernel(aff_ref, idx_ref, gate_ref, sel_scr, a_scr, code_scr, g_scr, acc_scr):
    g = pl.program_id(0)
    n_seg = GT // ROUTE_SEG
    seg = lambda s: slice(s * ROUTE_SEG, (s + 1) * ROUTE_SEG)
    aff = aff_ref[...]
    tok = lax.broadcasted_iota(jnp.int32, (N_EXPERTS, GT), 1)
    ones_seg = jnp.ones((ROUTE_SEG, ROUTE_SEG), BF16)

    def row_count(mask):
        return jnp.sum(jnp.where(mask, 1.0, 0.0), axis=1, keepdims=True)

    def seg_count(mask):
        m = jnp.where(mask, 1.0, 0.0).astype(BF16)
        return jnp.concatenate(
            [jnp.dot(m[:, seg(s)], ones_seg, preferred_element_type=F32) for s in range(n_seg)], axis=1)

    def select(count, k, shape):
        as_float = lambda b: pltpu.bitcast(b, F32)

        def value_bit(i, t):
            cand = t | jnp.left_shift(jnp.int32(1), 30 - i)
            return jnp.where(count(aff >= as_float(cand)) >= k, cand, t)

        thr = lax.fori_loop(0, 31, value_bit, jnp.zeros(shape, jnp.int32))
        above = aff >= as_float(thr + 1)
        tied = jnp.logical_and(aff >= as_float(thr), jnp.logical_not(above))
        need = k - count(above)

        def index_bit(i, v):
            cand = v | jnp.left_shift(jnp.int32(1), 11 - i)
            return jnp.where(count(jnp.logical_and(tied, tok < cand)) < need, cand, v)

        last = lax.fori_loop(0, 12, index_bit, jnp.zeros(shape, jnp.int32))
        chosen = jnp.logical_or(above, jnp.logical_and(tied, tok <= last))
        sel_scr[...] = jnp.where(chosen, 1.0, 0.0)

    @pl.when(g < DEC_BATCH)
    def _():
        select(row_count, float(2 * DEC_SEQ // N_EXPERTS), (N_EXPERTS, 1))

    @pl.when(g >= DEC_BATCH)
    def _():
        select(seg_count, float(2 * SEQ // N_EXPERTS), (N_EXPERTS, GT))

    sel = sel_scr[...]
    r_i = lax.broadcasted_iota(jnp.int32, (ROUTE_SEG, ROUTE_SEG), 0)
    c_i = lax.broadcasted_iota(jnp.int32, (ROUTE_SEG, ROUTE_SEG), 1)
    upper = jnp.where(r_i <= c_i, 1.0, 0.0).astype(BF16)
    sel_b = sel.astype(BF16)
    off = jnp.zeros((N_EXPERTS, 1), F32)
    for s in range(n_seg):
        inc = jnp.dot(sel_b[:, seg(s)], upper, preferred_element_type=F32)
        rank = inc + off - sel[:, seg(s)]
        a = jnp.floor(rank * (1.0 / RANK_LO))
        a_scr[:, seg(s)] = jnp.where(sel[:, seg(s)] > 0, a, -1.0)
        code_scr[:, seg(s)] = jnp.where(sel[:, seg(s)] > 0, rank - RANK_LO * a + 1.0, 0.0).astype(BF16)
        off = off + inc[:, ROUTE_SEG - 1:ROUTE_SEG]
    g1 = aff.astype(BF16).astype(F32)
    g2 = (aff - g1).astype(BF16).astype(F32)
    g_scr[0] = g1
    g_scr[1] = g2
    g_scr[2] = (aff - g1 - g2).astype(BF16).astype(F32)

    lane = lax.broadcasted_iota(jnp.int32, (N_EXPERTS, N_EXPERTS * RANK_LO), 1)
    row = lax.broadcasted_iota(jnp.int32, (N_EXPERTS, N_EXPERTS * RANK_LO), 0)
    lo_bits = RANK_LO.bit_length() - 1
    expand = jnp.where(jnp.right_shift(lane, lo_bits) == row, 1.0, 0.0).astype(BF16)
    digit = (jnp.bitwise_and(lax.broadcasted_iota(jnp.int32, (1, N_EXPERTS * RANK_LO), 1), RANK_LO - 1)
             + 1).astype(F32)
    a_iota = lax.broadcasted_iota(jnp.int32, (RANK_HI, 1), 0).astype(F32)
    acc_scr[...] = jnp.zeros_like(acc_scr)
    contract0 = (((0,), (0,)), ((), ()))

    def chunk(c):
        lanes = pl.ds(pl.multiple_of(c * ROUTE_SEG, ROUTE_SEG), ROUTE_SEG)
        a_c = a_scr[:, lanes]
        g_c = [g_scr[j, :, lanes] for j in range(3)]
        tok_c = lax.broadcasted_iota(jnp.int32, (1, ROUTE_SEG), 1) + c * ROUTE_SEG
        t_hi = jnp.right_shift(tok_c, TOK_LO.bit_length() - 1).astype(F32)
        t_lo = jnp.bitwise_and(tok_c, TOK_LO - 1).astype(F32)
        rows = []
        for e in range(N_EXPERTS):
            hit = a_c[e:e + 1, :] == a_iota
            for val in (t_hi, t_lo, g_c[0][e:e + 1, :], g_c[1][e:e + 1, :], g_c[2][e:e + 1, :]):
                rows.append(jnp.where(hit, val, 0.0))
        lhs = jnp.concatenate(rows, axis=0).astype(BF16)
        spread = lax.dot_general(code_scr[:, lanes], expand, contract0, preferred_element_type=F32)
        low_hot = jnp.where(spread == digit, 1.0, 0.0).astype(BF16)
        return jnp.dot(lhs, low_hot, preferred_element_type=F32)

    def chunk_pair(c, carry):
        acc_scr[...] += chunk(2 * c) + chunk(2 * c + 1)
        return carry

    lax.fori_loop(0, n_seg // 2, chunk_pair, 0)
    per_e = ROUTE_VALS * RANK_HI
    for e in range(N_EXPERTS):
        blk = acc_scr[e * per_e:(e + 1) * per_e, :][:, e * RANK_LO:(e + 1) * RANK_LO]
        part = lambda j: blk[j * RANK_HI:(j + 1) * RANK_HI]
        idx_ref[0, e] = (part(0) * TOK_LO + part(1)).astype(jnp.int32)
        gate_ref[0, e] = part(2) + part(3) + part(4)


def _route(aff_t):
    out_spec = pl.BlockSpec((1, N_EXPERTS, RANK_HI, RANK_LO), lambda g: (g, 0, 0, 0))
    idx, gate = pl.pallas_call(
        _route_kernel,
        grid=(NG,),
        in_specs=[pl.BlockSpec((N_EXPERTS, GT), lambda g: (0, g))],
        out_specs=[out_spec, out_spec],
        out_shape=[jax.ShapeDtypeStruct((NG, N_EXPERTS, RANK_HI, RANK_LO), jnp.int32),
                   jax.ShapeDtypeStruct((NG, N_EXPERTS, RANK_HI, RANK_LO), F32)],
        scratch_shapes=[pltpu.VMEM((N_EXPERTS, GT), F32),
                        pltpu.VMEM((N_EXPERTS, GT), F32),
                        pltpu.VMEM((N_EXPERTS, GT), BF16),
                        pltpu.VMEM((3, N_EXPERTS, GT), F32),
                        pltpu.VMEM((N_EXPERTS * ROUTE_VALS * RANK_HI, N_EXPERTS * RANK_LO), F32)],
        compiler_params=_cparams(("arbitrary",)),
        name="route",
    )(aff_t)
    return idx.reshape(NG, 1, N_EXPERTS * CAP), gate.reshape(NG, 1, N_EXPERTS * CAP)


HD = MLA_HEADS * LANES
Q_SCALE = math.log2(math.e) / math.sqrt(MLA_NOPE + MLA_ROPE)


def _mla_proj_kernel(xlo_ref, xhi_ref, mod_ref, gain_ref, win_ref, qn_g_ref, kvn_g_ref, wq_ref, wkv_ref,
                     cos_ref, sin_ref, q_ref, kn_ref, v_ref, krp_ref, ckv_ref, kraw_ref):
    i = pl.program_id(0)
    m = mod_ref[0]
    h = (_rms(_load_rows(xlo_ref, xhi_ref, i)) * gain_ref[...] * (1.0 + m[1:2]) + m[0:1]).astype(BF16)
    lat = jnp.dot(h, win_ref[...], preferred_element_type=F32)
    q_lat = lat[:, :MLA_Q_LORA]
    kv_lat = lat[:, MLA_Q_LORA:MLA_Q_LORA + MLA_KV_LORA]
    kr = lat[:, MLA_Q_LORA + MLA_KV_LORA:MLA_Q_LORA + MLA_KV_LORA + LANES]
    kr_rot = lat[:, MLA_Q_LORA + MLA_KV_LORA + LANES:]
    is_latent = i < SAMPLE_TILES
    cos = jnp.where(is_latent, cos_ref[...], 1.0)
    sin = jnp.where(is_latent, sin_ref[...], 0.0)

    qln = (_rms(q_lat) * qn_g_ref[...]).astype(BF16)
    q = jnp.dot(qln, wq_ref[...], preferred_element_type=F32) * Q_SCALE
    for hd in range(MLA_HEADS):
        lo = HD + hd * LANES
        q_ref[:, 2 * hd * LANES:(2 * hd + 1) * LANES] = q[:, hd * LANES:(hd + 1) * LANES].astype(BF16)
        q_ref[:, (2 * hd + 1) * LANES:(2 * hd + 2) * LANES] = (
            q[:, lo:lo + LANES] * cos + q[:, HD + lo:HD + lo + LANES] * sin).astype(BF16)

    ckv = _rms(kv_lat) * kvn_g_ref[...]
    kv = jnp.dot(ckv.astype(BF16), wkv_ref[...], preferred_element_type=F32)
    kn_ref[...] = kv[:, :HD].astype(BF16)
    v_ref[...] = kv[:, HD:].astype(BF16)
    krp_ref[...] = (kr * cos + kr_rot * sin).astype(BF16)

    @pl.when(i >= SAMPLE_TILES)
    def _():
        ckv_ref[...] = ckv
        kraw_ref[...] = kr


def _mla_proj(x_parts, joint, mods, gain, win_ext, qn_g, kvn_g, wq_ext, wkv_ext, cos128, sin128):
    row = lambda w: pl.BlockSpec((TM, w), lambda i: (i, 0))
    ctx_row = lambda w: pl.BlockSpec((TM, w), lambda i: (jnp.maximum(i - SAMPLE_TILES, 0), 0))
    tab = pl.BlockSpec((TM, LANES), lambda i: (i % TILES_PER_GROUP, 0))
    return pl.pallas_call(
        _mla_proj_kernel,
        grid=(NT // TM,),
        in_specs=[*_part_specs(D_MODEL, joint), _mod_spec(),
                  _const_spec(gain.shape), _const_spec(win_ext.shape), _const_spec(qn_g.shape),
                  _const_spec(kvn_g.shape), _const_spec(wq_ext.shape), _const_spec(wkv_ext.shape),
                  tab, tab],
        out_specs=[row(2 * HD), row(HD), row(HD), row(LANES), ctx_row(MLA_KV_LORA), ctx_row(LANES)],
        out_shape=[jax.ShapeDtypeStruct((NT, 2 * HD), BF16),
                   jax.ShapeDtypeStruct((NT, HD), BF16),
                   jax.ShapeDtypeStruct((NT, HD), BF16),
                   jax.ShapeDtypeStruct((NT, LANES), BF16),
                   jax.ShapeDtypeStruct((N_PROMPT, MLA_KV_LORA), F32),
                   jax.ShapeDtypeStruct((N_PROMPT, LANES), F32)],
        compiler_params=_cparams(("arbitrary",)),
        name="mla_proj",
    )(*x_parts, mods, gain, win_ext, qn_g, kvn_g, wq_ext, wkv_ext, cos128, sin128)


def _matmul_kernel(a_ref, w_ref, o_ref):
    o_ref[...] = jnp.dot(a_ref[...].astype(BF16), w_ref[...],
                         preferred_element_type=F32).astype(o_ref.dtype)


def _ctx_expand(ckv_ctx, wkv_ext):
    n = ckv_ctx.shape[0]
    return pl.pallas_call(
        _matmul_kernel,
        grid=(n // TM,),
        in_specs=[pl.BlockSpec((TM, MLA_KV_LORA), lambda i: (i, 0)),
                  pl.BlockSpec(wkv_ext.shape, lambda i: (0, 0))],
        out_specs=pl.BlockSpec((TM, 2 * HD), lambda i: (i, 0)),
        out_shape=jax.ShapeDtypeStruct((n, 2 * HD), BF16),
        compiler_params=_cparams(("parallel",)),
        name="ctx_expand",
    )(ckv_ctx, wkv_ext)


ATTN_CHUNK_UNROLL = 8


def _attn_kernel(q_ref, kn_ref, krp_ref, v_ref, *rest, tk, n_chunks, has_ctx, heads_per_step):
    if has_ctx:
        knc_ref, krpc_ref, vc_ref, o_ref = rest
    else:
        o_ref, = rest
    tq = q_ref.shape[0]
    contract1 = (((1,), (1,)), ((), ()))
    heads = range(heads_per_step)
    head_cols = lambda hd: slice(hd * LANES, (hd + 1) * LANES)
    qs = [q_ref[:, 2 * hd * LANES:2 * (hd + 1) * LANES] for hd in heads]

    def ones_column(rows):
        return (lax.broadcasted_iota(jnp.int32, (rows, LANES), 1) == 0).astype(BF16)

    def step(q, kn, krp, v, ones, carry):
        m, acc = carry
        kc = jnp.concatenate([kn, krp], axis=1)
        s = lax.dot_general(q, kc, contract1, preferred_element_type=F32)
        m_new = jnp.maximum(m, jnp.max(s, axis=-1, keepdims=True))
        p = jnp.exp2(s - m_new).astype(BF16)
        pv = jnp.dot(p, jnp.concatenate([v, ones], axis=1), preferred_element_type=F32)
        return m_new, jnp.exp2(m - m_new) * acc + pv

    carry = tuple((jnp.full((tq, 1), -jnp.inf, F32), jnp.zeros((tq, 2 * LANES), F32)) for _ in heads)
    if has_ctx:
        ones = ones_column(PAST_LEN)
        carry = tuple(step(qs[hd], knc_ref[0, :, head_cols(hd)], krpc_ref[0], vc_ref[0, :, head_cols(hd)],
                           ones, carry[hd]) for hd in heads)
    ones = ones_column(tk)

    def chunk(c, carry):
        rows = pl.ds(pl.multiple_of(c * tk, tk), tk)
        krp = krp_ref[rows, :]
        return tuple(step(qs[hd], kn_ref[rows, head_cols(hd)], krp, v_ref[rows, head_cols(hd)],
                          ones, carry[hd]) for hd in heads)

    unroll = math.gcd(n_chunks, ATTN_CHUNK_UNROLL)

    def body(c, carry):
        for u in range(unroll):
            carry = chunk(c * unroll + u, carry)
        return carry

    carry = lax.fori_loop(0, n_chunks // unroll, body, carry)
    for hd in heads:
        acc = carry[hd][1]
        o_ref[:, head_cols(hd)] = (acc[:, :MLA_V] / acc[:, MLA_V:MLA_V + 1]).astype(o_ref.dtype)


def _attention(q, kn, krp, v, ctx, *, n_req, seq_len, row0, tq, tk, heads_per_step):
    rb0 = row0 // seq_len
    qb0 = row0 // tq
    nq = seq_len // tq
    hw = heads_per_step * LANES
    o_spec = pl.BlockSpec((tq, hw), lambda b, h, i: (b * nq + i, h))
    k_spec = pl.BlockSpec((seq_len, hw), lambda b, h, i: (rb0 + b, h))
    in_specs = [pl.BlockSpec((tq, 2 * hw), lambda b, h, i: (qb0 + b * nq + i, h)), k_spec,
                pl.BlockSpec((seq_len, LANES), lambda b, h, i: (rb0 + b, 0)), k_spec]
    args = [q, kn, krp, v]
    if ctx is not None:
        c_spec = pl.BlockSpec((1, PAST_LEN, hw), lambda b, h, i: (b, 0, h))
        in_specs += [c_spec, pl.BlockSpec((1, PAST_LEN, LANES), lambda b, h, i: (b, 0, 0)), c_spec]
        args += list(ctx)
    return pl.pallas_call(
        functools.partial(_attn_kernel, tk=tk, n_chunks=seq_len // tk, has_ctx=ctx is not None,
                          heads_per_step=heads_per_step),
        grid=(n_req, MLA_HEADS // heads_per_step, nq),
        in_specs=in_specs,
        out_specs=o_spec,
        out_shape=jax.ShapeDtypeStruct((n_req * seq_len, HD), BF16),
        compiler_params=_cparams(("parallel", "parallel", "arbitrary")),
        name=f"attention_{seq_len}",
    )(*args)


def _rope_rot_cols(w):
    w1, w2, w3, w4 = jnp.split(w, 4, axis=-1)
    return jnp.concatenate([-w2, w1, -w4, w3], axis=-1)


def _pad_cols(w, width):
    return jnp.pad(w, ((0, 0), (0, width - w.shape[1])))


def _mla_weights(w_in, w_q_b, w_kv_b):
    w_rope = w_in[:, MLA_Q_LORA + MLA_KV_LORA:]
    win_ext = jnp.concatenate([w_in[:, :MLA_Q_LORA + MLA_KV_LORA], _pad_cols(w_rope, LANES),
                               _pad_cols(_rope_rot_cols(w_rope), LANES)], axis=1).astype(BF16)
    wq = w_q_b.reshape(MLA_Q_LORA, MLA_HEADS, MLA_NOPE + MLA_ROPE)
    wq_nope = wq[:, :, :MLA_NOPE].reshape(MLA_Q_LORA, HD)
    wq_rope = wq[:, :, MLA_NOPE:]
    pad = ((0, 0), (0, 0), (0, LANES - MLA_ROPE))
    wq_rope_p = jnp.pad(wq_rope, pad).reshape(MLA_Q_LORA, HD)
    wq_rot_p = jnp.pad(_rope_rot_cols(wq_rope), pad).reshape(MLA_Q_LORA, HD)
    wq_ext = jnp.concatenate([wq_nope, wq_rope_p, wq_rot_p], axis=1).astype(BF16)
    wkv = w_kv_b.reshape(MLA_KV_LORA, MLA_HEADS, MLA_NOPE + MLA_V)
    wkv_ext = jnp.concatenate([wkv[:, :, :MLA_NOPE].reshape(MLA_KV_LORA, HD),
                               wkv[:, :, MLA_NOPE:].reshape(MLA_KV_LORA, HD)], axis=1).astype(BF16)
    return win_ext, wq_ext, wkv_ext


def _rope_tables():
    rows = DEC_SEQ // GRID_W
    row = jnp.repeat(jnp.arange(rows, dtype=F32), GRID_W)
    col = jnp.tile(jnp.arange(GRID_W, dtype=F32), rows)
    half = MLA_ROPE // 2
    inv = 1.0 / (ROPE_BASE ** (jnp.arange(0, half, 2, dtype=F32) / half))
    ar = row[:, None] * inv
    ac = col[:, None] * inv
    ang = jnp.concatenate([ar, ar, ac, ac] * 2, axis=-1)
    return jnp.cos(ang), jnp.sin(ang)


def kernel(x_prompt, x_sample, state_ret_fwd, state_ret_bwd, cache_mla_ckv, cache_mla_krope, c, c_ctx,
           ada_w, ada_b, norm_pre, norm_post, ret_w_in, ret_decay_fwd, ret_decay_bwd, ret_w_out,
           mla_w_in, mla_q_norm, mla_kv_norm, mla_w_q_b, mla_w_kv_b, mla_w_out,
           moe_w_router, moe_w_gate, moe_w_up, moe_w_down):
    x_parts = (x_sample.reshape(N_SAMPLE, D_MODEL), x_prompt.reshape(N_PROMPT, D_MODEL))
    joint = False
    cond8 = jnp.concatenate([c, c_ctx[None, :], jnp.zeros((8 - NG, D_MODEL), F32)], axis=0)
    cos128, sin128 = _rope_tables()
    outs = {}

    for i in range(DEPTH):
        mods = _adaln(cond8, ada_w, ada_b.reshape(DEPTH, 1, 6 * D_MODEL), i)[:NG].reshape(NG, 6, D_MODEL)
        gpre1 = norm_pre[i, 0][None, :]
        gpre2 = norm_pre[i, 1][None, :]
        gpost1 = norm_post[i, 0][None, :]
        gpost2 = norm_post[i, 1][None, :]
        if i % 2 == 0:
            r = i // 2
            qkvg = _norm_mod_matmul(x_parts, joint, mods, gpre1, ret_w_in[r], 2048)
            lg = jnp.stack([jax.nn.log_sigmoid(ret_decay_fwd[r].astype(F32)),
                            jax.nn.log_sigmoid(ret_decay_bwd[r].astype(F32))])
            yg_s, _, _ = _retention(qkvg, lg, state_ret_fwd[:, r], state_ret_bwd[:, r],
                                    n_req=DEC_BATCH, seq_len=DEC_SEQ, row0=0, heads_per_step=1)
            yg_p, s_f, s_b = _retention(qkvg, lg, None, None, n_req=BATCH, seq_len=SEQ, row0=N_SAMPLE,
                                        heads_per_step=RET_HEADS)
            outs["fwd"], outs["bwd"] = s_f[:, None], s_b[:, None]
            mix, w_out = (yg_s, yg_p), ret_w_out[r]
        else:
            mi = i // 2
            win_ext, wq_ext, wkv_ext = _mla_weights(mla_w_in[mi], mla_w_q_b[mi], mla_w_kv_b[mi])
            q, kn, v, krp, ckv, kraw = _mla_proj(
                x_parts, joint, mods, gpre1, win_ext, mla_q_norm[mi][None, :], mla_kv_norm[mi][None, :],
                wq_ext, wkv_ext, cos128, sin128)
            outs["ckv"] = ckv.reshape(BATCH, 1, SEQ, MLA_KV_LORA)
            outs["krope"] = kraw[:, :MLA_ROPE].reshape(BATCH, 1, SEQ, MLA_ROPE)
            kvc = _ctx_expand(cache_mla_ckv[:, mi].reshape(DEC_BATCH * PAST_LEN, MLA_KV_LORA), wkv_ext)
            kvc = kvc.reshape(DEC_BATCH, PAST_LEN, 2 * HD)
            krpc = jnp.pad(cache_mla_krope[:, mi], ((0, 0), (0, 0), (0, LANES - MLA_ROPE))).astype(BF16)
            o_s = _attention(q, kn, krp, v, (kvc[:, :, :HD], krpc, kvc[:, :, HD:]),
                             n_req=DEC_BATCH, seq_len=DEC_SEQ, row0=0, tq=512, tk=512, heads_per_step=4)
            o_p = _attention(q, kn, krp, v, None, n_req=BATCH, seq_len=SEQ, row0=N_SAMPLE, tq=SEQ, tk=SEQ,
                             heads_per_step=MLA_HEADS)
            mix, w_out = (o_s, o_p), mla_w_out[mi]
        x1, h3, aff_t = _out_post(mix, w_out, x_parts, joint, mods, gpost1, gpre2, moe_w_router[i].T)
        idx, gate = _route(aff_t)
        f3_lo, wg16, wu16, wd16 = _moe(idx, gate, h3, moe_w_gate, moe_w_up, moe_w_down, i,
                                       group0=0, n_groups=MOE_FIRST_GROUPS, emit_bf16=True,
                                       ff_split=MOE_FIRST_FF_SPLIT)
        f3_hi, = _moe(idx, gate, h3, wg16, wu16, wd16, 0,
                      group0=MOE_FIRST_GROUPS, n_groups=NG - MOE_FIRST_GROUPS, emit_bf16=False,
                      ff_split=MOE_REST_FF_SPLIT)
        last = i == DEPTH - 1
        x_new = _moe_post((f3_lo, f3_hi), x1, mods, gpost2, split_out=last)
        if not last:
            x_parts, joint = (x_new, x_new), True

    y_sample, y_prompt = x_new
    return (y_prompt.reshape(BATCH, SEQ, D_MODEL), y_sample.reshape(DEC_BATCH, DEC_SEQ, D_MODEL),
            outs["fwd"], outs["bwd"], outs["ckv"], outs["krope"])
```
